```python
import math
import jax, jax.numpy as jnp
from jax import lax
import numpy as np

D_MODEL = 1024
BATCH = 8
SEQ = 2048
DEPTH = 4
DEC_BATCH = 128
DEC_SEQ = 8
PAST_LEN = 2048
PAGE_SIZE = 128

N_MIXERS = 3
N_META = 16
D_FF = 2816
N_HEADS = 8
HEAD_DIM = 64
V_DIM = 2 * HEAD_DIM
N_BUCKETS = 32
MAX_EXACT = N_BUCKETS // 2
MAX_DISTANCE = 128
POOL_WINDOWS = (2, 4, 8, 16)
N_POOL_GROUPS = len(POOL_WINDOWS)
POOL_GROUP_DIM = D_MODEL // N_POOL_GROUPS
POOL_BUF = max(POOL_WINDOWS) - 1
CONV_WIDTH = 31
CONV_BUF = CONV_WIDTH - 1
Q_BLOCK = 128
N_POOL_LAYERS = len(range(0, DEPTH, N_MIXERS))
N_ATTN_LAYERS = len(range(1, DEPTH, N_MIXERS))
N_CONV_LAYERS = len(range(2, DEPTH, N_MIXERS))
RMS_EPS = 1e-6
LN_EPS = 1e-5
NEG_INF = -1e30

kernel_name = "hybrid_pool_diffattn_conformer_decode_step"


def lambda_init(layer_idx):
    return 0.8 - 0.6 * math.exp(-0.3 * layer_idx)


def rmsnorm(x, g):
    xf = x.astype(jnp.float32)
    y = xf * lax.rsqrt(jnp.mean(xf * xf, axis=-1, keepdims=True) + RMS_EPS)
    return (y * g.astype(jnp.float32)).astype(x.dtype)


def swiglu(h, wg, wu, wd):
    return (jax.nn.silu(h @ wg) * (h @ wu)) @ wd


def pool_mixer(h, buf, start, w_grp, scale):
    B, T, D = h.shape
    cat = jnp.concatenate([buf.astype(h.dtype), h], axis=1)
    cs = jnp.cumsum(cat.astype(jnp.float32), axis=1)
    cs = jnp.pad(cs, ((0, 0), (1, 0), (0, 0)))
    pos = start + jnp.arange(T)
    hi = cs[:, POOL_BUF + 1:POOL_BUF + 1 + T]
    means = []
    for g, w in enumerate(POOL_WINDOWS):
        c0, c1 = g * POOL_GROUP_DIM, (g + 1) * POOL_GROUP_DIM
        lo = cs[:, POOL_BUF + 1 - w:POOL_BUF + 1 - w + T, c0:c1]
        cnt = jnp.minimum(w, pos + 1).astype(jnp.float32)[None, :, None]
        means.append((hi[..., c0:c1] - lo) / cnt)
    pooled = (jnp.concatenate(means, axis=-1) - h.astype(jnp.float32)).astype(h.dtype)
    pooled = pooled.reshape(B, T, N_POOL_GROUPS, POOL_GROUP_DIM)
    out = jnp.einsum('btgc,gcd->btgd', pooled, w_grp).reshape(B, T, D) * scale
    return out, cat[:, -POOL_BUF:]


def conv_module(h, buf, w1, b1, wdw, bdw, ln_g, ln_b, w2, b2):
    z = h @ w1 + b1
    u = z[..., :D_MODEL] * jax.nn.sigmoid(z[..., D_MODEL:])
    cat = jnp.concatenate([buf.astype(u.dtype), u], axis=1)
    y = lax.conv_general_dilated(cat, wdw[:, None, :], (1,), 'VALID',
                                 dimension_numbers=('NWC', 'WIO', 'NWC'),
                                 feature_group_count=D_MODEL) + bdw
    yf = y.astype(jnp.float32)
    mu = jnp.mean(yf, axis=-1, keepdims=True)
    var = jnp.mean(jnp.square(yf - mu), axis=-1, keepdims=True)
    yn = ((yf - mu) * lax.rsqrt(var + LN_EPS) * ln_g.astype(jnp.float32)
          + ln_b.astype(jnp.float32)).astype(y.dtype)
    return jax.nn.silu(yn) @ w2 + b2, cat[:, -CONV_BUF:]


def rel_pos_bias(q_pos, k_pos, table):
    n = jnp.maximum(q_pos[:, None] - k_pos[None, :], 0)
    nf = jnp.maximum(n, 1).astype(jnp.float32)
    large = MAX_EXACT + (jnp.log(nf / MAX_EXACT) / math.log(MAX_DISTANCE / MAX_EXACT)
                         * (N_BUCKETS - MAX_EXACT)).astype(jnp.int32)
    large = jnp.minimum(large, N_BUCKETS - 1)
    bucket = jnp.where(n < MAX_EXACT, n, large)
    return jnp.transpose(table.astype(jnp.float32)[bucket], (2, 0, 1))


def diff_qkv(h, wqkv):
    B, T, _ = h.shape
    qkv = (h @ wqkv).reshape(B, T, 3, N_HEADS, V_DIM)
    return qkv[:, :, 0], qkv[:, :, 1], qkv[:, :, 2]


def diff_core(q, k, v, q_pos, k_pos, table, lam):
    bias = rel_pos_bias(q_pos, k_pos, table)
    mask = k_pos[None, :] <= q_pos[:, None]
    scale = HEAD_DIM ** -0.5

    def softmax_map(qq, kk):
        s = jnp.einsum('bqhd,bkhd->bhqk', qq, kk, preferred_element_type=jnp.float32) * scale + bias
        return jax.nn.softmax(jnp.where(mask, s, NEG_INF), axis=-1)

    a = (softmax_map(q[..., :HEAD_DIM], k[..., :HEAD_DIM])
         - lam * softmax_map(q[..., HEAD_DIM:], k[..., HEAD_DIM:]))
    return jnp.einsum('bhqk,bkhd->bqhd', a.astype(v.dtype), v)


def diff_lambda(lp, lam_init):
    lpf = lp.astype(jnp.float32)
    return jnp.exp(jnp.sum(lpf[0] * lpf[1])) - jnp.exp(jnp.sum(lpf[2] * lpf[3])) + lam_init


def diff_out(o, lam_init, subln, wo):
    B, T = o.shape[:2]
    o = rmsnorm(o, subln) * (1.0 - lam_init)
    return o.reshape(B, T, N_HEADS * V_DIM) @ wo


def diff_attn_prompt(h, wqkv, wo, lp, subln, table, lam_init):
    q, k, v = diff_qkv(h, wqkv)
    B, T = h.shape[:2]
    lam = diff_lambda(lp, lam_init)
    t_pad = -(-T // Q_BLOCK) * Q_BLOCK
    nb = t_pad // Q_BLOCK
    qp = jnp.pad(q, ((0, 0), (0, t_pad - T), (0, 0), (0, 0)))
    qb = qp.reshape(B, nb, Q_BLOCK, N_HEADS, V_DIM).transpose(1, 0, 2, 3, 4)
    posb = jnp.arange(t_pad).reshape(nb, Q_BLOCK)
    k_pos = jnp.arange(T)
    ob = lax.map(lambda a: diff_core(a[0], k, v, a[1], k_pos, table, lam), (qb, posb))
    o = ob.transpose(1, 0, 2, 3, 4).reshape(B, t_pad, N_HEADS, V_DIM)[:, :T]
    return diff_out(o, lam_init, subln, wo), k, v


def diff_attn_sample(h, cache_k, cache_v, page_table, layer_b, wqkv, wo, lp, subln, table, lam_init):
    q, k, v = diff_qkv(h, wqkv)
    DB, S = h.shape[:2]
    past = page_table.shape[1] * PAGE_SIZE
    lam = diff_lambda(lp, lam_init)
    kp = cache_k[page_table, :, layer_b].reshape(DB, past, N_HEADS, V_DIM)
    vp = cache_v[page_table, :, layer_b].reshape(DB, past, N_HEADS, V_DIM)
    k_all = jnp.concatenate([kp.astype(k.dtype), k], axis=1)
    v_all = jnp.concatenate([vp.astype(v.dtype), v], axis=1)
    k_pos = jnp.arange(past + S)
    q_pos = past + jnp.arange(S)
    o = diff_core(q, k_all, v_all, q_pos, k_pos, table, lam)
    return diff_out(o, lam_init, subln, wo), k, v


def setup_inputs(seed: int = 0) -> dict:
    key = jax.random.key(seed)
    ks = iter(jax.random.split(key, 48))

    def nrm(shape, scale):
        return jax.random.normal(next(ks), shape, jnp.float32) * scale

    n_pages = PAST_LEN // PAGE_SIZE
    n_used = DEC_BATCH * n_pages
    n_pool = n_used + n_used // 4
    page_table = jax.random.permutation(next(ks), n_pool)[:n_used].reshape(DEC_BATCH, n_pages).astype(jnp.int32)
    D, F = D_MODEL, D_FF
    return {
        "x_prompt": nrm((BATCH, SEQ, D), 1.0),
        "x_sample": nrm((DEC_BATCH, DEC_SEQ, D), 1.0),
        "state_pool": nrm((N_POOL_LAYERS, DEC_BATCH, POOL_BUF, D), 1.0),
        "state_conv": nrm((N_CONV_LAYERS, DEC_BATCH, CONV_BUF, D), 0.5),
        "cache_k": nrm((n_pool, PAGE_SIZE, N_ATTN_LAYERS, N_HEADS, V_DIM), 1.0),
        "cache_v": nrm((n_pool, PAGE_SIZE, N_ATTN_LAYERS, N_HEADS, V_DIM), 1.0),
        "page_table": page_table,
        "meta_tokens": nrm((N_META, D), 1.0),
        "rel_bias_table": nrm((N_BUCKETS, N_HEADS), 0.5),
        "ffn_norm": 1.0 + nrm((DEPTH, 2, 2, D), 0.05),
        "ffn_wg": nrm((DEPTH, 2, D, F), D ** -0.5),
        "ffn_wu": nrm((DEPTH, 2, D, F), D ** -0.5),
        "ffn_wd": nrm((DEPTH, 2, F, D), F ** -0.5),
        "mix_norm": 1.0 + nrm((DEPTH, 2, D), 0.05),
        "pool_w": nrm((N_POOL_LAYERS, N_POOL_GROUPS, POOL_GROUP_DIM, POOL_GROUP_DIM), POOL_GROUP_DIM ** -0.5),
        "pool_scale": 1.0 + nrm((N_POOL_LAYERS, D), 0.1),
        "attn_wqkv": nrm((N_ATTN_LAYERS, D, 3 * N_HEADS * V_DIM), D ** -0.5),
        "attn_wo": nrm((N_ATTN_LAYERS, N_HEADS * V_DIM, D), (N_HEADS * V_DIM) ** -0.5),
        "attn_lambda": nrm((N_ATTN_LAYERS, 4, HEAD_DIM), 0.1),
        "attn_subln": 1.0 + nrm((N_ATTN_LAYERS, V_DIM), 0.05),
        "conv_w1": nrm((N_CONV_LAYERS, D, 2 * D), D ** -0.5),
        "conv_b1": nrm((N_CONV_LAYERS, 2 * D), 0.02),
        "conv_wdw": nrm((N_CONV_LAYERS, CONV_WIDTH, D), CONV_WIDTH ** -0.5),
        "conv_bdw": nrm((N_CONV_LAYERS, D), 0.02),
        "conv_ln_g": 1.0 + nrm((N_CONV_LAYERS, D), 0.05),
        "conv_ln_b": nrm((N_CONV_LAYERS, D), 0.02),
        "conv_w2": nrm((N_CONV_LAYERS, D, D), D ** -0.5),
        "conv_b2": nrm((N_CONV_LAYERS, D), 0.02),
        "final_norm": 1.0 + nrm((D,), 0.05),
    }


def reference(x_prompt, x_sample, state_pool, state_conv, cache_k, cache_v, page_table,
              meta_tokens, rel_bias_table, ffn_norm, ffn_wg, ffn_wu, ffn_wd, mix_norm,
              pool_w, pool_scale, attn_wqkv, attn_wo, attn_lambda, attn_subln,
              conv_w1, conv_b1, conv_wdw, conv_bdw, conv_ln_g, conv_ln_b, conv_w2, conv_b2,
              final_norm):
    B = x_prompt.shape[0]
    past_len = page_table.shape[1] * PAGE_SIZE
    meta = jnp.broadcast_to(meta_tokens[None].astype(x_prompt.dtype), (B, N_META, D_MODEL))
    x_p = jnp.concatenate([meta, x_prompt], axis=1)
    x_s = x_sample
    pool_p, pool_s, conv_p, conv_s = [], [], [], []
    k_p, v_p, k_s, v_s = [], [], [], []

    for i in range(DEPTH):
        kind, j = i % N_MIXERS, i // N_MIXERS

        def ffn_step(x, f):
            y = swiglu(rmsnorm(x, ffn_norm[i, f, 0]), ffn_wg[i, f], ffn_wu[i, f], ffn_wd[i, f])
            return x + 0.5 * rmsnorm(y, ffn_norm[i, f, 1])

        x_p = ffn_step(x_p, 0)
        x_s = ffn_step(x_s, 0)
        h_p = rmsnorm(x_p, mix_norm[i, 0])
        h_s = rmsnorm(x_s, mix_norm[i, 0])
        if kind == 0:
            zero_buf = jnp.zeros((B, POOL_BUF, D_MODEL), h_p.dtype)
            m_p, st_p = pool_mixer(h_p, zero_buf, 0, pool_w[j], pool_scale[j])
            m_s, st_s = pool_mixer(h_s, state_pool[j], past_len, pool_w[j], pool_scale[j])
            pool_p.append(st_p)
            pool_s.append(st_s)
        elif kind == 1:
            lam_init = lambda_init(i)
            m_p, kk, vv = diff_attn_prompt(h_p, attn_wqkv[j], attn_wo[j], attn_lambda[j],
                                           attn_subln[j], rel_bias_table, lam_init)
            k_p.append(kk)
            v_p.append(vv)
            m_s, kk, vv = diff_attn_sample(h_s, cache_k, cache_v, page_table, j, attn_wqkv[j],
                                           attn_wo[j], attn_lambda[j], attn_subln[j],
                                           rel_bias_table, lam_init)
            k_s.append(kk)
            v_s.append(vv)
        else:
            zero_buf = jnp.zeros((B, CONV_BUF, D_MODEL), h_p.dtype)
            cw = (conv_w1[j], conv_b1[j], conv_wdw[j], conv_bdw[j], conv_ln_g[j], conv_ln_b[j],
                  conv_w2[j], conv_b2[j])
            m_p, st_p = conv_module(h_p, zero_buf, *cw)
            m_s, st_s = conv_module(h_s, state_conv[j], *cw)
            conv_p.append(st_p)
            conv_s.append(st_s)
        x_p = x_p + rmsnorm(m_p, mix_norm[i, 1])
        x_s = x_s + rmsnorm(m_s, mix_norm[i, 1])
        x_p = ffn_step(x_p, 1)
        x_s = ffn_step(x_s, 1)

    y_prompt = rmsnorm(x_p, final_norm)[:, N_META:]
    y_sample = rmsnorm(x_s, final_norm)
    new_pool_p = jnp.stack(pool_p, axis=0)
    new_pool_s = jnp.stack(pool_s, axis=0)
    new_conv_p = jnp.stack(conv_p, axis=0)
    new_conv_s = jnp.stack(conv_s, axis=0)
    new_k_p = jnp.stack(k_p, axis=2)
    new_v_p = jnp.stack(v_p, axis=2)
    new_k_s = jnp.stack(k_s, axis=2)
    new_v_s = jnp.stack(v_s, axis=2)
    return (y_prompt, y_sample, new_pool_p, new_pool_s, new_conv_p, new_conv_s,
            new_k_p, new_v_p, new_k_s, new_v_s)
```

```python
import functools
import math

import jax
import jax.numpy as jnp
from jax import lax
from jax.experimental import pallas as pl
from jax.experimental.pallas import tpu as pltpu

F32 = jnp.float32
BF16 = jnp.bfloat16

N_MIXERS = 3
N_META = 16
N_HEADS = 8
HEAD_DIM = 64
V_DIM = 2 * HEAD_DIM
N_BUCKETS = 32
MAX_EXACT = N_BUCKETS // 2
MAX_DISTANCE = 128
POOL_WINDOWS = (2, 4, 8, 16)
POOL_BUF = max(POOL_WINDOWS) - 1
CONV_WIDTH = 31
CONV_BUF = CONV_WIDTH - 1
PAGE_SIZE = 128
RMS_EPS = 1e-6
LN_EPS = 1e-5
NEG_INF = -1e30
ATTN_BLOCK = 256
POOL_HALO = 16
CONV_HALO = 32
VMEM_LIMIT = 56 * 1024 * 1024


def _lambda_init(layer_idx):
    return 0.8 - 0.6 * math.exp(-0.3 * layer_idx)


def _first_far_distance():
    n = MAX_EXACT
    while MAX_EXACT + int(math.log(n / MAX_EXACT) / math.log(MAX_DISTANCE / MAX_EXACT) * (N_BUCKETS - MAX_EXACT)) < N_BUCKETS - 1:
        n += 1
    return n


def _params(*sem):
    return pltpu.CompilerParams(dimension_semantics=sem, vmem_limit_bytes=VMEM_LIMIT)


def _rms(x, g):
    return x * lax.rsqrt(jnp.mean(x * x, axis=-1, keepdims=True) + RMS_EPS) * g


def _silu(x):
    return x * jax.nn.sigmoid(x)


def _const_spec(shape, index):
    return pl.BlockSpec(shape, lambda *_: index)


def _ffn_body(x_ref, n0_ref, n1_ref, wg_ref, wu_ref, wd_ref, o_ref, *, n_chunks):
    x = x_ref[...]
    h = _rms(x, n0_ref[...]).astype(BF16)
    fc = wg_ref.shape[1] // n_chunks
    acc = jnp.zeros(x.shape, F32)
    for c in range(n_chunks):
        sl = pl.ds(c * fc, fc)
        g = jnp.dot(h, wg_ref[:, sl], preferred_element_type=F32)
        u = jnp.dot(h, wu_ref[:, sl], preferred_element_type=F32)
        a = (_silu(g) * u).astype(BF16)
        acc = acc + jnp.dot(a, wd_ref[sl, :], preferred_element_type=F32)
    o_ref[...] = x + 0.5 * _rms(acc, n1_ref[...])


def _ffn(x, norms, wg, wu, wd, layer, f, *, tm, n_chunks=2):
    n, d = x.shape
    ff = wg.shape[-1]
    nidx = (layer * 2 + f) * 2
    return pl.pallas_call(
        functools.partial(_ffn_body, n_chunks=n_chunks),
        grid=(pl.cdiv(n, tm),),
        in_specs=[
            pl.BlockSpec((tm, d), lambda i: (i, 0)),
            _const_spec((None, 1, d), (nidx, 0, 0)),
            _const_spec((None, 1, d), (nidx + 1, 0, 0)),
            _const_spec((None, None, d, ff), (layer, f, 0, 0)),
            _const_spec((None, None, d, ff), (layer, f, 0, 0)),
            _const_spec((None, None, ff, d), (layer, f, 0, 0)),
        ],
        out_specs=pl.BlockSpec((tm, d), lambda i: (i, 0)),
        out_shape=jax.ShapeDtypeStruct((n, d), F32),
        compiler_params=_params("parallel"),
        name="ffn",
    )(x, norms, norms, wg, wu, wd)


def _pool_p_body(x_ref, g0_ref, g1_ref, w_ref, sc_ref, o_ref, st_ref, cat_ref, *, ts, n_t):
    t = pl.program_id(1)
    d = x_ref.shape[-1]
    gd = d // len(POOL_WINDOWS)
    x = x_ref[0]
    h = _rms(x, g0_ref[...])

    @pl.when(t == 0)
    def _():
        cat_ref[0:POOL_HALO, :] = jnp.zeros((POOL_HALO, d), F32)

    cat_ref[POOL_HALO:POOL_HALO + ts, :] = h
    pos = t * ts + lax.broadcasted_iota(jnp.int32, (ts, 1), 0)
    outs = []
    for g, w in enumerate(POOL_WINDOWS):
        c0 = g * gd
        acc = cat_ref[POOL_HALO:POOL_HALO + ts, c0:c0 + gd]
        for j in range(1, w):
            acc = acc + cat_ref[POOL_HALO - j:POOL_HALO - j + ts, c0:c0 + gd]
        inv_cnt = 1.0 / jnp.minimum(w, pos + 1).astype(F32)
        pooled = (acc * inv_cnt - h[:, c0:c0 + gd]).astype(BF16)
        outs.append(jnp.dot(pooled, w_ref[g], preferred_element_type=F32))
    m = jnp.concatenate(outs, axis=-1) * sc_ref[...]
    o_ref[0] = x + _rms(m, g1_ref[...])

    @pl.when(t == n_t - 1)
    def _():
        st_ref[0] = cat_ref[POOL_HALO + ts - POOL_BUF:POOL_HALO + ts, :]

    cat_ref[0:POOL_HALO, :] = cat_ref[ts:ts + POOL_HALO, :]


def _pool_prompt(x, mix_norm, pool_w, pool_scale, layer, j, *, ts):
    b, t, d = x.shape
    n_t = t // ts
    assert n_t * ts == t and ts % 8 == 0 and ts >= POOL_HALO
    g = len(POOL_WINDOWS)
    return pl.pallas_call(
        functools.partial(_pool_p_body, ts=ts, n_t=n_t),
        grid=(b, n_t),
        in_specs=[
            pl.BlockSpec((1, ts, d), lambda bi, ti: (bi, ti, 0)),
            _const_spec((None, 1, d), (layer * 2, 0, 0)),
            _const_spec((None, 1, d), (layer * 2 + 1, 0, 0)),
            _const_spec((None, g, d // g, d // g), (j, 0, 0, 0)),
            _const_spec((None, 1, d), (j, 0, 0)),
        ],
        out_specs=[
            pl.BlockSpec((1, ts, d), lambda bi, ti: (bi, ti, 0)),
            pl.BlockSpec((1, POOL_BUF, d), lambda bi, ti: (bi, 0, 0)),
        ],
        out_shape=[jax.ShapeDtypeStruct((b, t, d), F32), jax.ShapeDtypeStruct((b, POOL_BUF, d), F32)],
        scratch_shapes=[pltpu.VMEM((POOL_HALO + ts, d), F32)],
        compiler_params=_params("parallel", "arbitrary"),
        name="pool_prompt",
    )(x, mix_norm, mix_norm, pool_w, pool_scale)


def _pool_s_body(x_ref, st_ref, g0_ref, g1_ref, w_ref, sc_ref, o_ref, so_ref, *, n_s, past):
    d = g0_ref.shape[-1]
    gd = d // len(POOL_WINDOWS)
    bt = x_ref.shape[0]
    xs = [x_ref[:, s * d:(s + 1) * d] for s in range(n_s)]
    hs = [_rms(x, g0_ref[...]) for x in xs]

    def cat(idx, c0, width):
        if idx < POOL_BUF:
            return st_ref[:, idx * d + c0:idx * d + c0 + width]
        return hs[idx - POOL_BUF][:, c0:c0 + width]

    outs = []
    for g, w in enumerate(POOL_WINDOWS):
        c0 = g * gd
        rows = []
        for s in range(n_s):
            acc = cat(POOL_BUF + s, c0, gd)
            for jj in range(1, w):
                acc = acc + cat(POOL_BUF + s - jj, c0, gd)
            rows.append(acc * (1.0 / min(w, past + s + 1)) - hs[s][:, c0:c0 + gd])
        pooled = jnp.concatenate(rows, axis=0).astype(BF16)
        outs.append(jnp.dot(pooled, w_ref[g], preferred_element_type=F32))
    for s in range(n_s):
        m = jnp.concatenate([o[s * bt:(s + 1) * bt] for o in outs], axis=-1) * sc_ref[...]
        o_ref[:, s * d:(s + 1) * d] = xs[s] + _rms(m, g1_ref[...])
    for jj in range(POOL_BUF):
        so_ref[:, jj * d:(jj + 1) * d] = cat(n_s + jj, 0, d)


def _pool_sample(x, state, mix_norm, pool_w, pool_scale, layer, j, *, n_s, past, bt):
    db = x.shape[0]
    d = mix_norm.shape[-1]
    g = len(POOL_WINDOWS)
    return pl.pallas_call(
        functools.partial(_pool_s_body, n_s=n_s, past=past),
        grid=(db // bt,),
        in_specs=[
            pl.BlockSpec((bt, n_s * d), lambda i: (i, 0)),
            pl.BlockSpec((None, bt, POOL_BUF * d), lambda i: (j, i, 0)),
            _const_spec((None, 1, d), (layer * 2, 0, 0)),
            _const_spec((None, 1, d), (layer * 2 + 1, 0, 0)),
            _const_spec((None, g, d // g, d // g), (j, 0, 0, 0)),
            _const_spec((None, 1, d), (j, 0, 0)),
        ],
        out_specs=[
            pl.BlockSpec((bt, n_s * d), lambda i: (i, 0)),
            pl.BlockSpec((bt, POOL_BUF * d), lambda i: (i, 0)),
        ],
        out_shape=[jax.ShapeDtypeStruct((db, n_s * d), F32), jax.ShapeDtypeStruct((db, POOL_BUF * d), F32)],
        compiler_params=_params("parallel"),
        name="pool_sample",
    )(x, state, mix_norm, mix_norm, pool_w, pool_scale)


def _layer_norm(y, g, b):
    mu = jnp.mean(y, axis=-1, keepdims=True)
    yc = y - mu
    var = jnp.mean(yc * yc, axis=-1, keepdims=True)
    return yc * lax.rsqrt(var + LN_EPS) * g + b


def _conv_p_body(x_ref, g0_ref, g1_ref, w1_ref, b1_ref, wdw_ref, bdw_ref, lng_ref, lnb_ref, w2_ref, b2_ref,
                 o_ref, st_ref, cat_ref, *, ts, n_t):
    t = pl.program_id(1)
    d = x_ref.shape[-1]
    x = x_ref[0]
    h = _rms(x, g0_ref[...]).astype(BF16)
    z = jnp.dot(h, w1_ref[...], preferred_element_type=F32) + b1_ref[...]
    u = z[:, :d] * jax.nn.sigmoid(z[:, d:])

    @pl.when(t == 0)
    def _():
        cat_ref[0:CONV_HALO, :] = jnp.zeros((CONV_HALO, d), F32)

    cat_ref[CONV_HALO:CONV_HALO + ts, :] = u
    off = CONV_HALO - CONV_BUF
    y = cat_ref[off:off + ts, :] * wdw_ref[0:1, :] + bdw_ref[...]
    for k in range(1, CONV_WIDTH):
        y = y + cat_ref[off + k:off + k + ts, :] * wdw_ref[k:k + 1, :]
    a = _silu(_layer_norm(y, lng_ref[...], lnb_ref[...])).astype(BF16)
    m = jnp.dot(a, w2_ref[...], preferred_element_type=F32) + b2_ref[...]
    o_ref[0] = x + _rms(m, g1_ref[...])

    @pl.when(t == n_t - 1)
    def _():
        st_ref[0] = cat_ref[CONV_HALO + ts - CONV_BUF:CONV_HALO + ts, :]

    cat_ref[0:CONV_HALO, :] = cat_ref[ts:ts + CONV_HALO, :]


def _conv_specs(layer, j, d):
    return [
        _const_spec((None, 1, d), (layer * 2, 0, 0)),
        _const_spec((None, 1, d), (layer * 2 + 1, 0, 0)),
        _const_spec((None, d, 2 * d), (j, 0, 0)),
        _const_spec((None, 1, 2 * d), (j, 0, 0)),
        _const_spec((None, CONV_WIDTH, d), (j, 0, 0)),
        _const_spec((None, 1, d), (j, 0, 0)),
        _const_spec((None, 1, d), (j, 0, 0)),
        _const_spec((None, 1, d), (j, 0, 0)),
        _const_spec((None, d, d), (j, 0, 0)),
        _const_spec((None, 1, d), (j, 0, 0)),
    ]


def _conv_prompt(x, mix_norm, cw, layer, j, *, ts):
    b, t, d = x.shape
    n_t = t // ts
    assert n_t * ts == t and ts % 8 == 0 and ts >= CONV_HALO
    return pl.pallas_call(
        functools.partial(_conv_p_body, ts=ts, n_t=n_t),
        grid=(b, n_t),
        in_specs=[pl.BlockSpec((1, ts, d), lambda bi, ti: (bi, ti, 0))] + _conv_specs(layer, j, d),
        out_specs=[
            pl.BlockSpec((1, ts, d), lambda bi, ti: (bi, ti, 0)),
            pl.BlockSpec((1, CONV_BUF, d), lambda bi, ti: (bi, 0, 0)),
        ],
        out_shape=[jax.ShapeDtypeStruct((b, t, d), F32), jax.ShapeDtypeStruct((b, CONV_BUF, d), F32)],
        scratch_shapes=[pltpu.VMEM((CONV_HALO + ts, d), F32)],
        compiler_params=_params("parallel", "arbitrary"),
        name="conv_prompt",
    )(x, mix_norm, mix_norm, *cw)


def _conv_s_body(x_ref, st_ref, g0_ref, g1_ref, w1_ref, b1_ref, wdw_ref, bdw_ref, lng_ref, lnb_ref, w2_ref, b2_ref,
                 o_ref, so_ref, *, n_s):
    d = g0_ref.shape[-1]
    bt = x_ref.shape[0]
    xs = [x_ref[:, s * d:(s + 1) * d] for s in range(n_s)]
    h = jnp.concatenate([_rms(x, g0_ref[...]) for x in xs], axis=0).astype(BF16)
    z = jnp.dot(h, w1_ref[...], preferred_element_type=F32) + b1_ref[...]
    u = z[:, :d] * jax.nn.sigmoid(z[:, d:])
    us = [u[s * bt:(s + 1) * bt] for s in range(n_s)]

    def cat(idx):
        if idx < CONV_BUF:
            return st_ref[:, idx * d:(idx + 1) * d]
        return us[idx - CONV_BUF]

    acts = []
    for s in range(n_s):
        y = cat(s) * wdw_ref[0:1, :] + bdw_ref[...]
        for k in range(1, CONV_WIDTH):
            y = y + cat(s + k) * wdw_ref[k:k + 1, :]
        acts.append(_silu(_layer_norm(y, lng_ref[...], lnb_ref[...])))
    a = jnp.concatenate(acts, axis=0).astype(BF16)
    m = jnp.dot(a, w2_ref[...], preferred_element_type=F32) + b2_ref[...]
    for s in range(n_s):
        o_ref[:, s * d:(s + 1) * d] = xs[s] + _rms(m[s * bt:(s + 1) * bt], g1_ref[...])
    for jj in range(CONV_BUF):
        so_ref[:, jj * d:(jj + 1) * d] = cat(n_s + jj)


def _conv_sample(x, state, mix_norm, cw, layer, j, *, n_s, bt):
    db = x.shape[0]
    d = mix_norm.shape[-1]
    return pl.pallas_call(
        functools.partial(_conv_s_body, n_s=n_s),
        grid=(db // bt,),
        in_specs=[
            pl.BlockSpec((bt, n_s * d), lambda i: (i, 0)),
            pl.BlockSpec((None, bt, CONV_BUF * d), lambda i: (j, i, 0)),
        ] + _conv_specs(layer, j, d),
        out_specs=[
            pl.BlockSpec((bt, n_s * d), lambda i: (i, 0)),
            pl.BlockSpec((bt, CONV_BUF * d), lambda i: (i, 0)),
        ],
        out_shape=[jax.ShapeDtypeStruct((db, n_s * d), F32), jax.ShapeDtypeStruct((db, CONV_BUF * d), F32)],
        compiler_params=_params("parallel"),
        name="conv_sample",
    )(x, state, mix_norm, mix_norm, *cw)


def _qkv_body(x_ref, g0_ref, w_ref, q_ref, k_ref, v_ref, kb_ref, vb_ref):
    d = x_ref.shape[-1]
    h = _rms(x_ref[...], g0_ref[...]).astype(BF16)
    qkv = jnp.dot(h, w_ref[...], preferred_element_type=F32)
    q_ref[...] = (qkv[:, :d] * (HEAD_DIM ** -0.5)).astype(BF16)
    k = qkv[:, d:2 * d]
    v = qkv[:, 2 * d:]
    k_ref[...] = k
    v_ref[...] = v
    kb_ref[...] = k.astype(BF16)
    vb_ref[...] = v.astype(BF16)


def _qkv(x, mix_norm, wqkv, layer, j, *, tm):
    n, d = x.shape
    row = pl.BlockSpec((tm, d), lambda i: (i, 0))
    return pl.pallas_call(
        _qkv_body,
        grid=(pl.cdiv(n, tm),),
        in_specs=[row, _const_spec((None, 1, d), (layer * 2, 0, 0)), _const_spec((None, d, 3 * d), (j, 0, 0))],
        out_specs=[row] * 5,
        out_shape=[jax.ShapeDtypeStruct((n, d), BF16), jax.ShapeDtypeStruct((n, d), F32),
                   jax.ShapeDtypeStruct((n, d), F32), jax.ShapeDtypeStruct((n, d), BF16),
                   jax.ShapeDtypeStruct((n, d), BF16)],
        compiler_params=_params("parallel"),
        name="qkv_proj",
    )(x, mix_norm, wqkv)


def _bucket(n):
    nf = jnp.maximum(n, 1).astype(F32)
    large = MAX_EXACT + (jnp.log(nf / MAX_EXACT) / math.log(MAX_DISTANCE / MAX_EXACT)
                         * (N_BUCKETS - MAX_EXACT)).astype(jnp.int32)
    large = jnp.minimum(large, N_BUCKETS - 1)
    return jnp.where(n < MAX_EXACT, n, large)


def _lookup(bucket, entry):
    out = jnp.zeros(bucket.shape, F32)
    for b in range(N_BUCKETS):
        out = jnp.where(bucket == b, entry(b), out)
    return out


def _bias_p_body(table_ref, bp_ref, *, blk):
    head = pl.program_id(0)
    a = lax.broadcasted_iota(jnp.int32, (blk, blk), 0)
    b = lax.broadcasted_iota(jnp.int32, (blk, blk), 1)
    for sel in range(3):
        n = sel * blk + a - b
        vals = _lookup(_bucket(jnp.maximum(n, 0)), lambda bb: table_ref[bb * N_HEADS + head])
        bp_ref[0, sel] = jnp.where(n >= 0, vals, NEG_INF)


def _bias_s_body(tt_ref, bs_ref, *, n_s):
    rows = 2 * n_s * N_HEADS
    cols = PAGE_SIZE * N_HEADS
    tt = tt_ref[...]
    trow = jnp.broadcast_to(tt[None], (rows // N_HEADS, N_HEADS, N_BUCKETS)).reshape(rows, N_BUCKETS)

    def tile(width, dist):
        r = lax.broadcasted_iota(jnp.int32, (rows, width), 0)
        c = lax.broadcasted_iota(jnp.int32, (rows, width), 1)
        n = dist((r // N_HEADS) % n_s, c // N_HEADS)
        vals = _lookup(_bucket(jnp.maximum(n, 0)), lambda bb: trow[:, bb:bb + 1])
        return jnp.where((r % N_HEADS == c % N_HEADS) & (n >= 0) & (c < cols), vals, NEG_INF)

    bs_ref[:, 0:cols] = tile(cols, lambda qi, kk: 2 * PAGE_SIZE + qi - kk)
    bs_ref[:, cols:2 * cols] = tile(cols, lambda qi, kk: PAGE_SIZE + qi - kk)
    bs_ref[:, 2 * cols:2 * cols + PAGE_SIZE] = tile(
        PAGE_SIZE, lambda qi, kk: jnp.where(kk < n_s, qi - kk, -1))


def _bias_tiles(table, *, blk, n_s):
    far = _first_far_distance()
    assert blk + 1 >= far and PAGE_SIZE + 1 >= far and n_s * N_HEADS <= PAGE_SIZE
    bias_p = pl.pallas_call(
        functools.partial(_bias_p_body, blk=blk),
        grid=(N_HEADS,),
        in_specs=[pl.BlockSpec(memory_space=pltpu.SMEM)],
        out_specs=pl.BlockSpec((1, 3, blk, blk), lambda h: (h, 0, 0, 0)),
        out_shape=jax.ShapeDtypeStruct((N_HEADS, 3, blk, blk), F32),
        compiler_params=_params("parallel"),
        name="rel_bias_prompt",
    )(table.reshape(-1))
    width = 2 * PAGE_SIZE * N_HEADS + PAGE_SIZE
    bias_s = pl.pallas_call(
        functools.partial(_bias_s_body, n_s=n_s),
        out_shape=jax.ShapeDtypeStruct((2 * n_s * N_HEADS, width), F32),
        compiler_params=pltpu.CompilerParams(vmem_limit_bytes=VMEM_LIMIT),
        name="rel_bias_sample",
    )(table.T)
    return bias_p, bias_s


def _diff_lambda(lam_ref, lam_init):
    lp = lam_ref[...]
    s1 = jnp.sum(lp[0:1] * lp[1:2], axis=-1, keepdims=True)
    s2 = jnp.sum(lp[2:3] * lp[3:4], axis=-1, keepdims=True)
    return jnp.exp(s1) - jnp.exp(s2) + lam_init


def _softmax_step(s, m, l):
    m_new = jnp.maximum(m, jnp.max(s, axis=-1, keepdims=True))
    alpha = jnp.exp(m - m_new)
    p = jnp.exp(s - m_new)
    return p, m_new, alpha, alpha * l + jnp.sum(p, axis=-1, keepdims=True)


_NT = (((1,), (1,)), ((), ()))


def _attn_p_body(q_ref, k_ref, v_ref, bias_ref, lam_ref, sub_ref, o_ref, acc_ref, *, blk, n_full, tail, lam_init):
    i = pl.program_id(1)
    lam = _diff_lambda(lam_ref, lam_init)

    def one_head(hd, tq, n_far_blocks, tail_keys):
        c0 = hd * V_DIM
        q1 = q_ref[0, 0:tq, c0:c0 + HEAD_DIM]
        q2 = q_ref[0, 0:tq, c0 + HEAD_DIM:c0 + V_DIM]
        acc_ref[...] = jnp.zeros(acc_ref.shape, F32)

        def block(carry, k_rows, v_rows, bias):
            m1, l1, m2, l2 = carry
            s1 = lax.dot_general(q1, k_rows[:, 0:HEAD_DIM], _NT, preferred_element_type=F32) + bias
            s2 = lax.dot_general(q2, k_rows[:, HEAD_DIM:V_DIM], _NT, preferred_element_type=F32) + bias
            p1, m1, a1, l1 = _softmax_step(s1, m1, l1)
            p2, m2, a2, l2 = _softmax_step(s2, m2, l2)
            acc_ref[0, 0:tq] = a1 * acc_ref[0, 0:tq] + jnp.dot(p1.astype(BF16), v_rows, preferred_element_type=F32)
            acc_ref[1, 0:tq] = a2 * acc_ref[1, 0:tq] + jnp.dot(p2.astype(BF16), v_rows, preferred_element_type=F32)
            return m1, l1, m2, l2

        def body(j, carry):
            r0 = pl.multiple_of(j * blk, blk)
            sel = jnp.minimum(n_far_blocks - 1 - j + (1 if tail_keys else 0), 2)
            return block(carry, k_ref[0, pl.ds(r0, blk), c0:c0 + V_DIM], v_ref[0, pl.ds(r0, blk), c0:c0 + V_DIM],
                         bias_ref[hd, sel, 0:tq, :])

        init = (jnp.full((tq, 1), NEG_INF, F32), jnp.zeros((tq, 1), F32)) * 2
        carry = lax.fori_loop(0, n_far_blocks, body, init)
        if tail_keys:
            r0 = n_full * blk
            carry = block(carry, k_ref[0, r0:r0 + tail_keys, c0:c0 + V_DIM],
                          v_ref[0, r0:r0 + tail_keys, c0:c0 + V_DIM], bias_ref[hd, 0, 0:tq, 0:tail_keys])
        _, l1, _, l2 = carry
        o = acc_ref[0, 0:tq] * (1.0 / l1) - lam * (acc_ref[1, 0:tq] * (1.0 / l2))
        o = _rms(o, sub_ref[...]) * (1.0 - lam_init)
        o_ref[0, 0:tq, c0:c0 + V_DIM] = o.astype(BF16)

    @pl.when(i < n_full)
    def _():
        for hd in range(N_HEADS):
            one_head(hd, blk, i + 1, 0)

    if tail:
        @pl.when(i == n_full)
        def _():
            for hd in range(N_HEADS):
                one_head(hd, tail, n_full, tail)


def _attn_prompt(q, kb, vb, bias_p, lam_p, subln, j, *, blk, lam_init):
    b, t, d = q.shape
    n_full, tail = t // blk, t % blk
    assert tail % 16 == 0
    return pl.pallas_call(
        functools.partial(_attn_p_body, blk=blk, n_full=n_full, tail=tail, lam_init=lam_init),
        grid=(b, n_full + (1 if tail else 0)),
        in_specs=[
            pl.BlockSpec((1, blk, d), lambda bi, qi: (bi, qi, 0)),
            pl.BlockSpec((1, t, d), lambda bi, qi: (bi, 0, 0)),
            pl.BlockSpec((1, t, d), lambda bi, qi: (bi, 0, 0)),
            _const_spec((N_HEADS, 3, blk, blk), (0, 0, 0, 0)),
            _const_spec((None, 4, HEAD_DIM), (j, 0, 0)),
            _const_spec((None, 1, V_DIM), (j, 0, 0)),
        ],
        out_specs=pl.BlockSpec((1, blk, d), lambda bi, qi: (bi, qi, 0)),
        out_shape=jax.ShapeDtypeStruct((b, t, d), BF16),
        scratch_shapes=[pltpu.VMEM((2, blk, V_DIM), F32)],
        compiler_params=_params("parallel", "arbitrary"),
        name="attn_prompt",
    )(q, kb, vb, bias_p, lam_p, subln)


def _attn_s_body(pt_ref, q_ref, kn_ref, vn_ref, bias_ref, lam_ref, sub_ref, *rest, n_pages, n_steps, n_s, lam_init):
    k_pages = rest[:n_pages]
    v_pages = rest[n_pages:2 * n_pages]
    o_ref, q2_ref, m_ref, l_ref, acc_ref = rest[2 * n_pages:]
    step = pl.program_id(1)
    half = n_s * N_HEADS
    cols = PAGE_SIZE * N_HEADS

    @pl.when(step == 0)
    def _():
        q = q_ref[...].reshape(half, V_DIM)
        lane = lax.broadcasted_iota(jnp.int32, (half, V_DIM), 1)
        q2_ref[0:half, :] = jnp.where(lane < HEAD_DIM, q, 0.0).astype(BF16)
        q2_ref[half:2 * half, :] = jnp.where(lane >= HEAD_DIM, q, 0.0).astype(BF16)
        m_ref[...] = jnp.full(m_ref.shape, NEG_INF, F32)
        l_ref[...] = jnp.zeros(l_ref.shape, F32)
        acc_ref[...] = jnp.zeros(acc_ref.shape, F32)

    q2 = q2_ref[...]
    last = step == n_steps - 1

    def update(keys, values, bias):
        s = lax.dot_general(q2, keys, _NT, preferred_element_type=F32) + bias
        p, m_new, alpha, l_new = _softmax_step(s, m_ref[...], l_ref[...])
        m_ref[...] = m_new
        l_ref[...] = l_new
        acc_ref[...] = alpha * acc_ref[...] + jnp.dot(p.astype(BF16), values, preferred_element_type=F32)

    for r_ in range(n_pages):
        bias = bias_ref[:, 0:cols]
        if r_ == n_pages - 1:
            bias = jnp.where(last, bias_ref[:, cols:2 * cols], bias)
        update(k_pages[r_][...].reshape(cols, V_DIM).astype(BF16),
               v_pages[r_][...].reshape(cols, V_DIM).astype(BF16), bias)

    @pl.when(last)
    def _():
        pad = jnp.zeros((PAGE_SIZE - half, V_DIM), F32)
        kn = jnp.concatenate([kn_ref[...].reshape(half, V_DIM), pad], axis=0).astype(BF16)
        vn = jnp.concatenate([vn_ref[...].reshape(half, V_DIM), pad], axis=0).astype(BF16)
        update(kn, vn, bias_ref[:, 2 * cols:2 * cols + PAGE_SIZE])
        lam = _diff_lambda(lam_ref, lam_init)
        o_all = acc_ref[...] * (1.0 / l_ref[...])
        o = o_all[0:half] - lam * o_all[half:2 * half]
        o_ref[...] = (_rms(o, sub_ref[...]) * (1.0 - lam_init)).reshape(o_ref.shape)


def _attn_sample(q, kn, vn, cache_k, cache_v, page_table, bias_s, lam_p, subln, j, *, n_pages, lam_init):
    db, n_s = q.shape[:2]
    total_pages = page_table.shape[1]
    n_steps = total_pages // n_pages
    assert n_steps * n_pages == total_pages
    rows = 2 * n_s * N_HEADS
    new_kv = pl.BlockSpec((1, n_s, 1, N_HEADS, V_DIM), lambda b, s, pt: (b, 0, 0, 0, 0))
    q_spec = pl.BlockSpec((1, n_s, N_HEADS, V_DIM), lambda b, s, pt: (b, 0, 0, 0))

    def page_spec(r):
        return pl.BlockSpec((1, PAGE_SIZE, 1, N_HEADS, V_DIM),
                            lambda b, s, pt: (pt[b, s * n_pages + r], 0, j, 0, 0))

    grid_spec = pltpu.PrefetchScalarGridSpec(
        num_scalar_prefetch=1,
        grid=(db, n_steps),
        in_specs=[
            q_spec, new_kv, new_kv,
            pl.BlockSpec(bias_s.shape, lambda b, s, pt: (0, 0)),
            pl.BlockSpec((None, 4, HEAD_DIM), lambda b, s, pt: (j, 0, 0)),
            pl.BlockSpec((None, 1, V_DIM), lambda b, s, pt: (j, 0, 0)),
        ] + [page_spec(r) for r in range(n_pages)] * 2,
        out_specs=q_spec,
        scratch_shapes=[pltpu.VMEM((rows, V_DIM), BF16), pltpu.VMEM((rows, 1), F32), pltpu.VMEM((rows, 1), F32),
                        pltpu.VMEM((rows, V_DIM), F32)],
    )
    return pl.pallas_call(
        functools.partial(_attn_s_body, n_pages=n_pages, n_steps=n_steps, n_s=n_s, lam_init=lam_init),
        grid_spec=grid_spec,
        out_shape=jax.ShapeDtypeStruct((db, n_s, N_HEADS, V_DIM), F32),
        compiler_params=_params("parallel", "arbitrary"),
        name="attn_sample",
    )(page_table, q, kn, vn, bias_s, lam_p, subln, *([cache_k] * n_pages), *([cache_v] * n_pages))


def _outproj_body(a_ref, x_ref, w_ref, g1_ref, o_ref):
    m = jnp.dot(a_ref[...].astype(BF16), w_ref[...], preferred_element_type=F32)
    o_ref[...] = x_ref[...] + _rms(m, g1_ref[...])


def _outproj(a, x, wo, mix_norm, layer, j, *, tm):
    n, d = x.shape
    row = pl.BlockSpec((tm, d), lambda i: (i, 0))
    return pl.pallas_call(
        _outproj_body,
        grid=(pl.cdiv(n, tm),),
        in_specs=[row, row, _const_spec((None, d, d), (j, 0, 0)), _const_spec((None, 1, d), (layer * 2 + 1, 0, 0))],
        out_specs=row,
        out_shape=jax.ShapeDtypeStruct((n, d), F32),
        compiler_params=_params("parallel"),
        name="attn_outproj",
    )(a, x, wo, mix_norm)


def _final_p_body(x_ref, g_ref, o_ref, *, skip, rows):
    t = pl.program_id(1)
    r0 = pl.multiple_of(skip + t * rows, 8)
    o_ref[0] = _rms(x_ref[0, pl.ds(r0, rows), :], g_ref[...])


def _final_prompt(x, g, *, skip, rows):
    b, t, d = x.shape
    n_t = (t - skip) // rows
    return pl.pallas_call(
        functools.partial(_final_p_body, skip=skip, rows=rows),
        grid=(b, n_t),
        in_specs=[pl.BlockSpec((1, t, d), lambda bi, ti: (bi, 0, 0)), _const_spec((1, d), (0, 0))],
        out_specs=pl.BlockSpec((1, rows, d), lambda bi, ti: (bi, ti, 0)),
        out_shape=jax.ShapeDtypeStruct((b, t - skip, d), F32),
        compiler_params=_params("parallel", "arbitrary"),
        name="final_norm_prompt",
    )(x, g)


def _final_s_body(x_ref, g_ref, o_ref):
    o_ref[...] = _rms(x_ref[...], g_ref[...])


def _final_sample(x, g, *, tm):
    n, d = x.shape
    row = pl.BlockSpec((tm, d), lambda i: (i, 0))
    return pl.pallas_call(
        _final_s_body,
        grid=(pl.cdiv(n, tm),),
        in_specs=[row, _const_spec((1, d), (0, 0))],
        out_specs=row,
        out_shape=jax.ShapeDtypeStruct((n, d), F32),
        compiler_params=_params("parallel"),
        name="final_norm_sample",
    )(x, g)


def _pick_tile(n, cap, align=8):
    best = None
    for c in range(align, min(n, cap) + 1, align):
        if n % c == 0:
            best = c
    assert best is not None, (n, cap, align)
    return best


def kernel(x_prompt, x_sample, state_pool, state_conv, cache_k, cache_v, page_table, meta_tokens, rel_bias_table, ffn_norm, ffn_wg, ffn_wu, ffn_wd, mix_norm, pool_w, pool_scale, attn_wqkv, attn_wo, attn_lambda, attn_subln, conv_w1, conv_b1, conv_wdw, conv_bdw, conv_ln_g, conv_ln_b, conv_w2, conv_b2, final_norm):
    b, seq, d = x_prompt.shape
    db, n_s, _ = x_sample.shape
    depth = ffn_wg.shape[0]
    t = seq + N_META
    past = page_table.shape[1] * PAGE_SIZE

    wg, wu, wd = ffn_wg.astype(BF16), ffn_wu.astype(BF16), ffn_wd.astype(BF16)
    ffn_norm3 = ffn_norm.reshape(-1, 1, d)
    mix_norm3 = mix_norm.reshape(-1, 1, d)
    pool_w_b = pool_w.astype(BF16)
    pool_scale3 = pool_scale.reshape(-1, 1, d)
    wqkv_b, wo_b = attn_wqkv.astype(BF16), attn_wo.astype(BF16)
    subln3 = attn_subln.reshape(-1, 1, V_DIM)
    cw = (conv_w1.astype(BF16), conv_b1.reshape(-1, 1, 2 * d), conv_wdw, conv_bdw.reshape(-1, 1, d),
          conv_ln_g.reshape(-1, 1, d), conv_ln_b.reshape(-1, 1, d), conv_w2.astype(BF16), conv_b2.reshape(-1, 1, d))
    state_pool2 = state_pool.reshape(state_pool.shape[0], db, POOL_BUF * d)
    state_conv2 = state_conv.reshape(state_conv.shape[0], db, CONV_BUF * d)

    meta = jnp.broadcast_to(meta_tokens[None].astype(x_prompt.dtype), (b, N_META, d))
    xp = jnp.concatenate([meta, x_prompt], axis=1).reshape(b * t, d)
    xs = x_sample.reshape(db * n_s, d)

    tm_p = min(384, b * t)
    tm_s = min(512, db * n_s)
    ts_pool = _pick_tile(t, 1032)
    ts_conv = _pick_tile(t, 344)
    bt = _pick_tile(db, 32)

    bias_p = bias_s = None
    pool_p, pool_s, conv_p, conv_s = [], [], [], []
    k_p = v_p = k_s = v_s = None
    for i in range(depth):
        kind, j = i % N_MIXERS, i // N_MIXERS
        xp = _ffn(xp, ffn_norm3, wg, wu, wd, i, 0, tm=tm_p)
        xs = _ffn(xs, ffn_norm3, wg, wu, wd, i, 0, tm=tm_s)
        if kind == 0:
            xp3, st = _pool_prompt(xp.reshape(b, t, d), mix_norm3, pool_w_b, pool_scale3, i, j, ts=ts_pool)
            xp = xp3.reshape(b * t, d)
            pool_p.append(st)
            xs2, st = _pool_sample(xs.reshape(db, n_s * d), state_pool2, mix_norm3, pool_w_b, pool_scale3, i, j,
                                   n_s=n_s, past=past, bt=bt)
            xs = xs2.reshape(db * n_s, d)
            pool_s.append(st.reshape(db, POOL_BUF, d))
        elif kind == 1:
            assert j == 0
            lam_init = _lambda_init(i)
            if bias_p is None:
                bias_p, bias_s = _bias_tiles(rel_bias_table, blk=ATTN_BLOCK, n_s=n_s)
            q, k_p, v_p, kb, vb = _qkv(xp, mix_norm3, wqkv_b, i, j, tm=tm_p)
            a = _attn_prompt(q.reshape(b, t, d), kb.reshape(b, t, d), vb.reshape(b, t, d), bias_p, attn_lambda,
                             subln3, j, blk=ATTN_BLOCK, lam_init=lam_init)
            xp = _outproj(a.reshape(b * t, d), xp, wo_b, mix_norm3, i, j, tm=tm_p)
            q, k_s, v_s, _, _ = _qkv(xs, mix_norm3, wqkv_b, i, j, tm=tm_s)
            k_s = k_s.reshape(db, n_s, 1, N_HEADS, V_DIM)
            v_s = v_s.reshape(db, n_s, 1, N_HEADS, V_DIM)
            a = _attn_sample(q.astype(F32).reshape(db, n_s, N_HEADS, V_DIM), k_s, v_s, cache_k, cache_v,
                             page_table, bias_s, attn_lambda, subln3, j, n_pages=8, lam_init=lam_init)
            xs = _outproj(a.reshape(db * n_s, d), xs, wo_b, mix_norm3, i, j, tm=tm_s)
        else:
            xp3, st = _conv_prompt(xp.reshape(b, t, d), mix_norm3, cw, i, j, ts=ts_conv)
            xp = xp3.reshape(b * t, d)
            conv_p.append(st)
            xs2, st = _conv_sample(xs.reshape(db, n_s * d), state_conv2, mix_norm3, cw, i, j, n_s=n_s, bt=bt)
            xs = xs2.reshape(db * n_s, d)
            conv_s.append(st.reshape(db, CONV_BUF, d))
        xp = _ffn(xp, ffn_norm3, wg, wu, wd, i, 1, tm=tm_p)
        xs = _ffn(xs, ffn_norm3, wg, wu, wd, i, 1, tm=tm_s)

    g = final_norm.reshape(1, d)
    y_prompt = _final_prompt(xp.reshape(b, t, d), g, skip=N_META, rows=_pick_tile(seq, 1024))
    y_sample = _final_sample(xs, g, tm=tm_s).reshape(db, n_s, d)
    kv_shape_p = (b, t, 1, N_HEADS, V_DIM)
    return (y_prompt, y_sample, jnp.stack(pool_p, axis=0), jnp.stack(pool_s, axis=0),
            jnp.stack(conv_p, axis=0), jnp.stack(conv_s, axis=0),
            k_p.reshape(kv_shape_p), v_p.reshape(kv_shape_p), k_s, v_s)
```

```python
import functools
import math

import jax
import jax.numpy as jnp
from jax import lax
from jax.experimental import pallas as pl
from jax.experimental.pallas import tpu as pltpu

F32 = jnp.float32
BF16 = jnp.bfloat16

N_MIXERS = 3
N_META = 16
N_HEADS = 8
HEAD_DIM = 64
V_DIM = 2 * HEAD_DIM
N_BUCKETS = 32
MAX_EXACT = N_BUCKETS // 2
MAX_DISTANCE = 128
POOL_WINDOWS = (2, 4, 8, 16)
POOL_BUF = max(POOL_WINDOWS) - 1
CONV_WIDTH = 31
CONV_BUF = CONV_WIDTH - 1
PAGE_SIZE = 128
RMS_EPS = 1e-6
LN_EPS = 1e-5
NEG_INF = -1e30
SUBLANES = 8
LANES = 128
ATTN_BLOCK = 256
POOL_HALO = 16
CONV_HALO = 32
VMEM_LIMIT = 56 * 1024 * 1024


def _lambda_init(layer_idx):
    return 0.8 - 0.6 * math.exp(-0.3 * layer_idx)


def _first_far_distance():
    n = MAX_EXACT
    while MAX_EXACT + int(math.log(n / MAX_EXACT) / math.log(MAX_DISTANCE / MAX_EXACT) * (N_BUCKETS - MAX_EXACT)) < N_BUCKETS - 1:
        n += 1
    return n


def _params(*sem):
    return pltpu.CompilerParams(dimension_semantics=sem, vmem_limit_bytes=VMEM_LIMIT)


def _rms(x, g):
    return x * lax.rsqrt(jnp.mean(x * x, axis=-1, keepdims=True) + RMS_EPS) * g


def _silu(x):
    return x * jax.nn.sigmoid(x)


def _const_spec(shape, index):
    return pl.BlockSpec(shape, lambda *_: index)


def _ffn_body(x_ref, n0_ref, n1_ref, wg_ref, wu_ref, wd_ref, o_ref, *, n_chunks):
    x = x_ref[...]
    h = _rms(x, n0_ref[...]).astype(BF16)
    fc = wg_ref.shape[1] // n_chunks
    acc = jnp.zeros(x.shape, F32)
    for c in range(n_chunks):
        sl = pl.ds(c * fc, fc)
        g = jnp.dot(h, wg_ref[:, sl], preferred_element_type=F32)
        u = jnp.dot(h, wu_ref[:, sl], preferred_element_type=F32)
        a = (_silu(g) * u).astype(BF16)
        acc = acc + jnp.dot(a, wd_ref[sl, :], preferred_element_type=F32)
    o_ref[...] = x + 0.5 * _rms(acc, n1_ref[...])


def _ffn(x, norms, wg, wu, wd, layer, f, *, tm, n_chunks=2):
    n, d = x.shape
    ff = wg.shape[-1]
    nidx = (layer * 2 + f) * 2
    return pl.pallas_call(
        functools.partial(_ffn_body, n_chunks=n_chunks),
        grid=(pl.cdiv(n, tm),),
        in_specs=[
            pl.BlockSpec((tm, d), lambda i: (i, 0)),
            _const_spec((None, 1, d), (nidx, 0, 0)),
            _const_spec((None, 1, d), (nidx + 1, 0, 0)),
            _const_spec((None, None, d, ff), (layer, f, 0, 0)),
            _const_spec((None, None, d, ff), (layer, f, 0, 0)),
            _const_spec((None, None, ff, d), (layer, f, 0, 0)),
        ],
        out_specs=pl.BlockSpec((tm, d), lambda i: (i, 0)),
        out_shape=jax.ShapeDtypeStruct((n, d), F32),
        compiler_params=_params("parallel"),
        name="ffn",
    )(x, norms, norms, wg, wu, wd)


def _pool_p_body(x_ref, g0_ref, g1_ref, w_ref, sc_ref, o_ref, st_ref, cat_ref, *, ts, n_t):
    t = pl.program_id(1)
    d = x_ref.shape[-1]
    gd = d // len(POOL_WINDOWS)
    x = x_ref[0]
    h = _rms(x, g0_ref[...])

    @pl.when(t == 0)
    def _():
        cat_ref[0:POOL_HALO, :] = jnp.zeros((POOL_HALO, d), F32)

    cat_ref[POOL_HALO:POOL_HALO + ts, :] = h
    pos = t * ts + lax.broadcasted_iota(jnp.int32, (ts, 1), 0)
    outs = []
    for g, w in enumerate(POOL_WINDOWS):
        c0 = g * gd
        acc = cat_ref[POOL_HALO:POOL_HALO + ts, c0:c0 + gd]
        for j in range(1, w):
            acc = acc + cat_ref[POOL_HALO - j:POOL_HALO - j + ts, c0:c0 + gd]
        inv_cnt = 1.0 / jnp.minimum(w, pos + 1).astype(F32)
        pooled = (acc * inv_cnt - h[:, c0:c0 + gd]).astype(BF16)
        outs.append(jnp.dot(pooled, w_ref[g], preferred_element_type=F32))
    m = jnp.concatenate(outs, axis=-1) * sc_ref[...]
    o_ref[0] = x + _rms(m, g1_ref[...])

    @pl.when(t == n_t - 1)
    def _():
        st_ref[0] = cat_ref[POOL_HALO + ts - POOL_BUF:POOL_HALO + ts, :]

    cat_ref[0:POOL_HALO, :] = cat_ref[ts:ts + POOL_HALO, :]


def _pool_prompt(x, mix_norm, pool_w, pool_scale, layer, j, *, ts):
    b, t, d = x.shape
    n_t = t // ts
    assert n_t * ts == t and ts % 8 == 0 and ts >= POOL_HALO
    g = len(POOL_WINDOWS)
    return pl.pallas_call(
        functools.partial(_pool_p_body, ts=ts, n_t=n_t),
        grid=(b, n_t),
        in_specs=[
            pl.BlockSpec((1, ts, d), lambda bi, ti: (bi, ti, 0)),
            _const_spec((None, 1, d), (layer * 2, 0, 0)),
            _const_spec((None, 1, d), (layer * 2 + 1, 0, 0)),
            _const_spec((None, g, d // g, d // g), (j, 0, 0, 0)),
            _const_spec((None, 1, d), (j, 0, 0)),
        ],
        out_specs=[
            pl.BlockSpec((1, ts, d), lambda bi, ti: (bi, ti, 0)),
            pl.BlockSpec((1, POOL_BUF, d), lambda bi, ti: (bi, 0, 0)),
        ],
        out_shape=[jax.ShapeDtypeStruct((b, t, d), F32), jax.ShapeDtypeStruct((b, POOL_BUF, d), F32)],
        scratch_shapes=[pltpu.VMEM((POOL_HALO + ts, d), F32)],
        compiler_params=_params("parallel", "arbitrary"),
        name="pool_prompt",
    )(x, mix_norm, mix_norm, pool_w, pool_scale)


def _pool_s_body(x_ref, st_ref, g0_ref, g1_ref, w_ref, sc_ref, o_ref, so_ref, *, n_s, past):
    d = g0_ref.shape[-1]
    gd = d // len(POOL_WINDOWS)
    bt = x_ref.shape[0]
    xs = [x_ref[:, s * d:(s + 1) * d] for s in range(n_s)]
    hs = [_rms(x, g0_ref[...]) for x in xs]

    def cat(idx, c0, width):
        if idx < POOL_BUF:
            return st_ref[:, idx * d + c0:idx * d + c0 + width]
        return hs[idx - POOL_BUF][:, c0:c0 + width]

    outs = []
    for g, w in enumerate(POOL_WINDOWS):
        c0 = g * gd
        rows = []
        for s in range(n_s):
            acc = cat(POOL_BUF + s, c0, gd)
            for jj in range(1, w):
                acc = acc + cat(POOL_BUF + s - jj, c0, gd)
            rows.append(acc * (1.0 / min(w, past + s + 1)) - hs[s][:, c0:c0 + gd])
        pooled = jnp.concatenate(rows, axis=0).astype(BF16)
        outs.append(jnp.dot(pooled, w_ref[g], preferred_element_type=F32))
    for s in range(n_s):
        m = jnp.concatenate([o[s * bt:(s + 1) * bt] for o in outs], axis=-1) * sc_ref[...]
        o_ref[:, s * d:(s + 1) * d] = xs[s] + _rms(m, g1_ref[...])
    for jj in range(POOL_BUF):
        so_ref[:, jj * d:(jj + 1) * d] = cat(n_s + jj, 0, d)


def _pool_sample(x, state, mix_norm, pool_w, pool_scale, layer, j, *, n_s, past, bt):
    db = x.shape[0]
    d = mix_norm.shape[-1]
    g = len(POOL_WINDOWS)
    return pl.pallas_call(
        functools.partial(_pool_s_body, n_s=n_s, past=past),
        grid=(db // bt,),
        in_specs=[
            pl.BlockSpec((bt, n_s * d), lambda i: (i, 0)),
            pl.BlockSpec((None, bt, POOL_BUF * d), lambda i: (j, i, 0)),
            _const_spec((None, 1, d), (layer * 2, 0, 0)),
            _const_spec((None, 1, d), (layer * 2 + 1, 0, 0)),
            _const_spec((None, g, d // g, d // g), (j, 0, 0, 0)),
            _const_spec((None, 1, d), (j, 0, 0)),
        ],
        out_specs=[
            pl.BlockSpec((bt, n_s * d), lambda i: (i, 0)),
            pl.BlockSpec((bt, POOL_BUF * d), lambda i: (i, 0)),
        ],
        out_shape=[jax.ShapeDtypeStruct((db, n_s * d), F32), jax.ShapeDtypeStruct((db, POOL_BUF * d), F32)],
        compiler_params=_params("parallel"),
        name="pool_sample",
    )(x, state, mix_norm, mix_norm, pool_w, pool_scale)


def _layer_norm(y, g, b):
    mu = jnp.mean(y, axis=-1, keepdims=True)
    yc = y - mu
    var = jnp.mean(yc * yc, axis=-1, keepdims=True)
    return yc * lax.rsqrt(var + LN_EPS) * g + b


def _conv_p_body(x_ref, g0_ref, g1_ref, w1_ref, b1_ref, wdw_ref, bdw_ref, lng_ref, lnb_ref, w2_ref, b2_ref,
                 o_ref, st_ref, cat_ref, y_ref, *, ts, n_t):
    t = pl.program_id(1)
    d = x_ref.shape[-1]
    x = x_ref[0]
    h = _rms(x, g0_ref[...]).astype(BF16)
    glu = jnp.dot(h, w1_ref[...], preferred_element_type=F32) + b1_ref[...]
    u = glu[:, :d] * jax.nn.sigmoid(glu[:, d:])

    @pl.when(t == 0)
    def _():
        cat_ref[0:CONV_HALO, :] = jnp.zeros((CONV_HALO, d), F32)
        cat_ref[CONV_HALO + ts:CONV_HALO + ts + SUBLANES, :] = jnp.zeros((SUBLANES, d), F32)

    cat_ref[CONV_HALO:CONV_HALO + ts, :] = u
    off = CONV_HALO - CONV_BUF
    zrows = ts + SUBLANES
    for c0 in range(0, d, LANES):
        y = None
        for r in range(SUBLANES):
            z = None
            for a in range((CONV_WIDTH + off) // SUBLANES + 1):
                k = SUBLANES * a + r - off
                if 0 <= k < CONV_WIDTH:
                    term = cat_ref[SUBLANES * a:SUBLANES * a + zrows, c0:c0 + LANES] * wdw_ref[k:k + 1, c0:c0 + LANES]
                    z = term if z is None else z + term
            zs = z[r:r + ts]
            y = zs if y is None else y + zs
        y_ref[:, c0:c0 + LANES] = y + bdw_ref[:, c0:c0 + LANES]
    a = _silu(_layer_norm(y_ref[...], lng_ref[...], lnb_ref[...])).astype(BF16)
    m = jnp.dot(a, w2_ref[...], preferred_element_type=F32) + b2_ref[...]
    o_ref[0] = x + _rms(m, g1_ref[...])

    @pl.when(t == n_t - 1)
    def _():
        st_ref[0] = cat_ref[CONV_HALO + ts - CONV_BUF:CONV_HALO + ts, :]

    cat_ref[0:CONV_HALO, :] = cat_ref[ts:ts + CONV_HALO, :]


def _conv_specs(layer, j, d):
    return [
        _const_spec((None, 1, d), (layer * 2, 0, 0)),
        _const_spec((None, 1, d), (layer * 2 + 1, 0, 0)),
        _const_spec((None, d, 2 * d), (j, 0, 0)),
        _const_spec((None, 1, 2 * d), (j, 0, 0)),
        _const_spec((None, CONV_WIDTH, d), (j, 0, 0)),
        _const_spec((None, 1, d), (j, 0, 0)),
        _const_spec((None, 1, d), (j, 0, 0)),
        _const_spec((None, 1, d), (j, 0, 0)),
        _const_spec((None, d, d), (j, 0, 0)),
        _const_spec((None, 1, d), (j, 0, 0)),
    ]


def _conv_prompt(x, mix_norm, cw, layer, j, *, ts):
    b, t, d = x.shape
    n_t = t // ts
    assert n_t * ts == t and ts % 8 == 0 and ts >= CONV_HALO
    return pl.pallas_call(
        functools.partial(_conv_p_body, ts=ts, n_t=n_t),
        grid=(b, n_t),
        in_specs=[pl.BlockSpec((1, ts, d), lambda bi, ti: (bi, ti, 0))] + _conv_specs(layer, j, d),
        out_specs=[
            pl.BlockSpec((1, ts, d), lambda bi, ti: (bi, ti, 0)),
            pl.BlockSpec((1, CONV_BUF, d), lambda bi, ti: (bi, 0, 0)),
        ],
        out_shape=[jax.ShapeDtypeStruct((b, t, d), F32), jax.ShapeDtypeStruct((b, CONV_BUF, d), F32)],
        scratch_shapes=[pltpu.VMEM((CONV_HALO + ts + SUBLANES, d), F32), pltpu.VMEM((ts, d), F32)],
        compiler_params=_params("parallel", "arbitrary"),
        name="conv_prompt",
    )(x, mix_norm, mix_norm, *cw)


def _conv_s_body(x_ref, st_ref, g0_ref, g1_ref, w1_ref, b1_ref, wdw_ref, bdw_ref, lng_ref, lnb_ref, w2_ref, b2_ref,
                 o_ref, so_ref, *, n_s):
    d = g0_ref.shape[-1]
    bt = x_ref.shape[0]
    xs = [x_ref[:, s * d:(s + 1) * d] for s in range(n_s)]
    h = jnp.concatenate([_rms(x, g0_ref[...]) for x in xs], axis=0).astype(BF16)
    z = jnp.dot(h, w1_ref[...], preferred_element_type=F32) + b1_ref[...]
    u = z[:, :d] * jax.nn.sigmoid(z[:, d:])
    us = [u[s * bt:(s + 1) * bt] for s in range(n_s)]

    def cat(idx):
        if idx < CONV_BUF:
            return st_ref[:, idx * d:(idx + 1) * d]
        return us[idx - CONV_BUF]

    acts = []
    for s in range(n_s):
        y = cat(s) * wdw_ref[0:1, :] + bdw_ref[...]
        for k in range(1, CONV_WIDTH):
            y = y + cat(s + k) * wdw_ref[k:k + 1, :]
        acts.append(_silu(_layer_norm(y, lng_ref[...], lnb_ref[...])))
    a = jnp.concatenate(acts, axis=0).astype(BF16)
    m = jnp.dot(a, w2_ref[...], preferred_element_type=F32) + b2_ref[...]
    for s in range(n_s):
        o_ref[:, s * d:(s + 1) * d] = xs[s] + _rms(m[s * bt:(s + 1) * bt], g1_ref[...])
    for jj in range(CONV_BUF):
        so_ref[:, jj * d:(jj + 1) * d] = cat(n_s + jj)


def _conv_sample(x, state, mix_norm, cw, layer, j, *, n_s, bt):
    db = x.shape[0]
    d = mix_norm.shape[-1]
    return pl.pallas_call(
        functools.partial(_conv_s_body, n_s=n_s),
        grid=(db // bt,),
        in_specs=[
            pl.BlockSpec((bt, n_s * d), lambda i: (i, 0)),
            pl.BlockSpec((None, bt, CONV_BUF * d), lambda i: (j, i, 0)),
        ] + _conv_specs(layer, j, d),
        out_specs=[
            pl.BlockSpec((bt, n_s * d), lambda i: (i, 0)),
            pl.BlockSpec((bt, CONV_BUF * d), lambda i: (i, 0)),
        ],
        out_shape=[jax.ShapeDtypeStruct((db, n_s * d), F32), jax.ShapeDtypeStruct((db, CONV_BUF * d), F32)],
        compiler_params=_params("parallel"),
        name="conv_sample",
    )(x, state, mix_norm, mix_norm, *cw)


def _qkv_body(x_ref, g0_ref, w_ref, q_ref, k_ref, v_ref, kb_ref, vb_ref):
    d = x_ref.shape[-1]
    h = _rms(x_ref[...], g0_ref[...]).astype(BF16)
    qkv = jnp.dot(h, w_ref[...], preferred_element_type=F32)
    q_ref[...] = (qkv[:, :d] * (HEAD_DIM ** -0.5)).astype(BF16)
    k = qkv[:, d:2 * d]
    v = qkv[:, 2 * d:]
    k_ref[...] = k
    v_ref[...] = v
    kb_ref[...] = k.astype(BF16)
    vb_ref[...] = v.astype(BF16)


def _qkv(x, mix_norm, wqkv, layer, j, *, tm):
    n, d = x.shape
    row = pl.BlockSpec((tm, d), lambda i: (i, 0))
    return pl.pallas_call(
        _qkv_body,
        grid=(pl.cdiv(n, tm),),
        in_specs=[row, _const_spec((None, 1, d), (layer * 2, 0, 0)), _const_spec((None, d, 3 * d), (j, 0, 0))],
        out_specs=[row] * 5,
        out_shape=[jax.ShapeDtypeStruct((n, d), BF16), jax.ShapeDtypeStruct((n, d), F32),
                   jax.ShapeDtypeStruct((n, d), F32), jax.ShapeDtypeStruct((n, d), BF16),
                   jax.ShapeDtypeStruct((n, d), BF16)],
        compiler_params=_params("parallel"),
        name="qkv_proj",
    )(x, mix_norm, wqkv)


def _bucket(n):
    nf = jnp.maximum(n, 1).astype(F32)
    large = MAX_EXACT + (jnp.log(nf / MAX_EXACT) / math.log(MAX_DISTANCE / MAX_EXACT)
                         * (N_BUCKETS - MAX_EXACT)).astype(jnp.int32)
    large = jnp.minimum(large, N_BUCKETS - 1)
    return jnp.where(n < MAX_EXACT, n, large)


def _lookup(bucket, entry):
    out = jnp.zeros(bucket.shape, F32)
    for b in range(N_BUCKETS):
        out = jnp.where(bucket == b, entry(b), out)
    return out


def _bias_p_body(table_ref, bp_ref, *, blk):
    head = pl.program_id(0)
    a = lax.broadcasted_iota(jnp.int32, (blk, blk), 0)
    b = lax.broadcasted_iota(jnp.int32, (blk, blk), 1)
    for sel in range(3):
        n = sel * blk + b - a
        vals = _lookup(_bucket(jnp.maximum(n, 0)), lambda bb: table_ref[bb * N_HEADS + head])
        bp_ref[0, sel] = jnp.where(n >= 0, vals, NEG_INF)


def _bias_s_body(tt_ref, bs_ref, *, n_s):
    rows = 2 * n_s * N_HEADS
    cols = PAGE_SIZE * N_HEADS
    tt = tt_ref[...]
    trow = jnp.broadcast_to(tt[None], (rows // N_HEADS, N_HEADS, N_BUCKETS)).reshape(rows, N_BUCKETS)

    def tile(width, dist):
        r = lax.broadcasted_iota(jnp.int32, (rows, width), 0)
        c = lax.broadcasted_iota(jnp.int32, (rows, width), 1)
        n = dist((r // N_HEADS) % n_s, c // N_HEADS)
        vals = _lookup(_bucket(jnp.maximum(n, 0)), lambda bb: trow[:, bb:bb + 1])
        return jnp.where((r % N_HEADS == c % N_HEADS) & (n >= 0) & (c < cols), vals, NEG_INF)

    bs_ref[:, 0:cols] = tile(cols, lambda qi, kk: 2 * PAGE_SIZE + qi - kk)
    bs_ref[:, cols:2 * cols] = tile(cols, lambda qi, kk: PAGE_SIZE + qi - kk)
    bs_ref[:, 2 * cols:2 * cols + PAGE_SIZE] = tile(
        PAGE_SIZE, lambda qi, kk: jnp.where(kk < n_s, qi - kk, -1))


def _bias_tiles(table, *, blk, n_s):
    far = _first_far_distance()
    assert blk + 1 >= far and PAGE_SIZE + 1 >= far and n_s * N_HEADS <= PAGE_SIZE
    bias_p = pl.pallas_call(
        functools.partial(_bias_p_body, blk=blk),
        grid=(N_HEADS,),
        in_specs=[pl.BlockSpec(memory_space=pltpu.SMEM)],
        out_specs=pl.BlockSpec((1, 3, blk, blk), lambda h: (h, 0, 0, 0)),
        out_shape=jax.ShapeDtypeStruct((N_HEADS, 3, blk, blk), F32),
        compiler_params=_params("parallel"),
        name="rel_bias_prompt",
    )(table.reshape(-1))
    width = 2 * PAGE_SIZE * N_HEADS + PAGE_SIZE
    bias_s = pl.pallas_call(
        functools.partial(_bias_s_body, n_s=n_s),
        out_shape=jax.ShapeDtypeStruct((2 * n_s * N_HEADS, width), F32),
        compiler_params=pltpu.CompilerParams(vmem_limit_bytes=VMEM_LIMIT),
        name="rel_bias_sample",
    )(table.T)
    return bias_p, bias_s


def _diff_lambda(lam_ref, lam_init):
    lp = lam_ref[...]
    s1 = jnp.sum(lp[0:1] * lp[1:2], axis=-1, keepdims=True)
    s2 = jnp.sum(lp[2:3] * lp[3:4], axis=-1, keepdims=True)
    return jnp.exp(s1) - jnp.exp(s2) + lam_init


def _softmax_step(s, m, l):
    m_new = jnp.maximum(m, jnp.max(s, axis=-1, keepdims=True))
    alpha = jnp.exp(m - m_new)
    p = jnp.exp(s - m_new)
    return p, m_new, alpha, alpha * l + jnp.sum(p, axis=-1, keepdims=True)


_NT = (((1,), (1,)), ((), ()))


def _attn_p_body(q_ref, k_ref, v_ref, bias_ref, lam_ref, sub_ref, o_ref, qz_ref, vt_ref, m_ref, l_ref, acc_ref, s_ref,
                 *, blk, n_full, tail, lam_init):
    i = pl.program_id(1)

    @pl.when(i == 0)
    def _():
        for hd in range(N_HEADS):
            c0 = hd * V_DIM

            def xpose(jb, carry):
                r0 = pl.multiple_of(jb * blk, blk)
                vt_ref[hd, jb] = v_ref[0, pl.ds(r0, blk), c0:c0 + V_DIM].astype(F32).T.astype(BF16)
                return carry

            lax.fori_loop(0, n_full, xpose, 0)
            if tail:
                r0 = n_full * blk
                vt_ref[hd, n_full, :, 0:tail] = v_ref[0, r0:r0 + tail, c0:c0 + V_DIM].astype(F32).T.astype(BF16)

    lane = lax.broadcasted_iota(jnp.int32, (blk, V_DIM), 1)
    for hd in range(N_HEADS):
        q = q_ref[0, :, hd * V_DIM:(hd + 1) * V_DIM]
        qz_ref[hd, 0:blk, :] = jnp.where(lane < HEAD_DIM, q, jnp.zeros_like(q))
        qz_ref[hd, blk:2 * blk, :] = jnp.where(lane >= HEAD_DIM, q, jnp.zeros_like(q))
    m_ref[...] = jnp.full(m_ref.shape, NEG_INF, F32)
    l_ref[...] = jnp.zeros(l_ref.shape, F32)
    acc_ref[...] = jnp.zeros(acc_ref.shape, F32)

    def all_heads(n, keys, vt, bias):
        def stage(hd):
            b2 = bias(hd)
            s_ref[hd % 2, 0:n, :] = (lax.dot_general(keys(hd), qz_ref[hd], _NT, preferred_element_type=F32)
                                     + jnp.concatenate([b2, b2], axis=1))

        stage(0)
        for hd in range(N_HEADS):
            if hd + 1 < N_HEADS:
                stage(hd + 1)
            s = s_ref[hd % 2, 0:n, :]
            m_old = m_ref[hd]
            m_new = jnp.maximum(m_old, jnp.max(s, axis=0, keepdims=True))
            alpha = jnp.exp(m_old - m_new)
            p = jnp.exp(s - m_new)
            l_ref[hd] = alpha * l_ref[hd] + jnp.sum(p, axis=0, keepdims=True)
            m_ref[hd] = m_new
            acc_ref[hd] = alpha * acc_ref[hd] + jnp.dot(vt(hd), p.astype(BF16), preferred_element_type=F32)

    def body(jb, carry):
        r0 = pl.multiple_of(jb * blk, blk)
        sel = jnp.minimum(i - jb, 2)
        all_heads(blk,
                  lambda hd: k_ref[0, pl.ds(r0, blk), hd * V_DIM:(hd + 1) * V_DIM],
                  lambda hd: vt_ref[hd, jb],
                  lambda hd: bias_ref[hd, sel])
        return carry

    lax.fori_loop(0, jnp.minimum(i + 1, n_full), body, 0)

    if tail:
        @pl.when(i == n_full)
        def _():
            r0 = n_full * blk
            all_heads(tail,
                      lambda hd: k_ref[0, r0:r0 + tail, hd * V_DIM:(hd + 1) * V_DIM],
                      lambda hd: vt_ref[hd, n_full, :, 0:tail],
                      lambda hd: bias_ref[hd, 0, 0:tail, :])

    lam = _diff_lambda(lam_ref, lam_init)
    for hd in range(N_HEADS):
        o_both = acc_ref[hd] * (1.0 / l_ref[hd])
        o = o_both[:, 0:blk] - lam * o_both[:, blk:2 * blk]
        o = o * lax.rsqrt(jnp.mean(o * o, axis=0, keepdims=True) + RMS_EPS) * sub_ref[...] * (1.0 - lam_init)
        o_ref[0, :, hd * V_DIM:(hd + 1) * V_DIM] = o.T.astype(BF16)


def _attn_prompt(q, kb, vb, bias_p, lam_p, subln_col, j, *, blk, lam_init):
    b, t, d = q.shape
    n_full, tail = t // blk, t % blk
    assert tail % 16 == 0
    n_blocks = n_full + (1 if tail else 0)
    return pl.pallas_call(
        functools.partial(_attn_p_body, blk=blk, n_full=n_full, tail=tail, lam_init=lam_init),
        grid=(b, n_blocks),
        in_specs=[
            pl.BlockSpec((1, blk, d), lambda bi, qi: (bi, qi, 0)),
            pl.BlockSpec((1, t, d), lambda bi, qi: (bi, 0, 0)),
            pl.BlockSpec((1, t, d), lambda bi, qi: (bi, 0, 0)),
            _const_spec((N_HEADS, 3, blk, blk), (0, 0, 0, 0)),
            _const_spec((None, 4, HEAD_DIM), (j, 0, 0)),
            _const_spec((None, V_DIM, 1), (j, 0, 0)),
        ],
        out_specs=pl.BlockSpec((1, blk, d), lambda bi, qi: (bi, qi, 0)),
        out_shape=jax.ShapeDtypeStruct((b, t, d), BF16),
        scratch_shapes=[
            pltpu.VMEM((N_HEADS, 2 * blk, V_DIM), BF16),
            pltpu.VMEM((N_HEADS, n_blocks, V_DIM, blk), BF16),
            pltpu.VMEM((N_HEADS, 1, 2 * blk), F32),
            pltpu.VMEM((N_HEADS, 1, 2 * blk), F32),
            pltpu.VMEM((N_HEADS, V_DIM, 2 * blk), F32),
            pltpu.VMEM((2, blk, 2 * blk), F32),
        ],
        compiler_params=_params("arbitrary", "arbitrary"),
        name="attn_prompt",
    )(q, kb, vb, bias_p, lam_p, subln_col)


def _attn_s_body(pt_ref, q_ref, kn_ref, vn_ref, bias_ref, lam_ref, sub_ref, *rest, n_pages, n_steps, n_s, lam_init):
    k_pages = rest[:n_pages]
    v_pages = rest[n_pages:2 * n_pages]
    o_ref, q2_ref, m_ref, l_ref, acc_ref, s_ref = rest[2 * n_pages:]
    step = pl.program_id(1)
    half = n_s * N_HEADS
    cols = PAGE_SIZE * N_HEADS

    @pl.when(step == 0)
    def _():
        q = q_ref[...].reshape(half, V_DIM)
        lane = lax.broadcasted_iota(jnp.int32, (half, V_DIM), 1)
        q2_ref[0:half, :] = jnp.where(lane < HEAD_DIM, q, 0.0).astype(BF16)
        q2_ref[half:2 * half, :] = jnp.where(lane >= HEAD_DIM, q, 0.0).astype(BF16)
        m_ref[...] = jnp.full(m_ref.shape, NEG_INF, F32)
        l_ref[...] = jnp.zeros(l_ref.shape, F32)
        acc_ref[...] = jnp.zeros(acc_ref.shape, F32)

    q2 = q2_ref[...]
    last = step == n_steps - 1

    def scores(keys, bias):
        return lax.dot_general(q2, keys, _NT, preferred_element_type=F32) + bias

    def update(s, values):
        p, m_new, alpha, l_new = _softmax_step(s, m_ref[...], l_ref[...])
        m_ref[...] = m_new
        l_ref[...] = l_new
        acc_ref[...] = alpha * acc_ref[...] + jnp.dot(p.astype(BF16), values, preferred_element_type=F32)

    def stage(r_):
        bias = bias_ref[:, 0:cols]
        if r_ == n_pages - 1:
            bias = jnp.where(last, bias_ref[:, cols:2 * cols], bias)
        s_ref[r_ % 2] = scores(k_pages[r_][...].reshape(cols, V_DIM).astype(BF16), bias)

    stage(0)
    for r_ in range(n_pages):
        if r_ + 1 < n_pages:
            stage(r_ + 1)
        update(s_ref[r_ % 2], v_pages[r_][...].reshape(cols, V_DIM).astype(BF16))

    @pl.when(last)
    def _():
        pad = jnp.zeros((PAGE_SIZE - half, V_DIM), F32)
        kn = jnp.concatenate([kn_ref[...].reshape(half, V_DIM), pad], axis=0).astype(BF16)
        vn = jnp.concatenate([vn_ref[...].reshape(half, V_DIM), pad], axis=0).astype(BF16)
        update(scores(kn, bias_ref[:, 2 * cols:2 * cols + PAGE_SIZE]), vn)
        lam = _diff_lambda(lam_ref, lam_init)
        o_all = acc_ref[...] * (1.0 / l_ref[...])
        o = o_all[0:half] - lam * o_all[half:2 * half]
        o_ref[...] = (_rms(o, sub_ref[...]) * (1.0 - lam_init)).reshape(o_ref.shape)


def _attn_sample(q, kn, vn, cache_k, cache_v, page_table, bias_s, lam_p, subln, j, *, n_pages, lam_init):
    db, n_s = q.shape[:2]
    total_pages = page_table.shape[1]
    n_steps = total_pages // n_pages
    assert n_steps * n_pages == total_pages
    rows = 2 * n_s * N_HEADS
    new_kv = pl.BlockSpec((1, n_s, 1, N_HEADS, V_DIM), lambda b, s, pt: (b, 0, 0, 0, 0))
    q_spec = pl.BlockSpec((1, n_s, N_HEADS, V_DIM), lambda b, s, pt: (b, 0, 0, 0))

    def page_spec(r):
        return pl.BlockSpec((1, PAGE_SIZE, 1, N_HEADS, V_DIM),
                            lambda b, s, pt: (pt[b, s * n_pages + r], 0, j, 0, 0))

    grid_spec = pltpu.PrefetchScalarGridSpec(
        num_scalar_prefetch=1,
        grid=(db, n_steps),
        in_specs=[
            q_spec, new_kv, new_kv,
            pl.BlockSpec(bias_s.shape, lambda b, s, pt: (0, 0)),
            pl.BlockSpec((None, 4, HEAD_DIM), lambda b, s, pt: (j, 0, 0)),
            pl.BlockSpec((None, 1, V_DIM), lambda b, s, pt: (j, 0, 0)),
        ] + [page_spec(r) for r in range(n_pages)] * 2,
        out_specs=q_spec,
        scratch_shapes=[pltpu.VMEM((rows, V_DIM), BF16), pltpu.VMEM((rows, 1), F32), pltpu.VMEM((rows, 1), F32),
                        pltpu.VMEM((rows, V_DIM), F32), pltpu.VMEM((2, rows, PAGE_SIZE * N_HEADS), F32)],
    )
    return pl.pallas_call(
        functools.partial(_attn_s_body, n_pages=n_pages, n_steps=n_steps, n_s=n_s, lam_init=lam_init),
        grid_spec=grid_spec,
        out_shape=jax.ShapeDtypeStruct((db, n_s, N_HEADS, V_DIM), F32),
        compiler_params=_params("parallel", "arbitrary"),
        name="attn_sample",
    )(page_table, q, kn, vn, bias_s, lam_p, subln, *([cache_k] * n_pages), *([cache_v] * n_pages))


def _outproj_body(a_ref, x_ref, w_ref, g1_ref, o_ref):
    m = jnp.dot(a_ref[...].astype(BF16), w_ref[...], preferred_element_type=F32)
    o_ref[...] = x_ref[...] + _rms(m, g1_ref[...])


def _outproj(a, x, wo, mix_norm, layer, j, *, tm):
    n, d = x.shape
    row = pl.BlockSpec((tm, d), lambda i: (i, 0))
    return pl.pallas_call(
        _outproj_body,
        grid=(pl.cdiv(n, tm),),
        in_specs=[row, row, _const_spec((None, d, d), (j, 0, 0)), _const_spec((None, 1, d), (layer * 2 + 1, 0, 0))],
        out_specs=row,
        out_shape=jax.ShapeDtypeStruct((n, d), F32),
        compiler_params=_params("parallel"),
        name="attn_outproj",
    )(a, x, wo, mix_norm)


def _final_p_body(x_ref, g_ref, o_ref, *, skip, rows):
    t = pl.program_id(1)
    r0 = pl.multiple_of(skip + t * rows, 8)
    o_ref[0] = _rms(x_ref[0, pl.ds(r0, rows), :], g_ref[...])


def _final_prompt(x, g, *, skip, rows):
    b, t, d = x.shape
    n_t = (t - skip) // rows
    return pl.pallas_call(
        functools.partial(_final_p_body, skip=skip, rows=rows),
        grid=(b, n_t),
        in_specs=[pl.BlockSpec((1, t, d), lambda bi, ti: (bi, 0, 0)), _const_spec((1, d), (0, 0))],
        out_specs=pl.BlockSpec((1, rows, d), lambda bi, ti: (bi, ti, 0)),
        out_shape=jax.ShapeDtypeStruct((b, t - skip, d), F32),
        compiler_params=_params("parallel", "arbitrary"),
        name="final_norm_prompt",
    )(x, g)


def _final_s_body(x_ref, g_ref, o_ref):
    o_ref[...] = _rms(x_ref[...], g_ref[...])


def _final_sample(x, g, *, tm):
    n, d = x.shape
    row = pl.BlockSpec((tm, d), lambda i: (i, 0))
    return pl.pallas_call(
        _final_s_body,
        grid=(pl.cdiv(n, tm),),
        in_specs=[row, _const_spec((1, d), (0, 0))],
        out_specs=row,
        out_shape=jax.ShapeDtypeStruct((n, d), F32),
        compiler_params=_params("parallel"),
        name="final_norm_sample",
    )(x, g)


def _pick_tile(n, cap, align=8):
    best = None
    for c in range(align, min(n, cap) + 1, align):
        if n % c == 0:
            best = c
    assert best is not None, (n, cap, align)
    return best


def kernel(x_prompt, x_sample, state_pool, state_conv, cache_k, cache_v, page_table, meta_tokens, rel_bias_table, ffn_norm, ffn_wg, ffn_wu, ffn_wd, mix_norm, pool_w, pool_scale, attn_wqkv, attn_wo, attn_lambda, attn_subln, conv_w1, conv_b1, conv_wdw, conv_bdw, conv_ln_g, conv_ln_b, conv_w2, conv_b2, final_norm):
    b, seq, d = x_prompt.shape
    db, n_s, _ = x_sample.shape
    depth = ffn_wg.shape[0]
    t = seq + N_META
    past = page_table.shape[1] * PAGE_SIZE

    wg, wu, wd = ffn_wg.astype(BF16), ffn_wu.astype(BF16), ffn_wd.astype(BF16)
    ffn_norm3 = ffn_norm.reshape(-1, 1, d)
    mix_norm3 = mix_norm.reshape(-1, 1, d)
    pool_w_b = pool_w.astype(BF16)
    pool_scale3 = pool_scale.reshape(-1, 1, d)
    wqkv_b, wo_b = attn_wqkv.astype(BF16), attn_wo.astype(BF16)
    subln3 = attn_subln.reshape(-1, 1, V_DIM)
    cw = (conv_w1.astype(BF16), conv_b1.reshape(-1, 1, 2 * d), conv_wdw, conv_bdw.reshape(-1, 1, d),
          conv_ln_g.reshape(-1, 1, d), conv_ln_b.reshape(-1, 1, d), conv_w2.astype(BF16), conv_b2.reshape(-1, 1, d))
    state_pool2 = state_pool.reshape(state_pool.shape[0], db, POOL_BUF * d)
    state_conv2 = state_conv.reshape(state_conv.shape[0], db, CONV_BUF * d)

    meta = jnp.broadcast_to(meta_tokens[None].astype(x_prompt.dtype), (b, N_META, d))
    xp = jnp.concatenate([meta, x_prompt], axis=1).reshape(b * t, d)
    xs = x_sample.reshape(db * n_s, d)

    tm_p = min(384, b * t)
    tm_s = min(512, db * n_s)
    ts_pool = _pick_tile(t, 1032)
    ts_conv = _pick_tile(t, 344)
    bt = _pick_tile(db, 32)

    bias_p = bias_s = None
    pool_p, pool_s, conv_p, conv_s = [], [], [], []
    k_p = v_p = k_s = v_s = None
    for i in range(depth):
        kind, j = i % N_MIXERS, i // N_MIXERS
        xp = _ffn(xp, ffn_norm3, wg, wu, wd, i, 0, tm=tm_p)
        xs = _ffn(xs, ffn_norm3, wg, wu, wd, i, 0, tm=tm_s)
        if kind == 0:
            xp3, st = _pool_prompt(xp.reshape(b, t, d), mix_norm3, pool_w_b, pool_scale3, i, j, ts=ts_pool)
            xp = xp3.reshape(b * t, d)
            pool_p.append(st)
            xs2, st = _pool_sample(xs.reshape(db, n_s * d), state_pool2, mix_norm3, pool_w_b, pool_scale3, i, j,
                                   n_s=n_s, past=past, bt=bt)
            xs = xs2.reshape(db * n_s, d)
            pool_s.append(st.reshape(db, POOL_BUF, d))
        elif kind == 1:
            assert j == 0
            lam_init = _lambda_init(i)
            if bias_p is None:
                bias_p, bias_s = _bias_tiles(rel_bias_table, blk=ATTN_BLOCK, n_s=n_s)
            q, k_p, v_p, kb, vb = _qkv(xp, mix_norm3, wqkv_b, i, j, tm=tm_p)
            a = _attn_prompt(q.reshape(b, t, d), kb.reshape(b, t, d), vb.reshape(b, t, d), bias_p, attn_lambda,
                             attn_subln.reshape(-1, V_DIM, 1), j, blk=ATTN_BLOCK, lam_init=lam_init)
            xp = _outproj(a.reshape(b * t, d), xp, wo_b, mix_norm3, i, j, tm=tm_p)
            q, k_s, v_s, _, _ = _qkv(xs, mix_norm3, wqkv_b, i, j, tm=tm_s)
            k_s = k_s.reshape(db, n_s, 1, N_HEADS, V_DIM)
            v_s = v_s.reshape(db, n_s, 1, N_HEADS, V_DIM)
            a = _attn_sample(q.astype(F32).reshape(db, n_s, N_HEADS, V_DIM), k_s, v_s, cache_k, cache_v,
                             page_table, bias_s, attn_lambda, subln3, j, n_pages=8, lam_init=lam_init)
            xs = _outproj(a.reshape(db * n_s, d), xs, wo_b, mix_norm3, i, j, tm=tm_s)
        else:
            xp3, st = _conv_prompt(xp.reshape(b, t, d), mix_norm3, cw, i, j, ts=ts_conv)
            xp = xp3.reshape(b * t, d)
            conv_p.append(st)
            xs2, st = _conv_sample(xs.reshape(db, n_s * d), state_conv2, mix_norm3, cw, i, j, n_s=n_s, bt=bt)
            xs = xs2.reshape(db * n_s, d)
            conv_s.append(st.reshape(db, CONV_BUF, d))
        xp = _ffn(xp, ffn_norm3, wg, wu, wd, i, 1, tm=tm_p)
        xs = _ffn(xs, ffn_norm3, wg, wu, wd, i, 1, tm=tm_s)

    g = final_norm.reshape(1, d)
    y_prompt = _final_prompt(xp.reshape(b, t, d), g, skip=N_META, rows=_pick_tile(seq, 1024))
    y_sample = _final_sample(xs, g, tm=tm_s).reshape(db, n_s, d)
    kv_shape_p = (b, t, 1, N_HEADS, V_DIM)
    return (y_prompt, y_sample, jnp.stack(pool_p, axis=0), jnp.stack(pool_s, axis=0),
            jnp.stack(conv_p, axis=0), jnp.stack(conv_s, axis=0),
            k_p.reshape(kv_shape_p), v_p.reshape(kv_shape_p), k_s, v_s)
```

```python
import functools
import math

import jax
import jax.numpy as jnp
from jax import lax
from jax.experimental import pallas as pl
from jax.experimental.pallas import tpu as pltpu

F32 = jnp.float32
BF16 = jnp.bfloat16

N_MIXERS = 3
N_META = 16
N_HEADS = 8
HEAD_DIM = 64
V_DIM = 2 * HEAD_DIM
N_BUCKETS = 32
MAX_EXACT = N_BUCKETS // 2
MAX_DISTANCE = 128
POOL_WINDOWS = (2, 4, 8, 16)
POOL_BUF = max(POOL_WINDOWS) - 1
CONV_WIDTH = 31
CONV_BUF = CONV_WIDTH - 1
PAGE_SIZE = 128
RMS_EPS = 1e-6
LN_EPS = 1e-5
NEG_INF = -1e30
SUBLANES = 8
LANES = 128
MXU_TILE = 256
SUM_ROWS = 16
ATTN_BLOCK = 256
POOL_HALO = 16
CONV_HALO = 32
VMEM_LIMIT = 56 * 1024 * 1024


def _lambda_init(layer_idx):
    return 0.8 - 0.6 * math.exp(-0.3 * layer_idx)


def _first_far_distance():
    n = MAX_EXACT
    while MAX_EXACT + int(math.log(n / MAX_EXACT) / math.log(MAX_DISTANCE / MAX_EXACT) * (N_BUCKETS - MAX_EXACT)) < N_BUCKETS - 1:
        n += 1
    return n


def _params(*sem):
    return pltpu.CompilerParams(dimension_semantics=sem, vmem_limit_bytes=VMEM_LIMIT)


def _rms(x, g):
    return x * lax.rsqrt(jnp.mean(x * x, axis=-1, keepdims=True) + RMS_EPS) * g


def _silu(x):
    return x * jax.nn.sigmoid(x)


def _const_spec(shape, index, single=False):
    if single:
        return pl.BlockSpec(shape, lambda *_: index, pipeline_mode=pl.Buffered(1))
    return pl.BlockSpec(shape, lambda *_: index)


def _ffn_chunks(ff, n_chunks):
    tiles = -(-ff // MXU_TILE)
    edges = [min(ff, MXU_TILE * (-(-tiles * c // n_chunks))) for c in range(n_chunks + 1)]
    return [(lo, hi - lo) for lo, hi in zip(edges[:-1], edges[1:]) if hi > lo]


def _ffn_body(x_ref, n0_ref, n1_ref, wg_ref, wu_ref, wd_ref, o_ref, *, n_chunks):
    x = x_ref[...]
    h = _rms(x, n0_ref[...]).astype(BF16)
    acc = jnp.zeros(x.shape, F32)
    for lo, width in _ffn_chunks(wg_ref.shape[1], n_chunks):
        sl = pl.ds(lo, width)
        g = jnp.dot(h, wg_ref[:, sl], preferred_element_type=F32)
        u = jnp.dot(h, wu_ref[:, sl], preferred_element_type=F32)
        a = (_silu(g) * u).astype(BF16)
        acc = acc + jnp.dot(a, wd_ref[sl, :], preferred_element_type=F32)
    o_ref[...] = x + 0.5 * _rms(acc, n1_ref[...])


def _ffn(x, norms, wg, wu, wd, layer, f, *, tm, n_chunks=2):
    n, d = x.shape
    ff = wg.shape[-1]
    nidx = (layer * 2 + f) * 2
    return pl.pallas_call(
        functools.partial(_ffn_body, n_chunks=n_chunks),
        grid=(pl.cdiv(n, tm),),
        in_specs=[
            pl.BlockSpec((tm, d), lambda i: (i, 0)),
            _const_spec((None, 1, d), (nidx, 0, 0)),
            _const_spec((None, 1, d), (nidx + 1, 0, 0)),
            _const_spec((None, None, d, ff), (layer, f, 0, 0), single=True),
            _const_spec((None, None, d, ff), (layer, f, 0, 0), single=True),
            _const_spec((None, None, ff, d), (layer, f, 0, 0), single=True),
        ],
        out_specs=pl.BlockSpec((tm, d), lambda i: (i, 0)),
        out_shape=jax.ShapeDtypeStruct((n, d), F32),
        compiler_params=_params("parallel"),
        name="ffn",
    )(x, norms, norms, wg, wu, wd)


def _pool_p_body(x_ref, g0_ref, g1_ref, w_ref, sc_ref, o_ref, st_ref, cat_ref, *, ts, n_t):
    t = pl.program_id(1)
    d = x_ref.shape[-1]
    gd = d // len(POOL_WINDOWS)
    x = x_ref[0]
    h = _rms(x, g0_ref[...])

    @pl.when(t == 0)
    def _():
        cat_ref[0:POOL_HALO, :] = jnp.zeros((POOL_HALO, d), F32)

    cat_ref[POOL_HALO:POOL_HALO + ts, :] = h
    pos = t * ts + lax.broadcasted_iota(jnp.int32, (ts, 1), 0)
    outs = []
    for g, w in enumerate(POOL_WINDOWS):
        c0 = g * gd
        acc = cat_ref[POOL_HALO:POOL_HALO + ts, c0:c0 + gd]
        for j in range(1, w):
            acc = acc + cat_ref[POOL_HALO - j:POOL_HALO - j + ts, c0:c0 + gd]
        inv_cnt = 1.0 / jnp.minimum(w, pos + 1).astype(F32)
        pooled = (acc * inv_cnt - h[:, c0:c0 + gd]).astype(BF16)
        outs.append(jnp.dot(pooled, w_ref[g], preferred_element_type=F32))
    m = jnp.concatenate(outs, axis=-1) * sc_ref[...]
    o_ref[0] = x + _rms(m, g1_ref[...])

    @pl.when(t == n_t - 1)
    def _():
        st_ref[0] = cat_ref[POOL_HALO + ts - POOL_BUF:POOL_HALO + ts, :]

    cat_ref[0:POOL_HALO, :] = cat_ref[ts:ts + POOL_HALO, :]


def _pool_prompt(x, mix_norm, pool_w, pool_scale, layer, j, *, ts):
    b, t, d = x.shape
    n_t = t // ts
    assert n_t * ts == t and ts % 8 == 0 and ts >= POOL_HALO
    g = len(POOL_WINDOWS)
    return pl.pallas_call(
        functools.partial(_pool_p_body, ts=ts, n_t=n_t),
        grid=(b, n_t),
        in_specs=[
            pl.BlockSpec((1, ts, d), lambda bi, ti: (bi, ti, 0)),
            _const_spec((None, 1, d), (layer * 2, 0, 0)),
            _const_spec((None, 1, d), (layer * 2 + 1, 0, 0)),
            _const_spec((None, g, d // g, d // g), (j, 0, 0, 0)),
            _const_spec((None, 1, d), (j, 0, 0)),
        ],
        out_specs=[
            pl.BlockSpec((1, ts, d), lambda bi, ti: (bi, ti, 0)),
            pl.BlockSpec((1, POOL_BUF, d), lambda bi, ti: (bi, 0, 0)),
        ],
        out_shape=[jax.ShapeDtypeStruct((b, t, d), F32), jax.ShapeDtypeStruct((b, POOL_BUF, d), F32)],
        scratch_shapes=[pltpu.VMEM((POOL_HALO + ts, d), F32)],
        compiler_params=_params("parallel", "arbitrary"),
        name="pool_prompt",
    )(x, mix_norm, mix_norm, pool_w, pool_scale)


def _pool_s_body(x_ref, st_ref, g0_ref, g1_ref, w_ref, sc_ref, o_ref, so_ref, *, n_s, past):
    d = g0_ref.shape[-1]
    gd = d // len(POOL_WINDOWS)
    bt = x_ref.shape[0]
    xs = [x_ref[:, s * d:(s + 1) * d] for s in range(n_s)]
    hs = [_rms(x, g0_ref[...]) for x in xs]

    def cat(idx, c0, width):
        if idx < POOL_BUF:
            return st_ref[:, idx * d + c0:idx * d + c0 + width]
        return hs[idx - POOL_BUF][:, c0:c0 + width]

    outs = []
    for g, w in enumerate(POOL_WINDOWS):
        c0 = g * gd
        rows = []
        for s in range(n_s):
            acc = cat(POOL_BUF + s, c0, gd)
            for jj in range(1, w):
                acc = acc + cat(POOL_BUF + s - jj, c0, gd)
            rows.append(acc * (1.0 / min(w, past + s + 1)) - hs[s][:, c0:c0 + gd])
        pooled = jnp.concatenate(rows, axis=0).astype(BF16)
        outs.append(jnp.dot(pooled, w_ref[g], preferred_element_type=F32))
    for s in range(n_s):
        m = jnp.concatenate([o[s * bt:(s + 1) * bt] for o in outs], axis=-1) * sc_ref[...]
        o_ref[:, s * d:(s + 1) * d] = xs[s] + _rms(m, g1_ref[...])
    for jj in range(POOL_BUF):
        so_ref[:, jj * d:(jj + 1) * d] = cat(n_s + jj, 0, d)


def _pool_sample(x, state, mix_norm, pool_w, pool_scale, layer, j, *, n_s, past, bt):
    db = x.shape[0]
    d = mix_norm.shape[-1]
    g = len(POOL_WINDOWS)
    return pl.pallas_call(
        functools.partial(_pool_s_body, n_s=n_s, past=past),
        grid=(db // bt,),
        in_specs=[
            pl.BlockSpec((bt, n_s * d), lambda i: (i, 0)),
            pl.BlockSpec((None, bt, POOL_BUF * d), lambda i: (j, i, 0)),
            _const_spec((None, 1, d), (layer * 2, 0, 0)),
            _const_spec((None, 1, d), (layer * 2 + 1, 0, 0)),
            _const_spec((None, g, d // g, d // g), (j, 0, 0, 0)),
            _const_spec((None, 1, d), (j, 0, 0)),
        ],
        out_specs=[
            pl.BlockSpec((bt, n_s * d), lambda i: (i, 0)),
            pl.BlockSpec((bt, POOL_BUF * d), lambda i: (i, 0)),
        ],
        out_shape=[jax.ShapeDtypeStruct((db, n_s * d), F32), jax.ShapeDtypeStruct((db, POOL_BUF * d), F32)],
        compiler_params=_params("parallel"),
        name="pool_sample",
    )(x, state, mix_norm, mix_norm, pool_w, pool_scale)


def _layer_norm(y, g, b):
    mu = jnp.mean(y, axis=-1, keepdims=True)
    yc = y - mu
    var = jnp.mean(yc * yc, axis=-1, keepdims=True)
    return yc * lax.rsqrt(var + LN_EPS) * g + b


def _conv_p_body(x_ref, g0_ref, g1_ref, w1_ref, b1_ref, wdw_ref, bdw_ref, lng_ref, lnb_ref, w2_ref, b2_ref,
                 o_ref, st_ref, cat_ref, y_ref, *, ts, n_t):
    t = pl.program_id(1)
    d = x_ref.shape[-1]
    x = x_ref[0]
    h = _rms(x, g0_ref[...]).astype(BF16)
    glu = jnp.dot(h, w1_ref[...], preferred_element_type=F32) + b1_ref[...]
    u = glu[:, :d] * jax.nn.sigmoid(glu[:, d:])

    @pl.when(t == 0)
    def _():
        cat_ref[0:CONV_HALO, :] = jnp.zeros((CONV_HALO, d), F32)
        cat_ref[CONV_HALO + ts:CONV_HALO + ts + SUBLANES, :] = jnp.zeros((SUBLANES, d), F32)

    cat_ref[CONV_HALO:CONV_HALO + ts, :] = u
    off = CONV_HALO - CONV_BUF
    zrows = ts + SUBLANES
    for c0 in range(0, d, LANES):
        y = None
        for r in range(SUBLANES):
            z = None
            for a in range((CONV_WIDTH + off) // SUBLANES + 1):
                k = SUBLANES * a + r - off
                if 0 <= k < CONV_WIDTH:
                    term = cat_ref[SUBLANES * a:SUBLANES * a + zrows, c0:c0 + LANES] * wdw_ref[k:k + 1, c0:c0 + LANES]
                    z = term if z is None else z + term
            zs = z[r:r + ts]
            y = zs if y is None else y + zs
        y_ref[:, c0:c0 + LANES] = y + bdw_ref[:, c0:c0 + LANES]
    a = _silu(_layer_norm(y_ref[...], lng_ref[...], lnb_ref[...])).astype(BF16)
    m = jnp.dot(a, w2_ref[...], preferred_element_type=F32) + b2_ref[...]
    o_ref[0] = x + _rms(m, g1_ref[...])

    @pl.when(t == n_t - 1)
    def _():
        st_ref[0] = cat_ref[CONV_HALO + ts - CONV_BUF:CONV_HALO + ts, :]

    cat_ref[0:CONV_HALO, :] = cat_ref[ts:ts + CONV_HALO, :]


def _conv_specs(layer, j, d):
    return [
        _const_spec((None, 1, d), (layer * 2, 0, 0)),
        _const_spec((None, 1, d), (layer * 2 + 1, 0, 0)),
        _const_spec((None, d, 2 * d), (j, 0, 0)),
        _const_spec((None, 1, 2 * d), (j, 0, 0)),
        _const_spec((None, CONV_WIDTH, d), (j, 0, 0)),
        _const_spec((None, 1, d), (j, 0, 0)),
        _const_spec((None, 1, d), (j, 0, 0)),
        _const_spec((None, 1, d), (j, 0, 0)),
        _const_spec((None, d, d), (j, 0, 0)),
        _const_spec((None, 1, d), (j, 0, 0)),
    ]


def _conv_prompt(x, mix_norm, cw, layer, j, *, ts):
    b, t, d = x.shape
    n_t = t // ts
    assert n_t * ts == t and ts % 8 == 0 and ts >= CONV_HALO
    return pl.pallas_call(
        functools.partial(_conv_p_body, ts=ts, n_t=n_t),
        grid=(b, n_t),
        in_specs=[pl.BlockSpec((1, ts, d), lambda bi, ti: (bi, ti, 0))] + _conv_specs(layer, j, d),
        out_specs=[
            pl.BlockSpec((1, ts, d), lambda bi, ti: (bi, ti, 0)),
            pl.BlockSpec((1, CONV_BUF, d), lambda bi, ti: (bi, 0, 0)),
        ],
        out_shape=[jax.ShapeDtypeStruct((b, t, d), F32), jax.ShapeDtypeStruct((b, CONV_BUF, d), F32)],
        scratch_shapes=[pltpu.VMEM((CONV_HALO + ts + SUBLANES, d), F32), pltpu.VMEM((ts, d), F32)],
        compiler_params=_params("parallel", "arbitrary"),
        name="conv_prompt",
    )(x, mix_norm, mix_norm, *cw)


def _conv_s_body(x_ref, st_ref, g0_ref, g1_ref, w1_ref, b1_ref, wdw_ref, bdw_ref, lng_ref, lnb_ref, w2_ref, b2_ref,
                 o_ref, so_ref, *, n_s):
    d = g0_ref.shape[-1]
    bt = x_ref.shape[0]
    xs = [x_ref[:, s * d:(s + 1) * d] for s in range(n_s)]
    h = jnp.concatenate([_rms(x, g0_ref[...]) for x in xs], axis=0).astype(BF16)
    z = jnp.dot(h, w1_ref[...], preferred_element_type=F32) + b1_ref[...]
    u = z[:, :d] * jax.nn.sigmoid(z[:, d:])
    us = [u[s * bt:(s + 1) * bt] for s in range(n_s)]

    def cat(idx):
        if idx < CONV_BUF:
            return st_ref[:, idx * d:(idx + 1) * d]
        return us[idx - CONV_BUF]

    acts = []
    for s in range(n_s):
        y = cat(s) * wdw_ref[0:1, :] + bdw_ref[...]
        for k in range(1, CONV_WIDTH):
            y = y + cat(s + k) * wdw_ref[k:k + 1, :]
        acts.append(_silu(_layer_norm(y, lng_ref[...], lnb_ref[...])))
    a = jnp.concatenate(acts, axis=0).astype(BF16)
    m = jnp.dot(a, w2_ref[...], preferred_element_type=F32) + b2_ref[...]
    for s in range(n_s):
        o_ref[:, s * d:(s + 1) * d] = xs[s] + _rms(m[s * bt:(s + 1) * bt], g1_ref[...])
    for jj in range(CONV_BUF):
        so_ref[:, jj * d:(jj + 1) * d] = cat(n_s + jj)


def _conv_sample(x, state, mix_norm, cw, layer, j, *, n_s, bt):
    db = x.shape[0]
    d = mix_norm.shape[-1]
    return pl.pallas_call(
        functools.partial(_conv_s_body, n_s=n_s),
        grid=(db // bt,),
        in_specs=[
            pl.BlockSpec((bt, n_s * d), lambda i: (i, 0)),
            pl.BlockSpec((None, bt, CONV_BUF * d), lambda i: (j, i, 0)),
        ] + _conv_specs(layer, j, d),
        out_specs=[
            pl.BlockSpec((bt, n_s * d), lambda i: (i, 0)),
            pl.BlockSpec((bt, CONV_BUF * d), lambda i: (i, 0)),
        ],
        out_shape=[jax.ShapeDtypeStruct((db, n_s * d), F32), jax.ShapeDtypeStruct((db, CONV_BUF * d), F32)],
        compiler_params=_params("parallel"),
        name="conv_sample",
    )(x, state, mix_norm, mix_norm, *cw)


def _qkv_body(x_ref, g0_ref, w_ref, q_ref, k_ref, v_ref, kb_ref, vb_ref):
    d = x_ref.shape[-1]
    h = _rms(x_ref[...], g0_ref[...]).astype(BF16)
    qkv = jnp.dot(h, w_ref[...], preferred_element_type=F32)
    q_ref[...] = (qkv[:, :d] * (HEAD_DIM ** -0.5)).astype(BF16)
    k = qkv[:, d:2 * d]
    v = qkv[:, 2 * d:]
    k_ref[...] = k
    v_ref[...] = v
    kb_ref[...] = k.astype(BF16)
    vb_ref[...] = v.astype(BF16)


def _qkv(x, mix_norm, wqkv, layer, j, *, tm):
    n, d = x.shape
    row = pl.BlockSpec((tm, d), lambda i: (i, 0))
    return pl.pallas_call(
        _qkv_body,
        grid=(pl.cdiv(n, tm),),
        in_specs=[row, _const_spec((None, 1, d), (layer * 2, 0, 0)), _const_spec((None, d, 3 * d), (j, 0, 0))],
        out_specs=[row] * 5,
        out_shape=[jax.ShapeDtypeStruct((n, d), BF16), jax.ShapeDtypeStruct((n, d), F32),
                   jax.ShapeDtypeStruct((n, d), F32), jax.ShapeDtypeStruct((n, d), BF16),
                   jax.ShapeDtypeStruct((n, d), BF16)],
        compiler_params=_params("parallel"),
        name="qkv_proj",
    )(x, mix_norm, wqkv)


def _bucket(n):
    nf = jnp.maximum(n, 1).astype(F32)
    large = MAX_EXACT + (jnp.log(nf / MAX_EXACT) / math.log(MAX_DISTANCE / MAX_EXACT)
                         * (N_BUCKETS - MAX_EXACT)).astype(jnp.int32)
    large = jnp.minimum(large, N_BUCKETS - 1)
    return jnp.where(n < MAX_EXACT, n, large)


def _lookup(bucket, entry):
    out = jnp.zeros(bucket.shape, F32)
    for b in range(N_BUCKETS):
        out = jnp.where(bucket == b, entry(b), out)
    return out


def _bias_p_body(table_ref, bp_ref, *, blk):
    head = pl.program_id(0)
    a = lax.broadcasted_iota(jnp.int32, (blk, blk), 0)
    b = lax.broadcasted_iota(jnp.int32, (blk, blk), 1)
    for sel in range(3):
        n = sel * blk + b - a
        vals = _lookup(_bucket(jnp.maximum(n, 0)), lambda bb: table_ref[bb * N_HEADS + head])
        bp_ref[0, sel] = jnp.where(n >= 0, vals, NEG_INF)


def _bias_s_body(tt_ref, bs_ref, *, n_s):
    rows = 2 * n_s * N_HEADS
    cols = PAGE_SIZE * N_HEADS
    tt = tt_ref[...]
    trow = jnp.broadcast_to(tt[None], (rows // N_HEADS, N_HEADS, N_BUCKETS)).reshape(rows, N_BUCKETS)

    def tile(width, dist):
        r = lax.broadcasted_iota(jnp.int32, (rows, width), 0)
        c = lax.broadcasted_iota(jnp.int32, (rows, width), 1)
        n = dist((r // N_HEADS) % n_s, c // N_HEADS)
        vals = _lookup(_bucket(jnp.maximum(n, 0)), lambda bb: trow[:, bb:bb + 1])
        return jnp.where((r % N_HEADS == c % N_HEADS) & (n >= 0) & (c < cols), vals, NEG_INF)

    bs_ref[:, 0:cols] = tile(cols, lambda qi, kk: 2 * PAGE_SIZE + qi - kk)
    bs_ref[:, cols:2 * cols] = tile(cols, lambda qi, kk: PAGE_SIZE + qi - kk)
    bs_ref[:, 2 * cols:2 * cols + PAGE_SIZE] = tile(
        PAGE_SIZE, lambda qi, kk: jnp.where(kk < n_s, qi - kk, -1))


def _bias_tiles(table, *, blk, n_s):
    far = _first_far_distance()
    assert blk + 1 >= far and PAGE_SIZE + 1 >= far and n_s * N_HEADS <= PAGE_SIZE
    bias_p = pl.pallas_call(
        functools.partial(_bias_p_body, blk=blk),
        grid=(N_HEADS,),
        in_specs=[pl.BlockSpec(memory_space=pltpu.SMEM)],
        out_specs=pl.BlockSpec((1, 3, blk, blk), lambda h: (h, 0, 0, 0)),
        out_shape=jax.ShapeDtypeStruct((N_HEADS, 3, blk, blk), F32),
        compiler_params=_params("parallel"),
        name="rel_bias_prompt",
    )(table.reshape(-1))
    width = 2 * PAGE_SIZE * N_HEADS + PAGE_SIZE
    bias_s = pl.pallas_call(
        functools.partial(_bias_s_body, n_s=n_s),
        out_shape=jax.ShapeDtypeStruct((2 * n_s * N_HEADS, width), F32),
        compiler_params=pltpu.CompilerParams(vmem_limit_bytes=VMEM_LIMIT),
        name="rel_bias_sample",
    )(table.T)
    return bias_p, bias_s


def _diff_lambda(lam_ref, lam_init):
    lp = lam_ref[...]
    s1 = jnp.sum(lp[0:1] * lp[1:2], axis=-1, keepdims=True)
    s2 = jnp.sum(lp[2:3] * lp[3:4], axis=-1, keepdims=True)
    return jnp.exp(s1) - jnp.exp(s2) + lam_init


def _softmax_step(s, m, l):
    m_new = jnp.maximum(m, jnp.max(s, axis=-1, keepdims=True))
    alpha = jnp.exp(m - m_new)
    p = jnp.exp(s - m_new)
    return p, m_new, alpha, alpha * l + jnp.sum(p, axis=-1, keepdims=True)


_NT = (((1,), (1,)), ((), ()))


def _attn_p_body(q_ref, k_ref, v_ref, bias_ref, lam_ref, sub_ref, o_ref, qz_ref, vt_ref, m_ref, acc_ref, s_ref,
                 *, blk, n_full, tail, lam_init):
    i = pl.program_id(1)

    @pl.when(i == 0)
    def _():
        for hd in range(N_HEADS):
            c0 = hd * V_DIM

            ones_row = (lax.broadcasted_iota(jnp.int32, (SUM_ROWS, blk), 0) == 0).astype(BF16)

            def xpose(jb, carry):
                r0 = pl.multiple_of(jb * blk, blk)
                vt_ref[hd, jb, 0:V_DIM, :] = v_ref[0, pl.ds(r0, blk), c0:c0 + V_DIM].astype(F32).T.astype(BF16)
                vt_ref[hd, jb, V_DIM:V_DIM + SUM_ROWS, :] = ones_row
                return carry

            lax.fori_loop(0, n_full, xpose, 0)
            if tail:
                r0 = n_full * blk
                vt_ref[hd, n_full, 0:V_DIM, 0:tail] = (
                    v_ref[0, r0:r0 + tail, c0:c0 + V_DIM].astype(F32).T.astype(BF16))
                vt_ref[hd, n_full, V_DIM:V_DIM + SUM_ROWS, :] = ones_row

    lane = lax.broadcasted_iota(jnp.int32, (blk, V_DIM), 1)
    for hd in range(N_HEADS):
        q = q_ref[0, :, hd * V_DIM:(hd + 1) * V_DIM]
        qz_ref[hd, 0:blk, :] = jnp.where(lane < HEAD_DIM, q, jnp.zeros_like(q))
        qz_ref[hd, blk:2 * blk, :] = jnp.where(lane >= HEAD_DIM, q, jnp.zeros_like(q))
    m_ref[...] = jnp.full(m_ref.shape, NEG_INF, F32)
    acc_ref[...] = jnp.zeros(acc_ref.shape, F32)

    def all_heads(n, keys, vt, bias):
        def stage(hd):
            b2 = bias(hd)
            s_ref[hd % 2, 0:n, :] = (lax.dot_general(keys(hd), qz_ref[hd], _NT, preferred_element_type=F32)
                                     + jnp.concatenate([b2, b2], axis=1))

        stage(0)
        for hd in range(N_HEADS):
            if hd + 1 < N_HEADS:
                stage(hd + 1)
            s = s_ref[hd % 2, 0:n, :]
            m_old = m_ref[hd]
            m_new = jnp.maximum(m_old, jnp.max(s, axis=0, keepdims=True))
            alpha = jnp.exp(m_old - m_new)
            p = jnp.exp(s - m_new)
            m_ref[hd] = m_new
            acc_ref[hd] = alpha * acc_ref[hd] + jnp.dot(vt(hd), p.astype(BF16), preferred_element_type=F32)

    def body(jb, carry):
        r0 = pl.multiple_of(jb * blk, blk)
        sel = jnp.minimum(i - jb, 2)
        all_heads(blk,
                  lambda hd: k_ref[0, pl.ds(r0, blk), hd * V_DIM:(hd + 1) * V_DIM],
                  lambda hd: vt_ref[hd, jb],
                  lambda hd: bias_ref[hd, sel])
        return carry

    lax.fori_loop(0, jnp.minimum(i + 1, n_full), body, 0)

    if tail:
        @pl.when(i == n_full)
        def _():
            r0 = n_full * blk
            all_heads(tail,
                      lambda hd: k_ref[0, r0:r0 + tail, hd * V_DIM:(hd + 1) * V_DIM],
                      lambda hd: vt_ref[hd, n_full, :, 0:tail],
                      lambda hd: bias_ref[hd, 0, 0:tail, :])

    lam = _diff_lambda(lam_ref, lam_init)
    for hd in range(N_HEADS):
        o_both = acc_ref[hd, 0:V_DIM, :] * (1.0 / acc_ref[hd, V_DIM:V_DIM + 1, :])
        o = o_both[:, 0:blk] - lam * o_both[:, blk:2 * blk]
        o = o * lax.rsqrt(jnp.mean(o * o, axis=0, keepdims=True) + RMS_EPS) * sub_ref[...] * (1.0 - lam_init)
        o_ref[0, :, hd * V_DIM:(hd + 1) * V_DIM] = o.T.astype(BF16)


def _attn_prompt(q, kb, vb, bias_p, lam_p, subln_col, j, *, blk, lam_init):
    b, t, d = q.shape
    n_full, tail = t // blk, t % blk
    assert tail % 16 == 0
    n_blocks = n_full + (1 if tail else 0)
    return pl.pallas_call(
        functools.partial(_attn_p_body, blk=blk, n_full=n_full, tail=tail, lam_init=lam_init),
        grid=(b, n_blocks),
        in_specs=[
            pl.BlockSpec((1, blk, d), lambda bi, qi: (bi, qi, 0)),
            pl.BlockSpec((1, t, d), lambda bi, qi: (bi, 0, 0)),
            pl.BlockSpec((1, t, d), lambda bi, qi: (bi, 0, 0)),
            _const_spec((N_HEADS, 3, blk, blk), (0, 0, 0, 0)),
            _const_spec((None, 4, HEAD_DIM), (j, 0, 0)),
            _const_spec((None, V_DIM, 1), (j, 0, 0)),
        ],
        out_specs=pl.BlockSpec((1, blk, d), lambda bi, qi: (bi, qi, 0)),
        out_shape=jax.ShapeDtypeStruct((b, t, d), BF16),
        scratch_shapes=[
            pltpu.VMEM((N_HEADS, 2 * blk, V_DIM), BF16),
            pltpu.VMEM((N_HEADS, n_blocks, V_DIM + SUM_ROWS, blk), BF16),
            pltpu.VMEM((N_HEADS, 1, 2 * blk), F32),
            pltpu.VMEM((N_HEADS, V_DIM + SUM_ROWS, 2 * blk), F32),
            pltpu.VMEM((2, blk, 2 * blk), F32),
        ],
        compiler_params=_params("arbitrary", "arbitrary"),
        name="attn_prompt",
    )(q, kb, vb, bias_p, lam_p, subln_col)


def _attn_s_body(pt_ref, q_ref, kn_ref, vn_ref, bias_ref, lam_ref, sub_ref, *rest, n_pages, n_steps, n_s, lam_init):
    k_pages = rest[:n_pages]
    v_pages = rest[n_pages:2 * n_pages]
    o_ref, q2_ref, m_ref, l_ref, acc_ref, s_ref = rest[2 * n_pages:]
    step = pl.program_id(1)
    half = n_s * N_HEADS
    cols = PAGE_SIZE * N_HEADS

    @pl.when(step == 0)
    def _():
        q = q_ref[...].reshape(half, V_DIM)
        lane = lax.broadcasted_iota(jnp.int32, (half, V_DIM), 1)
        q2_ref[0:half, :] = jnp.where(lane < HEAD_DIM, q, 0.0).astype(BF16)
        q2_ref[half:2 * half, :] = jnp.where(lane >= HEAD_DIM, q, 0.0).astype(BF16)
        m_ref[...] = jnp.full(m_ref.shape, NEG_INF, F32)
        l_ref[...] = jnp.zeros(l_ref.shape, F32)
        acc_ref[...] = jnp.zeros(acc_ref.shape, F32)

    q2 = q2_ref[...]
    last = step == n_steps - 1

    def scores(keys, bias):
        return lax.dot_general(q2, keys, _NT, preferred_element_type=F32) + bias

    def update(s, values):
        p, m_new, alpha, l_new = _softmax_step(s, m_ref[...], l_ref[...])
        m_ref[...] = m_new
        l_ref[...] = l_new
        acc_ref[...] = alpha * acc_ref[...] + jnp.dot(p.astype(BF16), values, preferred_element_type=F32)

    def stage(r_):
        bias = bias_ref[:, 0:cols]
        if r_ == n_pages - 1:
            bias = jnp.where(last, bias_ref[:, cols:2 * cols], bias)
        s_ref[r_ % 2] = scores(k_pages[r_][...].reshape(cols, V_DIM).astype(BF16), bias)

    stage(0)
    for r_ in range(n_pages):
        if r_ + 1 < n_pages:
            stage(r_ + 1)
        update(s_ref[r_ % 2], v_pages[r_][...].reshape(cols, V_DIM).astype(BF16))

    @pl.when(last)
    def _():
        pad = jnp.zeros((PAGE_SIZE - half, V_DIM), F32)
        kn = jnp.concatenate([kn_ref[...].reshape(half, V_DIM), pad], axis=0).astype(BF16)
        vn = jnp.concatenate([vn_ref[...].reshape(half, V_DIM), pad], axis=0).astype(BF16)
        update(scores(kn, bias_ref[:, 2 * cols:2 * cols + PAGE_SIZE]), vn)
        lam = _diff_lambda(lam_ref, lam_init)
        o_all = acc_ref[...] * (1.0 / l_ref[...])
        o = o_all[0:half] - lam * o_all[half:2 * half]
        o_ref[...] = (_rms(o, sub_ref[...]) * (1.0 - lam_init)).reshape(o_ref.shape)


def _attn_sample(q, kn, vn, cache_k, cache_v, page_table, bias_s, lam_p, subln, j, *, n_pages, lam_init):
    db, n_s = q.shape[:2]
    total_pages = page_table.shape[1]
    n_steps = total_pages // n_pages
    assert n_steps * n_pages == total_pages
    rows = 2 * n_s * N_HEADS
    new_kv = pl.BlockSpec((1, n_s, 1, N_HEADS, V_DIM), lambda b, s, pt: (b, 0, 0, 0, 0))
    q_spec = pl.BlockSpec((1, n_s, N_HEADS, V_DIM), lambda b, s, pt: (b, 0, 0, 0))

    def page_spec(r):
        return pl.BlockSpec((1, PAGE_SIZE, 1, N_HEADS, V_DIM),
                            lambda b, s, pt: (pt[b, s * n_pages + r], 0, j, 0, 0))

    grid_spec = pltpu.PrefetchScalarGridSpec(
        num_scalar_prefetch=1,
        grid=(db, n_steps),
        in_specs=[
            q_spec, new_kv, new_kv,
            pl.BlockSpec(bias_s.shape, lambda b, s, pt: (0, 0)),
            pl.BlockSpec((None, 4, HEAD_DIM), lambda b, s, pt: (j, 0, 0)),
            pl.BlockSpec((None, 1, V_DIM), lambda b, s, pt: (j, 0, 0)),
        ] + [page_spec(r) for r in range(n_pages)] * 2,
        out_specs=q_spec,
        scratch_shapes=[pltpu.VMEM((rows, V_DIM), BF16), pltpu.VMEM((rows, 1), F32), pltpu.VMEM((rows, 1), F32),
                        pltpu.VMEM((rows, V_DIM), F32), pltpu.VMEM((2, rows, PAGE_SIZE * N_HEADS), F32)],
    )
    return pl.pallas_call(
        functools.partial(_attn_s_body, n_pages=n_pages, n_steps=n_steps, n_s=n_s, lam_init=lam_init),
        grid_spec=grid_spec,
        out_shape=jax.ShapeDtypeStruct((db, n_s, N_HEADS, V_DIM), F32),
        compiler_params=_params("parallel", "arbitrary"),
        name="attn_sample",
    )(page_table, q, kn, vn, bias_s, lam_p, subln, *([cache_k] * n_pages), *([cache_v] * n_pages))


def _outproj_body(a_ref, x_ref, w_ref, g1_ref, o_ref):
    m = jnp.dot(a_ref[...].astype(BF16), w_ref[...], preferred_element_type=F32)
    o_ref[...] = x_ref[...] + _rms(m, g1_ref[...])


def _outproj(a, x, wo, mix_norm, layer, j, *, tm):
    n, d = x.shape
    row = pl.BlockSpec((tm, d), lambda i: (i, 0))
    return pl.pallas_call(
        _outproj_body,
        grid=(pl.cdiv(n, tm),),
        in_specs=[row, row, _const_spec((None, d, d), (j, 0, 0)), _const_spec((None, 1, d), (layer * 2 + 1, 0, 0))],
        out_specs=row,
        out_shape=jax.ShapeDtypeStruct((n, d), F32),
        compiler_params=_params("parallel"),
        name="attn_outproj",
    )(a, x, wo, mix_norm)


def _final_p_body(x_ref, g_ref, o_ref, *, skip, rows):
    t = pl.program_id(1)
    r0 = pl.multiple_of(skip + t * rows, 8)
    o_ref[0] = _rms(x_ref[0, pl.ds(r0, rows), :], g_ref[...])


def _final_prompt(x, g, *, skip, rows):
    b, t, d = x.shape
    n_t = (t - skip) // rows
    return pl.pallas_call(
        functools.partial(_final_p_body, skip=skip, rows=rows),
        grid=(b, n_t),
        in_specs=[pl.BlockSpec((1, t, d), lambda bi, ti: (bi, 0, 0)), _const_spec((1, d), (0, 0))],
        out_specs=pl.BlockSpec((1, rows, d), lambda bi, ti: (bi, ti, 0)),
        out_shape=jax.ShapeDtypeStruct((b, t - skip, d), F32),
        compiler_params=_params("parallel", "arbitrary"),
        name="final_norm_prompt",
    )(x, g)


def _final_s_body(x_ref, g_ref, o_ref):
    o_ref[...] = _rms(x_ref[...], g_ref[...])


def _final_sample(x, g, *, tm):
    n, d = x.shape
    row = pl.BlockSpec((tm, d), lambda i: (i, 0))
    return pl.pallas_call(
        _final_s_body,
        grid=(pl.cdiv(n, tm),),
        in_specs=[row, _const_spec((1, d), (0, 0))],
        out_specs=row,
        out_shape=jax.ShapeDtypeStruct((n, d), F32),
        compiler_params=_params("parallel"),
        name="final_norm_sample",
    )(x, g)


def _pick_tile(n, cap, align=8):
    best = None
    for c in range(align, min(n, cap) + 1, align):
        if n % c == 0:
            best = c
    assert best is not None, (n, cap, align)
    return best


def kernel(x_prompt, x_sample, state_pool, state_conv, cache_k, cache_v, page_table, meta_tokens, rel_bias_table, ffn_norm, ffn_wg, ffn_wu, ffn_wd, mix_norm, pool_w, pool_scale, attn_wqkv, attn_wo, attn_lambda, attn_subln, conv_w1, conv_b1, conv_wdw, conv_bdw, conv_ln_g, conv_ln_b, conv_w2, conv_b2, final_norm):
    b, seq, d = x_prompt.shape
    db, n_s, _ = x_sample.shape
    depth = ffn_wg.shape[0]
    t = seq + N_META
    past = page_table.shape[1] * PAGE_SIZE

    wg, wu, wd = ffn_wg.astype(BF16), ffn_wu.astype(BF16), ffn_wd.astype(BF16)
    ffn_norm3 = ffn_norm.reshape(-1, 1, d)
    mix_norm3 = mix_norm.reshape(-1, 1, d)
    pool_w_b = pool_w.astype(BF16)
    pool_scale3 = pool_scale.reshape(-1, 1, d)
    wqkv_b, wo_b = attn_wqkv.astype(BF16), attn_wo.astype(BF16)
    subln3 = attn_subln.reshape(-1, 1, V_DIM)
    cw = (conv_w1.astype(BF16), conv_b1.reshape(-1, 1, 2 * d), conv_wdw, conv_bdw.reshape(-1, 1, d),
          conv_ln_g.reshape(-1, 1, d), conv_ln_b.reshape(-1, 1, d), conv_w2.astype(BF16), conv_b2.reshape(-1, 1, d))
    state_pool2 = state_pool.reshape(state_pool.shape[0], db, POOL_BUF * d)
    state_conv2 = state_conv.reshape(state_conv.shape[0], db, CONV_BUF * d)

    meta = jnp.broadcast_to(meta_tokens[None].astype(x_prompt.dtype), (b, N_META, d))
    xp = jnp.concatenate([meta, x_prompt], axis=1).reshape(b * t, d)
    xs = x_sample.reshape(db * n_s, d)

    tm_p = _pick_tile(b * t, 768, 16)
    tm_s = min(512, db * n_s)
    ts_pool = _pick_tile(t, 1032)
    ts_conv = _pick_tile(t, 344)
    bt = _pick_tile(db, 32)

    bias_p = bias_s = None
    pool_p, pool_s, conv_p, conv_s = [], [], [], []
    k_p = v_p = k_s = v_s = None
    for i in range(depth):
        kind, j = i % N_MIXERS, i // N_MIXERS
        xp = _ffn(xp, ffn_norm3, wg, wu, wd, i, 0, tm=tm_p)
        xs = _ffn(xs, ffn_norm3, wg, wu, wd, i, 0, tm=tm_s)
        if kind == 0:
            xp3, st = _pool_prompt(xp.reshape(b, t, d), mix_norm3, pool_w_b, pool_scale3, i, j, ts=ts_pool)
            xp = xp3.reshape(b * t, d)
            pool_p.append(st)
            xs2, st = _pool_sample(xs.reshape(db, n_s * d), state_pool2, mix_norm3, pool_w_b, pool_scale3, i, j,
                                   n_s=n_s, past=past, bt=bt)
            xs = xs2.reshape(db * n_s, d)
            pool_s.append(st.reshape(db, POOL_BUF, d))
        elif kind == 1:
            assert j == 0
            lam_init = _lambda_init(i)
            if bias_p is None:
                bias_p, bias_s = _bias_tiles(rel_bias_table, blk=ATTN_BLOCK, n_s=n_s)
            q, k_p, v_p, kb, vb = _qkv(xp, mix_norm3, wqkv_b, i, j, tm=tm_p)
            a = _attn_prompt(q.reshape(b, t, d), kb.reshape(b, t, d), vb.reshape(b, t, d), bias_p, attn_lambda,
                             attn_subln.reshape(-1, V_DIM, 1), j, blk=ATTN_BLOCK, lam_init=lam_init)
            xp = _outproj(a.reshape(b * t, d), xp, wo_b, mix_norm3, i, j, tm=tm_p)
            q, k_s, v_s, _, _ = _qkv(xs, mix_norm3, wqkv_b, i, j, tm=tm_s)
            k_s = k_s.reshape(db, n_s, 1, N_HEADS, V_DIM)
            v_s = v_s.reshape(db, n_s, 1, N_HEADS, V_DIM)
            a = _attn_sample(q.astype(F32).reshape(db, n_s, N_HEADS, V_DIM), k_s, v_s, cache_k, cache_v,
                             page_table, bias_s, attn_lambda, subln3, j, n_pages=8, lam_init=lam_init)
            xs = _outproj(a.reshape(db * n_s, d), xs, wo_b, mix_norm3, i, j, tm=tm_s)
        else:
            xp3, st = _conv_prompt(xp.reshape(b, t, d), mix_norm3, cw, i, j, ts=ts_conv)
            xp = xp3.reshape(b * t, d)
            conv_p.append(st)
            xs2, st = _conv_sample(xs.reshape(db, n_s * d), state_conv2, mix_norm3, cw, i, j, n_s=n_s, bt=bt)
            xs = xs2.reshape(db * n_s, d)
            conv_s.append(st.reshape(db, CONV_BUF, d))
        xp = _ffn(xp, ffn_norm3, wg, wu, wd, i, 1, tm=tm_p)
        xs = _ffn(xs, ffn_norm3, wg, wu, wd, i, 1, tm=tm_s)

    g = final_norm.reshape(1, d)
    y_prompt = _final_prompt(xp.reshape(b, t, d), g, skip=N_META, rows=_pick_tile(seq, 1024))
    y_sample = _final_sample(xs, g, tm=tm_s).reshape(db, n_s, d)
    kv_shape_p = (b, t, 1, N_HEADS, V_DIM)
    return (y_prompt, y_sample, jnp.stack(pool_p, axis=0), jnp.stack(pool_s, axis=0),
            jnp.stack(conv_p, axis=0), jnp.stack(conv_s, axis=0),
            k_p.reshape(kv_shape_p), v_p.reshape(kv_shape_p), k_s, v_s)
```

```python
import functools
import math

import jax
import jax.numpy as jnp
from jax import lax
from jax.experimental import pallas as pl
from jax.experimental.pallas import tpu as pltpu

F32 = jnp.float32
BF16 = jnp.bfloat16

N_MIXERS = 3
N_META = 16
N_HEADS = 8
HEAD_DIM = 64
V_DIM = 2 * HEAD_DIM
N_BUCKETS = 32
MAX_EXACT = N_BUCKETS // 2
MAX_DISTANCE = 128
POOL_WINDOWS = (2, 4, 8, 16)
POOL_BUF = max(POOL_WINDOWS) - 1
CONV_WIDTH = 31
CONV_BUF = CONV_WIDTH - 1
PAGE_SIZE = 128
RMS_EPS = 1e-6
LN_EPS = 1e-5
NEG_INF = -1e30
LOG2E = math.log2(math.e)
SUBLANES = 8
LANES = 128
MXU_TILE = 256
SUM_ROWS = 16
ATTN_BLOCK = 256
POOL_HALO = 16
CONV_HALO = 32
VMEM_LIMIT = 56 * 1024 * 1024


def _lambda_init(layer_idx):
    return 0.8 - 0.6 * math.exp(-0.3 * layer_idx)


def _first_far_distance():
    n = MAX_EXACT
    while MAX_EXACT + int(math.log(n / MAX_EXACT) / math.log(MAX_DISTANCE / MAX_EXACT) * (N_BUCKETS - MAX_EXACT)) < N_BUCKETS - 1:
        n += 1
    return n


def _params(*sem):
    return pltpu.CompilerParams(dimension_semantics=sem, vmem_limit_bytes=VMEM_LIMIT)


def _rms(x, g):
    return x * lax.rsqrt(jnp.mean(x * x, axis=-1, keepdims=True) + RMS_EPS) * g


def _silu(x):
    return x * jax.nn.sigmoid(x)


def _const_spec(shape, index, single=False):
    if single:
        return pl.BlockSpec(shape, lambda *_: index, pipeline_mode=pl.Buffered(1))
    return pl.BlockSpec(shape, lambda *_: index)


def _ffn_chunks(ff, n_chunks):
    tiles = -(-ff // MXU_TILE)
    edges = [min(ff, MXU_TILE * (-(-tiles * c // n_chunks))) for c in range(n_chunks + 1)]
    return [(lo, hi - lo) for lo, hi in zip(edges[:-1], edges[1:]) if hi > lo]


def _ffn_body(xp_ref, xs_ref, n0_ref, n1_ref, wg_ref, wu_ref, wd_ref, op_ref, os_ref, *, n_chunks, n_p):
    i = pl.program_id(0)

    def apply(x_ref, o_ref):
        x = x_ref[...]
        h = _rms(x, n0_ref[...]).astype(BF16)
        acc = jnp.zeros(x.shape, F32)
        for lo, width in _ffn_chunks(wg_ref.shape[1], n_chunks):
            sl = pl.ds(lo, width)
            g = jnp.dot(h, wg_ref[:, sl], preferred_element_type=F32)
            u = jnp.dot(h, wu_ref[:, sl], preferred_element_type=F32)
            a = (_silu(g) * u).astype(BF16)
            acc = acc + jnp.dot(a, wd_ref[sl, :], preferred_element_type=F32)
        o_ref[...] = x + 0.5 * _rms(acc, n1_ref[...])

    @pl.when(i == 0)
    def _():
        os_ref[...] = jnp.zeros(os_ref.shape, F32)

    @pl.when(i < n_p)
    def _():
        apply(xp_ref, op_ref)

    @pl.when(i >= n_p)
    def _():
        apply(xs_ref, os_ref)


def _ffn(xp, xs, norms, wg, wu, wd, layer, f, *, tm, n_chunks=2):
    (n_rows_p, d), n_rows_s = xp.shape, xs.shape[0]
    ff = wg.shape[-1]
    nidx = (layer * 2 + f) * 2
    tm_s = min(tm, n_rows_s)
    n_p, n_s = pl.cdiv(n_rows_p, tm), pl.cdiv(n_rows_s, tm_s)
    p_spec = pl.BlockSpec((tm, d), lambda i: (jnp.minimum(i, n_p - 1), 0))
    s_spec = pl.BlockSpec((tm_s, d), lambda i: (jnp.maximum(i - n_p, 0), 0))
    return pl.pallas_call(
        functools.partial(_ffn_body, n_chunks=n_chunks, n_p=n_p),
        grid=(n_p + n_s,),
        in_specs=[
            p_spec, s_spec,
            _const_spec((None, 1, d), (nidx, 0, 0)),
            _const_spec((None, 1, d), (nidx + 1, 0, 0)),
            _const_spec((None, None, d, ff), (layer, f, 0, 0), single=True),
            _const_spec((None, None, d, ff), (layer, f, 0, 0), single=True),
            _const_spec((None, None, ff, d), (layer, f, 0, 0), single=True),
        ],
        out_specs=[p_spec, s_spec],
        out_shape=[jax.ShapeDtypeStruct((n_rows_p, d), F32), jax.ShapeDtypeStruct((n_rows_s, d), F32)],
        compiler_params=_params("arbitrary"),
        name="ffn",
    )(xp, xs, norms, norms, wg, wu, wd)


def _pool_p_body(x_ref, g0_ref, g1_ref, w_ref, sc_ref, o_ref, st_ref, cat_ref, *, ts, n_t):
    t = pl.program_id(1)
    d = x_ref.shape[-1]
    gd = d // len(POOL_WINDOWS)
    x = x_ref[0]
    h = _rms(x, g0_ref[...])

    @pl.when(t == 0)
    def _():
        cat_ref[0:POOL_HALO, :] = jnp.zeros((POOL_HALO, d), F32)

    cat_ref[POOL_HALO:POOL_HALO + ts, :] = h
    pos = t * ts + lax.broadcasted_iota(jnp.int32, (ts, 1), 0)
    outs = []
    for g, w in enumerate(POOL_WINDOWS):
        c0 = g * gd
        acc = cat_ref[POOL_HALO:POOL_HALO + ts, c0:c0 + gd]
        for j in range(1, w):
            acc = acc + cat_ref[POOL_HALO - j:POOL_HALO - j + ts, c0:c0 + gd]
        inv_cnt = 1.0 / jnp.minimum(w, pos + 1).astype(F32)
        pooled = (acc * inv_cnt - h[:, c0:c0 + gd]).astype(BF16)
        outs.append(jnp.dot(pooled, w_ref[g], preferred_element_type=F32))
    m = jnp.concatenate(outs, axis=-1) * sc_ref[...]
    o_ref[0] = x + _rms(m, g1_ref[...])

    @pl.when(t == n_t - 1)
    def _():
        st_ref[0] = cat_ref[POOL_HALO + ts - POOL_BUF:POOL_HALO + ts, :]

    cat_ref[0:POOL_HALO, :] = cat_ref[ts:ts + POOL_HALO, :]


def _pool_prompt(x, mix_norm, pool_w, pool_scale, layer, j, *, ts):
    b, t, d = x.shape
    n_t = t // ts
    assert n_t * ts == t and ts % 8 == 0 and ts >= POOL_HALO
    g = len(POOL_WINDOWS)
    return pl.pallas_call(
        functools.partial(_pool_p_body, ts=ts, n_t=n_t),
        grid=(b, n_t),
        in_specs=[
            pl.BlockSpec((1, ts, d), lambda bi, ti: (bi, ti, 0)),
            _const_spec((None, 1, d), (layer * 2, 0, 0)),
            _const_spec((None, 1, d), (layer * 2 + 1, 0, 0)),
            _const_spec((None, g, d // g, d // g), (j, 0, 0, 0)),
            _const_spec((None, 1, d), (j, 0, 0)),
        ],
        out_specs=[
            pl.BlockSpec((1, ts, d), lambda bi, ti: (bi, ti, 0)),
            pl.BlockSpec((1, POOL_BUF, d), lambda bi, ti: (bi, 0, 0)),
        ],
        out_shape=[jax.ShapeDtypeStruct((b, t, d), F32), jax.ShapeDtypeStruct((b, POOL_BUF, d), F32)],
        scratch_shapes=[pltpu.VMEM((POOL_HALO + ts, d), F32)],
        compiler_params=_params("parallel", "arbitrary"),
        name="pool_prompt",
    )(x, mix_norm, mix_norm, pool_w, pool_scale)


def _pool_s_body(x_ref, st_ref, g0_ref, g1_ref, w_ref, sc_ref, o_ref, so_ref, *, n_s, past):
    d = g0_ref.shape[-1]
    gd = d // len(POOL_WINDOWS)
    bt = x_ref.shape[0]
    xs = [x_ref[:, s * d:(s + 1) * d] for s in range(n_s)]
    hs = [_rms(x, g0_ref[...]) for x in xs]

    def cat(idx, c0, width):
        if idx < POOL_BUF:
            return st_ref[:, idx * d + c0:idx * d + c0 + width]
        return hs[idx - POOL_BUF][:, c0:c0 + width]

    outs = []
    for g, w in enumerate(POOL_WINDOWS):
        c0 = g * gd
        rows = []
        for s in range(n_s):
            acc = cat(POOL_BUF + s, c0, gd)
            for jj in range(1, w):
                acc = acc + cat(POOL_BUF + s - jj, c0, gd)
            rows.append(acc * (1.0 / min(w, past + s + 1)) - hs[s][:, c0:c0 + gd])
        pooled = jnp.concatenate(rows, axis=0).astype(BF16)
        outs.append(jnp.dot(pooled, w_ref[g], preferred_element_type=F32))
    for s in range(n_s):
        m = jnp.concatenate([o[s * bt:(s + 1) * bt] for o in outs], axis=-1) * sc_ref[...]
        o_ref[:, s * d:(s + 1) * d] = xs[s] + _rms(m, g1_ref[...])
    for jj in range(POOL_BUF):
        so_ref[:, jj * d:(jj + 1) * d] = cat(n_s + jj, 0, d)


def _pool_sample(x, state, mix_norm, pool_w, pool_scale, layer, j, *, n_s, past, bt):
    db = x.shape[0]
    d = mix_norm.shape[-1]
    g = len(POOL_WINDOWS)
    return pl.pallas_call(
        functools.partial(_pool_s_body, n_s=n_s, past=past),
        grid=(db // bt,),
        in_specs=[
            pl.BlockSpec((bt, n_s * d), lambda i: (i, 0)),
            pl.BlockSpec((None, bt, POOL_BUF * d), lambda i: (j, i, 0)),
            _const_spec((None, 1, d), (layer * 2, 0, 0)),
            _const_spec((None, 1, d), (layer * 2 + 1, 0, 0)),
            _const_spec((None, g, d // g, d // g), (j, 0, 0, 0)),
            _const_spec((None, 1, d), (j, 0, 0)),
        ],
        out_specs=[
            pl.BlockSpec((bt, n_s * d), lambda i: (i, 0)),
            pl.BlockSpec((bt, POOL_BUF * d), lambda i: (i, 0)),
        ],
        out_shape=[jax.ShapeDtypeStruct((db, n_s * d), F32), jax.ShapeDtypeStruct((db, POOL_BUF * d), F32)],
        compiler_params=_params("parallel"),
        name="pool_sample",
    )(x, state, mix_norm, mix_norm, pool_w, pool_scale)


def _layer_norm(y, g, b):
    mu = jnp.mean(y, axis=-1, keepdims=True)
    yc = y - mu
    var = jnp.mean(yc * yc, axis=-1, keepdims=True)
    return yc * lax.rsqrt(var + LN_EPS) * g + b


def _conv_p_body(x_ref, g0_ref, g1_ref, w1_ref, b1_ref, wdw_ref, bdw_ref, lng_ref, lnb_ref, w2_ref, b2_ref,
                 o_ref, st_ref, cat_ref, y_ref, *, ts, n_t):
    t = pl.program_id(1)
    d = x_ref.shape[-1]
    x = x_ref[0]
    h = _rms(x, g0_ref[...]).astype(BF16)
    glu = jnp.dot(h, w1_ref[...], preferred_element_type=F32) + b1_ref[...]
    u = glu[:, :d] * jax.nn.sigmoid(glu[:, d:])

    @pl.when(t == 0)
    def _():
        cat_ref[0:CONV_HALO, :] = jnp.zeros((CONV_HALO, d), F32)
        cat_ref[CONV_HALO + ts:CONV_HALO + ts + SUBLANES, :] = jnp.zeros((SUBLANES, d), F32)

    cat_ref[CONV_HALO:CONV_HALO + ts, :] = u
    off = CONV_HALO - CONV_BUF
    zrows = ts + SUBLANES
    for c0 in range(0, d, LANES):
        y = None
        for r in range(SUBLANES):
            z = None
            for a in range((CONV_WIDTH + off) // SUBLANES + 1):
                k = SUBLANES * a + r - off
                if 0 <= k < CONV_WIDTH:
                    term = cat_ref[SUBLANES * a:SUBLANES * a + zrows, c0:c0 + LANES] * wdw_ref[k:k + 1, c0:c0 + LANES]
                    z = term if z is None else z + term
            zs = z[r:r + ts]
            y = zs if y is None else y + zs
        y_ref[:, c0:c0 + LANES] = y + bdw_ref[:, c0:c0 + LANES]
    a = _silu(_layer_norm(y_ref[...], lng_ref[...], lnb_ref[...])).astype(BF16)
    m = jnp.dot(a, w2_ref[...], preferred_element_type=F32) + b2_ref[...]
    o_ref[0] = x + _rms(m, g1_ref[...])

    @pl.when(t == n_t - 1)
    def _():
        st_ref[0] = cat_ref[CONV_HALO + ts - CONV_BUF:CONV_HALO + ts, :]

    cat_ref[0:CONV_HALO, :] = cat_ref[ts:ts + CONV_HALO, :]


def _conv_specs(layer, j, d):
    return [
        _const_spec((None, 1, d), (layer * 2, 0, 0)),
        _const_spec((None, 1, d), (layer * 2 + 1, 0, 0)),
        _const_spec((None, d, 2 * d), (j, 0, 0)),
        _const_spec((None, 1, 2 * d), (j, 0, 0)),
        _const_spec((None, CONV_WIDTH, d), (j, 0, 0)),
        _const_spec((None, 1, d), (j, 0, 0)),
        _const_spec((None, 1, d), (j, 0, 0)),
        _const_spec((None, 1, d), (j, 0, 0)),
        _const_spec((None, d, d), (j, 0, 0)),
        _const_spec((None, 1, d), (j, 0, 0)),
    ]


def _conv_prompt(x, mix_norm, cw, layer, j, *, ts):
    b, t, d = x.shape
    n_t = t // ts
    assert n_t * ts == t and ts % 8 == 0 and ts >= CONV_HALO
    return pl.pallas_call(
        functools.partial(_conv_p_body, ts=ts, n_t=n_t),
        grid=(b, n_t),
        in_specs=[pl.BlockSpec((1, ts, d), lambda bi, ti: (bi, ti, 0))] + _conv_specs(layer, j, d),
        out_specs=[
            pl.BlockSpec((1, ts, d), lambda bi, ti: (bi, ti, 0)),
            pl.BlockSpec((1, CONV_BUF, d), lambda bi, ti: (bi, 0, 0)),
        ],
        out_shape=[jax.ShapeDtypeStruct((b, t, d), F32), jax.ShapeDtypeStruct((b, CONV_BUF, d), F32)],
        scratch_shapes=[pltpu.VMEM((CONV_HALO + ts + SUBLANES, d), F32), pltpu.VMEM((ts, d), F32)],
        compiler_params=_params("parallel", "arbitrary"),
        name="conv_prompt",
    )(x, mix_norm, mix_norm, *cw)


def _conv_s_body(x_ref, st_ref, g0_ref, g1_ref, w1_ref, b1_ref, wdw_ref, bdw_ref, lng_ref, lnb_ref, w2_ref, b2_ref,
                 o_ref, so_ref, *, n_s):
    d = g0_ref.shape[-1]
    bt = x_ref.shape[0]
    xs = [x_ref[:, s * d:(s + 1) * d] for s in range(n_s)]
    h = jnp.concatenate([_rms(x, g0_ref[...]) for x in xs], axis=0).astype(BF16)
    z = jnp.dot(h, w1_ref[...], preferred_element_type=F32) + b1_ref[...]
    u = z[:, :d] * jax.nn.sigmoid(z[:, d:])
    us = [u[s * bt:(s + 1) * bt] for s in range(n_s)]

    def cat(idx):
        if idx < CONV_BUF:
            return st_ref[:, idx * d:(idx + 1) * d]
        return us[idx - CONV_BUF]

    acts = []
    for s in range(n_s):
        y = cat(s) * wdw_ref[0:1, :] + bdw_ref[...]
        for k in range(1, CONV_WIDTH):
            y = y + cat(s + k) * wdw_ref[k:k + 1, :]
        acts.append(_silu(_layer_norm(y, lng_ref[...], lnb_ref[...])))
    a = jnp.concatenate(acts, axis=0).astype(BF16)
    m = jnp.dot(a, w2_ref[...], preferred_element_type=F32) + b2_ref[...]
    for s in range(n_s):
        o_ref[:, s * d:(s + 1) * d] = xs[s] + _rms(m[s * bt:(s + 1) * bt], g1_ref[...])
    for jj in range(CONV_BUF):
        so_ref[:, jj * d:(jj + 1) * d] = cat(n_s + jj)


def _conv_sample(x, state, mix_norm, cw, layer, j, *, n_s, bt):
    db = x.shape[0]
    d = mix_norm.shape[-1]
    return pl.pallas_call(
        functools.partial(_conv_s_body, n_s=n_s),
        grid=(db // bt,),
        in_specs=[
            pl.BlockSpec((bt, n_s * d), lambda i: (i, 0)),
            pl.BlockSpec((None, bt, CONV_BUF * d), lambda i: (j, i, 0)),
        ] + _conv_specs(layer, j, d),
        out_specs=[
            pl.BlockSpec((bt, n_s * d), lambda i: (i, 0)),
            pl.BlockSpec((bt, CONV_BUF * d), lambda i: (i, 0)),
        ],
        out_shape=[jax.ShapeDtypeStruct((db, n_s * d), F32), jax.ShapeDtypeStruct((db, CONV_BUF * d), F32)],
        compiler_params=_params("parallel"),
        name="conv_sample",
    )(x, state, mix_norm, mix_norm, *cw)


def _qkv_body(x_ref, g0_ref, w_ref, q_ref, k_ref, v_ref, kb_ref, vb_ref):
    d = x_ref.shape[-1]
    h = _rms(x_ref[...], g0_ref[...]).astype(BF16)
    qkv = jnp.dot(h, w_ref[...], preferred_element_type=F32)
    q_ref[...] = (qkv[:, :d] * (HEAD_DIM ** -0.5 * LOG2E)).astype(BF16)
    k = qkv[:, d:2 * d]
    v = qkv[:, 2 * d:]
    k_ref[...] = k
    v_ref[...] = v
    kb_ref[...] = k.astype(BF16)
    vb_ref[...] = v.astype(BF16)


def _qkv(x, mix_norm, wqkv, layer, j, *, tm):
    n, d = x.shape
    row = pl.BlockSpec((tm, d), lambda i: (i, 0))
    return pl.pallas_call(
        _qkv_body,
        grid=(pl.cdiv(n, tm),),
        in_specs=[row, _const_spec((None, 1, d), (layer * 2, 0, 0)), _const_spec((None, d, 3 * d), (j, 0, 0))],
        out_specs=[row] * 5,
        out_shape=[jax.ShapeDtypeStruct((n, d), BF16), jax.ShapeDtypeStruct((n, d), F32),
                   jax.ShapeDtypeStruct((n, d), F32), jax.ShapeDtypeStruct((n, d), BF16),
                   jax.ShapeDtypeStruct((n, d), BF16)],
        compiler_params=_params("parallel"),
        name="qkv_proj",
    )(x, mix_norm, wqkv)


def _bucket(n):
    nf = jnp.maximum(n, 1).astype(F32)
    large = MAX_EXACT + (jnp.log(nf / MAX_EXACT) / math.log(MAX_DISTANCE / MAX_EXACT)
                         * (N_BUCKETS - MAX_EXACT)).astype(jnp.int32)
    large = jnp.minimum(large, N_BUCKETS - 1)
    return jnp.where(n < MAX_EXACT, n, large)


def _lookup(bucket, entry):
    out = jnp.zeros(bucket.shape, F32)
    for b in range(N_BUCKETS):
        out = jnp.where(bucket == b, entry(b), out)
    return out


def _bias_p_body(table_ref, bp_ref, *, blk):
    head = pl.program_id(0)
    a = lax.broadcasted_iota(jnp.int32, (blk, blk), 0)
    b = lax.broadcasted_iota(jnp.int32, (blk, blk), 1)
    for sel in range(3):
        n = sel * blk + b - a
        vals = _lookup(_bucket(jnp.maximum(n, 0)), lambda bb: table_ref[bb * N_HEADS + head])
        bp_ref[0, sel] = jnp.where(n >= 0, vals * LOG2E, NEG_INF)


def _bias_s_body(tt_ref, bs_ref, *, n_s):
    rows = 2 * n_s * N_HEADS
    cols = PAGE_SIZE * N_HEADS
    tt = tt_ref[...]
    trow = jnp.broadcast_to(tt[None], (rows // N_HEADS, N_HEADS, N_BUCKETS)).reshape(rows, N_BUCKETS)

    def tile(width, dist):
        r = lax.broadcasted_iota(jnp.int32, (rows, width), 0)
        c = lax.broadcasted_iota(jnp.int32, (rows, width), 1)
        n = dist((r // N_HEADS) % n_s, c // N_HEADS)
        vals = _lookup(_bucket(jnp.maximum(n, 0)), lambda bb: trow[:, bb:bb + 1])
        return jnp.where((r % N_HEADS == c % N_HEADS) & (n >= 0) & (c < cols), vals * LOG2E, NEG_INF)

    bs_ref[:, 0:cols] = tile(cols, lambda qi, kk: 2 * PAGE_SIZE + qi - kk)
    bs_ref[:, cols:2 * cols] = tile(cols, lambda qi, kk: PAGE_SIZE + qi - kk)
    bs_ref[:, 2 * cols:2 * cols + PAGE_SIZE] = tile(
        PAGE_SIZE, lambda qi, kk: jnp.where(kk < n_s, qi - kk, -1))


def _bias_tiles(table, *, blk, n_s):
    far = _first_far_distance()
    assert blk + 1 >= far and PAGE_SIZE + 1 >= far and n_s * N_HEADS <= PAGE_SIZE
    bias_p = pl.pallas_call(
        functools.partial(_bias_p_body, blk=blk),
        grid=(N_HEADS,),
        in_specs=[pl.BlockSpec(memory_space=pltpu.SMEM)],
        out_specs=pl.BlockSpec((1, 3, blk, blk), lambda h: (h, 0, 0, 0)),
        out_shape=jax.ShapeDtypeStruct((N_HEADS, 3, blk, blk), F32),
        compiler_params=_params("parallel"),
        name="rel_bias_prompt",
    )(table.reshape(-1))
    width = 2 * PAGE_SIZE * N_HEADS + PAGE_SIZE
    bias_s = pl.pallas_call(
        functools.partial(_bias_s_body, n_s=n_s),
        out_shape=jax.ShapeDtypeStruct((2 * n_s * N_HEADS, width), F32),
        compiler_params=pltpu.CompilerParams(vmem_limit_bytes=VMEM_LIMIT),
        name="rel_bias_sample",
    )(table.T)
    return bias_p, bias_s


def _diff_lambda(lam_ref, lam_init):
    lp = lam_ref[...]
    s1 = jnp.sum(lp[0:1] * lp[1:2], axis=-1, keepdims=True)
    s2 = jnp.sum(lp[2:3] * lp[3:4], axis=-1, keepdims=True)
    return jnp.exp(s1) - jnp.exp(s2) + lam_init


def _softmax_step(s, m, l):
    m_new = jnp.maximum(m, jnp.max(s, axis=-1, keepdims=True))
    alpha = jnp.exp2(m - m_new)
    p = jnp.exp2(s - m_new)
    return p, m_new, alpha, alpha * l + jnp.sum(p, axis=-1, keepdims=True)


_NT = (((1,), (1,)), ((), ()))


def _attn_p_body(q_ref, k_ref, v_ref, x_ref, bias_ref, lam_ref, sub_ref, wo_ref, g1_ref, o_ref,
                 qz_ref, vt_ref, m_ref, acc_ref, s_ref, a_ref, *, blk, n_full, tail, lam_init):
    i = pl.program_id(1)

    @pl.when(i == 0)
    def _():
        for hd in range(N_HEADS):
            c0 = hd * V_DIM

            ones_row = (lax.broadcasted_iota(jnp.int32, (SUM_ROWS, blk), 0) == 0).astype(BF16)

            def xpose(jb, carry):
                r0 = pl.multiple_of(jb * blk, blk)
                vt_ref[hd, jb, 0:V_DIM, :] = v_ref[0, pl.ds(r0, blk), c0:c0 + V_DIM].astype(F32).T.astype(BF16)
                vt_ref[hd, jb, V_DIM:V_DIM + SUM_ROWS, :] = ones_row
                return carry

            lax.fori_loop(0, n_full, xpose, 0)
            if tail:
                r0 = n_full * blk
                vt_ref[hd, n_full, 0:V_DIM, 0:tail] = (
                    v_ref[0, r0:r0 + tail, c0:c0 + V_DIM].astype(F32).T.astype(BF16))
                vt_ref[hd, n_full, V_DIM:V_DIM + SUM_ROWS, :] = ones_row

    def run(wq, n_loop, tail_keys):
        lane = lax.broadcasted_iota(jnp.int32, (wq, V_DIM), 1)
        for hd in range(N_HEADS):
            q = q_ref[0, 0:wq, hd * V_DIM:(hd + 1) * V_DIM]
            qz_ref[hd, 0:wq, :] = jnp.where(lane < HEAD_DIM, q, jnp.zeros_like(q))
            qz_ref[hd, wq:2 * wq, :] = jnp.where(lane >= HEAD_DIM, q, jnp.zeros_like(q))
        m_ref[:, :, 0:2 * wq] = jnp.full((N_HEADS, 1, 2 * wq), NEG_INF, F32)
        acc_ref[:, :, 0:2 * wq] = jnp.zeros((N_HEADS, V_DIM + SUM_ROWS, 2 * wq), F32)

        def all_heads(n, keys, vt, bias):
            def stage(hd):
                b2 = bias(hd)
                s_ref[hd % 2, 0:n, 0:2 * wq] = (
                    lax.dot_general(keys(hd), qz_ref[hd, 0:2 * wq, :], _NT, preferred_element_type=F32)
                    + jnp.concatenate([b2, b2], axis=1))

            stage(0)
            for hd in range(N_HEADS):
                if hd + 1 < N_HEADS:
                    stage(hd + 1)
                s = s_ref[hd % 2, 0:n, 0:2 * wq]
                m_old = m_ref[hd, :, 0:2 * wq]
                m_new = jnp.maximum(m_old, jnp.max(s, axis=0, keepdims=True))
                alpha = jnp.exp2(m_old - m_new)
                p = jnp.exp2(s - m_new)
                m_ref[hd, :, 0:2 * wq] = m_new
                acc_ref[hd, :, 0:2 * wq] = (alpha * acc_ref[hd, :, 0:2 * wq]
                                            + jnp.dot(vt(hd), p.astype(BF16), preferred_element_type=F32))

        def body(jb, carry):
            r0 = pl.multiple_of(jb * blk, blk)
            sel = jnp.minimum(i - jb, 2)
            all_heads(blk,
                      lambda hd: k_ref[0, pl.ds(r0, blk), hd * V_DIM:(hd + 1) * V_DIM],
                      lambda hd: vt_ref[hd, jb],
                      lambda hd: bias_ref[hd, sel, :, 0:wq])
            return carry

        lax.fori_loop(0, n_loop, body, 0)
        if tail_keys:
            r0 = n_full * blk
            all_heads(tail_keys,
                      lambda hd: k_ref[0, r0:r0 + tail_keys, hd * V_DIM:(hd + 1) * V_DIM],
                      lambda hd: vt_ref[hd, n_full, :, 0:tail_keys],
                      lambda hd: bias_ref[hd, 0, 0:tail_keys, 0:wq])

        lam = _diff_lambda(lam_ref, lam_init)
        for hd in range(N_HEADS):
            o_both = acc_ref[hd, 0:V_DIM, 0:2 * wq] * (1.0 / acc_ref[hd, V_DIM:V_DIM + 1, 0:2 * wq])
            o = o_both[:, 0:wq] - lam * o_both[:, wq:2 * wq]
            o = o * lax.rsqrt(jnp.mean(o * o, axis=0, keepdims=True) + RMS_EPS) * sub_ref[...] * (1.0 - lam_init)
            a_ref[0:wq, hd * V_DIM:(hd + 1) * V_DIM] = o.T.astype(BF16)
        mix = jnp.dot(a_ref[0:wq, :], wo_ref[...], preferred_element_type=F32)
        o_ref[0, 0:wq, :] = x_ref[0, 0:wq, :] + _rms(mix, g1_ref[...])

    @pl.when(i < n_full)
    def _():
        run(blk, i + 1, 0)

    if tail:
        @pl.when(i == n_full)
        def _():
            run(-(-tail // LANES) * LANES, n_full, tail)


def _attn_prompt(q, kb, vb, x, bias_p, lam_p, subln_col, wo, mix_norm, layer, j, *, blk, lam_init):
    b, t, d = q.shape
    n_full, tail = t // blk, t % blk
    assert tail % 16 == 0
    n_blocks = n_full + (1 if tail else 0)
    seq_blk = pl.BlockSpec((1, blk, d), lambda bi, qi: (bi, qi, 0))
    return pl.pallas_call(
        functools.partial(_attn_p_body, blk=blk, n_full=n_full, tail=tail, lam_init=lam_init),
        grid=(b, n_blocks),
        in_specs=[
            seq_blk,
            pl.BlockSpec((1, t, d), lambda bi, qi: (bi, 0, 0)),
            pl.BlockSpec((1, t, d), lambda bi, qi: (bi, 0, 0)),
            seq_blk,
            _const_spec((N_HEADS, 3, blk, blk), (0, 0, 0, 0)),
            _const_spec((None, 4, HEAD_DIM), (j, 0, 0)),
            _const_spec((None, V_DIM, 1), (j, 0, 0)),
            _const_spec((None, d, d), (j, 0, 0)),
            _const_spec((None, 1, d), (layer * 2 + 1, 0, 0)),
        ],
        out_specs=seq_blk,
        out_shape=jax.ShapeDtypeStruct((b, t, d), F32),
        scratch_shapes=[
            pltpu.VMEM((N_HEADS, 2 * blk, V_DIM), BF16),
            pltpu.VMEM((N_HEADS, n_blocks, V_DIM + SUM_ROWS, blk), BF16),
            pltpu.VMEM((N_HEADS, 1, 2 * blk), F32),
            pltpu.VMEM((N_HEADS, V_DIM + SUM_ROWS, 2 * blk), F32),
            pltpu.VMEM((2, blk, 2 * blk), F32),
            pltpu.VMEM((blk, d), BF16),
        ],
        compiler_params=_params("arbitrary", "arbitrary"),
        name="attn_prompt",
    )(q, kb, vb, x, bias_p, lam_p, subln_col, wo, mix_norm)


def _attn_s_body(pt_ref, q_ref, kn_ref, vn_ref, bias_ref, lam_ref, sub_ref, *rest, n_pages, n_steps, n_s, lam_init):
    k_pages = rest[:n_pages]
    v_pages = rest[n_pages:2 * n_pages]
    o_ref, q2_ref, m_ref, l_ref, acc_ref, s_ref = rest[2 * n_pages:]
    step = pl.program_id(1)
    half = n_s * N_HEADS
    cols = PAGE_SIZE * N_HEADS

    @pl.when(step == 0)
    def _():
        q = q_ref[...].reshape(half, V_DIM)
        lane = lax.broadcasted_iota(jnp.int32, (half, V_DIM), 1)
        q2_ref[0:half, :] = jnp.where(lane < HEAD_DIM, q, 0.0).astype(BF16)
        q2_ref[half:2 * half, :] = jnp.where(lane >= HEAD_DIM, q, 0.0).astype(BF16)
        m_ref[...] = jnp.full(m_ref.shape, NEG_INF, F32)
        l_ref[...] = jnp.zeros(l_ref.shape, F32)
        acc_ref[...] = jnp.zeros(acc_ref.shape, F32)

    q2 = q2_ref[...]
    last = step == n_steps - 1

    def scores(keys, bias):
        return lax.dot_general(q2, keys, _NT, preferred_element_type=F32) + bias

    def update(s, values):
        p, m_new, alpha, l_new = _softmax_step(s, m_ref[...], l_ref[...])
        m_ref[...] = m_new
        l_ref[...] = l_new
        acc_ref[...] = alpha * acc_ref[...] + jnp.dot(p.astype(BF16), values, preferred_element_type=F32)

    def stage(r_):
        bias = bias_ref[:, 0:cols]
        if r_ == n_pages - 1:
            bias = jnp.where(last, bias_ref[:, cols:2 * cols], bias)
        s_ref[r_ % 2] = scores(k_pages[r_][...].reshape(cols, V_DIM).astype(BF16), bias)

    stage(0)
    for r_ in range(n_pages):
        if r_ + 1 < n_pages:
            stage(r_ + 1)
        update(s_ref[r_ % 2], v_pages[r_][...].reshape(cols, V_DIM).astype(BF16))

    @pl.when(last)
    def _():
        pad = jnp.zeros((PAGE_SIZE - half, V_DIM), F32)
        kn = jnp.concatenate([kn_ref[...].reshape(half, V_DIM), pad], axis=0).astype(BF16)
        vn = jnp.concatenate([vn_ref[...].reshape(half, V_DIM), pad], axis=0).astype(BF16)
        update(scores(kn, bias_ref[:, 2 * cols:2 * cols + PAGE_SIZE]), vn)
        lam = _diff_lambda(lam_ref, lam_init)
        o_all = acc_ref[...] * (1.0 / l_ref[...])
        o = o_all[0:half] - lam * o_all[half:2 * half]
        o_ref[...] = (_rms(o, sub_ref[...]) * (1.0 - lam_init)).reshape(o_ref.shape)


def _attn_sample(q, kn, vn, cache_k, cache_v, page_table, bias_s, lam_p, subln, j, *, n_pages, lam_init):
    db, n_s = q.shape[:2]
    total_pages = page_table.shape[1]
    n_steps = total_pages // n_pages
    assert n_steps * n_pages == total_pages
    rows = 2 * n_s * N_HEADS
    new_kv = pl.BlockSpec((1, n_s, 1, N_HEADS, V_DIM), lambda b, s, pt: (b, 0, 0, 0, 0))
    q_spec = pl.BlockSpec((1, n_s, N_HEADS, V_DIM), lambda b, s, pt: (b, 0, 0, 0))

    def page_spec(r):
        return pl.BlockSpec((1, PAGE_SIZE, 1, N_HEADS, V_DIM),
                            lambda b, s, pt: (pt[b, s * n_pages + r], 0, j, 0, 0))

    grid_spec = pltpu.PrefetchScalarGridSpec(
        num_scalar_prefetch=1,
        grid=(db, n_steps),
        in_specs=[
            q_spec, new_kv, new_kv,
            pl.BlockSpec(bias_s.shape, lambda b, s, pt: (0, 0)),
            pl.BlockSpec((None, 4, HEAD_DIM), lambda b, s, pt: (j, 0, 0)),
            pl.BlockSpec((None, 1, V_DIM), lambda b, s, pt: (j, 0, 0)),
        ] + [page_spec(r) for r in range(n_pages)] * 2,
        out_specs=q_spec,
        scratch_shapes=[pltpu.VMEM((rows, V_DIM), BF16), pltpu.VMEM((rows, 1), F32), pltpu.VMEM((rows, 1), F32),
                        pltpu.VMEM((rows, V_DIM), F32), pltpu.VMEM((2, rows, PAGE_SIZE * N_HEADS), F32)],
    )
    return pl.pallas_call(
        functools.partial(_attn_s_body, n_pages=n_pages, n_steps=n_steps, n_s=n_s, lam_init=lam_init),
        grid_spec=grid_spec,
        out_shape=jax.ShapeDtypeStruct((db, n_s, N_HEADS, V_DIM), F32),
        compiler_params=_params("parallel", "arbitrary"),
        name="attn_sample",
    )(page_table, q, kn, vn, bias_s, lam_p, subln, *([cache_k] * n_pages), *([cache_v] * n_pages))


def _outproj_body(a_ref, x_ref, w_ref, g1_ref, o_ref):
    m = jnp.dot(a_ref[...].astype(BF16), w_ref[...], preferred_element_type=F32)
    o_ref[...] = x_ref[...] + _rms(m, g1_ref[...])


def _outproj(a, x, wo, mix_norm, layer, j, *, tm):
    n, d = x.shape
    row = pl.BlockSpec((tm, d), lambda i: (i, 0))
    return pl.pallas_call(
        _outproj_body,
        grid=(pl.cdiv(n, tm),),
        in_specs=[row, row, _const_spec((None, d, d), (j, 0, 0)), _const_spec((None, 1, d), (layer * 2 + 1, 0, 0))],
        out_specs=row,
        out_shape=jax.ShapeDtypeStruct((n, d), F32),
        compiler_params=_params("parallel"),
        name="attn_outproj",
    )(a, x, wo, mix_norm)


def _final_p_body(x_ref, g_ref, o_ref, *, skip, rows):
    t = pl.program_id(1)
    r0 = pl.multiple_of(skip + t * rows, 8)
    o_ref[0] = _rms(x_ref[0, pl.ds(r0, rows), :], g_ref[...])


def _final_prompt(x, g, *, skip, rows):
    b, t, d = x.shape
    n_t = (t - skip) // rows
    return pl.pallas_call(
        functools.partial(_final_p_body, skip=skip, rows=rows),
        grid=(b, n_t),
        in_specs=[pl.BlockSpec((1, t, d), lambda bi, ti: (bi, 0, 0)), _const_spec((1, d), (0, 0))],
        out_specs=pl.BlockSpec((1, rows, d), lambda bi, ti: (bi, ti, 0)),
        out_shape=jax.ShapeDtypeStruct((b, t - skip, d), F32),
        compiler_params=_params("parallel", "arbitrary"),
        name="final_norm_prompt",
    )(x, g)


def _final_s_body(x_ref, g_ref, o_ref):
    o_ref[...] = _rms(x_ref[...], g_ref[...])


def _final_sample(x, g, *, tm):
    n, d = x.shape
    row = pl.BlockSpec((tm, d), lambda i: (i, 0))
    return pl.pallas_call(
        _final_s_body,
        grid=(pl.cdiv(n, tm),),
        in_specs=[row, _const_spec((1, d), (0, 0))],
        out_specs=row,
        out_shape=jax.ShapeDtypeStruct((n, d), F32),
        compiler_params=_params("parallel"),
        name="final_norm_sample",
    )(x, g)


def _pick_tile(n, cap, align=8):
    best = None
    for c in range(align, min(n, cap) + 1, align):
        if n % c == 0:
            best = c
    assert best is not None, (n, cap, align)
    return best


def kernel(x_prompt, x_sample, state_pool, state_conv, cache_k, cache_v, page_table, meta_tokens, rel_bias_table, ffn_norm, ffn_wg, ffn_wu, ffn_wd, mix_norm, pool_w, pool_scale, attn_wqkv, attn_wo, attn_lambda, attn_subln, conv_w1, conv_b1, conv_wdw, conv_bdw, conv_ln_g, conv_ln_b, conv_w2, conv_b2, final_norm):
    b, seq, d = x_prompt.shape
    db, n_s, _ = x_sample.shape
    depth = ffn_wg.shape[0]
    t = seq + N_META
    past = page_table.shape[1] * PAGE_SIZE

    wg, wu, wd = ffn_wg.astype(BF16), ffn_wu.astype(BF16), ffn_wd.astype(BF16)
    ffn_norm3 = ffn_norm.reshape(-1, 1, d)
    mix_norm3 = mix_norm.reshape(-1, 1, d)
    pool_w_b = pool_w.astype(BF16)
    pool_scale3 = pool_scale.reshape(-1, 1, d)
    wqkv_b, wo_b = attn_wqkv.astype(BF16), attn_wo.astype(BF16)
    subln3 = attn_subln.reshape(-1, 1, V_DIM)
    cw = (conv_w1.astype(BF16), conv_b1.reshape(-1, 1, 2 * d), conv_wdw, conv_bdw.reshape(-1, 1, d),
          conv_ln_g.reshape(-1, 1, d), conv_ln_b.reshape(-1, 1, d), conv_w2.astype(BF16), conv_b2.reshape(-1, 1, d))
    state_pool2 = state_pool.reshape(state_pool.shape[0], db, POOL_BUF * d)
    state_conv2 = state_conv.reshape(state_conv.shape[0], db, CONV_BUF * d)

    meta = jnp.broadcast_to(meta_tokens[None].astype(x_prompt.dtype), (b, N_META, d))
    xp = jnp.concatenate([meta, x_prompt], axis=1).reshape(b * t, d)
    xs = x_sample.reshape(db * n_s, d)

    tm_p = _pick_tile(b * t, 768, 16)
    tm_s = min(512, db * n_s)
    ts_pool = _pick_tile(t, 1032)
    ts_conv = _pick_tile(t, 344)
    bt = _pick_tile(db, 32)

    bias_p = bias_s = None
    pool_p, pool_s, conv_p, conv_s = [], [], [], []
    k_p = v_p = k_s = v_s = None
    for i in range(depth):
        kind, j = i % N_MIXERS, i // N_MIXERS
        xp, xs = _ffn(xp, xs, ffn_norm3, wg, wu, wd, i, 0, tm=tm_p)
        if kind == 0:
            xp3, st = _pool_prompt(xp.reshape(b, t, d), mix_norm3, pool_w_b, pool_scale3, i, j, ts=ts_pool)
            xp = xp3.reshape(b * t, d)
            pool_p.append(st)
            xs2, st = _pool_sample(xs.reshape(db, n_s * d), state_pool2, mix_norm3, pool_w_b, pool_scale3, i, j,
                                   n_s=n_s, past=past, bt=bt)
            xs = xs2.reshape(db * n_s, d)
            pool_s.append(st.reshape(db, POOL_BUF, d))
        elif kind == 1:
            assert j == 0
            lam_init = _lambda_init(i)
            if bias_p is None:
                bias_p, bias_s = _bias_tiles(rel_bias_table, blk=ATTN_BLOCK, n_s=n_s)
            q, k_p, v_p, kb, vb = _qkv(xp, mix_norm3, wqkv_b, i, j, tm=tm_p)
            xp = _attn_prompt(q.reshape(b, t, d), kb.reshape(b, t, d), vb.reshape(b, t, d), xp.reshape(b, t, d),
                              bias_p, attn_lambda, attn_subln.reshape(-1, V_DIM, 1), wo_b, mix_norm3, i, j,
                              blk=ATTN_BLOCK, lam_init=lam_init).reshape(b * t, d)
            q, k_s, v_s, _, _ = _qkv(xs, mix_norm3, wqkv_b, i, j, tm=tm_s)
            k_s = k_s.reshape(db, n_s, 1, N_HEADS, V_DIM)
            v_s = v_s.reshape(db, n_s, 1, N_HEADS, V_DIM)
            a = _attn_sample(q.astype(F32).reshape(db, n_s, N_HEADS, V_DIM), k_s, v_s, cache_k, cache_v,
                             page_table, bias_s, attn_lambda, subln3, j, n_pages=8, lam_init=lam_init)
            xs = _outproj(a.reshape(db * n_s, d), xs, wo_b, mix_norm3, i, j, tm=tm_s)
        else:
            xp3, st = _conv_prompt(xp.reshape(b, t, d), mix_norm3, cw, i, j, ts=ts_conv)
            xp = xp3.reshape(b * t, d)
            conv_p.append(st)
            xs2, st = _conv_sample(xs.reshape(db, n_s * d), state_conv2, mix_norm3, cw, i, j, n_s=n_s, bt=bt)
            xs = xs2.reshape(db * n_s, d)
            conv_s.append(st.reshape(db, CONV_BUF, d))
        xp, xs = _ffn(xp, xs, ffn_norm3, wg, wu, wd, i, 1, tm=tm_p)

    g = final_norm.reshape(1, d)
    y_prompt = _final_prompt(xp.reshape(b, t, d), g, skip=N_META, rows=_pick_tile(seq, 1024))
    y_sample = _final_sample(xs, g, tm=tm_s).reshape(db, n_s, d)
    kv_shape_p = (b, t, 1, N_HEADS, V_DIM)
    return (y_prompt, y_sample, jnp.stack(pool_p, axis=0), jnp.stack(pool_s, axis=0),
            jnp.stack(conv_p, axis=0), jnp.stack(conv_s, axis=0),
            k_p.reshape(kv_shape_p), v_p.reshape(kv_shape_p), k_s, v_s)
```

```python
import functools
import math

import jax
import jax.numpy as jnp
from jax import lax
from jax.experimental import pallas as pl
from jax.experimental.pallas import tpu as pltpu

F32 = jnp.float32
BF16 = jnp.bfloat16

N_MIXERS = 3
N_META = 16
N_HEADS = 8
HEAD_DIM = 64
V_DIM = 2 * HEAD_DIM
N_BUCKETS = 32
MAX_EXACT = N_BUCKETS // 2
MAX_DISTANCE = 128
POOL_WINDOWS = (2, 4, 8, 16)
POOL_BUF = max(POOL_WINDOWS) - 1
CONV_WIDTH = 31
CONV_BUF = CONV_WIDTH - 1
PAGE_SIZE = 128
RMS_EPS = 1e-6
LN_EPS = 1e-5
NEG_INF = -1e30
LOG2E = math.log2(math.e)
SUBLANES = 8
LANES = 128
BF16_ROWS = 16
MXU_TILE = 256
SUM_ROWS = 16
ATTN_BLOCK = 256
POOL_HALO = 16
CONV_HALO = 32
VMEM_LIMIT = 56 * 1024 * 1024


def _lambda_init(layer_idx):
    return 0.8 - 0.6 * math.exp(-0.3 * layer_idx)


def _first_far_distance():
    n = MAX_EXACT
    while MAX_EXACT + int(math.log(n / MAX_EXACT) / math.log(MAX_DISTANCE / MAX_EXACT) * (N_BUCKETS - MAX_EXACT)) < N_BUCKETS - 1:
        n += 1
    return n


def _params(*sem):
    return pltpu.CompilerParams(dimension_semantics=sem, vmem_limit_bytes=VMEM_LIMIT)


def _rms(x, g):
    return x * lax.rsqrt(jnp.mean(x * x, axis=-1, keepdims=True) + RMS_EPS) * g


def _silu(x):
    return x * jax.nn.sigmoid(x)


def _const_spec(shape, index, single=False):
    if single:
        return pl.BlockSpec(shape, lambda *_: index, pipeline_mode=pl.Buffered(1))
    return pl.BlockSpec(shape, lambda *_: index)


def _ffn_chunks(ff, n_chunks):
    tiles = -(-ff // MXU_TILE)
    edges = [min(ff, MXU_TILE * (-(-tiles * c // n_chunks))) for c in range(n_chunks + 1)]
    return [(lo, hi - lo) for lo, hi in zip(edges[:-1], edges[1:]) if hi > lo]


def _ffn_body(xp_ref, xs_ref, n0_ref, n1_ref, wg_ref, wu_ref, wd_ref, op_ref, os_ref, *, n_chunks, n_p):
    i = pl.program_id(0)

    def apply(x_ref, o_ref):
        rows = x_ref.shape[0]
        first = -(-rows // (2 * BF16_ROWS)) * BF16_ROWS
        for lo_row, n_rows in ((0, first), (first, rows - first))[:2 if rows > first else 1]:
            rs = pl.ds(lo_row, n_rows)
            x = x_ref[rs, :]
            h = _rms(x, n0_ref[...]).astype(BF16)
            acc = jnp.zeros(x.shape, F32)
            for lo, width in _ffn_chunks(wg_ref.shape[1], n_chunks):
                sl = pl.ds(lo, width)
                g = jnp.dot(h, wg_ref[:, sl], preferred_element_type=F32)
                u = jnp.dot(h, wu_ref[:, sl], preferred_element_type=F32)
                a = (_silu(g) * u).astype(BF16)
                acc = acc + jnp.dot(a, wd_ref[sl, :], preferred_element_type=F32)
            o_ref[rs, :] = x + 0.5 * _rms(acc, n1_ref[...])

    @pl.when(i == 0)
    def _():
        os_ref[...] = jnp.zeros(os_ref.shape, F32)

    @pl.when(i < n_p)
    def _():
        apply(xp_ref, op_ref)

    @pl.when(i >= n_p)
    def _():
        apply(xs_ref, os_ref)


def _ffn(xp, xs, norms, wg, wu, wd, layer, f, *, tm, n_chunks=2):
    (n_rows_p, d), n_rows_s = xp.shape, xs.shape[0]
    ff = wg.shape[-1]
    nidx = (layer * 2 + f) * 2
    tm_s = _pick_tile(n_rows_s, tm, 16)
    n_p, n_s = pl.cdiv(n_rows_p, tm), pl.cdiv(n_rows_s, tm_s)
    p_spec = pl.BlockSpec((tm, d), lambda i: (jnp.minimum(i, n_p - 1), 0))
    s_spec = pl.BlockSpec((tm_s, d), lambda i: (jnp.maximum(i - n_p, 0), 0))
    return pl.pallas_call(
        functools.partial(_ffn_body, n_chunks=n_chunks, n_p=n_p),
        grid=(n_p + n_s,),
        in_specs=[
            p_spec, s_spec,
            _const_spec((None, 1, d), (nidx, 0, 0)),
            _const_spec((None, 1, d), (nidx + 1, 0, 0)),
            _const_spec((None, None, d, ff), (layer, f, 0, 0), single=True),
            _const_spec((None, None, d, ff), (layer, f, 0, 0), single=True),
            _const_spec((None, None, ff, d), (layer, f, 0, 0), single=True),
        ],
        out_specs=[p_spec, s_spec],
        out_shape=[jax.ShapeDtypeStruct((n_rows_p, d), F32), jax.ShapeDtypeStruct((n_rows_s, d), F32)],
        compiler_params=_params("arbitrary"),
        name="ffn",
    )(xp, xs, norms, norms, wg, wu, wd)


def _pool_p_body(x_ref, g0_ref, g1_ref, w_ref, sc_ref, o_ref, st_ref, *lvl_refs, ts, n_t):
    t = pl.program_id(1)
    d = x_ref.shape[-1]
    n_g = len(POOL_WINDOWS)
    gd = d // n_g
    x = x_ref[0]
    h = _rms(x, g0_ref[...])
    cat_ref = lvl_refs[0]
    top = SUBLANES + POOL_HALO
    rows = POOL_HALO + ts

    @pl.when(t == 0)
    def _():
        cat_ref[0:top, :] = jnp.zeros((top, d), F32)
        for ref in lvl_refs[1:]:
            ref[0:SUBLANES, :] = jnp.zeros((SUBLANES, ref.shape[1]), F32)

    cat_ref[top:top + ts, :] = h
    for k in range(1, len(lvl_refs)):
        prev, cur = lvl_refs[k - 1], lvl_refs[k]
        lo = prev.shape[1] - cur.shape[1]
        cur[SUBLANES:SUBLANES + rows, :] = (prev[SUBLANES:SUBLANES + rows, lo:]
                                            + prev[SUBLANES - 2 ** (k - 1):SUBLANES - 2 ** (k - 1) + rows, lo:])
    pos = t * ts + lax.broadcasted_iota(jnp.int32, (ts, 1), 0)
    outs = []
    for g, w in enumerate(POOL_WINDOWS):
        c0 = g * gd
        k = w.bit_length() - 1
        if k < len(lvl_refs):
            ref = lvl_refs[k]
            l0 = c0 - (d - ref.shape[1])
            acc = ref[top:top + ts, l0:l0 + gd]
        else:
            ref = lvl_refs[k - 1]
            l0 = c0 - (d - ref.shape[1])
            acc = ref[top:top + ts, l0:l0 + gd] + ref[top - w // 2:top - w // 2 + ts, l0:l0 + gd]
        inv_cnt = 1.0 / jnp.minimum(w, pos + 1).astype(F32)
        pooled = (acc * inv_cnt - h[:, c0:c0 + gd]).astype(BF16)
        outs.append(jnp.dot(pooled, w_ref[g], preferred_element_type=F32))
    m = jnp.concatenate(outs, axis=-1) * sc_ref[...]
    o_ref[0] = x + _rms(m, g1_ref[...])

    @pl.when(t == n_t - 1)
    def _():
        st_ref[0] = cat_ref[top + ts - POOL_BUF:top + ts, :]

    cat_ref[SUBLANES:top, :] = cat_ref[SUBLANES + ts:top + ts, :]


def _pool_prompt(x, mix_norm, pool_w, pool_scale, layer, j, *, ts):
    b, t, d = x.shape
    n_t = t // ts
    assert n_t * ts == t and ts % 8 == 0 and ts >= POOL_HALO
    g = len(POOL_WINDOWS)
    assert all(w == 2 ** (i + 1) for i, w in enumerate(POOL_WINDOWS)) and POOL_WINDOWS[-1] // 2 == SUBLANES
    buf_rows = SUBLANES + POOL_HALO + ts
    level_lanes = [d] + [d - i * (d // g) for i in range(g - 1)]
    return pl.pallas_call(
        functools.partial(_pool_p_body, ts=ts, n_t=n_t),
        grid=(b, n_t),
        in_specs=[
            pl.BlockSpec((1, ts, d), lambda bi, ti: (bi, ti, 0)),
            _const_spec((None, 1, d), (layer * 2, 0, 0)),
            _const_spec((None, 1, d), (layer * 2 + 1, 0, 0)),
            _const_spec((None, g, d // g, d // g), (j, 0, 0, 0)),
            _const_spec((None, 1, d), (j, 0, 0)),
        ],
        out_specs=[
            pl.BlockSpec((1, ts, d), lambda bi, ti: (bi, ti, 0)),
            pl.BlockSpec((1, POOL_BUF, d), lambda bi, ti: (bi, 0, 0)),
        ],
        out_shape=[jax.ShapeDtypeStruct((b, t, d), F32), jax.ShapeDtypeStruct((b, POOL_BUF, d), F32)],
        scratch_shapes=[pltpu.VMEM((buf_rows, lanes), F32) for lanes in level_lanes],
        compiler_params=_params("parallel", "arbitrary"),
        name="pool_prompt",
    )(x, mix_norm, mix_norm, pool_w, pool_scale)


def _pool_s_body(x_ref, st_ref, g0_ref, g1_ref, w_ref, sc_ref, o_ref, so_ref, *, n_s, past):
    d = g0_ref.shape[-1]
    gd = d // len(POOL_WINDOWS)
    bt = x_ref.shape[0]
    xs = [x_ref[:, s * d:(s + 1) * d] for s in range(n_s)]
    hs = [_rms(x, g0_ref[...]) for x in xs]

    def cat(idx, c0, width):
        if idx < POOL_BUF:
            return st_ref[:, idx * d + c0:idx * d + c0 + width]
        return hs[idx - POOL_BUF][:, c0:c0 + width]

    outs = []
    for g, w in enumerate(POOL_WINDOWS):
        c0 = g * gd
        rows = []
        for s in range(n_s):
            acc = cat(POOL_BUF + s, c0, gd)
            for jj in range(1, w):
                acc = acc + cat(POOL_BUF + s - jj, c0, gd)
            rows.append(acc * (1.0 / min(w, past + s + 1)) - hs[s][:, c0:c0 + gd])
        pooled = jnp.concatenate(rows, axis=0).astype(BF16)
        outs.append(jnp.dot(pooled, w_ref[g], preferred_element_type=F32))
    for s in range(n_s):
        m = jnp.concatenate([o[s * bt:(s + 1) * bt] for o in outs], axis=-1) * sc_ref[...]
        o_ref[:, s * d:(s + 1) * d] = xs[s] + _rms(m, g1_ref[...])
    for jj in range(POOL_BUF):
        so_ref[:, jj * d:(jj + 1) * d] = cat(n_s + jj, 0, d)


def _pool_sample(x, state, mix_norm, pool_w, pool_scale, layer, j, *, n_s, past, bt):
    db = x.shape[0]
    d = mix_norm.shape[-1]
    g = len(POOL_WINDOWS)
    return pl.pallas_call(
        functools.partial(_pool_s_body, n_s=n_s, past=past),
        grid=(db // bt,),
        in_specs=[
            pl.BlockSpec((bt, n_s * d), lambda i: (i, 0)),
            pl.BlockSpec((None, bt, POOL_BUF * d), lambda i: (j, i, 0)),
            _const_spec((None, 1, d), (layer * 2, 0, 0)),
            _const_spec((None, 1, d), (layer * 2 + 1, 0, 0)),
            _const_spec((None, g, d // g, d // g), (j, 0, 0, 0)),
            _const_spec((None, 1, d), (j, 0, 0)),
        ],
        out_specs=[
            pl.BlockSpec((bt, n_s * d), lambda i: (i, 0)),
            pl.BlockSpec((bt, POOL_BUF * d), lambda i: (i, 0)),
        ],
        out_shape=[jax.ShapeDtypeStruct((db, n_s * d), F32), jax.ShapeDtypeStruct((db, POOL_BUF * d), F32)],
        compiler_params=_params("parallel"),
        name="pool_sample",
    )(x, state, mix_norm, mix_norm, pool_w, pool_scale)


def _layer_norm(y, g, b):
    mu = jnp.mean(y, axis=-1, keepdims=True)
    yc = y - mu
    var = jnp.mean(yc * yc, axis=-1, keepdims=True)
    return yc * lax.rsqrt(var + LN_EPS) * g + b


def _conv_p_body(x_ref, g0_ref, g1_ref, w1_ref, b1_ref, wdw_ref, bdw_ref, lng_ref, lnb_ref, w2_ref, b2_ref,
                 o_ref, st_ref, cat_ref, y_ref, *, ts, n_t):
    t = pl.program_id(1)
    d = x_ref.shape[-1]
    x = x_ref[0]
    h = _rms(x, g0_ref[...]).astype(BF16)
    glu = jnp.dot(h, w1_ref[...], preferred_element_type=F32) + b1_ref[...]
    u = glu[:, :d] * jax.nn.sigmoid(glu[:, d:])

    @pl.when(t == 0)
    def _():
        cat_ref[0:CONV_HALO, :] = jnp.zeros((CONV_HALO, d), F32)
        cat_ref[CONV_HALO + ts:CONV_HALO + ts + SUBLANES, :] = jnp.zeros((SUBLANES, d), F32)

    cat_ref[CONV_HALO:CONV_HALO + ts, :] = u
    off = CONV_HALO - CONV_BUF
    zrows = ts + SUBLANES
    for c0 in range(0, d, LANES):
        y = None
        for r in range(SUBLANES):
            z = None
            for a in range((CONV_WIDTH + off) // SUBLANES + 1):
                k = SUBLANES * a + r - off
                if 0 <= k < CONV_WIDTH:
                    term = cat_ref[SUBLANES * a:SUBLANES * a + zrows, c0:c0 + LANES] * wdw_ref[k:k + 1, c0:c0 + LANES]
                    z = term if z is None else z + term
            zs = z[r:r + ts]
            y = zs if y is None else y + zs
        y_ref[:, c0:c0 + LANES] = y + bdw_ref[:, c0:c0 + LANES]
    a = _silu(_layer_norm(y_ref[...], lng_ref[...], lnb_ref[...])).astype(BF16)
    m = jnp.dot(a, w2_ref[...], preferred_element_type=F32) + b2_ref[...]
    o_ref[0] = x + _rms(m, g1_ref[...])

    @pl.when(t == n_t - 1)
    def _():
        st_ref[0] = cat_ref[CONV_HALO + ts - CONV_BUF:CONV_HALO + ts, :]

    cat_ref[0:CONV_HALO, :] = cat_ref[ts:ts + CONV_HALO, :]


def _conv_specs(layer, j, d):
    return [
        _const_spec((None, 1, d), (layer * 2, 0, 0)),
        _const_spec((None, 1, d), (layer * 2 + 1, 0, 0)),
        _const_spec((None, d, 2 * d), (j, 0, 0)),
        _const_spec((None, 1, 2 * d), (j, 0, 0)),
        _const_spec((None, CONV_WIDTH, d), (j, 0, 0)),
        _const_spec((None, 1, d), (j, 0, 0)),
        _const_spec((None, 1, d), (j, 0, 0)),
        _const_spec((None, 1, d), (j, 0, 0)),
        _const_spec((None, d, d), (j, 0, 0)),
        _const_spec((None, 1, d), (j, 0, 0)),
    ]


def _conv_prompt(x, mix_norm, cw, layer, j, *, ts):
    b, t, d = x.shape
    n_t = t // ts
    assert n_t * ts == t and ts % 8 == 0 and ts >= CONV_HALO
    return pl.pallas_call(
        functools.partial(_conv_p_body, ts=ts, n_t=n_t),
        grid=(b, n_t),
        in_specs=[pl.BlockSpec((1, ts, d), lambda bi, ti: (bi, ti, 0))] + _conv_specs(layer, j, d),
        out_specs=[
            pl.BlockSpec((1, ts, d), lambda bi, ti: (bi, ti, 0)),
            pl.BlockSpec((1, CONV_BUF, d), lambda bi, ti: (bi, 0, 0)),
        ],
        out_shape=[jax.ShapeDtypeStruct((b, t, d), F32), jax.ShapeDtypeStruct((b, CONV_BUF, d), F32)],
        scratch_shapes=[pltpu.VMEM((CONV_HALO + ts + SUBLANES, d), F32), pltpu.VMEM((ts, d), F32)],
        compiler_params=_params("parallel", "arbitrary"),
        name="conv_prompt",
    )(x, mix_norm, mix_norm, *cw)


def _conv_s_body(x_ref, st_ref, g0_ref, g1_ref, w1_ref, b1_ref, wdw_ref, bdw_ref, lng_ref, lnb_ref, w2_ref, b2_ref,
                 o_ref, so_ref, *, n_s):
    d = g0_ref.shape[-1]
    bt = x_ref.shape[0]
    xs = [x_ref[:, s * d:(s + 1) * d] for s in range(n_s)]
    h = jnp.concatenate([_rms(x, g0_ref[...]) for x in xs], axis=0).astype(BF16)
    z = jnp.dot(h, w1_ref[...], preferred_element_type=F32) + b1_ref[...]
    u = z[:, :d] * jax.nn.sigmoid(z[:, d:])
    us = [u[s * bt:(s + 1) * bt] for s in range(n_s)]

    def cat(idx):
        if idx < CONV_BUF:
            return st_ref[:, idx * d:(idx + 1) * d]
        return us[idx - CONV_BUF]

    acts = []
    for s in range(n_s):
        y = cat(s) * wdw_ref[0:1, :] + bdw_ref[...]
        for k in range(1, CONV_WIDTH):
            y = y + cat(s + k) * wdw_ref[k:k + 1, :]
        acts.append(_silu(_layer_norm(y, lng_ref[...], lnb_ref[...])))
    a = jnp.concatenate(acts, axis=0).astype(BF16)
    m = jnp.dot(a, w2_ref[...], preferred_element_type=F32) + b2_ref[...]
    for s in range(n_s):
        o_ref[:, s * d:(s + 1) * d] = xs[s] + _rms(m[s * bt:(s + 1) * bt], g1_ref[...])
    for jj in range(CONV_BUF):
        so_ref[:, jj * d:(jj + 1) * d] = cat(n_s + jj)


def _conv_sample(x, state, mix_norm, cw, layer, j, *, n_s, bt):
    db = x.shape[0]
    d = mix_norm.shape[-1]
    return pl.pallas_call(
        functools.partial(_conv_s_body, n_s=n_s),
        grid=(db // bt,),
        in_specs=[
            pl.BlockSpec((bt, n_s * d), lambda i: (i, 0)),
            pl.BlockSpec((None, bt, CONV_BUF * d), lambda i: (j, i, 0)),
        ] + _conv_specs(layer, j, d),
        out_specs=[
            pl.BlockSpec((bt, n_s * d), lambda i: (i, 0)),
            pl.BlockSpec((bt, CONV_BUF * d), lambda i: (i, 0)),
        ],
        out_shape=[jax.ShapeDtypeStruct((db, n_s * d), F32), jax.ShapeDtypeStruct((db, CONV_BUF * d), F32)],
        compiler_params=_params("parallel"),
        name="conv_sample",
    )(x, state, mix_norm, mix_norm, *cw)


def _qkv_body(x_ref, g0_ref, w_ref, q_ref, k_ref, v_ref, kb_ref, vb_ref):
    d = x_ref.shape[-1]
    h = _rms(x_ref[...], g0_ref[...]).astype(BF16)
    qkv = jnp.dot(h, w_ref[...], preferred_element_type=F32)
    q_ref[...] = (qkv[:, :d] * (HEAD_DIM ** -0.5 * LOG2E)).astype(BF16)
    k = qkv[:, d:2 * d]
    v = qkv[:, 2 * d:]
    k_ref[...] = k
    v_ref[...] = v
    kb_ref[...] = k.astype(BF16)
    vb_ref[...] = v.astype(BF16)


def _qkv(x, mix_norm, wqkv, layer, j, *, tm):
    n, d = x.shape
    row = pl.BlockSpec((tm, d), lambda i: (i, 0))
    return pl.pallas_call(
        _qkv_body,
        grid=(pl.cdiv(n, tm),),
        in_specs=[row, _const_spec((None, 1, d), (layer * 2, 0, 0)), _const_spec((None, d, 3 * d), (j, 0, 0))],
        out_specs=[row] * 5,
        out_shape=[jax.ShapeDtypeStruct((n, d), BF16), jax.ShapeDtypeStruct((n, d), F32),
                   jax.ShapeDtypeStruct((n, d), F32), jax.ShapeDtypeStruct((n, d), BF16),
                   jax.ShapeDtypeStruct((n, d), BF16)],
        compiler_params=_params("parallel"),
        name="qkv_proj",
    )(x, mix_norm, wqkv)


def _bucket(n):
    nf = jnp.maximum(n, 1).astype(F32)
    large = MAX_EXACT + (jnp.log(nf / MAX_EXACT) / math.log(MAX_DISTANCE / MAX_EXACT)
                         * (N_BUCKETS - MAX_EXACT)).astype(jnp.int32)
    large = jnp.minimum(large, N_BUCKETS - 1)
    return jnp.where(n < MAX_EXACT, n, large)


def _lookup(bucket, entry):
    out = jnp.zeros(bucket.shape, F32)
    for b in range(N_BUCKETS):
        out = jnp.where(bucket == b, entry(b), out)
    return out


def _bias_p_body(table_ref, bp_ref, *, blk):
    head = pl.program_id(0)
    a = lax.broadcasted_iota(jnp.int32, (blk, blk), 0)
    b = lax.broadcasted_iota(jnp.int32, (blk, blk), 1)
    for sel in range(3):
        n = sel * blk + b - a
        vals = _lookup(_bucket(jnp.maximum(n, 0)), lambda bb: table_ref[bb * N_HEADS + head])
        bp_ref[0, sel] = jnp.where(n >= 0, vals * LOG2E, NEG_INF)


def _bias_s_body(tt_ref, bs_ref, *, n_s):
    rows = 2 * n_s * N_HEADS
    cols = PAGE_SIZE * N_HEADS
    tt = tt_ref[...]
    trow = jnp.broadcast_to(tt[None], (rows // N_HEADS, N_HEADS, N_BUCKETS)).reshape(rows, N_BUCKETS)

    def tile(width, dist):
        r = lax.broadcasted_iota(jnp.int32, (rows, width), 0)
        c = lax.broadcasted_iota(jnp.int32, (rows, width), 1)
        n = dist((r // N_HEADS) % n_s, c // N_HEADS)
        vals = _lookup(_bucket(jnp.maximum(n, 0)), lambda bb: trow[:, bb:bb + 1])
        return jnp.where((r % N_HEADS == c % N_HEADS) & (n >= 0) & (c < cols), vals * LOG2E, NEG_INF)

    bs_ref[:, 0:cols] = tile(cols, lambda qi, kk: 2 * PAGE_SIZE + qi - kk)
    bs_ref[:, cols:2 * cols] = tile(cols, lambda qi, kk: PAGE_SIZE + qi - kk)
    bs_ref[:, 2 * cols:2 * cols + PAGE_SIZE] = tile(
        PAGE_SIZE, lambda qi, kk: jnp.where(kk < n_s, qi - kk, -1))


def _bias_tiles(table, *, blk, n_s):
    far = _first_far_distance()
    assert blk + 1 >= far and PAGE_SIZE + 1 >= far and n_s * N_HEADS <= PAGE_SIZE
    bias_p = pl.pallas_call(
        functools.partial(_bias_p_body, blk=blk),
        grid=(N_HEADS,),
        in_specs=[pl.BlockSpec(memory_space=pltpu.SMEM)],
        out_specs=pl.BlockSpec((1, 3, blk, blk), lambda h: (h, 0, 0, 0)),
        out_shape=jax.ShapeDtypeStruct((N_HEADS, 3, blk, blk), F32),
        compiler_params=_params("parallel"),
        name="rel_bias_prompt",
    )(table.reshape(-1))
    width = 2 * PAGE_SIZE * N_HEADS + PAGE_SIZE
    bias_s = pl.pallas_call(
        functools.partial(_bias_s_body, n_s=n_s),
        out_shape=jax.ShapeDtypeStruct((2 * n_s * N_HEADS, width), F32),
        compiler_params=pltpu.CompilerParams(vmem_limit_bytes=VMEM_LIMIT),
        name="rel_bias_sample",
    )(table.T)
    return bias_p, bias_s


def _diff_lambda(lam_ref, lam_init):
    lp = lam_ref[...]
    s1 = jnp.sum(lp[0:1] * lp[1:2], axis=-1, keepdims=True)
    s2 = jnp.sum(lp[2:3] * lp[3:4], axis=-1, keepdims=True)
    return jnp.exp(s1) - jnp.exp(s2) + lam_init


def _softmax_step(s, m, l):
    m_new = jnp.maximum(m, jnp.max(s, axis=-1, keepdims=True))
    alpha = jnp.exp2(m - m_new)
    p = jnp.exp2(s - m_new)
    return p, m_new, alpha, alpha * l + jnp.sum(p, axis=-1, keepdims=True)


_NT = (((1,), (1,)), ((), ()))


def _attn_p_body(q_ref, k_ref, v_ref, x_ref, bias_ref, lam_ref, sub_ref, wo_ref, g1_ref, o_ref,
                 qz_ref, vt_ref, m_ref, acc_ref, s_ref, a_ref, *, blk, n_full, tail, lam_init):
    i = pl.program_id(1)

    @pl.when(i == 0)
    def _():
        for hd in range(N_HEADS):
            c0 = hd * V_DIM

            ones_row = (lax.broadcasted_iota(jnp.int32, (SUM_ROWS, blk), 0) == 0).astype(BF16)

            def xpose(jb, carry):
                r0 = pl.multiple_of(jb * blk, blk)
                vt_ref[hd, jb, 0:V_DIM, :] = v_ref[0, pl.ds(r0, blk), c0:c0 + V_DIM].astype(F32).T.astype(BF16)
                vt_ref[hd, jb, V_DIM:V_DIM + SUM_ROWS, :] = ones_row
                return carry

            lax.fori_loop(0, n_full, xpose, 0)
            if tail:
                r0 = n_full * blk
                vt_ref[hd, n_full, 0:V_DIM, 0:tail] = (
                    v_ref[0, r0:r0 + tail, c0:c0 + V_DIM].astype(F32).T.astype(BF16))
                vt_ref[hd, n_full, V_DIM:V_DIM + SUM_ROWS, :] = ones_row

    def run(wq, n_loop, tail_keys):
        lane = lax.broadcasted_iota(jnp.int32, (wq, V_DIM), 1)
        for hd in range(N_HEADS):
            q = q_ref[0, 0:wq, hd * V_DIM:(hd + 1) * V_DIM]
            qz_ref[hd, 0:wq, :] = jnp.where(lane < HEAD_DIM, q, jnp.zeros_like(q))
            qz_ref[hd, wq:2 * wq, :] = jnp.where(lane >= HEAD_DIM, q, jnp.zeros_like(q))
        m_ref[:, :, 0:2 * wq] = jnp.full((N_HEADS, 1, 2 * wq), NEG_INF, F32)
        acc_ref[:, :, 0:2 * wq] = jnp.zeros((N_HEADS, V_DIM + SUM_ROWS, 2 * wq), F32)

        def all_heads(n, keys, vt, bias):
            def stage(hd):
                b2 = bias(hd)
                s_ref[hd % 2, 0:n, 0:2 * wq] = (
                    lax.dot_general(keys(hd), qz_ref[hd, 0:2 * wq, :], _NT, preferred_element_type=F32)
                    + jnp.concatenate([b2, b2], axis=1))

            stage(0)
            for hd in range(N_HEADS):
                if hd + 1 < N_HEADS:
                    stage(hd + 1)
                s = s_ref[hd % 2, 0:n, 0:2 * wq]
                m_old = m_ref[hd, :, 0:2 * wq]
                m_new = jnp.maximum(m_old, jnp.max(s, axis=0, keepdims=True))
                alpha = jnp.exp2(m_old - m_new)
                p = jnp.exp2(s - m_new)
                m_ref[hd, :, 0:2 * wq] = m_new
                acc_ref[hd, :, 0:2 * wq] = (alpha * acc_ref[hd, :, 0:2 * wq]
                                            + jnp.dot(vt(hd), p.astype(BF16), preferred_element_type=F32))

        def body(jb, carry):
            r0 = pl.multiple_of(jb * blk, blk)
            sel = jnp.minimum(i - jb, 2)
            all_heads(blk,
                      lambda hd: k_ref[0, pl.ds(r0, blk), hd * V_DIM:(hd + 1) * V_DIM],
                      lambda hd: vt_ref[hd, jb],
                      lambda hd: bias_ref[hd, sel, :, 0:wq])
            return carry

        lax.fori_loop(0, n_loop, body, 0)
        if tail_keys:
            r0 = n_full * blk
            all_heads(tail_keys,
                      lambda hd: k_ref[0, r0:r0 + tail_keys, hd * V_DIM:(hd + 1) * V_DIM],
                      lambda hd: vt_ref[hd, n_full, :, 0:tail_keys],
                      lambda hd: bias_ref[hd, 0, 0:tail_keys, 0:wq])

        lam = _diff_lambda(lam_ref, lam_init)
        for hd in range(N_HEADS):
            o_both = acc_ref[hd, 0:V_DIM, 0:2 * wq] * (1.0 / acc_ref[hd, V_DIM:V_DIM + 1, 0:2 * wq])
            o = o_both[:, 0:wq] - lam * o_both[:, wq:2 * wq]
            o = o * lax.rsqrt(jnp.mean(o * o, axis=0, keepdims=True) + RMS_EPS) * sub_ref[...] * (1.0 - lam_init)
            a_ref[0:wq, hd * V_DIM:(hd + 1) * V_DIM] = o.T.astype(BF16)
        mix = jnp.dot(a_ref[0:wq, :], wo_ref[...], preferred_element_type=F32)
        o_ref[0, 0:wq, :] = x_ref[0, 0:wq, :] + _rms(mix, g1_ref[...])

    @pl.when(i < n_full)
    def _():
        run(blk, i + 1, 0)

    if tail:
        @pl.when(i == n_full)
        def _():
            run(-(-tail // LANES) * LANES, n_full, tail)


def _attn_prompt(q, kb, vb, x, bias_p, lam_p, subln_col, wo, mix_norm, layer, j, *, blk, lam_init):
    b, t, d = q.shape
    n_full, tail = t // blk, t % blk
    assert tail % 16 == 0
    n_blocks = n_full + (1 if tail else 0)
    seq_blk = pl.BlockSpec((1, blk, d), lambda bi, qi: (bi, qi, 0))
    return pl.pallas_call(
        functools.partial(_attn_p_body, blk=blk, n_full=n_full, tail=tail, lam_init=lam_init),
        grid=(b, n_blocks),
        in_specs=[
            seq_blk,
            pl.BlockSpec((1, t, d), lambda bi, qi: (bi, 0, 0)),
            pl.BlockSpec((1, t, d), lambda bi, qi: (bi, 0, 0)),
            seq_blk,
            _const_spec((N_HEADS, 3, blk, blk), (0, 0, 0, 0)),
            _const_spec((None, 4, HEAD_DIM), (j, 0, 0)),
            _const_spec((None, V_DIM, 1), (j, 0, 0)),
            _const_spec((None, d, d), (j, 0, 0)),
            _const_spec((None, 1, d), (layer * 2 + 1, 0, 0)),
        ],
        out_specs=seq_blk,
        out_shape=jax.ShapeDtypeStruct((b, t, d), F32),
        scratch_shapes=[
            pltpu.VMEM((N_HEADS, 2 * blk, V_DIM), BF16),
            pltpu.VMEM((N_HEADS, n_blocks, V_DIM + SUM_ROWS, blk), BF16),
            pltpu.VMEM((N_HEADS, 1, 2 * blk), F32),
            pltpu.VMEM((N_HEADS, V_DIM + SUM_ROWS, 2 * blk), F32),
            pltpu.VMEM((2, blk, 2 * blk), F32),
            pltpu.VMEM((blk, d), BF16),
        ],
        compiler_params=_params("arbitrary", "arbitrary"),
        name="attn_prompt",
    )(q, kb, vb, x, bias_p, lam_p, subln_col, wo, mix_norm)


def _attn_s_body(pt_ref, q_ref, kn_ref, vn_ref, bias_ref, lam_ref, sub_ref, *rest, n_pages, n_steps, n_s, lam_init):
    k_pages = rest[:n_pages]
    v_pages = rest[n_pages:2 * n_pages]
    o_ref, q2_ref, m_ref, l_ref, acc_ref, s_ref = rest[2 * n_pages:]
    step = pl.program_id(1)
    half = n_s * N_HEADS
    cols = PAGE_SIZE * N_HEADS

    @pl.when(step == 0)
    def _():
        q = q_ref[...].reshape(half, V_DIM)
        lane = lax.broadcasted_iota(jnp.int32, (half, V_DIM), 1)
        q2_ref[0:half, :] = jnp.where(lane < HEAD_DIM, q, 0.0).astype(BF16)
        q2_ref[half:2 * half, :] = jnp.where(lane >= HEAD_DIM, q, 0.0).astype(BF16)
        m_ref[...] = jnp.full(m_ref.shape, NEG_INF, F32)
        l_ref[...] = jnp.zeros(l_ref.shape, F32)
        acc_ref[...] = jnp.zeros(acc_ref.shape, F32)

    q2 = q2_ref[...]
    last = step == n_steps - 1

    def scores(keys, bias):
        return lax.dot_general(q2, keys, _NT, preferred_element_type=F32) + bias

    def update(s, values):
        p, m_new, alpha, l_new = _softmax_step(s, m_ref[...], l_ref[...])
        m_ref[...] = m_new
        l_ref[...] = l_new
        acc_ref[...] = alpha * acc_ref[...] + jnp.dot(p.astype(BF16), values, preferred_element_type=F32)

    def stage(r_):
        bias = bias_ref[:, 0:cols]
        if r_ == n_pages - 1:
            bias = jnp.where(last, bias_ref[:, cols:2 * cols], bias)
        s_ref[r_ % 2] = scores(k_pages[r_][...].reshape(cols, V_DIM).astype(BF16), bias)

    stage(0)
    for r_ in range(n_pages):
        if r_ + 1 < n_pages:
            stage(r_ + 1)
        update(s_ref[r_ % 2], v_pages[r_][...].reshape(cols, V_DIM).astype(BF16))

    @pl.when(last)
    def _():
        pad = jnp.zeros((PAGE_SIZE - half, V_DIM), F32)
        kn = jnp.concatenate([kn_ref[...].reshape(half, V_DIM), pad], axis=0).astype(BF16)
        vn = jnp.concatenate([vn_ref[...].reshape(half, V_DIM), pad], axis=0).astype(BF16)
        update(scores(kn, bias_ref[:, 2 * cols:2 * cols + PAGE_SIZE]), vn)
        lam = _diff_lambda(lam_ref, lam_init)
        o_all = acc_ref[...] * (1.0 / l_ref[...])
        o = o_all[0:half] - lam * o_all[half:2 * half]
        o_ref[...] = (_rms(o, sub_ref[...]) * (1.0 - lam_init)).reshape(o_ref.shape)


def _attn_sample(q, kn, vn, cache_k, cache_v, page_table, bias_s, lam_p, subln, j, *, n_pages, lam_init):
    db, n_s = q.shape[:2]
    total_pages = page_table.shape[1]
    n_steps = total_pages // n_pages
    assert n_steps * n_pages == total_pages
    rows = 2 * n_s * N_HEADS
    new_kv = pl.BlockSpec((1, n_s, 1, N_HEADS, V_DIM), lambda b, s, pt: (b, 0, 0, 0, 0))
    q_spec = pl.BlockSpec((1, n_s, N_HEADS, V_DIM), lambda b, s, pt: (b, 0, 0, 0))

    def page_spec(r):
        return pl.BlockSpec((1, PAGE_SIZE, 1, N_HEADS, V_DIM),
                            lambda b, s, pt: (pt[b, s * n_pages + r], 0, j, 0, 0))

    grid_spec = pltpu.PrefetchScalarGridSpec(
        num_scalar_prefetch=1,
        grid=(db, n_steps),
        in_specs=[
            q_spec, new_kv, new_kv,
            pl.BlockSpec(bias_s.shape, lambda b, s, pt: (0, 0)),
            pl.BlockSpec((None, 4, HEAD_DIM), lambda b, s, pt: (j, 0, 0)),
            pl.BlockSpec((None, 1, V_DIM), lambda b, s, pt: (j, 0, 0)),
        ] + [page_spec(r) for r in range(n_pages)] * 2,
        out_specs=q_spec,
        scratch_shapes=[pltpu.VMEM((rows, V_DIM), BF16), pltpu.VMEM((rows, 1), F32), pltpu.VMEM((rows, 1), F32),
                        pltpu.VMEM((rows, V_DIM), F32), pltpu.VMEM((2, rows, PAGE_SIZE * N_HEADS), F32)],
    )
    return pl.pallas_call(
        functools.partial(_attn_s_body, n_pages=n_pages, n_steps=n_steps, n_s=n_s, lam_init=lam_init),
        grid_spec=grid_spec,
        out_shape=jax.ShapeDtypeStruct((db, n_s, N_HEADS, V_DIM), F32),
        compiler_params=_params("parallel", "arbitrary"),
        name="attn_sample",
    )(page_table, q, kn, vn, bias_s, lam_p, subln, *([cache_k] * n_pages), *([cache_v] * n_pages))


def _outproj_body(a_ref, x_ref, w_ref, g1_ref, o_ref):
    m = jnp.dot(a_ref[...].astype(BF16), w_ref[...], preferred_element_type=F32)
    o_ref[...] = x_ref[...] + _rms(m, g1_ref[...])


def _outproj(a, x, wo, mix_norm, layer, j, *, tm):
    n, d = x.shape
    row = pl.BlockSpec((tm, d), lambda i: (i, 0))
    return pl.pallas_call(
        _outproj_body,
        grid=(pl.cdiv(n, tm),),
        in_specs=[row, row, _const_spec((None, d, d), (j, 0, 0)), _const_spec((None, 1, d), (layer * 2 + 1, 0, 0))],
        out_specs=row,
        out_shape=jax.ShapeDtypeStruct((n, d), F32),
        compiler_params=_params("parallel"),
        name="attn_outproj",
    )(a, x, wo, mix_norm)


def _final_p_body(x_ref, g_ref, o_ref, *, skip, rows):
    t = pl.program_id(1)
    r0 = pl.multiple_of(skip + t * rows, 8)
    o_ref[0] = _rms(x_ref[0, pl.ds(r0, rows), :], g_ref[...])


def _final_prompt(x, g, *, skip, rows):
    b, t, d = x.shape
    n_t = (t - skip) // rows
    return pl.pallas_call(
        functools.partial(_final_p_body, skip=skip, rows=rows),
        grid=(b, n_t),
        in_specs=[pl.BlockSpec((1, t, d), lambda bi, ti: (bi, 0, 0)), _const_spec((1, d), (0, 0))],
        out_specs=pl.BlockSpec((1, rows, d), lambda bi, ti: (bi, ti, 0)),
        out_shape=jax.ShapeDtypeStruct((b, t - skip, d), F32),
        compiler_params=_params("parallel", "arbitrary"),
        name="final_norm_prompt",
    )(x, g)


def _final_s_body(x_ref, g_ref, o_ref):
    o_ref[...] = _rms(x_ref[...], g_ref[...])


def _final_sample(x, g, *, tm):
    n, d = x.shape
    row = pl.BlockSpec((tm, d), lambda i: (i, 0))
    return pl.pallas_call(
        _final_s_body,
        grid=(pl.cdiv(n, tm),),
        in_specs=[row, _const_spec((1, d), (0, 0))],
        out_specs=row,
        out_shape=jax.ShapeDtypeStruct((n, d), F32),
        compiler_params=_params("parallel"),
        name="final_norm_sample",
    )(x, g)


def _pick_tile(n, cap, align=8):
    best = None
    for c in range(align, min(n, cap) + 1, align):
        if n % c == 0:
            best = c
    assert best is not None, (n, cap, align)
    return best


def kernel(x_prompt, x_sample, state_pool, state_conv, cache_k, cache_v, page_table, meta_tokens, rel_bias_table, ffn_norm, ffn_wg, ffn_wu, ffn_wd, mix_norm, pool_w, pool_scale, attn_wqkv, attn_wo, attn_lambda, attn_subln, conv_w1, conv_b1, conv_wdw, conv_bdw, conv_ln_g, conv_ln_b, conv_w2, conv_b2, final_norm):
    b, seq, d = x_prompt.shape
    db, n_s, _ = x_sample.shape
    depth = ffn_wg.shape[0]
    t = seq + N_META
    past = page_table.shape[1] * PAGE_SIZE

    wg, wu, wd = ffn_wg.astype(BF16), ffn_wu.astype(BF16), ffn_wd.astype(BF16)
    ffn_norm3 = ffn_norm.reshape(-1, 1, d)
    mix_norm3 = mix_norm.reshape(-1, 1, d)
    pool_w_b = pool_w.astype(BF16)
    pool_scale3 = pool_scale.reshape(-1, 1, d)
    wqkv_b, wo_b = attn_wqkv.astype(BF16), attn_wo.astype(BF16)
    subln3 = attn_subln.reshape(-1, 1, V_DIM)
    cw = (conv_w1.astype(BF16), conv_b1.reshape(-1, 1, 2 * d), conv_wdw, conv_bdw.reshape(-1, 1, d),
          conv_ln_g.reshape(-1, 1, d), conv_ln_b.reshape(-1, 1, d), conv_w2.astype(BF16), conv_b2.reshape(-1, 1, d))
    state_pool2 = state_pool.reshape(state_pool.shape[0], db, POOL_BUF * d)
    state_conv2 = state_conv.reshape(state_conv.shape[0], db, CONV_BUF * d)

    meta = jnp.broadcast_to(meta_tokens[None].astype(x_prompt.dtype), (b, N_META, d))
    xp = jnp.concatenate([meta, x_prompt], axis=1).reshape(b * t, d)
    xs = x_sample.reshape(db * n_s, d)

    tm_p = _pick_tile(b * t, 768, 16)
    tm_s = min(512, db * n_s)
    ts_pool = _pick_tile(t, 1032)
    ts_conv = _pick_tile(t, 344)
    bt = _pick_tile(db, 32)

    bias_p = bias_s = None
    pool_p, pool_s, conv_p, conv_s = [], [], [], []
    k_p = v_p = k_s = v_s = None
    for i in range(depth):
        kind, j = i % N_MIXERS, i // N_MIXERS
        xp, xs = _ffn(xp, xs, ffn_norm3, wg, wu, wd, i, 0, tm=tm_p)
        if kind == 0:
            xp3, st = _pool_prompt(xp.reshape(b, t, d), mix_norm3, pool_w_b, pool_scale3, i, j, ts=ts_pool)
            xp = xp3.reshape(b * t, d)
            pool_p.append(st)
            xs2, st = _pool_sample(xs.reshape(db, n_s * d), state_pool2, mix_norm3, pool_w_b, pool_scale3, i, j,
                                   n_s=n_s, past=past, bt=bt)
            xs = xs2.reshape(db * n_s, d)
            pool_s.append(st.reshape(db, POOL_BUF, d))
        elif kind == 1:
            assert j == 0
            lam_init = _lambda_init(i)
            if bias_p is None:
                bias_p, bias_s = _bias_tiles(rel_bias_table, blk=ATTN_BLOCK, n_s=n_s)
            q, k_p, v_p, kb, vb = _qkv(xp, mix_norm3, wqkv_b, i, j, tm=tm_p)
            xp = _attn_prompt(q.reshape(b, t, d), kb.reshape(b, t, d), vb.reshape(b, t, d), xp.reshape(b, t, d),
                              bias_p, attn_lambda, attn_subln.reshape(-1, V_DIM, 1), wo_b, mix_norm3, i, j,
                              blk=ATTN_BLOCK, lam_init=lam_init).reshape(b * t, d)
            q, k_s, v_s, _, _ = _qkv(xs, mix_norm3, wqkv_b, i, j, tm=tm_s)
            k_s = k_s.reshape(db, n_s, 1, N_HEADS, V_DIM)
            v_s = v_s.reshape(db, n_s, 1, N_HEADS, V_DIM)
            a = _attn_sample(q.astype(F32).reshape(db, n_s, N_HEADS, V_DIM), k_s, v_s, cache_k, cache_v,
                             page_table, bias_s, attn_lambda, subln3, j, n_pages=page_table.shape[1], lam_init=lam_init)
            xs = _outproj(a.reshape(db * n_s, d), xs, wo_b, mix_norm3, i, j, tm=tm_s)
        else:
            xp3, st = _conv_prompt(xp.reshape(b, t, d), mix_norm3, cw, i, j, ts=ts_conv)
            xp = xp3.reshape(b * t, d)
            conv_p.append(st)
            xs2, st = _conv_sample(xs.reshape(db, n_s * d), state_conv2, mix_norm3, cw, i, j, n_s=n_s, bt=bt)
            xs = xs2.reshape(db * n_s, d)
            conv_s.append(st.reshape(db, CONV_BUF, d))
        xp, xs = _ffn(xp, xs, ffn_norm3, wg, wu, wd, i, 1, tm=tm_p)

    g = final_norm.reshape(1, d)
    y_prompt = _final_prompt(xp.reshape(b, t, d), g, skip=N_META, rows=_pick_tile(seq, 1024))
    y_sample = _final_sample(xs, g, tm=tm_s).reshape(db, n_s, d)
    kv_shape_p = (b, t, 1, N_HEADS, V_DIM)
    return (y_prompt, y_sample, jnp.stack(pool_p, axis=0), jnp.stack(pool_s, axis=0),
            jnp.stack(conv_p, axis=0), jnp.stack(conv_s, axis=0),
            k_p.reshape(kv_shape_p), v_p.reshape(kv_shape_p), k_s, v_s)
```

```python
import functools
import math

import jax
import jax.numpy as jnp
from jax import lax
from jax.experimental import pallas as pl
from jax.experimental.pallas import tpu as pltpu

F32 = jnp.float32
BF16 = jnp.bfloat16

N_MIXERS = 3
N_META = 16
N_HEADS = 8
HEAD_DIM = 64
V_DIM = 2 * HEAD_DIM
N_BUCKETS = 32
MAX_EXACT = N_BUCKETS // 2
MAX_DISTANCE = 128
POOL_WINDOWS = (2, 4, 8, 16)
POOL_BUF = max(POOL_WINDOWS) - 1
CONV_WIDTH = 31
CONV_BUF = CONV_WIDTH - 1
PAGE_SIZE = 128
RMS_EPS = 1e-6
LN_EPS = 1e-5
NEG_INF = -1e30
LOG2E = math.log2(math.e)
SUBLANES = 8
LANES = 128
BF16_ROWS = 16
MXU_TILE = 256
SUM_ROWS = 16
ATTN_BLOCK = 256
POOL_HALO = 16
CONV_HALO = 32
VMEM_LIMIT = 56 * 1024 * 1024


def _lambda_init(layer_idx):
    return 0.8 - 0.6 * math.exp(-0.3 * layer_idx)


def _first_far_distance():
    n = MAX_EXACT
    while MAX_EXACT + int(math.log(n / MAX_EXACT) / math.log(MAX_DISTANCE / MAX_EXACT) * (N_BUCKETS - MAX_EXACT)) < N_BUCKETS - 1:
        n += 1
    return n


def _params(*sem):
    return pltpu.CompilerParams(dimension_semantics=sem, vmem_limit_bytes=VMEM_LIMIT)


def _rms(x, g):
    return x * lax.rsqrt(jnp.mean(x * x, axis=-1, keepdims=True) + RMS_EPS) * g


def _silu(x):
    return x * jax.nn.sigmoid(x)


def _const_spec(shape, index, single=False):
    if single:
        return pl.BlockSpec(shape, lambda *_: index, pipeline_mode=pl.Buffered(1))
    return pl.BlockSpec(shape, lambda *_: index)


def _ffn_chunks(ff, n_chunks):
    tiles = -(-ff // MXU_TILE)
    edges = [min(ff, MXU_TILE * (-(-tiles * c // n_chunks))) for c in range(n_chunks + 1)]
    return [(lo, hi - lo) for lo, hi in zip(edges[:-1], edges[1:]) if hi > lo]


def _ffn_body(xp_ref, xs_ref, n0_ref, n1_ref, gf_ref, wg_ref, wu_ref, wd_ref, op_ref, os_ref, *, n_chunks, n_p, final):
    i = pl.program_id(0)

    def apply(x_ref, o_ref):
        rows = x_ref.shape[0]
        first = -(-rows // (2 * BF16_ROWS)) * BF16_ROWS
        for lo_row, n_rows in ((0, first), (first, rows - first))[:2 if rows > first else 1]:
            rs = pl.ds(lo_row, n_rows)
            x = x_ref[rs, :]
            h = _rms(x, n0_ref[...]).astype(BF16)
            acc = jnp.zeros(x.shape, F32)
            for lo, width in _ffn_chunks(wg_ref.shape[1], n_chunks):
                sl = pl.ds(lo, width)
                g = jnp.dot(h, wg_ref[:, sl], preferred_element_type=F32)
                u = jnp.dot(h, wu_ref[:, sl], preferred_element_type=F32)
                a = (_silu(g) * u).astype(BF16)
                acc = acc + jnp.dot(a, wd_ref[sl, :], preferred_element_type=F32)
            y = x + 0.5 * _rms(acc, n1_ref[...])
            o_ref[rs, :] = _rms(y, gf_ref[...]) if final else y

    @pl.when(i == 0)
    def _():
        os_ref[...] = jnp.zeros(os_ref.shape, F32)

    @pl.when(i < n_p)
    def _():
        apply(xp_ref, op_ref)

    @pl.when(i >= n_p)
    def _():
        apply(xs_ref, os_ref)


def _ffn(xp, xs, norms, final_g, wg, wu, wd, layer, f, *, tm, n_chunks=2, final_seq=None):
    (n_rows_p, d), n_rows_s = xp.shape, xs.shape[0]
    ff = wg.shape[-1]
    nidx = (layer * 2 + f) * 2
    tm_s = _pick_tile(n_rows_s, tm, BF16_ROWS)
    if final_seq is None:
        n_p, out_rows_p = pl.cdiv(n_rows_p, tm), n_rows_p
        p_in = p_out = pl.BlockSpec((tm, d), lambda i: (jnp.minimum(i, n_p - 1), 0))
    else:
        t, skip = final_seq
        assert skip % BF16_ROWS == 0 and t % BF16_ROWS == 0
        tm = _pick_tile(t - skip, tm, BF16_ROWS)
        per_seq = (t - skip) // tm
        n_p, out_rows_p = (n_rows_p // t) * per_seq, (n_rows_p // t) * (t - skip)

        def in_rows(i):
            ip = jnp.minimum(i, n_p - 1)
            return pl.multiple_of((ip // per_seq) * t + skip + (ip % per_seq) * tm, BF16_ROWS), 0

        p_in = pl.BlockSpec((pl.Element(tm), pl.Element(d)), in_rows)
        p_out = pl.BlockSpec((tm, d), lambda i: (jnp.minimum(i, n_p - 1), 0))
    n_s = pl.cdiv(n_rows_s, tm_s)
    s_spec = pl.BlockSpec((tm_s, d), lambda i: (jnp.maximum(i - n_p, 0), 0))
    return pl.pallas_call(
        functools.partial(_ffn_body, n_chunks=n_chunks, n_p=n_p, final=final_seq is not None),
        grid=(n_p + n_s,),
        in_specs=[
            p_in, s_spec,
            _const_spec((None, 1, d), (nidx, 0, 0)),
            _const_spec((None, 1, d), (nidx + 1, 0, 0)),
            _const_spec((1, d), (0, 0)),
            _const_spec((None, None, d, ff), (layer, f, 0, 0), single=True),
            _const_spec((None, None, d, ff), (layer, f, 0, 0), single=True),
            _const_spec((None, None, ff, d), (layer, f, 0, 0), single=True),
        ],
        out_specs=[p_out, s_spec],
        out_shape=[jax.ShapeDtypeStruct((out_rows_p, d), F32), jax.ShapeDtypeStruct((n_rows_s, d), F32)],
        compiler_params=_params("arbitrary"),
        name="ffn",
    )(xp, xs, norms, norms, final_g, wg, wu, wd)


def _pool_p_body(x_ref, g0_ref, g1_ref, w_ref, sc_ref, o_ref, st_ref, *lvl_refs, ts, n_t):
    t = pl.program_id(1)
    d = x_ref.shape[-1]
    n_g = len(POOL_WINDOWS)
    gd = d // n_g
    x = x_ref[0]
    h = _rms(x, g0_ref[...])
    cat_ref = lvl_refs[0]
    top = SUBLANES + POOL_HALO
    rows = POOL_HALO + ts

    @pl.when(t == 0)
    def _():
        cat_ref[0:top, :] = jnp.zeros((top, d), F32)
        for ref in lvl_refs[1:]:
            ref[0:SUBLANES, :] = jnp.zeros((SUBLANES, ref.shape[1]), F32)

    cat_ref[top:top + ts, :] = h
    for k in range(1, len(lvl_refs)):
        prev, cur = lvl_refs[k - 1], lvl_refs[k]
        lo = prev.shape[1] - cur.shape[1]
        cur[SUBLANES:SUBLANES + rows, :] = (prev[SUBLANES:SUBLANES + rows, lo:]
                                            + prev[SUBLANES - 2 ** (k - 1):SUBLANES - 2 ** (k - 1) + rows, lo:])
    pos = t * ts + lax.broadcasted_iota(jnp.int32, (ts, 1), 0)
    outs = []
    for g, w in enumerate(POOL_WINDOWS):
        c0 = g * gd
        k = w.bit_length() - 1
        if k < len(lvl_refs):
            ref = lvl_refs[k]
            l0 = c0 - (d - ref.shape[1])
            acc = ref[top:top + ts, l0:l0 + gd]
        else:
            ref = lvl_refs[k - 1]
            l0 = c0 - (d - ref.shape[1])
            acc = ref[top:top + ts, l0:l0 + gd] + ref[top - w // 2:top - w // 2 + ts, l0:l0 + gd]
        inv_cnt = 1.0 / jnp.minimum(w, pos + 1).astype(F32)
        pooled = (acc * inv_cnt - h[:, c0:c0 + gd]).astype(BF16)
        outs.append(jnp.dot(pooled, w_ref[g], preferred_element_type=F32))
    m = jnp.concatenate(outs, axis=-1) * sc_ref[...]
    o_ref[0] = x + _rms(m, g1_ref[...])

    @pl.when(t == n_t - 1)
    def _():
        st_ref[0] = cat_ref[top + ts - POOL_BUF:top + ts, :]

    cat_ref[SUBLANES:top, :] = cat_ref[SUBLANES + ts:top + ts, :]


def _pool_prompt(x, mix_norm, pool_w, pool_scale, layer, j, *, ts):
    b, t, d = x.shape
    n_t = t // ts
    assert n_t * ts == t and ts % 8 == 0 and ts >= POOL_HALO
    g = len(POOL_WINDOWS)
    assert all(w == 2 ** (i + 1) for i, w in enumerate(POOL_WINDOWS)) and POOL_WINDOWS[-1] // 2 == SUBLANES
    buf_rows = SUBLANES + POOL_HALO + ts
    level_lanes = [d] + [d - i * (d // g) for i in range(g - 1)]
    return pl.pallas_call(
        functools.partial(_pool_p_body, ts=ts, n_t=n_t),
        grid=(b, n_t),
        in_specs=[
            pl.BlockSpec((1, ts, d), lambda bi, ti: (bi, ti, 0)),
            _const_spec((None, 1, d), (layer * 2, 0, 0)),
            _const_spec((None, 1, d), (layer * 2 + 1, 0, 0)),
            _const_spec((None, g, d // g, d // g), (j, 0, 0, 0)),
            _const_spec((None, 1, d), (j, 0, 0)),
        ],
        out_specs=[
            pl.BlockSpec((1, ts, d), lambda bi, ti: (bi, ti, 0)),
            pl.BlockSpec((1, POOL_BUF, d), lambda bi, ti: (bi, 0, 0)),
        ],
        out_shape=[jax.ShapeDtypeStruct((b, t, d), F32), jax.ShapeDtypeStruct((b, POOL_BUF, d), F32)],
        scratch_shapes=[pltpu.VMEM((buf_rows, lanes), F32) for lanes in level_lanes],
        compiler_params=_params("parallel", "arbitrary"),
        name="pool_prompt",
    )(x, mix_norm, mix_norm, pool_w, pool_scale)


def _pool_s_body(x_ref, st_ref, g0_ref, g1_ref, w_ref, sc_ref, o_ref, so_ref, *, n_s, past):
    d = g0_ref.shape[-1]
    gd = d // len(POOL_WINDOWS)
    bt = x_ref.shape[0]
    xs = [x_ref[:, s * d:(s + 1) * d] for s in range(n_s)]
    hs = [_rms(x, g0_ref[...]) for x in xs]

    def cat(idx, c0, width):
        if idx < POOL_BUF:
            return st_ref[:, idx * d + c0:idx * d + c0 + width]
        return hs[idx - POOL_BUF][:, c0:c0 + width]

    outs = []
    for g, w in enumerate(POOL_WINDOWS):
        c0 = g * gd
        rows = []
        for s in range(n_s):
            acc = cat(POOL_BUF + s, c0, gd)
            for jj in range(1, w):
                acc = acc + cat(POOL_BUF + s - jj, c0, gd)
            rows.append(acc * (1.0 / min(w, past + s + 1)) - hs[s][:, c0:c0 + gd])
        pooled = jnp.concatenate(rows, axis=0).astype(BF16)
        outs.append(jnp.dot(pooled, w_ref[g], preferred_element_type=F32))
    for s in range(n_s):
        m = jnp.concatenate([o[s * bt:(s + 1) * bt] for o in outs], axis=-1) * sc_ref[...]
        o_ref[:, s * d:(s + 1) * d] = xs[s] + _rms(m, g1_ref[...])
    for jj in range(POOL_BUF):
        so_ref[:, jj * d:(jj + 1) * d] = cat(n_s + jj, 0, d)


def _pool_sample(x, state, mix_norm, pool_w, pool_scale, layer, j, *, n_s, past, bt):
    db = x.shape[0]
    d = mix_norm.shape[-1]
    g = len(POOL_WINDOWS)
    return pl.pallas_call(
        functools.partial(_pool_s_body, n_s=n_s, past=past),
        grid=(db // bt,),
        in_specs=[
            pl.BlockSpec((bt, n_s * d), lambda i: (i, 0)),
            pl.BlockSpec((None, bt, POOL_BUF * d), lambda i: (j, i, 0)),
            _const_spec((None, 1, d), (layer * 2, 0, 0)),
            _const_spec((None, 1, d), (layer * 2 + 1, 0, 0)),
            _const_spec((None, g, d // g, d // g), (j, 0, 0, 0)),
            _const_spec((None, 1, d), (j, 0, 0)),
        ],
        out_specs=[
            pl.BlockSpec((bt, n_s * d), lambda i: (i, 0)),
            pl.BlockSpec((bt, POOL_BUF * d), lambda i: (i, 0)),
        ],
        out_shape=[jax.ShapeDtypeStruct((db, n_s * d), F32), jax.ShapeDtypeStruct((db, POOL_BUF * d), F32)],
        compiler_params=_params("parallel"),
        name="pool_sample",
    )(x, state, mix_norm, mix_norm, pool_w, pool_scale)


def _layer_norm(y, g, b):
    mu = jnp.mean(y, axis=-1, keepdims=True)
    yc = y - mu
    var = jnp.mean(yc * yc, axis=-1, keepdims=True)
    return yc * lax.rsqrt(var + LN_EPS) * g + b


def _conv_p_body(x_ref, g0_ref, g1_ref, w1_ref, b1_ref, wdw_ref, bdw_ref, lng_ref, lnb_ref, w2_ref, b2_ref,
                 o_ref, st_ref, cat_ref, y_ref, *, ts, n_t):
    t = pl.program_id(1)
    d = x_ref.shape[-1]
    x = x_ref[0]
    h = _rms(x, g0_ref[...]).astype(BF16)
    glu = jnp.dot(h, w1_ref[...], preferred_element_type=F32) + b1_ref[...]
    u = glu[:, :d] * jax.nn.sigmoid(glu[:, d:])

    @pl.when(t == 0)
    def _():
        cat_ref[0:CONV_HALO, :] = jnp.zeros((CONV_HALO, d), F32)
        cat_ref[CONV_HALO + ts:CONV_HALO + ts + SUBLANES, :] = jnp.zeros((SUBLANES, d), F32)

    cat_ref[CONV_HALO:CONV_HALO + ts, :] = u
    off = CONV_HALO - CONV_BUF
    zrows = ts + SUBLANES
    for c0 in range(0, d, LANES):
        y = None
        for r in range(SUBLANES):
            z = None
            for a in range((CONV_WIDTH + off) // SUBLANES + 1):
                k = SUBLANES * a + r - off
                if 0 <= k < CONV_WIDTH:
                    term = cat_ref[SUBLANES * a:SUBLANES * a + zrows, c0:c0 + LANES] * wdw_ref[k:k + 1, c0:c0 + LANES]
                    z = term if z is None else z + term
            zs = z[r:r + ts]
            y = zs if y is None else y + zs
        y_ref[:, c0:c0 + LANES] = y + bdw_ref[:, c0:c0 + LANES]
    a = _silu(_layer_norm(y_ref[...], lng_ref[...], lnb_ref[...])).astype(BF16)
    m = jnp.dot(a, w2_ref[...], preferred_element_type=F32) + b2_ref[...]
    o_ref[0] = x + _rms(m, g1_ref[...])

    @pl.when(t == n_t - 1)
    def _():
        st_ref[0] = cat_ref[CONV_HALO + ts - CONV_BUF:CONV_HALO + ts, :]

    cat_ref[0:CONV_HALO, :] = cat_ref[ts:ts + CONV_HALO, :]


def _conv_specs(layer, j, d):
    return [
        _const_spec((None, 1, d), (layer * 2, 0, 0)),
        _const_spec((None, 1, d), (layer * 2 + 1, 0, 0)),
        _const_spec((None, d, 2 * d), (j, 0, 0)),
        _const_spec((None, 1, 2 * d), (j, 0, 0)),
        _const_spec((None, CONV_WIDTH, d), (j, 0, 0)),
        _const_spec((None, 1, d), (j, 0, 0)),
        _const_spec((None, 1, d), (j, 0, 0)),
        _const_spec((None, 1, d), (j, 0, 0)),
        _const_spec((None, d, d), (j, 0, 0)),
        _const_spec((None, 1, d), (j, 0, 0)),
    ]


def _conv_prompt(x, mix_norm, cw, layer, j, *, ts):
    b, t, d = x.shape
    n_t = t // ts
    assert n_t * ts == t and ts % 8 == 0 and ts >= CONV_HALO
    return pl.pallas_call(
        functools.partial(_conv_p_body, ts=ts, n_t=n_t),
        grid=(b, n_t),
        in_specs=[pl.BlockSpec((1, ts, d), lambda bi, ti: (bi, ti, 0))] + _conv_specs(layer, j, d),
        out_specs=[
            pl.BlockSpec((1, ts, d), lambda bi, ti: (bi, ti, 0)),
            pl.BlockSpec((1, CONV_BUF, d), lambda bi, ti: (bi, 0, 0)),
        ],
        out_shape=[jax.ShapeDtypeStruct((b, t, d), F32), jax.ShapeDtypeStruct((b, CONV_BUF, d), F32)],
        scratch_shapes=[pltpu.VMEM((CONV_HALO + ts + SUBLANES, d), F32), pltpu.VMEM((ts, d), F32)],
        compiler_params=_params("parallel", "arbitrary"),
        name="conv_prompt",
    )(x, mix_norm, mix_norm, *cw)


def _conv_s_body(x_ref, st_ref, g0_ref, g1_ref, w1_ref, b1_ref, wdw_ref, bdw_ref, lng_ref, lnb_ref, w2_ref, b2_ref,
                 o_ref, so_ref, *, n_s):
    d = g0_ref.shape[-1]
    bt = x_ref.shape[0]
    xs = [x_ref[:, s * d:(s + 1) * d] for s in range(n_s)]
    h = jnp.concatenate([_rms(x, g0_ref[...]) for x in xs], axis=0).astype(BF16)
    z = jnp.dot(h, w1_ref[...], preferred_element_type=F32) + b1_ref[...]
    u = z[:, :d] * jax.nn.sigmoid(z[:, d:])
    us = [u[s * bt:(s + 1) * bt] for s in range(n_s)]

    def cat(idx):
        if idx < CONV_BUF:
            return st_ref[:, idx * d:(idx + 1) * d]
        return us[idx - CONV_BUF]

    acts = []
    for s in range(n_s):
        y = cat(s) * wdw_ref[0:1, :] + bdw_ref[...]
        for k in range(1, CONV_WIDTH):
            y = y + cat(s + k) * wdw_ref[k:k + 1, :]
        acts.append(_silu(_layer_norm(y, lng_ref[...], lnb_ref[...])))
    a = jnp.concatenate(acts, axis=0).astype(BF16)
    m = jnp.dot(a, w2_ref[...], preferred_element_type=F32) + b2_ref[...]
    for s in range(n_s):
        o_ref[:, s * d:(s + 1) * d] = xs[s] + _rms(m[s * bt:(s + 1) * bt], g1_ref[...])
    for jj in range(CONV_BUF):
        so_ref[:, jj * d:(jj + 1) * d] = cat(n_s + jj)


def _conv_sample(x, state, mix_norm, cw, layer, j, *, n_s, bt):
    db = x.shape[0]
    d = mix_norm.shape[-1]
    return pl.pallas_call(
        functools.partial(_conv_s_body, n_s=n_s),
        grid=(db // bt,),
        in_specs=[
            pl.BlockSpec((bt, n_s * d), lambda i: (i, 0)),
            pl.BlockSpec((None, bt, CONV_BUF * d), lambda i: (j, i, 0)),
        ] + _conv_specs(layer, j, d),
        out_specs=[
            pl.BlockSpec((bt, n_s * d), lambda i: (i, 0)),
            pl.BlockSpec((bt, CONV_BUF * d), lambda i: (i, 0)),
        ],
        out_shape=[jax.ShapeDtypeStruct((db, n_s * d), F32), jax.ShapeDtypeStruct((db, CONV_BUF * d), F32)],
        compiler_params=_params("parallel"),
        name="conv_sample",
    )(x, state, mix_norm, mix_norm, *cw)


def _qkv_body(x_ref, g0_ref, w_ref, q_ref, k_ref, v_ref, kb_ref, vb_ref):
    d = x_ref.shape[-1]
    h = _rms(x_ref[...], g0_ref[...]).astype(BF16)
    qkv = jnp.dot(h, w_ref[...], preferred_element_type=F32)
    q_ref[...] = (qkv[:, :d] * (HEAD_DIM ** -0.5 * LOG2E)).astype(BF16)
    k = qkv[:, d:2 * d]
    v = qkv[:, 2 * d:]
    k_ref[...] = k
    v_ref[...] = v
    kb_ref[...] = k.astype(BF16)
    vb_ref[...] = v.astype(BF16)


def _qkv(x, mix_norm, wqkv, layer, j, *, tm):
    n, d = x.shape
    row = pl.BlockSpec((tm, d), lambda i: (i, 0))
    return pl.pallas_call(
        _qkv_body,
        grid=(pl.cdiv(n, tm),),
        in_specs=[row, _const_spec((None, 1, d), (layer * 2, 0, 0)), _const_spec((None, d, 3 * d), (j, 0, 0))],
        out_specs=[row] * 5,
        out_shape=[jax.ShapeDtypeStruct((n, d), BF16), jax.ShapeDtypeStruct((n, d), F32),
                   jax.ShapeDtypeStruct((n, d), F32), jax.ShapeDtypeStruct((n, d), BF16),
                   jax.ShapeDtypeStruct((n, d), BF16)],
        compiler_params=_params("parallel"),
        name="qkv_proj",
    )(x, mix_norm, wqkv)


def _bucket(n):
    nf = jnp.maximum(n, 1).astype(F32)
    large = MAX_EXACT + (jnp.log(nf / MAX_EXACT) / math.log(MAX_DISTANCE / MAX_EXACT)
                         * (N_BUCKETS - MAX_EXACT)).astype(jnp.int32)
    large = jnp.minimum(large, N_BUCKETS - 1)
    return jnp.where(n < MAX_EXACT, n, large)


def _lookup(bucket, entry):
    out = jnp.zeros(bucket.shape, F32)
    for b in range(N_BUCKETS):
        out = jnp.where(bucket == b, entry(b), out)
    return out


def _bias_p_body(table_ref, bp_ref, *, blk):
    head = pl.program_id(0)
    a = lax.broadcasted_iota(jnp.int32, (blk, blk), 0)
    b = lax.broadcasted_iota(jnp.int32, (blk, blk), 1)
    for sel in range(3):
        n = sel * blk + b - a
        vals = _lookup(_bucket(jnp.maximum(n, 0)), lambda bb: table_ref[bb * N_HEADS + head])
        bp_ref[0, sel] = jnp.where(n >= 0, vals * LOG2E, NEG_INF)


def _bias_s_body(tt_ref, bs_ref, *, n_s):
    rows = 2 * n_s * N_HEADS
    cols = PAGE_SIZE * N_HEADS
    tt = tt_ref[...]
    trow = jnp.broadcast_to(tt[None], (rows // N_HEADS, N_HEADS, N_BUCKETS)).reshape(rows, N_BUCKETS)

    def tile(width, dist):
        r = lax.broadcasted_iota(jnp.int32, (rows, width), 0)
        c = lax.broadcasted_iota(jnp.int32, (rows, width), 1)
        n = dist((r // N_HEADS) % n_s, c // N_HEADS)
        vals = _lookup(_bucket(jnp.maximum(n, 0)), lambda bb: trow[:, bb:bb + 1])
        return jnp.where((r % N_HEADS == c % N_HEADS) & (n >= 0) & (c < cols), vals * LOG2E, NEG_INF)

    bs_ref[:, 0:cols] = tile(cols, lambda qi, kk: 2 * PAGE_SIZE + qi - kk)
    bs_ref[:, cols:2 * cols] = tile(cols, lambda qi, kk: PAGE_SIZE + qi - kk)
    bs_ref[:, 2 * cols:2 * cols + PAGE_SIZE] = tile(
        PAGE_SIZE, lambda qi, kk: jnp.where(kk < n_s, qi - kk, -1))


def _bias_tiles(table, *, blk, n_s):
    far = _first_far_distance()
    assert blk + 1 >= far and PAGE_SIZE + 1 >= far and n_s * N_HEADS <= PAGE_SIZE
    bias_p = pl.pallas_call(
        functools.partial(_bias_p_body, blk=blk),
        grid=(N_HEADS,),
        in_specs=[pl.BlockSpec(memory_space=pltpu.SMEM)],
        out_specs=pl.BlockSpec((1, 3, blk, blk), lambda h: (h, 0, 0, 0)),
        out_shape=jax.ShapeDtypeStruct((N_HEADS, 3, blk, blk), F32),
        compiler_params=_params("parallel"),
        name="rel_bias_prompt",
    )(table.reshape(-1))
    width = 2 * PAGE_SIZE * N_HEADS + PAGE_SIZE
    bias_s = pl.pallas_call(
        functools.partial(_bias_s_body, n_s=n_s),
        out_shape=jax.ShapeDtypeStruct((2 * n_s * N_HEADS, width), F32),
        compiler_params=pltpu.CompilerParams(vmem_limit_bytes=VMEM_LIMIT),
        name="rel_bias_sample",
    )(table.T)
    return bias_p, bias_s


def _diff_lambda(lam_ref, lam_init):
    lp = lam_ref[...]
    s1 = jnp.sum(lp[0:1] * lp[1:2], axis=-1, keepdims=True)
    s2 = jnp.sum(lp[2:3] * lp[3:4], axis=-1, keepdims=True)
    return jnp.exp(s1) - jnp.exp(s2) + lam_init


_NT = (((1,), (1,)), ((), ()))


def _attn_p_body(q_ref, k_ref, v_ref, x_ref, bias_ref, lam_ref, sub_ref, wo_ref, g1_ref, o_ref,
                 qz_ref, vt_ref, m_ref, acc_ref, s_ref, a_ref, *, blk, n_full, tail, lam_init):
    i = pl.program_id(1)

    @pl.when(i == 0)
    def _():
        for hd in range(N_HEADS):
            c0 = hd * V_DIM

            ones_row = (lax.broadcasted_iota(jnp.int32, (SUM_ROWS, blk), 0) == 0).astype(BF16)

            def xpose(jb, carry):
                r0 = pl.multiple_of(jb * blk, blk)
                vt_ref[hd, jb, 0:V_DIM, :] = v_ref[0, pl.ds(r0, blk), c0:c0 + V_DIM].astype(F32).T.astype(BF16)
                vt_ref[hd, jb, V_DIM:V_DIM + SUM_ROWS, :] = ones_row
                return carry

            lax.fori_loop(0, n_full, xpose, 0)
            if tail:
                r0 = n_full * blk
                vt_ref[hd, n_full, 0:V_DIM, 0:tail] = (
                    v_ref[0, r0:r0 + tail, c0:c0 + V_DIM].astype(F32).T.astype(BF16))
                vt_ref[hd, n_full, V_DIM:V_DIM + SUM_ROWS, :] = ones_row

    def run(wq, n_loop, tail_keys):
        lane = lax.broadcasted_iota(jnp.int32, (wq, V_DIM), 1)
        for hd in range(N_HEADS):
            q = q_ref[0, 0:wq, hd * V_DIM:(hd + 1) * V_DIM]
            qz_ref[hd, 0:wq, :] = jnp.where(lane < HEAD_DIM, q, jnp.zeros_like(q))
            qz_ref[hd, wq:2 * wq, :] = jnp.where(lane >= HEAD_DIM, q, jnp.zeros_like(q))
        m_ref[:, :, 0:2 * wq] = jnp.full((N_HEADS, 1, 2 * wq), NEG_INF, F32)
        acc_ref[:, :, 0:2 * wq] = jnp.zeros((N_HEADS, V_DIM + SUM_ROWS, 2 * wq), F32)

        def all_heads(n, keys, vt, bias):
            def stage(hd):
                b2 = bias(hd)
                s_ref[hd % 2, 0:n, 0:2 * wq] = (
                    lax.dot_general(keys(hd), qz_ref[hd, 0:2 * wq, :], _NT, preferred_element_type=F32)
                    + jnp.concatenate([b2, b2], axis=1))

            stage(0)
            for hd in range(N_HEADS):
                if hd + 1 < N_HEADS:
                    stage(hd + 1)
                s = s_ref[hd % 2, 0:n, 0:2 * wq]
                m_old = m_ref[hd, :, 0:2 * wq]
                m_new = jnp.maximum(m_old, jnp.max(s, axis=0, keepdims=True))
                alpha = jnp.exp2(m_old - m_new)
                p = jnp.exp2(s - m_new)
                m_ref[hd, :, 0:2 * wq] = m_new
                acc_ref[hd, :, 0:2 * wq] = (alpha * acc_ref[hd, :, 0:2 * wq]
                                            + jnp.dot(vt(hd), p.astype(BF16), preferred_element_type=F32))

        def body(jb, carry):
            r0 = pl.multiple_of(jb * blk, blk)
            sel = jnp.minimum(i - jb, 2)
            all_heads(blk,
                      lambda hd: k_ref[0, pl.ds(r0, blk), hd * V_DIM:(hd + 1) * V_DIM],
                      lambda hd: vt_ref[hd, jb],
                      lambda hd: bias_ref[hd, sel, :, 0:wq])
            return carry

        lax.fori_loop(0, n_loop, body, 0)
        if tail_keys:
            r0 = n_full * blk
            all_heads(tail_keys,
                      lambda hd: k_ref[0, r0:r0 + tail_keys, hd * V_DIM:(hd + 1) * V_DIM],
                      lambda hd: vt_ref[hd, n_full, :, 0:tail_keys],
                      lambda hd: bias_ref[hd, 0, 0:tail_keys, 0:wq])

        lam = _diff_lambda(lam_ref, lam_init)
        for hd in range(N_HEADS):
            o_both = acc_ref[hd, 0:V_DIM, 0:2 * wq] * (1.0 / acc_ref[hd, V_DIM:V_DIM + 1, 0:2 * wq])
            o = o_both[:, 0:wq] - lam * o_both[:, wq:2 * wq]
            o = o * lax.rsqrt(jnp.mean(o * o, axis=0, keepdims=True) + RMS_EPS) * sub_ref[...] * (1.0 - lam_init)
            a_ref[0:wq, hd * V_DIM:(hd + 1) * V_DIM] = o.T.astype(BF16)
        mix = jnp.dot(a_ref[0:wq, :], wo_ref[...], preferred_element_type=F32)
        o_ref[0, 0:wq, :] = x_ref[0, 0:wq, :] + _rms(mix, g1_ref[...])

    @pl.when(i < n_full)
    def _():
        run(blk, i + 1, 0)

    if tail:
        @pl.when(i == n_full)
        def _():
            run(-(-tail // LANES) * LANES, n_full, tail)


def _attn_prompt(q, kb, vb, x, bias_p, lam_p, subln_col, wo, mix_norm, layer, j, *, blk, lam_init):
    b, t, d = q.shape
    n_full, tail = t // blk, t % blk
    assert tail % 16 == 0
    n_blocks = n_full + (1 if tail else 0)
    seq_blk = pl.BlockSpec((1, blk, d), lambda bi, qi: (bi, qi, 0))
    return pl.pallas_call(
        functools.partial(_attn_p_body, blk=blk, n_full=n_full, tail=tail, lam_init=lam_init),
        grid=(b, n_blocks),
        in_specs=[
            seq_blk,
            pl.BlockSpec((1, t, d), lambda bi, qi: (bi, 0, 0)),
            pl.BlockSpec((1, t, d), lambda bi, qi: (bi, 0, 0)),
            seq_blk,
            _const_spec((N_HEADS, 3, blk, blk), (0, 0, 0, 0)),
            _const_spec((None, 4, HEAD_DIM), (j, 0, 0)),
            _const_spec((None, V_DIM, 1), (j, 0, 0)),
            _const_spec((None, d, d), (j, 0, 0)),
            _const_spec((None, 1, d), (layer * 2 + 1, 0, 0)),
        ],
        out_specs=seq_blk,
        out_shape=jax.ShapeDtypeStruct((b, t, d), F32),
        scratch_shapes=[
            pltpu.VMEM((N_HEADS, 2 * blk, V_DIM), BF16),
            pltpu.VMEM((N_HEADS, n_blocks, V_DIM + SUM_ROWS, blk), BF16),
            pltpu.VMEM((N_HEADS, 1, 2 * blk), F32),
            pltpu.VMEM((N_HEADS, V_DIM + SUM_ROWS, 2 * blk), F32),
            pltpu.VMEM((2, blk, 2 * blk), F32),
            pltpu.VMEM((blk, d), BF16),
        ],
        compiler_params=_params("arbitrary", "arbitrary"),
        name="attn_prompt",
    )(q, kb, vb, x, bias_p, lam_p, subln_col, wo, mix_norm)


def _attn_s_body(pt_ref, q_ref, kn_ref, vn_ref, bias_ref, lam_ref, sub_ref, *rest, n_pages, n_s, lam_init):
    k_pages = rest[:n_pages]
    v_pages = rest[n_pages:2 * n_pages]
    o_ref, s_ref = rest[2 * n_pages:]
    half = n_s * N_HEADS
    cols = PAGE_SIZE * N_HEADS

    q = q_ref[...].reshape(half, V_DIM)
    lane = lax.broadcasted_iota(jnp.int32, (half, V_DIM), 1)
    q2 = jnp.concatenate([jnp.where(lane < HEAD_DIM, q, 0.0), jnp.where(lane >= HEAD_DIM, q, 0.0)], axis=0).astype(BF16)

    def scores(keys, bias):
        return lax.dot_general(q2, keys, _NT, preferred_element_type=F32) + bias

    pad = jnp.zeros((PAGE_SIZE - half, V_DIM), F32)
    kn = jnp.concatenate([kn_ref[...].reshape(half, V_DIM), pad], axis=0).astype(BF16)
    vn = jnp.concatenate([vn_ref[...].reshape(half, V_DIM), pad], axis=0).astype(BF16)
    s_new = scores(kn, bias_ref[:, 2 * cols:2 * cols + PAGE_SIZE])
    m = jnp.max(s_new, axis=-1, keepdims=True)
    for r_ in range(n_pages):
        near = r_ == n_pages - 1
        s = scores(k_pages[r_][...].reshape(cols, V_DIM).astype(BF16),
                   bias_ref[:, cols:2 * cols] if near else bias_ref[:, 0:cols])
        s_ref[r_] = s
        m = jnp.maximum(m, jnp.max(s, axis=-1, keepdims=True))

    p = jnp.exp2(s_new - m)
    l = jnp.sum(p, axis=-1, keepdims=True)
    acc = jnp.dot(p.astype(BF16), vn, preferred_element_type=F32)
    for r_ in range(n_pages):
        p = jnp.exp2(s_ref[r_] - m)
        l = l + jnp.sum(p, axis=-1, keepdims=True)
        acc = acc + jnp.dot(p.astype(BF16), v_pages[r_][...].reshape(cols, V_DIM).astype(BF16),
                            preferred_element_type=F32)

    lam = _diff_lambda(lam_ref, lam_init)
    o_all = acc * (1.0 / l)
    o = o_all[0:half] - lam * o_all[half:2 * half]
    o_ref[...] = (_rms(o, sub_ref[...]) * (1.0 - lam_init)).reshape(o_ref.shape)


def _attn_sample(q, kn, vn, cache_k, cache_v, page_table, bias_s, lam_p, subln, j, *, n_pages, lam_init):
    db, n_s = q.shape[:2]
    assert n_pages == page_table.shape[1]
    rows = 2 * n_s * N_HEADS
    new_kv = pl.BlockSpec((1, n_s, 1, N_HEADS, V_DIM), lambda b, pt: (b, 0, 0, 0, 0))
    q_spec = pl.BlockSpec((1, n_s, N_HEADS, V_DIM), lambda b, pt: (b, 0, 0, 0))

    def page_spec(r):
        return pl.BlockSpec((1, PAGE_SIZE, 1, N_HEADS, V_DIM), lambda b, pt: (pt[b, r], 0, j, 0, 0))

    grid_spec = pltpu.PrefetchScalarGridSpec(
        num_scalar_prefetch=1,
        grid=(db,),
        in_specs=[
            q_spec, new_kv, new_kv,
            pl.BlockSpec(bias_s.shape, lambda b, pt: (0, 0)),
            pl.BlockSpec((None, 4, HEAD_DIM), lambda b, pt: (j, 0, 0)),
            pl.BlockSpec((None, 1, V_DIM), lambda b, pt: (j, 0, 0)),
        ] + [page_spec(r) for r in range(n_pages)] * 2,
        out_specs=q_spec,
        scratch_shapes=[pltpu.VMEM((n_pages, rows, PAGE_SIZE * N_HEADS), F32)],
    )
    return pl.pallas_call(
        functools.partial(_attn_s_body, n_pages=n_pages, n_s=n_s, lam_init=lam_init),
        grid_spec=grid_spec,
        out_shape=jax.ShapeDtypeStruct((db, n_s, N_HEADS, V_DIM), F32),
        compiler_params=_params("parallel"),
        name="attn_sample",
    )(page_table, q, kn, vn, bias_s, lam_p, subln, *([cache_k] * n_pages), *([cache_v] * n_pages))


def _outproj_body(a_ref, x_ref, w_ref, g1_ref, o_ref):
    m = jnp.dot(a_ref[...].astype(BF16), w_ref[...], preferred_element_type=F32)
    o_ref[...] = x_ref[...] + _rms(m, g1_ref[...])


def _outproj(a, x, wo, mix_norm, layer, j, *, tm):
    n, d = x.shape
    row = pl.BlockSpec((tm, d), lambda i: (i, 0))
    return pl.pallas_call(
        _outproj_body,
        grid=(pl.cdiv(n, tm),),
        in_specs=[row, row, _const_spec((None, d, d), (j, 0, 0)), _const_spec((None, 1, d), (layer * 2 + 1, 0, 0))],
        out_specs=row,
        out_shape=jax.ShapeDtypeStruct((n, d), F32),
        compiler_params=_params("parallel"),
        name="attn_outproj",
    )(a, x, wo, mix_norm)


def _pick_tile(n, cap, align=8):
    best = None
    for c in range(align, min(n, cap) + 1, align):
        if n % c == 0:
            best = c
    assert best is not None, (n, cap, align)
    return best


def kernel(x_prompt, x_sample, state_pool, state_conv, cache_k, cache_v, page_table, meta_tokens, rel_bias_table, ffn_norm, ffn_wg, ffn_wu, ffn_wd, mix_norm, pool_w, pool_scale, attn_wqkv, attn_wo, attn_lambda, attn_subln, conv_w1, conv_b1, conv_wdw, conv_bdw, conv_ln_g, conv_ln_b, conv_w2, conv_b2, final_norm):
    b, seq, d = x_prompt.shape
    db, n_s, _ = x_sample.shape
    depth = ffn_wg.shape[0]
    t = seq + N_META
    past = page_table.shape[1] * PAGE_SIZE

    wg, wu, wd = ffn_wg.astype(BF16), ffn_wu.astype(BF16), ffn_wd.astype(BF16)
    ffn_norm3 = ffn_norm.reshape(-1, 1, d)
    mix_norm3 = mix_norm.reshape(-1, 1, d)
    pool_w_b = pool_w.astype(BF16)
    pool_scale3 = pool_scale.reshape(-1, 1, d)
    wqkv_b, wo_b = attn_wqkv.astype(BF16), attn_wo.astype(BF16)
    subln3 = attn_subln.reshape(-1, 1, V_DIM)
    cw = (conv_w1.astype(BF16), conv_b1.reshape(-1, 1, 2 * d), conv_wdw, conv_bdw.reshape(-1, 1, d),
          conv_ln_g.reshape(-1, 1, d), conv_ln_b.reshape(-1, 1, d), conv_w2.astype(BF16), conv_b2.reshape(-1, 1, d))
    state_pool2 = state_pool.reshape(state_pool.shape[0], db, POOL_BUF * d)
    state_conv2 = state_conv.reshape(state_conv.shape[0], db, CONV_BUF * d)
    final_g = final_norm.reshape(1, d)

    meta = jnp.broadcast_to(meta_tokens[None].astype(x_prompt.dtype), (b, N_META, d))
    xp = jnp.concatenate([meta, x_prompt], axis=1).reshape(b * t, d)
    xs = x_sample.reshape(db * n_s, d)

    tm_p = _pick_tile(b * t, 768, 16)
    tm_s = min(512, db * n_s)
    ts_pool = _pick_tile(t, 1032)
    ts_conv = _pick_tile(t, 344)
    bt = _pick_tile(db, 32)

    bias_p = bias_s = None
    pool_p, pool_s, conv_p, conv_s = [], [], [], []
    k_p = v_p = k_s = v_s = None
    for i in range(depth):
        kind, j = i % N_MIXERS, i // N_MIXERS
        xp, xs = _ffn(xp, xs, ffn_norm3, final_g, wg, wu, wd, i, 0, tm=tm_p)
        if kind == 0:
            xp3, st = _pool_prompt(xp.reshape(b, t, d), mix_norm3, pool_w_b, pool_scale3, i, j, ts=ts_pool)
            xp = xp3.reshape(b * t, d)
            pool_p.append(st)
            xs2, st = _pool_sample(xs.reshape(db, n_s * d), state_pool2, mix_norm3, pool_w_b, pool_scale3, i, j,
                                   n_s=n_s, past=past, bt=bt)
            xs = xs2.reshape(db * n_s, d)
            pool_s.append(st.reshape(db, POOL_BUF, d))
        elif kind == 1:
            assert j == 0
            lam_init = _lambda_init(i)
            if bias_p is None:
                bias_p, bias_s = _bias_tiles(rel_bias_table, blk=ATTN_BLOCK, n_s=n_s)
            q, k_p, v_p, kb, vb = _qkv(xp, mix_norm3, wqkv_b, i, j, tm=tm_p)
            xp = _attn_prompt(q.reshape(b, t, d), kb.reshape(b, t, d), vb.reshape(b, t, d), xp.reshape(b, t, d),
                              bias_p, attn_lambda, attn_subln.reshape(-1, V_DIM, 1), wo_b, mix_norm3, i, j,
                              blk=ATTN_BLOCK, lam_init=lam_init).reshape(b * t, d)
            q, k_s, v_s, _, _ = _qkv(xs, mix_norm3, wqkv_b, i, j, tm=tm_s)
            k_s = k_s.reshape(db, n_s, 1, N_HEADS, V_DIM)
            v_s = v_s.reshape(db, n_s, 1, N_HEADS, V_DIM)
            a = _attn_sample(q.astype(F32).reshape(db, n_s, N_HEADS, V_DIM), k_s, v_s, cache_k, cache_v,
                             page_table, bias_s, attn_lambda, subln3, j, n_pages=page_table.shape[1], lam_init=lam_init)
            xs = _outproj(a.reshape(db * n_s, d), xs, wo_b, mix_norm3, i, j, tm=tm_s)
        else:
            xp3, st = _conv_prompt(xp.reshape(b, t, d), mix_norm3, cw, i, j, ts=ts_conv)
            xp = xp3.reshape(b * t, d)
            conv_p.append(st)
            xs2, st = _conv_sample(xs.reshape(db, n_s * d), state_conv2, mix_norm3, cw, i, j, n_s=n_s, bt=bt)
            xs = xs2.reshape(db * n_s, d)
            conv_s.append(st.reshape(db, CONV_BUF, d))
        xp, xs = _ffn(xp, xs, ffn_norm3, final_g, wg, wu, wd, i, 1, tm=tm_p,
                      final_seq=(t, N_META) if i == depth - 1 else None)

    kv_shape_p = (b, t, 1, N_HEADS, V_DIM)
    return (xp.reshape(b, seq, d), xs.reshape(db, n_s, d), jnp.stack(pool_p, axis=0), jnp.stack(pool_s, axis=0),
            jnp.stack(conv_p, axis=0), jnp.stack(conv_s, axis=0),
            k_p.reshape(kv_shape_p), v_p.reshape(kv_shape_p), k_s, v_s)
```

```python
import functools
import math

import jax
import jax.numpy as jnp
from jax import lax
from jax.experimental import pallas as pl
from jax.experimental.pallas import tpu as pltpu

F32 = jnp.float32
BF16 = jnp.bfloat16

N_MIXERS = 3
N_META = 16
N_HEADS = 8
HEAD_DIM = 64
V_DIM = 2 * HEAD_DIM
N_BUCKETS = 32
MAX_EXACT = N_BUCKETS // 2
MAX_DISTANCE = 128
POOL_WINDOWS = (2, 4, 8, 16)
POOL_BUF = max(POOL_WINDOWS) - 1
CONV_WIDTH = 31
CONV_BUF = CONV_WIDTH - 1
PAGE_SIZE = 128
RMS_EPS = 1e-6
LN_EPS = 1e-5
NEG_INF = -1e30
LOG2E = math.log2(math.e)
SUBLANES = 8
LANES = 128
BF16_ROWS = 16
MXU_TILE = 256
SUM_ROWS = 16
ATTN_BLOCK = 256
POOL_HALO = 16
CONV_HALO = 32
VMEM_LIMIT = 56 * 1024 * 1024
WEIGHT_CHUNK_BYTES = 3 * 512 * 1024


def _lambda_init(layer_idx):
    return 0.8 - 0.6 * math.exp(-0.3 * layer_idx)


def _first_far_distance():
    n = MAX_EXACT
    while MAX_EXACT + int(math.log(n / MAX_EXACT) / math.log(MAX_DISTANCE / MAX_EXACT) * (N_BUCKETS - MAX_EXACT)) < N_BUCKETS - 1:
        n += 1
    return n


def _params(*sem):
    return pltpu.CompilerParams(dimension_semantics=sem, vmem_limit_bytes=VMEM_LIMIT)


def _rms(x, g):
    return x * lax.rsqrt(jnp.mean(x * x, axis=-1, keepdims=True) + RMS_EPS) * g


def _silu(x):
    return x * jax.nn.sigmoid(x)


def _const_spec(shape, index, single=False):
    if single:
        return pl.BlockSpec(shape, lambda *_: index, pipeline_mode=pl.Buffered(1))
    return pl.BlockSpec(shape, lambda *_: index)


def _ffn_chunks(ff, n_chunks):
    tiles = -(-ff // MXU_TILE)
    edges = [min(ff, MXU_TILE * (-(-tiles * c // n_chunks))) for c in range(n_chunks + 1)]
    return [(lo, hi - lo) for lo, hi in zip(edges[:-1], edges[1:]) if hi > lo]


def _ffn_body(xp_ref, xs_ref, n0_ref, n1_ref, gf_ref, wg_hbm, wu_hbm, wd_hbm, op_ref, os_ref,
              wg_ref, wu_ref, wd_ref, stage_in, stage_out, sem, *, n_chunks, n_p, final, layer, f):
    i = pl.program_id(0)

    def fetch(w_hbm, stage, dst):
        rows = stage.shape[1]
        n = dst.shape[0] // rows

        def copy(c):
            return pltpu.make_async_copy(w_hbm.at[layer, f, pl.ds(c * rows, rows), :], stage.at[c % 2], sem.at[c % 2])

        copy(0).start()
        for c in range(n):
            if c + 1 < n:
                copy(c + 1).start()
            copy(c).wait()
            dst[pl.ds(c * rows, rows), :] = stage[c % 2].astype(BF16)

    @pl.when(i == 0)
    def _():
        fetch(wg_hbm, stage_in, wg_ref)
        fetch(wu_hbm, stage_in, wu_ref)
        fetch(wd_hbm, stage_out, wd_ref)

    def apply(x_ref, o_ref):
        rows = x_ref.shape[0]
        first = -(-rows // (2 * BF16_ROWS)) * BF16_ROWS
        for lo_row, n_rows in ((0, first), (first, rows - first))[:2 if rows > first else 1]:
            rs = pl.ds(lo_row, n_rows)
            x = x_ref[rs, :]
            h = _rms(x, n0_ref[...]).astype(BF16)
            acc = jnp.zeros(x.shape, F32)
            for lo, width in _ffn_chunks(wg_ref.shape[1], n_chunks):
                sl = pl.ds(lo, width)
                g = jnp.dot(h, wg_ref[:, sl], preferred_element_type=F32)
                u = jnp.dot(h, wu_ref[:, sl], preferred_element_type=F32)
                a = (_silu(g) * u).astype(BF16)
                acc = acc + jnp.dot(a, wd_ref[sl, :], preferred_element_type=F32)
            y = x + 0.5 * _rms(acc, n1_ref[...])
            o_ref[rs, :] = _rms(y, gf_ref[...]) if final else y

    @pl.when(i == 0)
    def _():
        os_ref[...] = jnp.zeros(os_ref.shape, F32)

    @pl.when(i < n_p)
    def _():
        apply(xp_ref, op_ref)

    @pl.when(i >= n_p)
    def _():
        apply(xs_ref, os_ref)


def _ffn(xp, xs, norms, final_g, wg, wu, wd, layer, f, *, tm, n_chunks=2, final_seq=None):
    (n_rows_p, d), n_rows_s = xp.shape, xs.shape[0]
    ff = wg.shape[-1]
    nidx = (layer * 2 + f) * 2
    tm_s = _pick_tile(n_rows_s, tm, BF16_ROWS)
    if final_seq is None:
        n_p, out_rows_p = pl.cdiv(n_rows_p, tm), n_rows_p
        p_in = p_out = pl.BlockSpec((tm, d), lambda i: (jnp.minimum(i, n_p - 1), 0))
    else:
        t, skip = final_seq
        assert skip % BF16_ROWS == 0 and t % BF16_ROWS == 0
        tm = _pick_tile(t - skip, tm, BF16_ROWS)
        per_seq = (t - skip) // tm
        n_p, out_rows_p = (n_rows_p // t) * per_seq, (n_rows_p // t) * (t - skip)

        def in_rows(i):
            ip = jnp.minimum(i, n_p - 1)
            return pl.multiple_of((ip // per_seq) * t + skip + (ip % per_seq) * tm, BF16_ROWS), 0

        p_in = pl.BlockSpec((pl.Element(tm), pl.Element(d)), in_rows)
        p_out = pl.BlockSpec((tm, d), lambda i: (jnp.minimum(i, n_p - 1), 0))
    n_s = pl.cdiv(n_rows_s, tm_s)
    s_spec = pl.BlockSpec((tm_s, d), lambda i: (jnp.maximum(i - n_p, 0), 0))
    in_rows, out_rows = _pick_tile(d, WEIGHT_CHUNK_BYTES // (4 * ff)), _pick_tile(ff, WEIGHT_CHUNK_BYTES // (4 * d))
    hbm = pl.BlockSpec(memory_space=pl.ANY)
    return pl.pallas_call(
        functools.partial(_ffn_body, n_chunks=n_chunks, n_p=n_p, final=final_seq is not None, layer=layer, f=f),
        grid=(n_p + n_s,),
        in_specs=[
            p_in, s_spec,
            _const_spec((None, 1, d), (nidx, 0, 0)),
            _const_spec((None, 1, d), (nidx + 1, 0, 0)),
            _const_spec((1, d), (0, 0)),
            hbm, hbm, hbm,
        ],
        out_specs=[p_out, s_spec],
        out_shape=[jax.ShapeDtypeStruct((out_rows_p, d), F32), jax.ShapeDtypeStruct((n_rows_s, d), F32)],
        scratch_shapes=[
            pltpu.VMEM((d, ff), BF16), pltpu.VMEM((d, ff), BF16), pltpu.VMEM((ff, d), BF16),
            pltpu.VMEM((2, in_rows, ff), F32), pltpu.VMEM((2, out_rows, d), F32),
            pltpu.SemaphoreType.DMA((2,)),
        ],
        compiler_params=_params("arbitrary"),
        name="ffn",
    )(xp, xs, norms, norms, final_g, wg, wu, wd)


def _pool_p_body(x_ref, g0_ref, g1_ref, w_ref, sc_ref, o_ref, st_ref, *lvl_refs, ts, n_t):
    t = pl.program_id(1)
    d = x_ref.shape[-1]
    n_g = len(POOL_WINDOWS)
    gd = d // n_g
    x = x_ref[0]
    h = _rms(x, g0_ref[...])
    cat_ref = lvl_refs[0]
    top = SUBLANES + POOL_HALO
    rows = POOL_HALO + ts

    @pl.when(t == 0)
    def _():
        cat_ref[0:top, :] = jnp.zeros((top, d), F32)
        for ref in lvl_refs[1:]:
            ref[0:SUBLANES, :] = jnp.zeros((SUBLANES, ref.shape[1]), F32)

    cat_ref[top:top + ts, :] = h
    for k in range(1, len(lvl_refs)):
        prev, cur = lvl_refs[k - 1], lvl_refs[k]
        lo = prev.shape[1] - cur.shape[1]
        cur[SUBLANES:SUBLANES + rows, :] = (prev[SUBLANES:SUBLANES + rows, lo:]
                                            + prev[SUBLANES - 2 ** (k - 1):SUBLANES - 2 ** (k - 1) + rows, lo:])
    pos = t * ts + lax.broadcasted_iota(jnp.int32, (ts, 1), 0)
    outs = []
    for g, w in enumerate(POOL_WINDOWS):
        c0 = g * gd
        k = w.bit_length() - 1
        if k < len(lvl_refs):
            ref = lvl_refs[k]
            l0 = c0 - (d - ref.shape[1])
            acc = ref[top:top + ts, l0:l0 + gd]
        else:
            ref = lvl_refs[k - 1]
            l0 = c0 - (d - ref.shape[1])
            acc = ref[top:top + ts, l0:l0 + gd] + ref[top - w // 2:top - w // 2 + ts, l0:l0 + gd]
        inv_cnt = 1.0 / jnp.minimum(w, pos + 1).astype(F32)
        pooled = (acc * inv_cnt - h[:, c0:c0 + gd]).astype(BF16)
        outs.append(jnp.dot(pooled, w_ref[g], preferred_element_type=F32))
    m = jnp.concatenate(outs, axis=-1) * sc_ref[...]
    o_ref[0] = x + _rms(m, g1_ref[...])

    @pl.when(t == n_t - 1)
    def _():
        st_ref[0] = cat_ref[top + ts - POOL_BUF:top + ts, :]

    cat_ref[SUBLANES:top, :] = cat_ref[SUBLANES + ts:top + ts, :]


def _pool_prompt(x, mix_norm, pool_w, pool_scale, layer, j, *, ts):
    b, t, d = x.shape
    n_t = t // ts
    assert n_t * ts == t and ts % 8 == 0 and ts >= POOL_HALO
    g = len(POOL_WINDOWS)
    assert all(w == 2 ** (i + 1) for i, w in enumerate(POOL_WINDOWS)) and POOL_WINDOWS[-1] // 2 == SUBLANES
    buf_rows = SUBLANES + POOL_HALO + ts
    level_lanes = [d] + [d - i * (d // g) for i in range(g - 1)]
    return pl.pallas_call(
        functools.partial(_pool_p_body, ts=ts, n_t=n_t),
        grid=(b, n_t),
        in_specs=[
            pl.BlockSpec((1, ts, d), lambda bi, ti: (bi, ti, 0)),
            _const_spec((None, 1, d), (layer * 2, 0, 0)),
            _const_spec((None, 1, d), (layer * 2 + 1, 0, 0)),
            _const_spec((None, g, d // g, d // g), (j, 0, 0, 0)),
            _const_spec((None, 1, d), (j, 0, 0)),
        ],
        out_specs=[
            pl.BlockSpec((1, ts, d), lambda bi, ti: (bi, ti, 0)),
            pl.BlockSpec((1, POOL_BUF, d), lambda bi, ti: (bi, 0, 0)),
        ],
        out_shape=[jax.ShapeDtypeStruct((b, t, d), F32), jax.ShapeDtypeStruct((b, POOL_BUF, d), F32)],
        scratch_shapes=[pltpu.VMEM((buf_rows, lanes), F32) for lanes in level_lanes],
        compiler_params=_params("parallel", "arbitrary"),
        name="pool_prompt",
    )(x, mix_norm, mix_norm, pool_w, pool_scale)


def _pool_s_body(x_ref, st_ref, g0_ref, g1_ref, w_ref, sc_ref, o_ref, so_ref, *, n_s, past):
    d = g0_ref.shape[-1]
    gd = d // len(POOL_WINDOWS)
    bt = x_ref.shape[0]
    xs = [x_ref[:, s * d:(s + 1) * d] for s in range(n_s)]
    hs = [_rms(x, g0_ref[...]) for x in xs]

    def cat(idx, c0, width):
        if idx < POOL_BUF:
            return st_ref[:, idx * d + c0:idx * d + c0 + width]
        return hs[idx - POOL_BUF][:, c0:c0 + width]

    outs = []
    for g, w in enumerate(POOL_WINDOWS):
        c0 = g * gd
        rows = []
        for s in range(n_s):
            acc = cat(POOL_BUF + s, c0, gd)
            for jj in range(1, w):
                acc = acc + cat(POOL_BUF + s - jj, c0, gd)
            rows.append(acc * (1.0 / min(w, past + s + 1)) - hs[s][:, c0:c0 + gd])
        pooled = jnp.concatenate(rows, axis=0).astype(BF16)
        outs.append(jnp.dot(pooled, w_ref[g], preferred_element_type=F32))
    for s in range(n_s):
        m = jnp.concatenate([o[s * bt:(s + 1) * bt] for o in outs], axis=-1) * sc_ref[...]
        o_ref[:, s * d:(s + 1) * d] = xs[s] + _rms(m, g1_ref[...])
    for jj in range(POOL_BUF):
        so_ref[:, jj * d:(jj + 1) * d] = cat(n_s + jj, 0, d)


def _pool_sample(x, state, mix_norm, pool_w, pool_scale, layer, j, *, n_s, past, bt):
    db = x.shape[0]
    d = mix_norm.shape[-1]
    g = len(POOL_WINDOWS)
    return pl.pallas_call(
        functools.partial(_pool_s_body, n_s=n_s, past=past),
        grid=(db // bt,),
        in_specs=[
            pl.BlockSpec((bt, n_s * d), lambda i: (i, 0)),
            pl.BlockSpec((None, bt, POOL_BUF * d), lambda i: (j, i, 0)),
            _const_spec((None, 1, d), (layer * 2, 0, 0)),
            _const_spec((None, 1, d), (layer * 2 + 1, 0, 0)),
            _const_spec((None, g, d // g, d // g), (j, 0, 0, 0)),
            _const_spec((None, 1, d), (j, 0, 0)),
        ],
        out_specs=[
            pl.BlockSpec((bt, n_s * d), lambda i: (i, 0)),
            pl.BlockSpec((bt, POOL_BUF * d), lambda i: (i, 0)),
        ],
        out_shape=[jax.ShapeDtypeStruct((db, n_s * d), F32), jax.ShapeDtypeStruct((db, POOL_BUF * d), F32)],
        compiler_params=_params("parallel"),
        name="pool_sample",
    )(x, state, mix_norm, mix_norm, pool_w, pool_scale)


def _layer_norm(y, g, b):
    mu = jnp.mean(y, axis=-1, keepdims=True)
    yc = y - mu
    var = jnp.mean(yc * yc, axis=-1, keepdims=True)
    return yc * lax.rsqrt(var + LN_EPS) * g + b


def _conv_p_body(x_ref, g0_ref, g1_ref, w1_ref, b1_ref, wdw_ref, bdw_ref, lng_ref, lnb_ref, w2_ref, b2_ref,
                 o_ref, st_ref, cat_ref, y_ref, *, ts, n_t):
    t = pl.program_id(1)
    d = x_ref.shape[-1]
    x = x_ref[0]
    h = _rms(x, g0_ref[...]).astype(BF16)
    glu = jnp.dot(h, w1_ref[...], preferred_element_type=F32) + b1_ref[...]
    u = glu[:, :d] * jax.nn.sigmoid(glu[:, d:])

    @pl.when(t == 0)
    def _():
        cat_ref[0:CONV_HALO, :] = jnp.zeros((CONV_HALO, d), F32)
        cat_ref[CONV_HALO + ts:CONV_HALO + ts + SUBLANES, :] = jnp.zeros((SUBLANES, d), F32)

    cat_ref[CONV_HALO:CONV_HALO + ts, :] = u
    off = CONV_HALO - CONV_BUF
    zrows = ts + SUBLANES
    for c0 in range(0, d, LANES):
        y = None
        for r in range(SUBLANES):
            z = None
            for a in range((CONV_WIDTH + off) // SUBLANES + 1):
                k = SUBLANES * a + r - off
                if 0 <= k < CONV_WIDTH:
                    term = cat_ref[SUBLANES * a:SUBLANES * a + zrows, c0:c0 + LANES] * wdw_ref[k:k + 1, c0:c0 + LANES]
                    z = term if z is None else z + term
            zs = z[r:r + ts]
            y = zs if y is None else y + zs
        y_ref[:, c0:c0 + LANES] = y + bdw_ref[:, c0:c0 + LANES]
    a = _silu(_layer_norm(y_ref[...], lng_ref[...], lnb_ref[...])).astype(BF16)
    m = jnp.dot(a, w2_ref[...], preferred_element_type=F32) + b2_ref[...]
    o_ref[0] = x + _rms(m, g1_ref[...])

    @pl.when(t == n_t - 1)
    def _():
        st_ref[0] = cat_ref[CONV_HALO + ts - CONV_BUF:CONV_HALO + ts, :]

    cat_ref[0:CONV_HALO, :] = cat_ref[ts:ts + CONV_HALO, :]


def _conv_specs(layer, j, d):
    return [
        _const_spec((None, 1, d), (layer * 2, 0, 0)),
        _const_spec((None, 1, d), (layer * 2 + 1, 0, 0)),
        _const_spec((None, d, 2 * d), (j, 0, 0)),
        _const_spec((None, 1, 2 * d), (j, 0, 0)),
        _const_spec((None, CONV_WIDTH, d), (j, 0, 0)),
        _const_spec((None, 1, d), (j, 0, 0)),
        _const_spec((None, 1, d), (j, 0, 0)),
        _const_spec((None, 1, d), (j, 0, 0)),
        _const_spec((None, d, d), (j, 0, 0)),
        _const_spec((None, 1, d), (j, 0, 0)),
    ]


def _conv_prompt(x, mix_norm, cw, layer, j, *, ts):
    b, t, d = x.shape
    n_t = t // ts
    assert n_t * ts == t and ts % 8 == 0 and ts >= CONV_HALO
    return pl.pallas_call(
        functools.partial(_conv_p_body, ts=ts, n_t=n_t),
        grid=(b, n_t),
        in_specs=[pl.BlockSpec((1, ts, d), lambda bi, ti: (bi, ti, 0))] + _conv_specs(layer, j, d),
        out_specs=[
            pl.BlockSpec((1, ts, d), lambda bi, ti: (bi, ti, 0)),
            pl.BlockSpec((1, CONV_BUF, d), lambda bi, ti: (bi, 0, 0)),
        ],
        out_shape=[jax.ShapeDtypeStruct((b, t, d), F32), jax.ShapeDtypeStruct((b, CONV_BUF, d), F32)],
        scratch_shapes=[pltpu.VMEM((CONV_HALO + ts + SUBLANES, d), F32), pltpu.VMEM((ts, d), F32)],
        compiler_params=_params("parallel", "arbitrary"),
        name="conv_prompt",
    )(x, mix_norm, mix_norm, *cw)


def _conv_s_body(x_ref, st_ref, g0_ref, g1_ref, w1_ref, b1_ref, wdw_ref, bdw_ref, lng_ref, lnb_ref, w2_ref, b2_ref,
                 o_ref, so_ref, *, n_s):
    d = g0_ref.shape[-1]
    bt = x_ref.shape[0]
    xs = [x_ref[:, s * d:(s + 1) * d] for s in range(n_s)]
    h = jnp.concatenate([_rms(x, g0_ref[...]) for x in xs], axis=0).astype(BF16)
    z = jnp.dot(h, w1_ref[...], preferred_element_type=F32) + b1_ref[...]
    u = z[:, :d] * jax.nn.sigmoid(z[:, d:])
    us = [u[s * bt:(s + 1) * bt] for s in range(n_s)]

    def cat(idx):
        if idx < CONV_BUF:
            return st_ref[:, idx * d:(idx + 1) * d]
        return us[idx - CONV_BUF]

    acts = []
    for s in range(n_s):
        y = cat(s) * wdw_ref[0:1, :] + bdw_ref[...]
        for k in range(1, CONV_WIDTH):
            y = y + cat(s + k) * wdw_ref[k:k + 1, :]
        acts.append(_silu(_layer_norm(y, lng_ref[...], lnb_ref[...])))
    a = jnp.concatenate(acts, axis=0).astype(BF16)
    m = jnp.dot(a, w2_ref[...], preferred_element_type=F32) + b2_ref[...]
    for s in range(n_s):
        o_ref[:, s * d:(s + 1) * d] = xs[s] + _rms(m[s * bt:(s + 1) * bt], g1_ref[...])
    for jj in range(CONV_BUF):
        so_ref[:, jj * d:(jj + 1) * d] = cat(n_s + jj)


def _conv_sample(x, state, mix_norm, cw, layer, j, *, n_s, bt):
    db = x.shape[0]
    d = mix_norm.shape[-1]
    return pl.pallas_call(
        functools.partial(_conv_s_body, n_s=n_s),
        grid=(db // bt,),
        in_specs=[
            pl.BlockSpec((bt, n_s * d), lambda i: (i, 0)),
            pl.BlockSpec((None, bt, CONV_BUF * d), lambda i: (j, i, 0)),
        ] + _conv_specs(layer, j, d),
        out_specs=[
            pl.BlockSpec((bt, n_s * d), lambda i: (i, 0)),
            pl.BlockSpec((bt, CONV_BUF * d), lambda i: (i, 0)),
        ],
        out_shape=[jax.ShapeDtypeStruct((db, n_s * d), F32), jax.ShapeDtypeStruct((db, CONV_BUF * d), F32)],
        compiler_params=_params("parallel"),
        name="conv_sample",
    )(x, state, mix_norm, mix_norm, *cw)


def _qkv_body(x_ref, g0_ref, w_ref, q_ref, k_ref, v_ref, kb_ref, vb_ref):
    d = x_ref.shape[-1]
    h = _rms(x_ref[...], g0_ref[...]).astype(BF16)
    qkv = jnp.dot(h, w_ref[...], preferred_element_type=F32)
    q_ref[...] = (qkv[:, :d] * (HEAD_DIM ** -0.5 * LOG2E)).astype(BF16)
    k = qkv[:, d:2 * d]
    v = qkv[:, 2 * d:]
    k_ref[...] = k
    v_ref[...] = v
    kb_ref[...] = k.astype(BF16)
    vb_ref[...] = v.astype(BF16)


def _qkv(x, mix_norm, wqkv, layer, j, *, tm):
    n, d = x.shape
    row = pl.BlockSpec((tm, d), lambda i: (i, 0))
    return pl.pallas_call(
        _qkv_body,
        grid=(pl.cdiv(n, tm),),
        in_specs=[row, _const_spec((None, 1, d), (layer * 2, 0, 0)), _const_spec((None, d, 3 * d), (j, 0, 0))],
        out_specs=[row] * 5,
        out_shape=[jax.ShapeDtypeStruct((n, d), BF16), jax.ShapeDtypeStruct((n, d), F32),
                   jax.ShapeDtypeStruct((n, d), F32), jax.ShapeDtypeStruct((n, d), BF16),
                   jax.ShapeDtypeStruct((n, d), BF16)],
        compiler_params=_params("parallel"),
        name="qkv_proj",
    )(x, mix_norm, wqkv)


def _bucket(n):
    nf = jnp.maximum(n, 1).astype(F32)
    large = MAX_EXACT + (jnp.log(nf / MAX_EXACT) / math.log(MAX_DISTANCE / MAX_EXACT)
                         * (N_BUCKETS - MAX_EXACT)).astype(jnp.int32)
    large = jnp.minimum(large, N_BUCKETS - 1)
    return jnp.where(n < MAX_EXACT, n, large)


def _lookup(bucket, entry):
    out = jnp.zeros(bucket.shape, F32)
    for b in range(N_BUCKETS):
        out = jnp.where(bucket == b, entry(b), out)
    return out


def _bias_p_body(table_ref, bp_ref, *, blk):
    head = pl.program_id(0)
    a = lax.broadcasted_iota(jnp.int32, (blk, blk), 0)
    b = lax.broadcasted_iota(jnp.int32, (blk, blk), 1)
    for sel in range(3):
        n = sel * blk + b - a
        vals = _lookup(_bucket(jnp.maximum(n, 0)), lambda bb: table_ref[bb * N_HEADS + head])
        bp_ref[0, sel] = jnp.where(n >= 0, vals * LOG2E, NEG_INF)


def _bias_s_body(tt_ref, bs_ref, *, n_s):
    rows = 2 * n_s * N_HEADS
    cols = PAGE_SIZE * N_HEADS
    tt = tt_ref[...]
    trow = jnp.broadcast_to(tt[None], (rows // N_HEADS, N_HEADS, N_BUCKETS)).reshape(rows, N_BUCKETS)

    def tile(width, dist):
        r = lax.broadcasted_iota(jnp.int32, (rows, width), 0)
        c = lax.broadcasted_iota(jnp.int32, (rows, width), 1)
        n = dist((r // N_HEADS) % n_s, c // N_HEADS)
        vals = _lookup(_bucket(jnp.maximum(n, 0)), lambda bb: trow[:, bb:bb + 1])
        return jnp.where((r % N_HEADS == c % N_HEADS) & (n >= 0) & (c < cols), vals * LOG2E, NEG_INF)

    bs_ref[:, 0:cols] = tile(cols, lambda qi, kk: 2 * PAGE_SIZE + qi - kk)
    bs_ref[:, cols:2 * cols] = tile(cols, lambda qi, kk: PAGE_SIZE + qi - kk)
    bs_ref[:, 2 * cols:2 * cols + PAGE_SIZE] = tile(
        PAGE_SIZE, lambda qi, kk: jnp.where(kk < n_s, qi - kk, -1))


def _bias_tiles(table, *, blk, n_s):
    far = _first_far_distance()
    assert blk + 1 >= far and PAGE_SIZE + 1 >= far and n_s * N_HEADS <= PAGE_SIZE
    bias_p = pl.pallas_call(
        functools.partial(_bias_p_body, blk=blk),
        grid=(N_HEADS,),
        in_specs=[pl.BlockSpec(memory_space=pltpu.SMEM)],
        out_specs=pl.BlockSpec((1, 3, blk, blk), lambda h: (h, 0, 0, 0)),
        out_shape=jax.ShapeDtypeStruct((N_HEADS, 3, blk, blk), F32),
        compiler_params=_params("parallel"),
        name="rel_bias_prompt",
    )(table.reshape(-1))
    width = 2 * PAGE_SIZE * N_HEADS + PAGE_SIZE
    bias_s = pl.pallas_call(
        functools.partial(_bias_s_body, n_s=n_s),
        out_shape=jax.ShapeDtypeStruct((2 * n_s * N_HEADS, width), F32),
        compiler_params=pltpu.CompilerParams(vmem_limit_bytes=VMEM_LIMIT),
        name="rel_bias_sample",
    )(table.T)
    return bias_p, bias_s


def _diff_lambda(lam_ref, lam_init):
    lp = lam_ref[...]
    s1 = jnp.sum(lp[0:1] * lp[1:2], axis=-1, keepdims=True)
    s2 = jnp.sum(lp[2:3] * lp[3:4], axis=-1, keepdims=True)
    return jnp.exp(s1) - jnp.exp(s2) + lam_init


_NT = (((1,), (1,)), ((), ()))


def _attn_p_body(q_ref, k_ref, v_ref, x_ref, bias_ref, lam_ref, sub_ref, wo_ref, g1_ref, o_ref,
                 qz_ref, vt_ref, m_ref, acc_ref, s_ref, a_ref, *, blk, n_full, tail, lam_init):
    i = pl.program_id(1)

    @pl.when(i == 0)
    def _():
        for hd in range(N_HEADS):
            c0 = hd * V_DIM

            ones_row = (lax.broadcasted_iota(jnp.int32, (SUM_ROWS, blk), 0) == 0).astype(BF16)

            def xpose(jb, carry):
                r0 = pl.multiple_of(jb * blk, blk)
                vt_ref[hd, jb, 0:V_DIM, :] = v_ref[0, pl.ds(r0, blk), c0:c0 + V_DIM].T
                vt_ref[hd, jb, V_DIM:V_DIM + SUM_ROWS, :] = ones_row
                return carry

            lax.fori_loop(0, n_full, xpose, 0)
            if tail:
                r0 = n_full * blk
                vt_ref[hd, n_full, 0:V_DIM, 0:tail] = (
                    v_ref[0, r0:r0 + tail, c0:c0 + V_DIM].astype(F32).T.astype(BF16))
                vt_ref[hd, n_full, V_DIM:V_DIM + SUM_ROWS, :] = ones_row

    def run(wq, n_loop, tail_keys):
        lane = lax.broadcasted_iota(jnp.int32, (wq, V_DIM), 1)
        for hd in range(N_HEADS):
            q = q_ref[0, 0:wq, hd * V_DIM:(hd + 1) * V_DIM]
            qz_ref[hd, 0:wq, :] = jnp.where(lane < HEAD_DIM, q, jnp.zeros_like(q))
            qz_ref[hd, wq:2 * wq, :] = jnp.where(lane >= HEAD_DIM, q, jnp.zeros_like(q))
        m_ref[:, :, 0:2 * wq] = jnp.full((N_HEADS, 1, 2 * wq), NEG_INF, F32)
        acc_ref[:, :, 0:2 * wq] = jnp.zeros((N_HEADS, V_DIM + SUM_ROWS, 2 * wq), F32)

        def all_heads(n, keys, vt, bias):
            def stage(hd):
                b2 = bias(hd)
                s_ref[hd % 2, 0:n, 0:2 * wq] = (
                    lax.dot_general(keys(hd), qz_ref[hd, 0:2 * wq, :], _NT, preferred_element_type=F32)
                    + jnp.concatenate([b2, b2], axis=1))

            stage(0)
            for hd in range(N_HEADS):
                if hd + 1 < N_HEADS:
                    stage(hd + 1)
                s = s_ref[hd % 2, 0:n, 0:2 * wq]
                m_old = m_ref[hd, :, 0:2 * wq]
                m_new = jnp.maximum(m_old, jnp.max(s, axis=0, keepdims=True))
                alpha = jnp.exp2(m_old - m_new)
                p = jnp.exp2(s - m_new)
                m_ref[hd, :, 0:2 * wq] = m_new
                acc_ref[hd, :, 0:2 * wq] = (alpha * acc_ref[hd, :, 0:2 * wq]
                                            + jnp.dot(vt(hd), p.astype(BF16), preferred_element_type=F32))

        def body(jb, carry):
            r0 = pl.multiple_of(jb * blk, blk)
            sel = jnp.minimum(i - jb, 2)
            all_heads(blk,
                      lambda hd: k_ref[0, pl.ds(r0, blk), hd * V_DIM:(hd + 1) * V_DIM],
                      lambda hd: vt_ref[hd, jb],
                      lambda hd: bias_ref[hd, sel, :, 0:wq])
            return carry

        lax.fori_loop(0, n_loop, body, 0)
        if tail_keys:
            r0 = n_full * blk
            all_heads(tail_keys,
                      lambda hd: k_ref[0, r0:r0 + tail_keys, hd * V_DIM:(hd + 1) * V_DIM],
                      lambda hd: vt_ref[hd, n_full, :, 0:tail_keys],
                      lambda hd: bias_ref[hd, 0, 0:tail_keys, 0:wq])

        lam = _diff_lambda(lam_ref, lam_init)
        for hd in range(N_HEADS):
            o_both = acc_ref[hd, 0:V_DIM, 0:2 * wq] * (1.0 / acc_ref[hd, V_DIM:V_DIM + 1, 0:2 * wq])
            o = o_both[:, 0:wq] - lam * o_both[:, wq:2 * wq]
            o = o * lax.rsqrt(jnp.mean(o * o, axis=0, keepdims=True) + RMS_EPS) * sub_ref[...] * (1.0 - lam_init)
            a_ref[0:wq, hd * V_DIM:(hd + 1) * V_DIM] = o.T.astype(BF16)
        mix = jnp.dot(a_ref[0:wq, :], wo_ref[...], preferred_element_type=F32)
        o_ref[0, 0:wq, :] = x_ref[0, 0:wq, :] + _rms(mix, g1_ref[...])

    @pl.when(i < n_full)
    def _():
        run(blk, i + 1, 0)

    if tail:
        @pl.when(i == n_full)
        def _():
            run(-(-tail // LANES) * LANES, n_full, tail)


def _attn_prompt(q, kb, vb, x, bias_p, lam_p, subln_col, wo, mix_norm, layer, j, *, blk, lam_init):
    b, t, d = q.shape
    n_full, tail = t // blk, t % blk
    assert tail % 16 == 0
    n_blocks = n_full + (1 if tail else 0)
    seq_blk = pl.BlockSpec((1, blk, d), lambda bi, qi: (bi, qi, 0))
    return pl.pallas_call(
        functools.partial(_attn_p_body, blk=blk, n_full=n_full, tail=tail, lam_init=lam_init),
        grid=(b, n_blocks),
        in_specs=[
            seq_blk,
            pl.BlockSpec((1, t, d), lambda bi, qi: (bi, 0, 0)),
            pl.BlockSpec((1, t, d), lambda bi, qi: (bi, 0, 0)),
            seq_blk,
            _const_spec((N_HEADS, 3, blk, blk), (0, 0, 0, 0)),
            _const_spec((None, 4, HEAD_DIM), (j, 0, 0)),
            _const_spec((None, V_DIM, 1), (j, 0, 0)),
            _const_spec((None, d, d), (j, 0, 0)),
            _const_spec((None, 1, d), (layer * 2 + 1, 0, 0)),
        ],
        out_specs=seq_blk,
        out_shape=jax.ShapeDtypeStruct((b, t, d), F32),
        scratch_shapes=[
            pltpu.VMEM((N_HEADS, 2 * blk, V_DIM), BF16),
            pltpu.VMEM((N_HEADS, n_blocks, V_DIM + SUM_ROWS, blk), BF16),
            pltpu.VMEM((N_HEADS, 1, 2 * blk), F32),
            pltpu.VMEM((N_HEADS, V_DIM + SUM_ROWS, 2 * blk), F32),
            pltpu.VMEM((2, blk, 2 * blk), F32),
            pltpu.VMEM((blk, d), BF16),
        ],
        compiler_params=_params("arbitrary", "arbitrary"),
        name="attn_prompt",
    )(q, kb, vb, x, bias_p, lam_p, subln_col, wo, mix_norm)


def _attn_s_body(pt_ref, q_ref, kn_ref, vn_ref, bias_ref, lam_ref, sub_ref, *rest, n_pages, n_s, lam_init):
    k_pages = rest[:n_pages]
    v_pages = rest[n_pages:2 * n_pages]
    o_ref, s_ref = rest[2 * n_pages:]
    half = n_s * N_HEADS
    cols = PAGE_SIZE * N_HEADS

    q = q_ref[...].reshape(half, V_DIM)
    lane = lax.broadcasted_iota(jnp.int32, (half, V_DIM), 1)
    q2 = jnp.concatenate([jnp.where(lane < HEAD_DIM, q, 0.0), jnp.where(lane >= HEAD_DIM, q, 0.0)], axis=0).astype(BF16)

    def scores(keys, bias):
        return lax.dot_general(q2, keys, _NT, preferred_element_type=F32) + bias

    pad = jnp.zeros((PAGE_SIZE - half, V_DIM), F32)
    kn = jnp.concatenate([kn_ref[...].reshape(half, V_DIM), pad], axis=0).astype(BF16)
    vn = jnp.concatenate([vn_ref[...].reshape(half, V_DIM), pad], axis=0).astype(BF16)
    s_new = scores(kn, bias_ref[:, 2 * cols:2 * cols + PAGE_SIZE])
    m = jnp.max(s_new, axis=-1, keepdims=True)
    for r_ in range(n_pages):
        near = r_ == n_pages - 1
        s = scores(k_pages[r_][...].reshape(cols, V_DIM).astype(BF16),
                   bias_ref[:, cols:2 * cols] if near else bias_ref[:, 0:cols])
        s_ref[r_] = s
        m = jnp.maximum(m, jnp.max(s, axis=-1, keepdims=True))

    p = jnp.exp2(s_new - m)
    l = jnp.sum(p, axis=-1, keepdims=True)
    acc = jnp.dot(p.astype(BF16), vn, preferred_element_type=F32)
    for r_ in range(n_pages):
        p = jnp.exp2(s_ref[r_] - m)
        l = l + jnp.sum(p, axis=-1, keepdims=True)
        acc = acc + jnp.dot(p.astype(BF16), v_pages[r_][...].reshape(cols, V_DIM).astype(BF16),
                            preferred_element_type=F32)

    lam = _diff_lambda(lam_ref, lam_init)
    o_all = acc * (1.0 / l)
    o = o_all[0:half] - lam * o_all[half:2 * half]
    o_ref[...] = (_rms(o, sub_ref[...]) * (1.0 - lam_init)).reshape(o_ref.shape)


def _attn_sample(q, kn, vn, cache_k, cache_v, page_table, bias_s, lam_p, subln, j, *, n_pages, lam_init):
    db, n_s = q.shape[:2]
    assert n_pages == page_table.shape[1]
    rows = 2 * n_s * N_HEADS
    new_kv = pl.BlockSpec((1, n_s, 1, N_HEADS, V_DIM), lambda b, pt: (b, 0, 0, 0, 0))
    q_spec = pl.BlockSpec((1, n_s, N_HEADS, V_DIM), lambda b, pt: (b, 0, 0, 0))

    def page_spec(r):
        return pl.BlockSpec((1, PAGE_SIZE, 1, N_HEADS, V_DIM), lambda b, pt: (pt[b, r], 0, j, 0, 0))

    grid_spec = pltpu.PrefetchScalarGridSpec(
        num_scalar_prefetch=1,
        grid=(db,),
        in_specs=[
            q_spec, new_kv, new_kv,
            pl.BlockSpec(bias_s.shape, lambda b, pt: (0, 0)),
            pl.BlockSpec((None, 4, HEAD_DIM), lambda b, pt: (j, 0, 0)),
            pl.BlockSpec((None, 1, V_DIM), lambda b, pt: (j, 0, 0)),
        ] + [page_spec(r) for r in range(n_pages)] * 2,
        out_specs=q_spec,
        scratch_shapes=[pltpu.VMEM((n_pages, rows, PAGE_SIZE * N_HEADS), F32)],
    )
    return pl.pallas_call(
        functools.partial(_attn_s_body, n_pages=n_pages, n_s=n_s, lam_init=lam_init),
        grid_spec=grid_spec,
        out_shape=jax.ShapeDtypeStruct((db, n_s, N_HEADS, V_DIM), F32),
        compiler_params=_params("parallel"),
        name="attn_sample",
    )(page_table, q, kn, vn, bias_s, lam_p, subln, *([cache_k] * n_pages), *([cache_v] * n_pages))


def _outproj_body(a_ref, x_ref, w_ref, g1_ref, o_ref):
    m = jnp.dot(a_ref[...].astype(BF16), w_ref[...], preferred_element_type=F32)
    o_ref[...] = x_ref[...] + _rms(m, g1_ref[...])


def _outproj(a, x, wo, mix_norm, layer, j, *, tm):
    n, d = x.shape
    row = pl.BlockSpec((tm, d), lambda i: (i, 0))
    return pl.pallas_call(
        _outproj_body,
        grid=(pl.cdiv(n, tm),),
        in_specs=[row, row, _const_spec((None, d, d), (j, 0, 0)), _const_spec((None, 1, d), (layer * 2 + 1, 0, 0))],
        out_specs=row,
        out_shape=jax.ShapeDtypeStruct((n, d), F32),
        compiler_params=_params("parallel"),
        name="attn_outproj",
    )(a, x, wo, mix_norm)


def _pick_tile(n, cap, align=8):
    best = None
    for c in range(align, min(n, cap) + 1, align):
        if n % c == 0:
            best = c
    assert best is not None, (n, cap, align)
    return best


def kernel(x_prompt, x_sample, state_pool, state_conv, cache_k, cache_v, page_table, meta_tokens, rel_bias_table, ffn_norm, ffn_wg, ffn_wu, ffn_wd, mix_norm, pool_w, pool_scale, attn_wqkv, attn_wo, attn_lambda, attn_subln, conv_w1, conv_b1, conv_wdw, conv_bdw, conv_ln_g, conv_ln_b, conv_w2, conv_b2, final_norm):
    b, seq, d = x_prompt.shape
    db, n_s, _ = x_sample.shape
    depth = ffn_wg.shape[0]
    t = seq + N_META
    past = page_table.shape[1] * PAGE_SIZE

    wg, wu, wd = ffn_wg, ffn_wu, ffn_wd
    ffn_norm3 = ffn_norm.reshape(-1, 1, d)
    mix_norm3 = mix_norm.reshape(-1, 1, d)
    pool_w_b = pool_w.astype(BF16)
    pool_scale3 = pool_scale.reshape(-1, 1, d)
    wqkv_b, wo_b = attn_wqkv.astype(BF16), attn_wo.astype(BF16)
    subln3 = attn_subln.reshape(-1, 1, V_DIM)
    cw = (conv_w1.astype(BF16), conv_b1.reshape(-1, 1, 2 * d), conv_wdw, conv_bdw.reshape(-1, 1, d),
          conv_ln_g.reshape(-1, 1, d), conv_ln_b.reshape(-1, 1, d), conv_w2.astype(BF16), conv_b2.reshape(-1, 1, d))
    state_pool2 = state_pool.reshape(state_pool.shape[0], db, POOL_BUF * d)
    state_conv2 = state_conv.reshape(state_conv.shape[0], db, CONV_BUF * d)
    final_g = final_norm.reshape(1, d)

    meta = jnp.broadcast_to(meta_tokens[None].astype(x_prompt.dtype), (b, N_META, d))
    xp = jnp.concatenate([meta, x_prompt], axis=1).reshape(b * t, d)
    xs = x_sample.reshape(db * n_s, d)

    tm_p = _pick_tile(b * t, 768, 16)
    tm_s = min(512, db * n_s)
    ts_pool = _pick_tile(t, 1032)
    ts_conv = _pick_tile(t, 344)
    bt = _pick_tile(db, 32)

    bias_p = bias_s = None
    pool_p, pool_s, conv_p, conv_s = [], [], [], []
    k_p = v_p = k_s = v_s = None
    for i in range(depth):
        kind, j = i % N_MIXERS, i // N_MIXERS
        xp, xs = _ffn(xp, xs, ffn_norm3, final_g, wg, wu, wd, i, 0, tm=tm_p)
        if kind == 0:
            xp3, st = _pool_prompt(xp.reshape(b, t, d), mix_norm3, pool_w_b, pool_scale3, i, j, ts=ts_pool)
            xp = xp3.reshape(b * t, d)
            pool_p.append(st)
            xs2, st = _pool_sample(xs.reshape(db, n_s * d), state_pool2, mix_norm3, pool_w_b, pool_scale3, i, j,
                                   n_s=n_s, past=past, bt=bt)
            xs = xs2.reshape(db * n_s, d)
            pool_s.append(st.reshape(db, POOL_BUF, d))
        elif kind == 1:
            assert j == 0
            lam_init = _lambda_init(i)
            if bias_p is None:
                bias_p, bias_s = _bias_tiles(rel_bias_table, blk=ATTN_BLOCK, n_s=n_s)
            q, k_p, v_p, kb, vb = _qkv(xp, mix_norm3, wqkv_b, i, j, tm=tm_p)
            xp = _attn_prompt(q.reshape(b, t, d), kb.reshape(b, t, d), vb.reshape(b, t, d), xp.reshape(b, t, d),
                              bias_p, attn_lambda, attn_subln.reshape(-1, V_DIM, 1), wo_b, mix_norm3, i, j,
                              blk=ATTN_BLOCK, lam_init=lam_init).reshape(b * t, d)
            q, k_s, v_s, _, _ = _qkv(xs, mix_norm3, wqkv_b, i, j, tm=tm_s)
            k_s = k_s.reshape(db, n_s, 1, N_HEADS, V_DIM)
            v_s = v_s.reshape(db, n_s, 1, N_HEADS, V_DIM)
            a = _attn_sample(q.astype(F32).reshape(db, n_s, N_HEADS, V_DIM), k_s, v_s, cache_k, cache_v,
                             page_table, bias_s, attn_lambda, subln3, j, n_pages=page_table.shape[1], lam_init=lam_init)
            xs = _outproj(a.reshape(db * n_s, d), xs, wo_b, mix_norm3, i, j, tm=tm_s)
        else:
            xp3, st = _conv_prompt(xp.reshape(b, t, d), mix_norm3, cw, i, j, ts=ts_conv)
            xp = xp3.reshape(b * t, d)
            conv_p.append(st)
            xs2, st = _conv_sample(xs.reshape(db, n_s * d), state_conv2, mix_norm3, cw, i, j, n_s=n_s, bt=bt)
            xs = xs2.reshape(db * n_s, d)
            conv_s.append(st.reshape(db, CONV_BUF, d))
        xp, xs = _ffn(xp, xs, ffn_norm3, final_g, wg, wu, wd, i, 1, tm=tm_p,
                      final_seq=(t, N_META) if i == depth - 1 else None)

    kv_shape_p = (b, t, 1, N_HEADS, V_DIM)
    return (xp.reshape(b, seq, d), xs.reshape(db, n_s, d), jnp.stack(pool_p, axis=0), jnp.stack(pool_s, axis=0),
            jnp.stack(conv_p, axis=0), jnp.stack(conv_s, axis=0),
            k_p.reshape(kv_shape_p), v_p.reshape(kv_shape_p), k_s, v_s)
```

```python
import functools
import math

import jax
import jax.numpy as jnp
from jax import lax
from jax.experimental import pallas as pl
from jax.experimental.pallas import tpu as pltpu

F32 = jnp.float32
BF16 = jnp.bfloat16

N_MIXERS = 3
N_META = 16
N_HEADS = 8
HEAD_DIM = 64
V_DIM = 2 * HEAD_DIM
N_BUCKETS = 32
MAX_EXACT = N_BUCKETS // 2
MAX_DISTANCE = 128
POOL_WINDOWS = (2, 4, 8, 16)
POOL_BUF = max(POOL_WINDOWS) - 1
CONV_WIDTH = 31
CONV_BUF = CONV_WIDTH - 1
PAGE_SIZE = 128
RMS_EPS = 1e-6
LN_EPS = 1e-5
NEG_INF = -1e30
LOG2E = math.log2(math.e)
SUBLANES = 8
LANES = 128
BF16_ROWS = 16
MXU_TILE = 256
SUM_ROWS = 16
ATTN_BLOCK = 256
POOL_HALO = 16
CONV_HALO = 32
VMEM_LIMIT = 56 * 1024 * 1024
WEIGHT_CHUNK_BYTES = 3 * 256 * 1024
WEIGHT_SLOTS = 4


def _lambda_init(layer_idx):
    return 0.8 - 0.6 * math.exp(-0.3 * layer_idx)


def _first_far_distance():
    n = MAX_EXACT
    while MAX_EXACT + int(math.log(n / MAX_EXACT) / math.log(MAX_DISTANCE / MAX_EXACT) * (N_BUCKETS - MAX_EXACT)) < N_BUCKETS - 1:
        n += 1
    return n


def _params(*sem):
    return pltpu.CompilerParams(dimension_semantics=sem, vmem_limit_bytes=VMEM_LIMIT)


def _rms(x, g):
    return x * lax.rsqrt(jnp.mean(x * x, axis=-1, keepdims=True) + RMS_EPS) * g


def _silu(x):
    return x * jax.nn.sigmoid(x)


def _const_spec(shape, index, single=False):
    if single:
        return pl.BlockSpec(shape, lambda *_: index, pipeline_mode=pl.Buffered(1))
    return pl.BlockSpec(shape, lambda *_: index)


def _ffn_chunks(ff, n_chunks):
    tiles = -(-ff // MXU_TILE)
    edges = [min(ff, MXU_TILE * (-(-tiles * c // n_chunks))) for c in range(n_chunks + 1)]
    return [(lo, hi - lo) for lo, hi in zip(edges[:-1], edges[1:]) if hi > lo]


def _ffn_body(xp_ref, xs_ref, n0_ref, n1_ref, gf_ref, wg_hbm, wu_hbm, wd_hbm, op_ref, os_ref,
              wg_ref, wu_ref, wd_ref, stage_in, stage_out, sem, *, n_chunks, n_p, final, layer, f):
    i = pl.program_id(0)

    def fetch(w_hbm, stage, dst):
        slots, rows = stage.shape[0], stage.shape[1]
        n = dst.shape[0] // rows

        def copy(c):
            return pltpu.make_async_copy(w_hbm.at[layer, f, pl.ds(c * rows, rows), :], stage.at[c % slots],
                                         sem.at[c % slots])

        for c in range(min(slots - 1, n)):
            copy(c).start()
        for c in range(n):
            if c + slots - 1 < n:
                copy(c + slots - 1).start()
            copy(c).wait()
            dst[pl.ds(c * rows, rows), :] = stage[c % slots].astype(BF16)

    @pl.when(i == 0)
    def _():
        fetch(wg_hbm, stage_in, wg_ref)
        fetch(wu_hbm, stage_in, wu_ref)
        fetch(wd_hbm, stage_out, wd_ref)

    def apply(x_ref, o_ref):
        rows = x_ref.shape[0]
        first = -(-rows // (2 * BF16_ROWS)) * BF16_ROWS
        for lo_row, n_rows in ((0, first), (first, rows - first))[:2 if rows > first else 1]:
            rs = pl.ds(lo_row, n_rows)
            x = x_ref[rs, :]
            h = _rms(x, n0_ref[...]).astype(BF16)
            acc = jnp.zeros(x.shape, F32)
            for lo, width in _ffn_chunks(wg_ref.shape[1], n_chunks):
                sl = pl.ds(lo, width)
                g = jnp.dot(h, wg_ref[:, sl], preferred_element_type=F32)
                u = jnp.dot(h, wu_ref[:, sl], preferred_element_type=F32)
                a = (_silu(g) * u).astype(BF16)
                acc = acc + jnp.dot(a, wd_ref[sl, :], preferred_element_type=F32)
            y = x + 0.5 * _rms(acc, n1_ref[...])
            o_ref[rs, :] = _rms(y, gf_ref[...]) if final else y

    @pl.when(i == 0)
    def _():
        os_ref[...] = jnp.zeros(os_ref.shape, F32)

    @pl.when(i < n_p)
    def _():
        apply(xp_ref, op_ref)

    @pl.when(i >= n_p)
    def _():
        apply(xs_ref, os_ref)


def _ffn(xp, xs, norms, final_g, wg, wu, wd, layer, f, *, tm, n_chunks=2, final_seq=None):
    (n_rows_p, d), n_rows_s = xp.shape, xs.shape[0]
    ff = wg.shape[-1]
    nidx = (layer * 2 + f) * 2
    tm_s = _pick_tile(n_rows_s, tm, BF16_ROWS)
    if final_seq is None:
        n_p, out_rows_p = pl.cdiv(n_rows_p, tm), n_rows_p
        p_in = p_out = pl.BlockSpec((tm, d), lambda i: (jnp.minimum(i, n_p - 1), 0))
    else:
        t, skip = final_seq
        assert skip % BF16_ROWS == 0 and t % BF16_ROWS == 0
        tm = _pick_tile(t - skip, tm, BF16_ROWS)
        per_seq = (t - skip) // tm
        n_p, out_rows_p = (n_rows_p // t) * per_seq, (n_rows_p // t) * (t - skip)

        def in_rows(i):
            ip = jnp.minimum(i, n_p - 1)
            return pl.multiple_of((ip // per_seq) * t + skip + (ip % per_seq) * tm, BF16_ROWS), 0

        p_in = pl.BlockSpec((pl.Element(tm), pl.Element(d)), in_rows)
        p_out = pl.BlockSpec((tm, d), lambda i: (jnp.minimum(i, n_p - 1), 0))
    n_s = pl.cdiv(n_rows_s, tm_s)
    s_spec = pl.BlockSpec((tm_s, d), lambda i: (jnp.maximum(i - n_p, 0), 0))
    in_rows, out_rows = _pick_tile(d, WEIGHT_CHUNK_BYTES // (4 * ff)), _pick_tile(ff, WEIGHT_CHUNK_BYTES // (4 * d))
    hbm = pl.BlockSpec(memory_space=pl.ANY)
    return pl.pallas_call(
        functools.partial(_ffn_body, n_chunks=n_chunks, n_p=n_p, final=final_seq is not None, layer=layer, f=f),
        grid=(n_p + n_s,),
        in_specs=[
            p_in, s_spec,
            _const_spec((None, 1, d), (nidx, 0, 0)),
            _const_spec((None, 1, d), (nidx + 1, 0, 0)),
            _const_spec((1, d), (0, 0)),
            hbm, hbm, hbm,
        ],
        out_specs=[p_out, s_spec],
        out_shape=[jax.ShapeDtypeStruct((out_rows_p, d), F32), jax.ShapeDtypeStruct((n_rows_s, d), F32)],
        scratch_shapes=[
            pltpu.VMEM((d, ff), BF16), pltpu.VMEM((d, ff), BF16), pltpu.VMEM((ff, d), BF16),
            pltpu.VMEM((WEIGHT_SLOTS, in_rows, ff), F32), pltpu.VMEM((WEIGHT_SLOTS, out_rows, d), F32),
            pltpu.SemaphoreType.DMA((WEIGHT_SLOTS,)),
        ],
        compiler_params=_params("arbitrary"),
        name="ffn",
    )(xp, xs, norms, norms, final_g, wg, wu, wd)


def _pool_p_body(x_ref, g0_ref, g1_ref, w_ref, sc_ref, o_ref, st_ref, *lvl_refs, ts, n_t):
    t = pl.program_id(1)
    d = x_ref.shape[-1]
    n_g = len(POOL_WINDOWS)
    gd = d // n_g
    x = x_ref[0]
    h = _rms(x, g0_ref[...])
    cat_ref = lvl_refs[0]
    top = SUBLANES + POOL_HALO
    rows = POOL_HALO + ts

    @pl.when(t == 0)
    def _():
        cat_ref[0:top, :] = jnp.zeros((top, d), F32)
        for ref in lvl_refs[1:]:
            ref[0:SUBLANES, :] = jnp.zeros((SUBLANES, ref.shape[1]), F32)

    cat_ref[top:top + ts, :] = h
    for k in range(1, len(lvl_refs)):
        prev, cur = lvl_refs[k - 1], lvl_refs[k]
        lo = prev.shape[1] - cur.shape[1]
        cur[SUBLANES:SUBLANES + rows, :] = (prev[SUBLANES:SUBLANES + rows, lo:]
                                            + prev[SUBLANES - 2 ** (k - 1):SUBLANES - 2 ** (k - 1) + rows, lo:])
    pos = t * ts + lax.broadcasted_iota(jnp.int32, (ts, 1), 0)
    outs = []
    for g, w in enumerate(POOL_WINDOWS):
        c0 = g * gd
        k = w.bit_length() - 1
        if k < len(lvl_refs):
            ref = lvl_refs[k]
            l0 = c0 - (d - ref.shape[1])
            acc = ref[top:top + ts, l0:l0 + gd]
        else:
            ref = lvl_refs[k - 1]
            l0 = c0 - (d - ref.shape[1])
            acc = ref[top:top + ts, l0:l0 + gd] + ref[top - w // 2:top - w // 2 + ts, l0:l0 + gd]
        inv_cnt = 1.0 / jnp.minimum(w, pos + 1).astype(F32)
        pooled = (acc * inv_cnt - h[:, c0:c0 + gd]).astype(BF16)
        outs.append(jnp.dot(pooled, w_ref[g], preferred_element_type=F32))
    m = jnp.concatenate(outs, axis=-1) * sc_ref[...]
    o_ref[0] = x + _rms(m, g1_ref[...])

    @pl.when(t == n_t - 1)
    def _():
        st_ref[0] = cat_ref[top + ts - POOL_BUF:top + ts, :]

    cat_ref[SUBLANES:top, :] = cat_ref[SUBLANES + ts:top + ts, :]


def _pool_prompt(x, mix_norm, pool_w, pool_scale, layer, j, *, ts):
    b, t, d = x.shape
    n_t = t // ts
    assert n_t * ts == t and ts % 8 == 0 and ts >= POOL_HALO
    g = len(POOL_WINDOWS)
    assert all(w == 2 ** (i + 1) for i, w in enumerate(POOL_WINDOWS)) and POOL_WINDOWS[-1] // 2 == SUBLANES
    buf_rows = SUBLANES + POOL_HALO + ts
    level_lanes = [d] + [d - i * (d // g) for i in range(g - 1)]
    return pl.pallas_call(
        functools.partial(_pool_p_body, ts=ts, n_t=n_t),
        grid=(b, n_t),
        in_specs=[
            pl.BlockSpec((1, ts, d), lambda bi, ti: (bi, ti, 0)),
            _const_spec((None, 1, d), (layer * 2, 0, 0)),
            _const_spec((None, 1, d), (layer * 2 + 1, 0, 0)),
            _const_spec((None, g, d // g, d // g), (j, 0, 0, 0)),
            _const_spec((None, 1, d), (j, 0, 0)),
        ],
        out_specs=[
            pl.BlockSpec((1, ts, d), lambda bi, ti: (bi, ti, 0)),
            pl.BlockSpec((1, POOL_BUF, d), lambda bi, ti: (bi, 0, 0)),
        ],
        out_shape=[jax.ShapeDtypeStruct((b, t, d), F32), jax.ShapeDtypeStruct((b, POOL_BUF, d), F32)],
        scratch_shapes=[pltpu.VMEM((buf_rows, lanes), F32) for lanes in level_lanes],
        compiler_params=_params("parallel", "arbitrary"),
        name="pool_prompt",
    )(x, mix_norm, mix_norm, pool_w, pool_scale)


def _pool_s_body(x_ref, st_ref, g0_ref, g1_ref, w_ref, sc_ref, o_ref, so_ref, *, n_s, past):
    d = g0_ref.shape[-1]
    gd = d // len(POOL_WINDOWS)
    bt = x_ref.shape[0]
    xs = [x_ref[:, s * d:(s + 1) * d] for s in range(n_s)]
    hs = [_rms(x, g0_ref[...]) for x in xs]

    def cat(idx, c0, width):
        if idx < POOL_BUF:
            return st_ref[:, idx * d + c0:idx * d + c0 + width]
        return hs[idx - POOL_BUF][:, c0:c0 + width]

    outs = []
    for g, w in enumerate(POOL_WINDOWS):
        c0 = g * gd
        rows = []
        for s in range(n_s):
            acc = cat(POOL_BUF + s, c0, gd)
            for jj in range(1, w):
                acc = acc + cat(POOL_BUF + s - jj, c0, gd)
            rows.append(acc * (1.0 / min(w, past + s + 1)) - hs[s][:, c0:c0 + gd])
        pooled = jnp.concatenate(rows, axis=0).astype(BF16)
        outs.append(jnp.dot(pooled, w_ref[g], preferred_element_type=F32))
    for s in range(n_s):
        m = jnp.concatenate([o[s * bt:(s + 1) * bt] for o in outs], axis=-1) * sc_ref[...]
        o_ref[:, s * d:(s + 1) * d] = xs[s] + _rms(m, g1_ref[...])
    for jj in range(POOL_BUF):
        so_ref[:, jj * d:(jj + 1) * d] = cat(n_s + jj, 0, d)


def _pool_sample(x, state, mix_norm, pool_w, pool_scale, layer, j, *, n_s, past, bt):
    db = x.shape[0]
    d = mix_norm.shape[-1]
    g = len(POOL_WINDOWS)
    return pl.pallas_call(
        functools.partial(_pool_s_body, n_s=n_s, past=past),
        grid=(db // bt,),
        in_specs=[
            pl.BlockSpec((bt, n_s * d), lambda i: (i, 0)),
            pl.BlockSpec((None, bt, POOL_BUF * d), lambda i: (j, i, 0)),
            _const_spec((None, 1, d), (layer * 2, 0, 0)),
            _const_spec((None, 1, d), (layer * 2 + 1, 0, 0)),
            _const_spec((None, g, d // g, d // g), (j, 0, 0, 0)),
            _const_spec((None, 1, d), (j, 0, 0)),
        ],
        out_specs=[
            pl.BlockSpec((bt, n_s * d), lambda i: (i, 0)),
            pl.BlockSpec((bt, POOL_BUF * d), lambda i: (i, 0)),
        ],
        out_shape=[jax.ShapeDtypeStruct((db, n_s * d), F32), jax.ShapeDtypeStruct((db, POOL_BUF * d), F32)],
        compiler_params=_params("parallel"),
        name="pool_sample",
    )(x, state, mix_norm, mix_norm, pool_w, pool_scale)


def _layer_norm(y, g, b):
    mu = jnp.mean(y, axis=-1, keepdims=True)
    yc = y - mu
    var = jnp.mean(yc * yc, axis=-1, keepdims=True)
    return yc * lax.rsqrt(var + LN_EPS) * g + b


def _conv_p_body(x_ref, g0_ref, g1_ref, w1_ref, b1_ref, wdw_ref, bdw_ref, lng_ref, lnb_ref, w2_ref, b2_ref,
                 o_ref, st_ref, cat_ref, y_ref, *, ts, n_t):
    t = pl.program_id(1)
    d = x_ref.shape[-1]
    x = x_ref[0]
    h = _rms(x, g0_ref[...]).astype(BF16)
    glu = jnp.dot(h, w1_ref[...], preferred_element_type=F32) + b1_ref[...]
    u = glu[:, :d] * jax.nn.sigmoid(glu[:, d:])

    @pl.when(t == 0)
    def _():
        cat_ref[0:CONV_HALO, :] = jnp.zeros((CONV_HALO, d), F32)
        cat_ref[CONV_HALO + ts:CONV_HALO + ts + SUBLANES, :] = jnp.zeros((SUBLANES, d), F32)

    cat_ref[CONV_HALO:CONV_HALO + ts, :] = u
    off = CONV_HALO - CONV_BUF
    zrows = ts + SUBLANES
    for c0 in range(0, d, LANES):
        y = None
        for r in range(SUBLANES):
            z = None
            for a in range((CONV_WIDTH + off) // SUBLANES + 1):
                k = SUBLANES * a + r - off
                if 0 <= k < CONV_WIDTH:
                    term = cat_ref[SUBLANES * a:SUBLANES * a + zrows, c0:c0 + LANES] * wdw_ref[k:k + 1, c0:c0 + LANES]
                    z = term if z is None else z + term
            zs = z[r:r + ts]
            y = zs if y is None else y + zs
        y_ref[:, c0:c0 + LANES] = y + bdw_ref[:, c0:c0 + LANES]
    a = _silu(_layer_norm(y_ref[...], lng_ref[...], lnb_ref[...])).astype(BF16)
    m = jnp.dot(a, w2_ref[...], preferred_element_type=F32) + b2_ref[...]
    o_ref[0] = x + _rms(m, g1_ref[...])

    @pl.when(t == n_t - 1)
    def _():
        st_ref[0] = cat_ref[CONV_HALO + ts - CONV_BUF:CONV_HALO + ts, :]

    cat_ref[0:CONV_HALO, :] = cat_ref[ts:ts + CONV_HALO, :]


def _conv_specs(layer, j, d):
    return [
        _const_spec((None, 1, d), (layer * 2, 0, 0)),
        _const_spec((None, 1, d), (layer * 2 + 1, 0, 0)),
        _const_spec((None, d, 2 * d), (j, 0, 0)),
        _const_spec((None, 1, 2 * d), (j, 0, 0)),
        _const_spec((None, CONV_WIDTH, d), (j, 0, 0)),
        _const_spec((None, 1, d), (j, 0, 0)),
        _const_spec((None, 1, d), (j, 0, 0)),
        _const_spec((None, 1, d), (j, 0, 0)),
        _const_spec((None, d, d), (j, 0, 0)),
        _const_spec((None, 1, d), (j, 0, 0)),
    ]


def _conv_prompt(x, mix_norm, cw, layer, j, *, ts):
    b, t, d = x.shape
    n_t = t // ts
    assert n_t * ts == t and ts % 8 == 0 and ts >= CONV_HALO
    return pl.pallas_call(
        functools.partial(_conv_p_body, ts=ts, n_t=n_t),
        grid=(b, n_t),
        in_specs=[pl.BlockSpec((1, ts, d), lambda bi, ti: (bi, ti, 0))] + _conv_specs(layer, j, d),
        out_specs=[
            pl.BlockSpec((1, ts, d), lambda bi, ti: (bi, ti, 0)),
            pl.BlockSpec((1, CONV_BUF, d), lambda bi, ti: (bi, 0, 0)),
        ],
        out_shape=[jax.ShapeDtypeStruct((b, t, d), F32), jax.ShapeDtypeStruct((b, CONV_BUF, d), F32)],
        scratch_shapes=[pltpu.VMEM((CONV_HALO + ts + SUBLANES, d), F32), pltpu.VMEM((ts, d), F32)],
        compiler_params=_params("parallel", "arbitrary"),
        name="conv_prompt",
    )(x, mix_norm, mix_norm, *cw)


def _conv_s_body(x_ref, st_ref, g0_ref, g1_ref, w1_ref, b1_ref, wdw_ref, bdw_ref, lng_ref, lnb_ref, w2_ref, b2_ref,
                 o_ref, so_ref, *, n_s):
    d = g0_ref.shape[-1]
    bt = x_ref.shape[0]
    xs = [x_ref[:, s * d:(s + 1) * d] for s in range(n_s)]
    h = jnp.concatenate([_rms(x, g0_ref[...]) for x in xs], axis=0).astype(BF16)
    z = jnp.dot(h, w1_ref[...], preferred_element_type=F32) + b1_ref[...]
    u = z[:, :d] * jax.nn.sigmoid(z[:, d:])
    us = [u[s * bt:(s + 1) * bt] for s in range(n_s)]

    def cat(idx):
        if idx < CONV_BUF:
            return st_ref[:, idx * d:(idx + 1) * d]
        return us[idx - CONV_BUF]

    acts = []
    for s in range(n_s):
        y = cat(s) * wdw_ref[0:1, :] + bdw_ref[...]
        for k in range(1, CONV_WIDTH):
            y = y + cat(s + k) * wdw_ref[k:k + 1, :]
        acts.append(_silu(_layer_norm(y, lng_ref[...], lnb_ref[...])))
    a = jnp.concatenate(acts, axis=0).astype(BF16)
    m = jnp.dot(a, w2_ref[...], preferred_element_type=F32) + b2_ref[...]
    for s in range(n_s):
        o_ref[:, s * d:(s + 1) * d] = xs[s] + _rms(m[s * bt:(s + 1) * bt], g1_ref[...])
    for jj in range(CONV_BUF):
        so_ref[:, jj * d:(jj + 1) * d] = cat(n_s + jj)


def _conv_sample(x, state, mix_norm, cw, layer, j, *, n_s, bt):
    db = x.shape[0]
    d = mix_norm.shape[-1]
    return pl.pallas_call(
        functools.partial(_conv_s_body, n_s=n_s),
        grid=(db // bt,),
        in_specs=[
            pl.BlockSpec((bt, n_s * d), lambda i: (i, 0)),
            pl.BlockSpec((None, bt, CONV_BUF * d), lambda i: (j, i, 0)),
        ] + _conv_specs(layer, j, d),
        out_specs=[
            pl.BlockSpec((bt, n_s * d), lambda i: (i, 0)),
            pl.BlockSpec((bt, CONV_BUF * d), lambda i: (i, 0)),
        ],
        out_shape=[jax.ShapeDtypeStruct((db, n_s * d), F32), jax.ShapeDtypeStruct((db, CONV_BUF * d), F32)],
        compiler_params=_params("parallel"),
        name="conv_sample",
    )(x, state, mix_norm, mix_norm, *cw)


def _qkv_body(x_ref, g0_ref, w_ref, q_ref, k_ref, v_ref, kb_ref, vb_ref):
    d = x_ref.shape[-1]
    h = _rms(x_ref[...], g0_ref[...]).astype(BF16)
    qkv = jnp.dot(h, w_ref[...], preferred_element_type=F32)
    q_ref[...] = (qkv[:, :d] * (HEAD_DIM ** -0.5 * LOG2E)).astype(BF16)
    k = qkv[:, d:2 * d]
    v = qkv[:, 2 * d:]
    k_ref[...] = k
    v_ref[...] = v
    kb_ref[...] = k.astype(BF16)
    vb_ref[...] = v.astype(BF16)


def _qkv(x, mix_norm, wqkv, layer, j, *, tm):
    n, d = x.shape
    row = pl.BlockSpec((tm, d), lambda i: (i, 0))
    return pl.pallas_call(
        _qkv_body,
        grid=(pl.cdiv(n, tm),),
        in_specs=[row, _const_spec((None, 1, d), (layer * 2, 0, 0)), _const_spec((None, d, 3 * d), (j, 0, 0))],
        out_specs=[row] * 5,
        out_shape=[jax.ShapeDtypeStruct((n, d), BF16), jax.ShapeDtypeStruct((n, d), F32),
                   jax.ShapeDtypeStruct((n, d), F32), jax.ShapeDtypeStruct((n, d), BF16),
                   jax.ShapeDtypeStruct((n, d), BF16)],
        compiler_params=_params("parallel"),
        name="qkv_proj",
    )(x, mix_norm, wqkv)


def _bucket(n):
    nf = jnp.maximum(n, 1).astype(F32)
    large = MAX_EXACT + (jnp.log(nf / MAX_EXACT) / math.log(MAX_DISTANCE / MAX_EXACT)
                         * (N_BUCKETS - MAX_EXACT)).astype(jnp.int32)
    large = jnp.minimum(large, N_BUCKETS - 1)
    return jnp.where(n < MAX_EXACT, n, large)


def _lookup(bucket, entry):
    out = jnp.zeros(bucket.shape, F32)
    for b in range(N_BUCKETS):
        out = jnp.where(bucket == b, entry(b), out)
    return out


def _bias_p_body(table_ref, bp_ref, *, blk):
    head = pl.program_id(0)
    a = lax.broadcasted_iota(jnp.int32, (blk, blk), 0)
    b = lax.broadcasted_iota(jnp.int32, (blk, blk), 1)
    for sel in range(3):
        n = sel * blk + b - a
        vals = _lookup(_bucket(jnp.maximum(n, 0)), lambda bb: table_ref[bb * N_HEADS + head])
        bp_ref[0, sel] = jnp.where(n >= 0, vals * LOG2E, NEG_INF)


def _bias_s_body(tt_ref, bs_ref, *, n_s):
    rows = 2 * n_s * N_HEADS
    cols = PAGE_SIZE * N_HEADS
    tt = tt_ref[...]
    trow = jnp.broadcast_to(tt[None], (rows // N_HEADS, N_HEADS, N_BUCKETS)).reshape(rows, N_BUCKETS)

    def tile(width, dist):
        r = lax.broadcasted_iota(jnp.int32, (rows, width), 0)
        c = lax.broadcasted_iota(jnp.int32, (rows, width), 1)
        n = dist((r // N_HEADS) % n_s, c // N_HEADS)
        vals = _lookup(_bucket(jnp.maximum(n, 0)), lambda bb: trow[:, bb:bb + 1])
        return jnp.where((r % N_HEADS == c % N_HEADS) & (n >= 0) & (c < cols), vals * LOG2E, NEG_INF)

    bs_ref[:, 0:cols] = tile(cols, lambda qi, kk: 2 * PAGE_SIZE + qi - kk)
    bs_ref[:, cols:2 * cols] = tile(cols, lambda qi, kk: PAGE_SIZE + qi - kk)
    bs_ref[:, 2 * cols:2 * cols + PAGE_SIZE] = tile(
        PAGE_SIZE, lambda qi, kk: jnp.where(kk < n_s, qi - kk, -1))


def _bias_tiles(table, *, blk, n_s):
    far = _first_far_distance()
    assert blk + 1 >= far and PAGE_SIZE + 1 >= far and n_s * N_HEADS <= PAGE_SIZE
    bias_p = pl.pallas_call(
        functools.partial(_bias_p_body, blk=blk),
        grid=(N_HEADS,),
        in_specs=[pl.BlockSpec(memory_space=pltpu.SMEM)],
        out_specs=pl.BlockSpec((1, 3, blk, blk), lambda h: (h, 0, 0, 0)),
        out_shape=jax.ShapeDtypeStruct((N_HEADS, 3, blk, blk), F32),
        compiler_params=_params("parallel"),
        name="rel_bias_prompt",
    )(table.reshape(-1))
    width = 2 * PAGE_SIZE * N_HEADS + PAGE_SIZE
    bias_s = pl.pallas_call(
        functools.partial(_bias_s_body, n_s=n_s),
        out_shape=jax.ShapeDtypeStruct((2 * n_s * N_HEADS, width), F32),
        compiler_params=pltpu.CompilerParams(vmem_limit_bytes=VMEM_LIMIT),
        name="rel_bias_sample",
    )(table.T)
    return bias_p, bias_s


def _diff_lambda(lam_ref, lam_init):
    lp = lam_ref[...]
    s1 = jnp.sum(lp[0:1] * lp[1:2], axis=-1, keepdims=True)
    s2 = jnp.sum(lp[2:3] * lp[3:4], axis=-1, keepdims=True)
    return jnp.exp(s1) - jnp.exp(s2) + lam_init


_NT = (((1,), (1,)), ((), ()))


def _attn_p_body(q_ref, k_ref, v_ref, x_ref, bias_ref, lam_ref, sub_ref, wo_ref, g1_ref, o_ref,
                 qz_ref, vt_ref, m_ref, acc_ref, s_ref, a_ref, *, blk, n_full, tail, lam_init):
    i = pl.program_id(1)

    @pl.when(i == 0)
    def _():
        for hd in range(N_HEADS):
            c0 = hd * V_DIM

            ones_row = (lax.broadcasted_iota(jnp.int32, (SUM_ROWS, blk), 0) == 0).astype(BF16)

            def xpose(jb, carry):
                r0 = pl.multiple_of(jb * blk, blk)
                vt_ref[hd, jb, 0:V_DIM, :] = v_ref[0, pl.ds(r0, blk), c0:c0 + V_DIM].T
                vt_ref[hd, jb, V_DIM:V_DIM + SUM_ROWS, :] = ones_row
                return carry

            lax.fori_loop(0, n_full, xpose, 0)
            if tail:
                r0 = n_full * blk
                vt_ref[hd, n_full, 0:V_DIM, 0:tail] = (
                    v_ref[0, r0:r0 + tail, c0:c0 + V_DIM].astype(F32).T.astype(BF16))
                vt_ref[hd, n_full, V_DIM:V_DIM + SUM_ROWS, :] = ones_row

    def run(wq, n_loop, tail_keys):
        lane = lax.broadcasted_iota(jnp.int32, (wq, V_DIM), 1)
        for hd in range(N_HEADS):
            q = q_ref[0, 0:wq, hd * V_DIM:(hd + 1) * V_DIM]
            qz_ref[hd, 0:wq, :] = jnp.where(lane < HEAD_DIM, q, jnp.zeros_like(q))
            qz_ref[hd, wq:2 * wq, :] = jnp.where(lane >= HEAD_DIM, q, jnp.zeros_like(q))
        m_ref[:, :, 0:2 * wq] = jnp.full((N_HEADS, 1, 2 * wq), NEG_INF, F32)
        acc_ref[:, :, 0:2 * wq] = jnp.zeros((N_HEADS, V_DIM + SUM_ROWS, 2 * wq), F32)

        def all_heads(n, keys, vt, bias):
            def stage(hd):
                b2 = bias(hd)
                s_ref[hd % 2, 0:n, 0:2 * wq] = (
                    lax.dot_general(keys(hd), qz_ref[hd, 0:2 * wq, :], _NT, preferred_element_type=F32)
                    + jnp.concatenate([b2, b2], axis=1))

            stage(0)
            for hd in range(N_HEADS):
                if hd + 1 < N_HEADS:
                    stage(hd + 1)
                s = s_ref[hd % 2, 0:n, 0:2 * wq]
                m_old = m_ref[hd, :, 0:2 * wq]
                m_new = jnp.maximum(m_old, jnp.max(s, axis=0, keepdims=True))
                alpha = jnp.exp2(m_old - m_new)
                p = jnp.exp2(s - m_new)
                m_ref[hd, :, 0:2 * wq] = m_new
                acc_ref[hd, :, 0:2 * wq] = (alpha * acc_ref[hd, :, 0:2 * wq]
                                            + jnp.dot(vt(hd), p.astype(BF16), preferred_element_type=F32))

        def body(jb, carry):
            r0 = pl.multiple_of(jb * blk, blk)
            sel = jnp.minimum(i - jb, 2)
            all_heads(blk,
                      lambda hd: k_ref[0, pl.ds(r0, blk), hd * V_DIM:(hd + 1) * V_DIM],
                      lambda hd: vt_ref[hd, jb],
                      lambda hd: bias_ref[hd, sel, :, 0:wq])
            return carry

        lax.fori_loop(0, n_loop, body, 0)
        if tail_keys:
            r0 = n_full * blk
            all_heads(tail_keys,
                      lambda hd: k_ref[0, r0:r0 + tail_keys, hd * V_DIM:(hd + 1) * V_DIM],
                      lambda hd: vt_ref[hd, n_full, :, 0:tail_keys],
                      lambda hd: bias_ref[hd, 0, 0:tail_keys, 0:wq])

        lam = _diff_lambda(lam_ref, lam_init)
        for hd in range(N_HEADS):
            o_both = acc_ref[hd, 0:V_DIM, 0:2 * wq] * (1.0 / acc_ref[hd, V_DIM:V_DIM + 1, 0:2 * wq])
            o = o_both[:, 0:wq] - lam * o_both[:, wq:2 * wq]
            o = o * lax.rsqrt(jnp.mean(o * o, axis=0, keepdims=True) + RMS_EPS) * sub_ref[...] * (1.0 - lam_init)
            a_ref[0:wq, hd * V_DIM:(hd + 1) * V_DIM] = o.T.astype(BF16)
        mix = jnp.dot(a_ref[0:wq, :], wo_ref[...], preferred_element_type=F32)
        o_ref[0, 0:wq, :] = x_ref[0, 0:wq, :] + _rms(mix, g1_ref[...])

    @pl.when(i < n_full)
    def _():
        run(blk, i + 1, 0)

    if tail:
        @pl.when(i == n_full)
        def _():
            run(-(-tail // LANES) * LANES, n_full, tail)


def _attn_prompt(q, kb, vb, x, bias_p, lam_p, subln_col, wo, mix_norm, layer, j, *, blk, lam_init):
    b, t, d = q.shape
    n_full, tail = t // blk, t % blk
    assert tail % 16 == 0
    n_blocks = n_full + (1 if tail else 0)
    seq_blk = pl.BlockSpec((1, blk, d), lambda bi, qi: (bi, qi, 0))
    return pl.pallas_call(
        functools.partial(_attn_p_body, blk=blk, n_full=n_full, tail=tail, lam_init=lam_init),
        grid=(b, n_blocks),
        in_specs=[
            seq_blk,
            pl.BlockSpec((1, t, d), lambda bi, qi: (bi, 0, 0)),
            pl.BlockSpec((1, t, d), lambda bi, qi: (bi, 0, 0)),
            seq_blk,
            _const_spec((N_HEADS, 3, blk, blk), (0, 0, 0, 0)),
            _const_spec((None, 4, HEAD_DIM), (j, 0, 0)),
            _const_spec((None, V_DIM, 1), (j, 0, 0)),
            _const_spec((None, d, d), (j, 0, 0)),
            _const_spec((None, 1, d), (layer * 2 + 1, 0, 0)),
        ],
        out_specs=seq_blk,
        out_shape=jax.ShapeDtypeStruct((b, t, d), F32),
        scratch_shapes=[
            pltpu.VMEM((N_HEADS, 2 * blk, V_DIM), BF16),
            pltpu.VMEM((N_HEADS, n_blocks, V_DIM + SUM_ROWS, blk), BF16),
            pltpu.VMEM((N_HEADS, 1, 2 * blk), F32),
            pltpu.VMEM((N_HEADS, V_DIM + SUM_ROWS, 2 * blk), F32),
            pltpu.VMEM((2, blk, 2 * blk), F32),
            pltpu.VMEM((blk, d), BF16),
        ],
        compiler_params=_params("arbitrary", "arbitrary"),
        name="attn_prompt",
    )(q, kb, vb, x, bias_p, lam_p, subln_col, wo, mix_norm)


def _attn_s_body(pt_ref, q_ref, kn_ref, vn_ref, bias_ref, lam_ref, sub_ref, *rest, n_pages, n_s, lam_init):
    k_pages = rest[:n_pages]
    v_pages = rest[n_pages:2 * n_pages]
    o_ref, s_ref = rest[2 * n_pages:]
    half = n_s * N_HEADS
    cols = PAGE_SIZE * N_HEADS

    q = q_ref[...].reshape(half, V_DIM)
    lane = lax.broadcasted_iota(jnp.int32, (half, V_DIM), 1)
    q2 = jnp.concatenate([jnp.where(lane < HEAD_DIM, q, 0.0), jnp.where(lane >= HEAD_DIM, q, 0.0)], axis=0).astype(BF16)

    def scores(keys, bias):
        return lax.dot_general(q2, keys, _NT, preferred_element_type=F32) + bias

    pad = jnp.zeros((PAGE_SIZE - half, V_DIM), F32)
    kn = jnp.concatenate([kn_ref[...].reshape(half, V_DIM), pad], axis=0).astype(BF16)
    vn = jnp.concatenate([vn_ref[...].reshape(half, V_DIM), pad], axis=0).astype(BF16)
    s_new = scores(kn, bias_ref[:, 2 * cols:2 * cols + PAGE_SIZE])
    m = jnp.max(s_new, axis=-1, keepdims=True)
    for r_ in range(n_pages):
        near = r_ == n_pages - 1
        s = scores(k_pages[r_][...].reshape(cols, V_DIM).astype(BF16),
                   bias_ref[:, cols:2 * cols] if near else bias_ref[:, 0:cols])
        s_ref[r_] = s
        m = jnp.maximum(m, jnp.max(s, axis=-1, keepdims=True))

    p = jnp.exp2(s_new - m)
    l = jnp.sum(p, axis=-1, keepdims=True)
    acc = jnp.dot(p.astype(BF16), vn, preferred_element_type=F32)
    for r_ in range(n_pages):
        p = jnp.exp2(s_ref[r_] - m)
        l = l + jnp.sum(p, axis=-1, keepdims=True)
        acc = acc + jnp.dot(p.astype(BF16), v_pages[r_][...].reshape(cols, V_DIM).astype(BF16),
                            preferred_element_type=F32)

    lam = _diff_lambda(lam_ref, lam_init)
    o_all = acc * (1.0 / l)
    o = o_all[0:half] - lam * o_all[half:2 * half]
    o_ref[...] = (_rms(o, sub_ref[...]) * (1.0 - lam_init)).reshape(o_ref.shape)


def _attn_sample(q, kn, vn, cache_k, cache_v, page_table, bias_s, lam_p, subln, j, *, n_pages, lam_init):
    db, n_s = q.shape[:2]
    assert n_pages == page_table.shape[1]
    rows = 2 * n_s * N_HEADS
    new_kv = pl.BlockSpec((1, n_s, 1, N_HEADS, V_DIM), lambda b, pt: (b, 0, 0, 0, 0))
    q_spec = pl.BlockSpec((1, n_s, N_HEADS, V_DIM), lambda b, pt: (b, 0, 0, 0))

    def page_spec(r):
        return pl.BlockSpec((1, PAGE_SIZE, 1, N_HEADS, V_DIM), lambda b, pt: (pt[b, r], 0, j, 0, 0))

    grid_spec = pltpu.PrefetchScalarGridSpec(
        num_scalar_prefetch=1,
        grid=(db,),
        in_specs=[
            q_spec, new_kv, new_kv,
            pl.BlockSpec(bias_s.shape, lambda b, pt: (0, 0)),
            pl.BlockSpec((None, 4, HEAD_DIM), lambda b, pt: (j, 0, 0)),
            pl.BlockSpec((None, 1, V_DIM), lambda b, pt: (j, 0, 0)),
        ] + [page_spec(r) for r in range(n_pages)] * 2,
        out_specs=q_spec,
        scratch_shapes=[pltpu.VMEM((n_pages, rows, PAGE_SIZE * N_HEADS), F32)],
    )
    return pl.pallas_call(
        functools.partial(_attn_s_body, n_pages=n_pages, n_s=n_s, lam_init=lam_init),
        grid_spec=grid_spec,
        out_shape=jax.ShapeDtypeStruct((db, n_s, N_HEADS, V_DIM), F32),
        compiler_params=_params("parallel"),
        name="attn_sample",
    )(page_table, q, kn, vn, bias_s, lam_p, subln, *([cache_k] * n_pages), *([cache_v] * n_pages))


def _outproj_body(a_ref, x_ref, w_ref, g1_ref, o_ref):
    m = jnp.dot(a_ref[...].astype(BF16), w_ref[...], preferred_element_type=F32)
    o_ref[...] = x_ref[...] + _rms(m, g1_ref[...])


def _outproj(a, x, wo, mix_norm, layer, j, *, tm):
    n, d = x.shape
    row = pl.BlockSpec((tm, d), lambda i: (i, 0))
    return pl.pallas_call(
        _outproj_body,
        grid=(pl.cdiv(n, tm),),
        in_specs=[row, row, _const_spec((None, d, d), (j, 0, 0)), _const_spec((None, 1, d), (layer * 2 + 1, 0, 0))],
        out_specs=row,
        out_shape=jax.ShapeDtypeStruct((n, d), F32),
        compiler_params=_params("parallel"),
        name="attn_outproj",
    )(a, x, wo, mix_norm)


def _pick_tile(n, cap, align=8):
    best = None
    for c in range(align, min(n, cap) + 1, align):
        if n % c == 0:
            best = c
    assert best is not None, (n, cap, align)
    return best


def kernel(x_prompt, x_sample, state_pool, state_conv, cache_k, cache_v, page_table, meta_tokens, rel_bias_table, ffn_norm, ffn_wg, ffn_wu, ffn_wd, mix_norm, pool_w, pool_scale, attn_wqkv, attn_wo, attn_lambda, attn_subln, conv_w1, conv_b1, conv_wdw, conv_bdw, conv_ln_g, conv_ln_b, conv_w2, conv_b2, final_norm):
    b, seq, d = x_prompt.shape
    db, n_s, _ = x_sample.shape
    depth = ffn_wg.shape[0]
    t = seq + N_META
    past = page_table.shape[1] * PAGE_SIZE

    wg, wu, wd = ffn_wg, ffn_wu, ffn_wd
    ffn_norm3 = ffn_norm.reshape(-1, 1, d)
    mix_norm3 = mix_norm.reshape(-1, 1, d)
    pool_w_b = pool_w.astype(BF16)
    pool_scale3 = pool_scale.reshape(-1, 1, d)
    wqkv_b, wo_b = attn_wqkv.astype(BF16), attn_wo.astype(BF16)
    subln3 = attn_subln.reshape(-1, 1, V_DIM)
    cw = (conv_w1.astype(BF16), conv_b1.reshape(-1, 1, 2 * d), conv_wdw, conv_bdw.reshape(-1, 1, d),
          conv_ln_g.reshape(-1, 1, d), conv_ln_b.reshape(-1, 1, d), conv_w2.astype(BF16), conv_b2.reshape(-1, 1, d))
    state_pool2 = state_pool.reshape(state_pool.shape[0], db, POOL_BUF * d)
    state_conv2 = state_conv.reshape(state_conv.shape[0], db, CONV_BUF * d)
    final_g = final_norm.reshape(1, d)

    meta = jnp.broadcast_to(meta_tokens[None].astype(x_prompt.dtype), (b, N_META, d))
    xp = jnp.concatenate([meta, x_prompt], axis=1).reshape(b * t, d)
    xs = x_sample.reshape(db * n_s, d)

    tm_p = _pick_tile(b * t, 768, 16)
    tm_s = min(512, db * n_s)
    ts_pool = _pick_tile(t, 1032)
    ts_conv = _pick_tile(t, 344)
    bt = _pick_tile(db, 32)

    bias_p = bias_s = None
    pool_p, pool_s, conv_p, conv_s = [], [], [], []
    k_p = v_p = k_s = v_s = None
    for i in range(depth):
        kind, j = i % N_MIXERS, i // N_MIXERS
        xp, xs = _ffn(xp, xs, ffn_norm3, final_g, wg, wu, wd, i, 0, tm=tm_p)
        if kind == 0:
            xp3, st = _pool_prompt(xp.reshape(b, t, d), mix_norm3, pool_w_b, pool_scale3, i, j, ts=ts_pool)
            xp = xp3.reshape(b * t, d)
            pool_p.append(st)
            xs2, st = _pool_sample(xs.reshape(db, n_s * d), state_pool2, mix_norm3, pool_w_b, pool_scale3, i, j,
                                   n_s=n_s, past=past, bt=bt)
            xs = xs2.reshape(db * n_s, d)
            pool_s.append(st.reshape(db, POOL_BUF, d))
        elif kind == 1:
            assert j == 0
            lam_init = _lambda_init(i)
            if bias_p is None:
                bias_p, bias_s = _bias_tiles(rel_bias_table, blk=ATTN_BLOCK, n_s=n_s)
            q, k_p, v_p, kb, vb = _qkv(xp, mix_norm3, wqkv_b, i, j, tm=tm_p)
            xp = _attn_prompt(q.reshape(b, t, d), kb.reshape(b, t, d), vb.reshape(b, t, d), xp.reshape(b, t, d),
                              bias_p, attn_lambda, attn_subln.reshape(-1, V_DIM, 1), wo_b, mix_norm3, i, j,
                              blk=ATTN_BLOCK, lam_init=lam_init).reshape(b * t, d)
            q, k_s, v_s, _, _ = _qkv(xs, mix_norm3, wqkv_b, i, j, tm=tm_s)
            k_s = k_s.reshape(db, n_s, 1, N_HEADS, V_DIM)
            v_s = v_s.reshape(db, n_s, 1, N_HEADS, V_DIM)
            a = _attn_sample(q.astype(F32).reshape(db, n_s, N_HEADS, V_DIM), k_s, v_s, cache_k, cache_v,
                             page_table, bias_s, attn_lambda, subln3, j, n_pages=page_table.shape[1], lam_init=lam_init)
            xs = _outproj(a.reshape(db * n_s, d), xs, wo_b, mix_norm3, i, j, tm=tm_s)
        else:
            xp3, st = _conv_prompt(xp.reshape(b, t, d), mix_norm3, cw, i, j, ts=ts_conv)
            xp = xp3.reshape(b * t, d)
            conv_p.append(st)
            xs2, st = _conv_sample(xs.reshape(db, n_s * d), state_conv2, mix_norm3, cw, i, j, n_s=n_s, bt=bt)
            xs = xs2.reshape(db * n_s, d)
            conv_s.append(st.reshape(db, CONV_BUF, d))
        xp, xs = _ffn(xp, xs, ffn_norm3, final_g, wg, wu, wd, i, 1, tm=tm_p,
                      final_seq=(t, N_META) if i == depth - 1 else None)

    kv_shape_p = (b, t, 1, N_HEADS, V_DIM)
    return (xp.reshape(b, seq, d), xs.reshape(db, n_s, d), jnp.stack(pool_p, axis=0), jnp.stack(pool_s, axis=0),
            jnp.stack(conv_p, axis=0), jnp.stack(conv_s, axis=0),
            k_p.reshape(kv_shape_p), v_p.reshape(kv_shape_p), k_s, v_s)
```

```python
import functools
import math

import jax
import jax.numpy as jnp
from jax import lax
from jax.experimental import pallas as pl
from jax.experimental.pallas import tpu as pltpu

F32 = jnp.float32
BF16 = jnp.bfloat16

N_MIXERS = 3
N_META = 16
N_HEADS = 8
HEAD_DIM = 64
V_DIM = 2 * HEAD_DIM
N_BUCKETS = 32
MAX_EXACT = N_BUCKETS // 2
MAX_DISTANCE = 128
POOL_WINDOWS = (2, 4, 8, 16)
POOL_BUF = max(POOL_WINDOWS) - 1
CONV_WIDTH = 31
CONV_BUF = CONV_WIDTH - 1
PAGE_SIZE = 128
RMS_EPS = 1e-6
LN_EPS = 1e-5
NEG_INF = -1e30
LOG2E = math.log2(math.e)
SUBLANES = 8
LANES = 128
BF16_ROWS = 16
MXU_TILE = 256
SUM_ROWS = 16
ATTN_BLOCK = 256
POOL_HALO = 16
CONV_HALO = 32
VMEM_LIMIT = 56 * 1024 * 1024
WEIGHT_CHUNK_BYTES = 3 * 256 * 1024
WEIGHT_SLOTS = 4


def _lambda_init(layer_idx):
    return 0.8 - 0.6 * math.exp(-0.3 * layer_idx)


def _first_far_distance():
    n = MAX_EXACT
    while MAX_EXACT + int(math.log(n / MAX_EXACT) / math.log(MAX_DISTANCE / MAX_EXACT) * (N_BUCKETS - MAX_EXACT)) < N_BUCKETS - 1:
        n += 1
    return n


def _params(*sem):
    return pltpu.CompilerParams(dimension_semantics=sem, vmem_limit_bytes=VMEM_LIMIT)


def _rms(x, g):
    return x * lax.rsqrt(jnp.mean(x * x, axis=-1, keepdims=True) + RMS_EPS) * g


def _silu(x):
    return x * jax.nn.sigmoid(x)


def _const_spec(shape, index, single=False):
    if single:
        return pl.BlockSpec(shape, lambda *_: index, pipeline_mode=pl.Buffered(1))
    return pl.BlockSpec(shape, lambda *_: index)


def _ffn_chunks(ff, n_chunks):
    tiles = -(-ff // MXU_TILE)
    edges = [min(ff, MXU_TILE * (-(-tiles * c // n_chunks))) for c in range(n_chunks + 1)]
    return [(lo, hi - lo) for lo, hi in zip(edges[:-1], edges[1:]) if hi > lo]


def _ffn_body(xp_ref, xs_ref, n0_ref, n1_ref, gf_ref, wg_hbm, wu_hbm, wd_hbm, op_ref, os_ref,
              wg_ref, wu_ref, wd_ref, stage_in, stage_out, sem, *, n_chunks, n_p, final, layer, f):
    i = pl.program_id(0)

    def fetch(w_hbm, stage, dst):
        slots, rows = stage.shape[0], stage.shape[1]
        n = dst.shape[0] // rows

        def copy(c):
            return pltpu.make_async_copy(w_hbm.at[layer, f, pl.ds(c * rows, rows), :], stage.at[c % slots],
                                         sem.at[c % slots])

        for c in range(min(slots - 1, n)):
            copy(c).start()
        for c in range(n):
            if c + slots - 1 < n:
                copy(c + slots - 1).start()
            copy(c).wait()
            dst[pl.ds(c * rows, rows), :] = stage[c % slots].astype(BF16)

    @pl.when(i == 0)
    def _():
        fetch(wg_hbm, stage_in, wg_ref)
        fetch(wu_hbm, stage_in, wu_ref)
        fetch(wd_hbm, stage_out, wd_ref)

    def apply(x_ref, o_ref):
        rows = x_ref.shape[0]
        first = -(-rows // (2 * BF16_ROWS)) * BF16_ROWS
        for lo_row, n_rows in ((0, first), (first, rows - first))[:2 if rows > first else 1]:
            rs = pl.ds(lo_row, n_rows)
            x = x_ref[rs, :]
            h = _rms(x, n0_ref[...]).astype(BF16)
            acc = jnp.zeros(x.shape, F32)
            for lo, width in _ffn_chunks(wg_ref.shape[1], n_chunks):
                sl = pl.ds(lo, width)
                g = jnp.dot(h, wg_ref[:, sl], preferred_element_type=F32)
                u = jnp.dot(h, wu_ref[:, sl], preferred_element_type=F32)
                a = (_silu(g) * u).astype(BF16)
                acc = acc + jnp.dot(a, wd_ref[sl, :], preferred_element_type=F32)
            y = x + 0.5 * _rms(acc, n1_ref[...])
            o_ref[rs, :] = _rms(y, gf_ref[...]) if final else y

    @pl.when(i == 0)
    def _():
        os_ref[...] = jnp.zeros(os_ref.shape, F32)

    @pl.when(i < n_p)
    def _():
        apply(xp_ref, op_ref)

    @pl.when(i >= n_p)
    def _():
        apply(xs_ref, os_ref)


def _ffn(xp, xs, norms, final_g, wg, wu, wd, layer, f, *, tm, n_chunks=2, final_seq=None):
    (n_rows_p, d), n_rows_s = xp.shape, xs.shape[0]
    ff = wg.shape[-1]
    nidx = (layer * 2 + f) * 2
    tm_s = _pick_tile(n_rows_s, tm, BF16_ROWS)
    if final_seq is None:
        n_p, out_rows_p = pl.cdiv(n_rows_p, tm), n_rows_p
        p_in = p_out = pl.BlockSpec((tm, d), lambda i: (jnp.minimum(i, n_p - 1), 0))
    else:
        t, skip = final_seq
        assert skip % BF16_ROWS == 0 and t % BF16_ROWS == 0
        tm = _pick_tile(t - skip, tm, BF16_ROWS)
        per_seq = (t - skip) // tm
        n_p, out_rows_p = (n_rows_p // t) * per_seq, (n_rows_p // t) * (t - skip)

        def in_rows(i):
            ip = jnp.minimum(i, n_p - 1)
            return pl.multiple_of((ip // per_seq) * t + skip + (ip % per_seq) * tm, BF16_ROWS), 0

        p_in = pl.BlockSpec((pl.Element(tm), pl.Element(d)), in_rows)
        p_out = pl.BlockSpec((tm, d), lambda i: (jnp.minimum(i, n_p - 1), 0))
    n_s = pl.cdiv(n_rows_s, tm_s)
    s_spec = pl.BlockSpec((tm_s, d), lambda i: (jnp.maximum(i - n_p, 0), 0))
    in_rows, out_rows = _pick_tile(d, WEIGHT_CHUNK_BYTES // (4 * ff)), _pick_tile(ff, WEIGHT_CHUNK_BYTES // (4 * d))
    hbm = pl.BlockSpec(memory_space=pl.ANY)
    return pl.pallas_call(
        functools.partial(_ffn_body, n_chunks=n_chunks, n_p=n_p, final=final_seq is not None, layer=layer, f=f),
        grid=(n_p + n_s,),
        in_specs=[
            p_in, s_spec,
            _const_spec((None, 1, d), (nidx, 0, 0)),
            _const_spec((None, 1, d), (nidx + 1, 0, 0)),
            _const_spec((1, d), (0, 0)),
            hbm, hbm, hbm,
        ],
        out_specs=[p_out, s_spec],
        out_shape=[jax.ShapeDtypeStruct((out_rows_p, d), F32), jax.ShapeDtypeStruct((n_rows_s, d), F32)],
        scratch_shapes=[
            pltpu.VMEM((d, ff), BF16), pltpu.VMEM((d, ff), BF16), pltpu.VMEM((ff, d), BF16),
            pltpu.VMEM((WEIGHT_SLOTS, in_rows, ff), F32), pltpu.VMEM((WEIGHT_SLOTS, out_rows, d), F32),
            pltpu.SemaphoreType.DMA((WEIGHT_SLOTS,)),
        ],
        compiler_params=_params("arbitrary"),
        name="ffn",
    )(xp, xs, norms, norms, final_g, wg, wu, wd)


def _pool_p_body(x_ref, g0_ref, g1_ref, w_ref, sc_ref, o_ref, st_ref, *lvl_refs, ts, n_t):
    t = pl.program_id(1)
    d = x_ref.shape[-1]
    n_g = len(POOL_WINDOWS)
    gd = d // n_g
    x = x_ref[0]
    h = _rms(x, g0_ref[...])
    cat_ref = lvl_refs[0]
    top = SUBLANES + POOL_HALO
    rows = POOL_HALO + ts

    @pl.when(t == 0)
    def _():
        cat_ref[0:top, :] = jnp.zeros((top, d), F32)
        for ref in lvl_refs[1:]:
            ref[0:SUBLANES, :] = jnp.zeros((SUBLANES, ref.shape[1]), F32)

    cat_ref[top:top + ts, :] = h
    for k in range(1, len(lvl_refs)):
        prev, cur = lvl_refs[k - 1], lvl_refs[k]
        lo = prev.shape[1] - cur.shape[1]
        cur[SUBLANES:SUBLANES + rows, :] = (prev[SUBLANES:SUBLANES + rows, lo:]
                                            + prev[SUBLANES - 2 ** (k - 1):SUBLANES - 2 ** (k - 1) + rows, lo:])
    pos = t * ts + lax.broadcasted_iota(jnp.int32, (ts, 1), 0)
    outs = []
    for g, w in enumerate(POOL_WINDOWS):
        c0 = g * gd
        k = w.bit_length() - 1
        if k < len(lvl_refs):
            ref = lvl_refs[k]
            l0 = c0 - (d - ref.shape[1])
            acc = ref[top:top + ts, l0:l0 + gd]
        else:
            ref = lvl_refs[k - 1]
            l0 = c0 - (d - ref.shape[1])
            acc = ref[top:top + ts, l0:l0 + gd] + ref[top - w // 2:top - w // 2 + ts, l0:l0 + gd]
        inv_cnt = 1.0 / jnp.minimum(w, pos + 1).astype(F32)
        pooled = (acc * inv_cnt - h[:, c0:c0 + gd]).astype(BF16)
        outs.append(jnp.dot(pooled, w_ref[g], preferred_element_type=F32))
    m = jnp.concatenate(outs, axis=-1) * sc_ref[...]
    o_ref[0] = x + _rms(m, g1_ref[...])

    @pl.when(t == n_t - 1)
    def _():
        st_ref[0] = cat_ref[top + ts - POOL_BUF:top + ts, :]

    cat_ref[SUBLANES:top, :] = cat_ref[SUBLANES + ts:top + ts, :]


def _pool_prompt(x, mix_norm, pool_w, pool_scale, layer, j, *, ts):
    b, t, d = x.shape
    n_t = t // ts
    assert n_t * ts == t and ts % 8 == 0 and ts >= POOL_HALO
    g = len(POOL_WINDOWS)
    assert all(w == 2 ** (i + 1) for i, w in enumerate(POOL_WINDOWS)) and POOL_WINDOWS[-1] // 2 == SUBLANES
    buf_rows = SUBLANES + POOL_HALO + ts
    level_lanes = [d] + [d - i * (d // g) for i in range(g - 1)]
    return pl.pallas_call(
        functools.partial(_pool_p_body, ts=ts, n_t=n_t),
        grid=(b, n_t),
        in_specs=[
            pl.BlockSpec((1, ts, d), lambda bi, ti: (bi, ti, 0)),
            _const_spec((None, 1, d), (layer * 2, 0, 0)),
            _const_spec((None, 1, d), (layer * 2 + 1, 0, 0)),
            _const_spec((None, g, d // g, d // g), (j, 0, 0, 0)),
            _const_spec((None, 1, d), (j, 0, 0)),
        ],
        out_specs=[
            pl.BlockSpec((1, ts, d), lambda bi, ti: (bi, ti, 0)),
            pl.BlockSpec((1, POOL_BUF, d), lambda bi, ti: (bi, 0, 0)),
        ],
        out_shape=[jax.ShapeDtypeStruct((b, t, d), F32), jax.ShapeDtypeStruct((b, POOL_BUF, d), F32)],
        scratch_shapes=[pltpu.VMEM((buf_rows, lanes), F32) for lanes in level_lanes],
        compiler_params=_params("parallel", "arbitrary"),
        name="pool_prompt",
    )(x, mix_norm, mix_norm, pool_w, pool_scale)


def _pool_s_body(x_ref, st_ref, g0_ref, g1_ref, w_ref, sc_ref, o_ref, so_ref, *, n_s, past):
    d = g0_ref.shape[-1]
    gd = d // len(POOL_WINDOWS)
    bt = x_ref.shape[0]
    xs = [x_ref[:, s * d:(s + 1) * d] for s in range(n_s)]
    hs = [_rms(x, g0_ref[...]) for x in xs]

    def cat(idx, c0, width):
        if idx < POOL_BUF:
            return st_ref[:, idx * d + c0:idx * d + c0 + width]
        return hs[idx - POOL_BUF][:, c0:c0 + width]

    outs = []
    for g, w in enumerate(POOL_WINDOWS):
        c0 = g * gd
        rows = []
        for s in range(n_s):
            acc = cat(POOL_BUF + s, c0, gd)
            for jj in range(1, w):
                acc = acc + cat(POOL_BUF + s - jj, c0, gd)
            rows.append(acc * (1.0 / min(w, past + s + 1)) - hs[s][:, c0:c0 + gd])
        pooled = jnp.concatenate(rows, axis=0).astype(BF16)
        outs.append(jnp.dot(pooled, w_ref[g], preferred_element_type=F32))
    for s in range(n_s):
        m = jnp.concatenate([o[s * bt:(s + 1) * bt] for o in outs], axis=-1) * sc_ref[...]
        o_ref[:, s * d:(s + 1) * d] = xs[s] + _rms(m, g1_ref[...])
    for jj in range(POOL_BUF):
        so_ref[:, jj * d:(jj + 1) * d] = cat(n_s + jj, 0, d)


def _pool_sample(x, state, mix_norm, pool_w, pool_scale, layer, j, *, n_s, past, bt):
    db = x.shape[0]
    d = mix_norm.shape[-1]
    g = len(POOL_WINDOWS)
    return pl.pallas_call(
        functools.partial(_pool_s_body, n_s=n_s, past=past),
        grid=(db // bt,),
        in_specs=[
            pl.BlockSpec((bt, n_s * d), lambda i: (i, 0)),
            pl.BlockSpec((None, bt, POOL_BUF * d), lambda i: (j, i, 0)),
            _const_spec((None, 1, d), (layer * 2, 0, 0)),
            _const_spec((None, 1, d), (layer * 2 + 1, 0, 0)),
            _const_spec((None, g, d // g, d // g), (j, 0, 0, 0)),
            _const_spec((None, 1, d), (j, 0, 0)),
        ],
        out_specs=[
            pl.BlockSpec((bt, n_s * d), lambda i: (i, 0)),
            pl.BlockSpec((bt, POOL_BUF * d), lambda i: (i, 0)),
        ],
        out_shape=[jax.ShapeDtypeStruct((db, n_s * d), F32), jax.ShapeDtypeStruct((db, POOL_BUF * d), F32)],
        compiler_params=_params("parallel"),
        name="pool_sample",
    )(x, state, mix_norm, mix_norm, pool_w, pool_scale)


def _layer_norm(y, g, b):
    mu = jnp.mean(y, axis=-1, keepdims=True)
    yc = y - mu
    var = jnp.mean(yc * yc, axis=-1, keepdims=True)
    return yc * lax.rsqrt(var + LN_EPS) * g + b


def _conv_p_body(x_ref, g0_ref, g1_ref, w1_ref, b1_ref, wdw_ref, bdw_ref, lng_ref, lnb_ref, w2_ref, b2_ref,
                 o_ref, st_ref, cat_ref, y_ref, *, ts, n_t):
    t = pl.program_id(1)
    d = x_ref.shape[-1]
    x = x_ref[0]
    h = _rms(x, g0_ref[...]).astype(BF16)
    glu = jnp.dot(h, w1_ref[...], preferred_element_type=F32) + b1_ref[...]
    u = glu[:, :d] * jax.nn.sigmoid(glu[:, d:])

    @pl.when(t == 0)
    def _():
        cat_ref[0:CONV_HALO, :] = jnp.zeros((CONV_HALO, d), F32)
        cat_ref[CONV_HALO + ts:CONV_HALO + ts + SUBLANES, :] = jnp.zeros((SUBLANES, d), F32)

    cat_ref[CONV_HALO:CONV_HALO + ts, :] = u
    off = CONV_HALO - CONV_BUF
    zrows = ts + SUBLANES
    for c0 in range(0, d, LANES):
        y = None
        for r in range(SUBLANES):
            z = None
            for a in range((CONV_WIDTH + off) // SUBLANES + 1):
                k = SUBLANES * a + r - off
                if 0 <= k < CONV_WIDTH:
                    term = cat_ref[SUBLANES * a:SUBLANES * a + zrows, c0:c0 + LANES] * wdw_ref[k:k + 1, c0:c0 + LANES]
                    z = term if z is None else z + term
            zs = z[r:r + ts]
            y = zs if y is None else y + zs
        y_ref[:, c0:c0 + LANES] = y + bdw_ref[:, c0:c0 + LANES]
    a = _silu(_layer_norm(y_ref[...], lng_ref[...], lnb_ref[...])).astype(BF16)
    m = jnp.dot(a, w2_ref[...], preferred_element_type=F32) + b2_ref[...]
    o_ref[0] = x + _rms(m, g1_ref[...])

    @pl.when(t == n_t - 1)
    def _():
        st_ref[0] = cat_ref[CONV_HALO + ts - CONV_BUF:CONV_HALO + ts, :]

    cat_ref[0:CONV_HALO, :] = cat_ref[ts:ts + CONV_HALO, :]


def _conv_specs(layer, j, d):
    return [
        _const_spec((None, 1, d), (layer * 2, 0, 0)),
        _const_spec((None, 1, d), (layer * 2 + 1, 0, 0)),
        _const_spec((None, d, 2 * d), (j, 0, 0)),
        _const_spec((None, 1, 2 * d), (j, 0, 0)),
        _const_spec((None, CONV_WIDTH, d), (j, 0, 0)),
        _const_spec((None, 1, d), (j, 0, 0)),
        _const_spec((None, 1, d), (j, 0, 0)),
        _const_spec((None, 1, d), (j, 0, 0)),
        _const_spec((None, d, d), (j, 0, 0)),
        _const_spec((None, 1, d), (j, 0, 0)),
    ]


def _conv_prompt(x, mix_norm, cw, layer, j, *, ts):
    b, t, d = x.shape
    n_t = t // ts
    assert n_t * ts == t and ts % 8 == 0 and ts >= CONV_HALO
    return pl.pallas_call(
        functools.partial(_conv_p_body, ts=ts, n_t=n_t),
        grid=(b, n_t),
        in_specs=[pl.BlockSpec((1, ts, d), lambda bi, ti: (bi, ti, 0))] + _conv_specs(layer, j, d),
        out_specs=[
            pl.BlockSpec((1, ts, d), lambda bi, ti: (bi, ti, 0)),
            pl.BlockSpec((1, CONV_BUF, d), lambda bi, ti: (bi, 0, 0)),
        ],
        out_shape=[jax.ShapeDtypeStruct((b, t, d), F32), jax.ShapeDtypeStruct((b, CONV_BUF, d), F32)],
        scratch_shapes=[pltpu.VMEM((CONV_HALO + ts + SUBLANES, d), F32), pltpu.VMEM((ts, d), F32)],
        compiler_params=_params("parallel", "arbitrary"),
        name="conv_prompt",
    )(x, mix_norm, mix_norm, *cw)


def _conv_s_body(x_ref, st_ref, g0_ref, g1_ref, w1_ref, b1_ref, wdw_ref, bdw_ref, lng_ref, lnb_ref, w2_ref, b2_ref,
                 o_ref, so_ref, *, n_s):
    d = g0_ref.shape[-1]
    bt = x_ref.shape[0]
    xs = [x_ref[:, s * d:(s + 1) * d] for s in range(n_s)]
    h = jnp.concatenate([_rms(x, g0_ref[...]) for x in xs], axis=0).astype(BF16)
    z = jnp.dot(h, w1_ref[...], preferred_element_type=F32) + b1_ref[...]
    u = z[:, :d] * jax.nn.sigmoid(z[:, d:])
    us = [u[s * bt:(s + 1) * bt] for s in range(n_s)]

    def cat(idx):
        if idx < CONV_BUF:
            return st_ref[:, idx * d:(idx + 1) * d]
        return us[idx - CONV_BUF]

    acts = []
    for s in range(n_s):
        y = cat(s) * wdw_ref[0:1, :] + bdw_ref[...]
        for k in range(1, CONV_WIDTH):
            y = y + cat(s + k) * wdw_ref[k:k + 1, :]
        acts.append(_silu(_layer_norm(y, lng_ref[...], lnb_ref[...])))
    a = jnp.concatenate(acts, axis=0).astype(BF16)
    m = jnp.dot(a, w2_ref[...], preferred_element_type=F32) + b2_ref[...]
    for s in range(n_s):
        o_ref[:, s * d:(s + 1) * d] = xs[s] + _rms(m[s * bt:(s + 1) * bt], g1_ref[...])
    for jj in range(CONV_BUF):
        so_ref[:, jj * d:(jj + 1) * d] = cat(n_s + jj)


def _conv_sample(x, state, mix_norm, cw, layer, j, *, n_s, bt):
    db = x.shape[0]
    d = mix_norm.shape[-1]
    return pl.pallas_call(
        functools.partial(_conv_s_body, n_s=n_s),
        grid=(db // bt,),
        in_specs=[
            pl.BlockSpec((bt, n_s * d), lambda i: (i, 0)),
            pl.BlockSpec((None, bt, CONV_BUF * d), lambda i: (j, i, 0)),
        ] + _conv_specs(layer, j, d),
        out_specs=[
            pl.BlockSpec((bt, n_s * d), lambda i: (i, 0)),
            pl.BlockSpec((bt, CONV_BUF * d), lambda i: (i, 0)),
        ],
        out_shape=[jax.ShapeDtypeStruct((db, n_s * d), F32), jax.ShapeDtypeStruct((db, CONV_BUF * d), F32)],
        compiler_params=_params("parallel"),
        name="conv_sample",
    )(x, state, mix_norm, mix_norm, *cw)


def _qkv_body(x_ref, g0_ref, w_ref, q_ref, k_ref, v_ref, *kv_bf16_refs, decode):
    rows, d = x_ref.shape
    h = _rms(x_ref[...], g0_ref[...]).astype(BF16)
    qkv = jnp.dot(h, w_ref[...], preferred_element_type=F32)
    q = qkv[:, :d] * (HEAD_DIM ** -0.5 * LOG2E)
    k = qkv[:, d:2 * d]
    v = qkv[:, 2 * d:]
    k_ref[...] = k.reshape(rows, N_HEADS, V_DIM)
    v_ref[...] = v.reshape(rows, N_HEADS, V_DIM)
    if decode:
        q_ref[...] = q.reshape(rows, N_HEADS, V_DIM)
    else:
        kb_ref, vb_ref = kv_bf16_refs
        q_ref[...] = q.astype(BF16)
        kb_ref[...] = k.astype(BF16)
        vb_ref[...] = v.astype(BF16)


def _qkv(x, mix_norm, wqkv, layer, j, *, tm, decode):
    n, d = x.shape
    row = pl.BlockSpec((tm, d), lambda i: (i, 0))
    per_head = pl.BlockSpec((tm, N_HEADS, V_DIM), lambda i: (i, 0, 0))
    head_shape = jax.ShapeDtypeStruct((n, N_HEADS, V_DIM), F32)
    flat_bf16 = jax.ShapeDtypeStruct((n, d), BF16)
    return pl.pallas_call(
        functools.partial(_qkv_body, decode=decode),
        grid=(pl.cdiv(n, tm),),
        in_specs=[row, _const_spec((None, 1, d), (layer * 2, 0, 0)), _const_spec((None, d, 3 * d), (j, 0, 0))],
        out_specs=[per_head] * 3 if decode else [row, per_head, per_head, row, row],
        out_shape=[head_shape] * 3 if decode else [flat_bf16, head_shape, head_shape, flat_bf16, flat_bf16],
        compiler_params=_params("parallel"),
        name="qkv_proj",
    )(x, mix_norm, wqkv)


def _bucket(n):
    nf = jnp.maximum(n, 1).astype(F32)
    large = MAX_EXACT + (jnp.log(nf / MAX_EXACT) / math.log(MAX_DISTANCE / MAX_EXACT)
                         * (N_BUCKETS - MAX_EXACT)).astype(jnp.int32)
    large = jnp.minimum(large, N_BUCKETS - 1)
    return jnp.where(n < MAX_EXACT, n, large)


def _lookup(bucket, entry):
    out = jnp.zeros(bucket.shape, F32)
    for b in range(N_BUCKETS):
        out = jnp.where(bucket == b, entry(b), out)
    return out


def _bias_p_body(table_ref, bp_ref, *, blk):
    head = pl.program_id(0)
    a = lax.broadcasted_iota(jnp.int32, (blk, blk), 0)
    b = lax.broadcasted_iota(jnp.int32, (blk, blk), 1)
    for sel in range(3):
        n = sel * blk + b - a
        vals = _lookup(_bucket(jnp.maximum(n, 0)), lambda bb: table_ref[bb * N_HEADS + head])
        bp_ref[0, sel] = jnp.where(n >= 0, vals * LOG2E, NEG_INF)


def _bias_s_body(tt_ref, bs_ref, *, n_s):
    rows = 2 * n_s * N_HEADS
    cols = PAGE_SIZE * N_HEADS
    tt = tt_ref[...]
    trow = jnp.broadcast_to(tt[None], (rows // N_HEADS, N_HEADS, N_BUCKETS)).reshape(rows, N_BUCKETS)

    def tile(width, dist):
        r = lax.broadcasted_iota(jnp.int32, (rows, width), 0)
        c = lax.broadcasted_iota(jnp.int32, (rows, width), 1)
        n = dist((r // N_HEADS) % n_s, c // N_HEADS)
        vals = _lookup(_bucket(jnp.maximum(n, 0)), lambda bb: trow[:, bb:bb + 1])
        return jnp.where((r % N_HEADS == c % N_HEADS) & (n >= 0) & (c < cols), vals * LOG2E, NEG_INF)

    bs_ref[:, 0:cols] = tile(cols, lambda qi, kk: 2 * PAGE_SIZE + qi - kk)
    bs_ref[:, cols:2 * cols] = tile(cols, lambda qi, kk: PAGE_SIZE + qi - kk)
    bs_ref[:, 2 * cols:2 * cols + PAGE_SIZE] = tile(
        PAGE_SIZE, lambda qi, kk: jnp.where(kk < n_s, qi - kk, -1))


def _bias_tiles(table, *, blk, n_s):
    far = _first_far_distance()
    assert blk + 1 >= far and PAGE_SIZE + 1 >= far and n_s * N_HEADS <= PAGE_SIZE
    bias_p = pl.pallas_call(
        functools.partial(_bias_p_body, blk=blk),
        grid=(N_HEADS,),
        in_specs=[pl.BlockSpec(memory_space=pltpu.SMEM)],
        out_specs=pl.BlockSpec((1, 3, blk, blk), lambda h: (h, 0, 0, 0)),
        out_shape=jax.ShapeDtypeStruct((N_HEADS, 3, blk, blk), F32),
        compiler_params=_params("parallel"),
        name="rel_bias_prompt",
    )(table.reshape(-1))
    width = 2 * PAGE_SIZE * N_HEADS + PAGE_SIZE
    bias_s = pl.pallas_call(
        functools.partial(_bias_s_body, n_s=n_s),
        out_shape=jax.ShapeDtypeStruct((2 * n_s * N_HEADS, width), F32),
        compiler_params=pltpu.CompilerParams(vmem_limit_bytes=VMEM_LIMIT),
        name="rel_bias_sample",
    )(table.T)
    return bias_p, bias_s


def _diff_lambda(lam_ref, lam_init):
    lp = lam_ref[...]
    s1 = jnp.sum(lp[0:1] * lp[1:2], axis=-1, keepdims=True)
    s2 = jnp.sum(lp[2:3] * lp[3:4], axis=-1, keepdims=True)
    return jnp.exp(s1) - jnp.exp(s2) + lam_init


_NT = (((1,), (1,)), ((), ()))


def _attn_p_body(q_ref, k_ref, v_ref, x_ref, bias_ref, lam_ref, sub_ref, wo_ref, g1_ref, o_ref,
                 qz_ref, vt_ref, m_ref, acc_ref, s_ref, a_ref, *, blk, n_full, tail, lam_init):
    i = pl.program_id(1)

    @pl.when(i == 0)
    def _():
        for hd in range(N_HEADS):
            c0 = hd * V_DIM

            ones_row = (lax.broadcasted_iota(jnp.int32, (SUM_ROWS, blk), 0) == 0).astype(BF16)

            def xpose(jb, carry):
                r0 = pl.multiple_of(jb * blk, blk)
                vt_ref[hd, jb, 0:V_DIM, :] = v_ref[0, pl.ds(r0, blk), c0:c0 + V_DIM].T
                vt_ref[hd, jb, V_DIM:V_DIM + SUM_ROWS, :] = ones_row
                return carry

            lax.fori_loop(0, n_full, xpose, 0)
            if tail:
                r0 = n_full * blk
                vt_ref[hd, n_full, 0:V_DIM, 0:tail] = (
                    v_ref[0, r0:r0 + tail, c0:c0 + V_DIM].astype(F32).T.astype(BF16))
                vt_ref[hd, n_full, V_DIM:V_DIM + SUM_ROWS, :] = ones_row

    def run(wq, n_loop, tail_keys):
        lane = lax.broadcasted_iota(jnp.int32, (wq, V_DIM), 1)
        for hd in range(N_HEADS):
            q = q_ref[0, 0:wq, hd * V_DIM:(hd + 1) * V_DIM]
            qz_ref[hd, 0:wq, :] = jnp.where(lane < HEAD_DIM, q, jnp.zeros_like(q))
            qz_ref[hd, wq:2 * wq, :] = jnp.where(lane >= HEAD_DIM, q, jnp.zeros_like(q))
        m_ref[:, :, 0:2 * wq] = jnp.full((N_HEADS, 1, 2 * wq), NEG_INF, F32)
        acc_ref[:, :, 0:2 * wq] = jnp.zeros((N_HEADS, V_DIM + SUM_ROWS, 2 * wq), F32)

        def all_heads(n, keys, vt, bias):
            def stage(hd):
                b2 = bias(hd)
                s_ref[hd % 2, 0:n, 0:2 * wq] = (
                    lax.dot_general(keys(hd), qz_ref[hd, 0:2 * wq, :], _NT, preferred_element_type=F32)
                    + jnp.concatenate([b2, b2], axis=1))

            stage(0)
            for hd in range(N_HEADS):
                if hd + 1 < N_HEADS:
                    stage(hd + 1)
                s = s_ref[hd % 2, 0:n, 0:2 * wq]
                m_old = m_ref[hd, :, 0:2 * wq]
                m_new = jnp.maximum(m_old, jnp.max(s, axis=0, keepdims=True))
                alpha = jnp.exp2(m_old - m_new)
                p = jnp.exp2(s - m_new)
                m_ref[hd, :, 0:2 * wq] = m_new
                acc_ref[hd, :, 0:2 * wq] = (alpha * acc_ref[hd, :, 0:2 * wq]
                                            + jnp.dot(vt(hd), p.astype(BF16), preferred_element_type=F32))

        def body(jb, carry):
            r0 = pl.multiple_of(jb * blk, blk)
            sel = jnp.minimum(i - jb, 2)
            all_heads(blk,
                      lambda hd: k_ref[0, pl.ds(r0, blk), hd * V_DIM:(hd + 1) * V_DIM],
                      lambda hd: vt_ref[hd, jb],
                      lambda hd: bias_ref[hd, sel, :, 0:wq])
            return carry

        lax.fori_loop(0, n_loop, body, 0)
        if tail_keys:
            r0 = n_full * blk
            all_heads(tail_keys,
                      lambda hd: k_ref[0, r0:r0 + tail_keys, hd * V_DIM:(hd + 1) * V_DIM],
                      lambda hd: vt_ref[hd, n_full, :, 0:tail_keys],
                      lambda hd: bias_ref[hd, 0, 0:tail_keys, 0:wq])

        lam = _diff_lambda(lam_ref, lam_init)
        for hd in range(N_HEADS):
            o_both = acc_ref[hd, 0:V_DIM, 0:2 * wq] * (1.0 / acc_ref[hd, V_DIM:V_DIM + 1, 0:2 * wq])
            o = o_both[:, 0:wq] - lam * o_both[:, wq:2 * wq]
            o = o * lax.rsqrt(jnp.mean(o * o, axis=0, keepdims=True) + RMS_EPS) * sub_ref[...] * (1.0 - lam_init)
            a_ref[0:wq, hd * V_DIM:(hd + 1) * V_DIM] = o.T.astype(BF16)
        mix = jnp.dot(a_ref[0:wq, :], wo_ref[...], preferred_element_type=F32)
        o_ref[0, 0:wq, :] = x_ref[0, 0:wq, :] + _rms(mix, g1_ref[...])

    @pl.when(i < n_full)
    def _():
        run(blk, i + 1, 0)

    if tail:
        @pl.when(i == n_full)
        def _():
            run(-(-tail // LANES) * LANES, n_full, tail)


def _attn_prompt(q, kb, vb, x, bias_p, lam_p, subln_col, wo, mix_norm, layer, j, *, blk, lam_init):
    b, t, d = q.shape
    n_full, tail = t // blk, t % blk
    assert tail % 16 == 0
    n_blocks = n_full + (1 if tail else 0)
    seq_blk = pl.BlockSpec((1, blk, d), lambda bi, qi: (bi, qi, 0))
    return pl.pallas_call(
        functools.partial(_attn_p_body, blk=blk, n_full=n_full, tail=tail, lam_init=lam_init),
        grid=(b, n_blocks),
        in_specs=[
            seq_blk,
            pl.BlockSpec((1, t, d), lambda bi, qi: (bi, 0, 0)),
            pl.BlockSpec((1, t, d), lambda bi, qi: (bi, 0, 0)),
            seq_blk,
            _const_spec((N_HEADS, 3, blk, blk), (0, 0, 0, 0)),
            _const_spec((None, 4, HEAD_DIM), (j, 0, 0)),
            _const_spec((None, V_DIM, 1), (j, 0, 0)),
            _const_spec((None, d, d), (j, 0, 0)),
            _const_spec((None, 1, d), (layer * 2 + 1, 0, 0)),
        ],
        out_specs=seq_blk,
        out_shape=jax.ShapeDtypeStruct((b, t, d), F32),
        scratch_shapes=[
            pltpu.VMEM((N_HEADS, 2 * blk, V_DIM), BF16),
            pltpu.VMEM((N_HEADS, n_blocks, V_DIM + SUM_ROWS, blk), BF16),
            pltpu.VMEM((N_HEADS, 1, 2 * blk), F32),
            pltpu.VMEM((N_HEADS, V_DIM + SUM_ROWS, 2 * blk), F32),
            pltpu.VMEM((2, blk, 2 * blk), F32),
            pltpu.VMEM((blk, d), BF16),
        ],
        compiler_params=_params("arbitrary", "arbitrary"),
        name="attn_prompt",
    )(q, kb, vb, x, bias_p, lam_p, subln_col, wo, mix_norm)


def _attn_s_body(pt_ref, q_ref, kn_ref, vn_ref, bias_ref, lam_ref, sub_ref, *rest, n_pages, n_s, lam_init):
    k_pages = rest[:n_pages]
    v_pages = rest[n_pages:2 * n_pages]
    o_ref, s_ref = rest[2 * n_pages:]
    half = n_s * N_HEADS
    cols = PAGE_SIZE * N_HEADS

    q = q_ref[...].reshape(half, V_DIM)
    lane = lax.broadcasted_iota(jnp.int32, (half, V_DIM), 1)
    q2 = jnp.concatenate([jnp.where(lane < HEAD_DIM, q, 0.0), jnp.where(lane >= HEAD_DIM, q, 0.0)], axis=0).astype(BF16)

    def scores(keys, bias):
        return lax.dot_general(q2, keys, _NT, preferred_element_type=F32) + bias

    pad = jnp.zeros((PAGE_SIZE - half, V_DIM), F32)
    kn = jnp.concatenate([kn_ref[...].reshape(half, V_DIM), pad], axis=0).astype(BF16)
    vn = jnp.concatenate([vn_ref[...].reshape(half, V_DIM), pad], axis=0).astype(BF16)
    s_new = scores(kn, bias_ref[:, 2 * cols:2 * cols + PAGE_SIZE])
    m = jnp.max(s_new, axis=-1, keepdims=True)
    for r_ in range(n_pages):
        near = r_ == n_pages - 1
        s = scores(k_pages[r_][...].reshape(cols, V_DIM).astype(BF16),
                   bias_ref[:, cols:2 * cols] if near else bias_ref[:, 0:cols])
        s_ref[r_] = s
        m = jnp.maximum(m, jnp.max(s, axis=-1, keepdims=True))

    p = jnp.exp2(s_new - m)
    l = jnp.sum(p, axis=-1, keepdims=True)
    acc = jnp.dot(p.astype(BF16), vn, preferred_element_type=F32)
    for r_ in range(n_pages):
        p = jnp.exp2(s_ref[r_] - m)
        l = l + jnp.sum(p, axis=-1, keepdims=True)
        acc = acc + jnp.dot(p.astype(BF16), v_pages[r_][...].reshape(cols, V_DIM).astype(BF16),
                            preferred_element_type=F32)

    lam = _diff_lambda(lam_ref, lam_init)
    o_all = acc * (1.0 / l)
    o = o_all[0:half] - lam * o_all[half:2 * half]
    o_ref[...] = (_rms(o, sub_ref[...]) * (1.0 - lam_init)).reshape(o_ref.shape)


def _attn_sample(q, kn, vn, cache_k, cache_v, page_table, bias_s, lam_p, subln, j, *, n_pages, lam_init):
    db, n_s = q.shape[:2]
    assert n_pages == page_table.shape[1]
    rows = 2 * n_s * N_HEADS
    new_kv = pl.BlockSpec((1, n_s, 1, N_HEADS, V_DIM), lambda b, pt: (b, 0, 0, 0, 0))
    q_spec = pl.BlockSpec((1, n_s, N_HEADS, V_DIM), lambda b, pt: (b, 0, 0, 0))

    def page_spec(r):
        return pl.BlockSpec((1, PAGE_SIZE, 1, N_HEADS, V_DIM), lambda b, pt: (pt[b, r], 0, j, 0, 0))

    grid_spec = pltpu.PrefetchScalarGridSpec(
        num_scalar_prefetch=1,
        grid=(db,),
        in_specs=[
            q_spec, new_kv, new_kv,
            pl.BlockSpec(bias_s.shape, lambda b, pt: (0, 0)),
            pl.BlockSpec((None, 4, HEAD_DIM), lambda b, pt: (j, 0, 0)),
            pl.BlockSpec((None, 1, V_DIM), lambda b, pt: (j, 0, 0)),
        ] + [page_spec(r) for r in range(n_pages)] * 2,
        out_specs=q_spec,
        scratch_shapes=[pltpu.VMEM((n_pages, rows, PAGE_SIZE * N_HEADS), F32)],
    )
    return pl.pallas_call(
        functools.partial(_attn_s_body, n_pages=n_pages, n_s=n_s, lam_init=lam_init),
        grid_spec=grid_spec,
        out_shape=jax.ShapeDtypeStruct((db, n_s, N_HEADS, V_DIM), F32),
        compiler_params=_params("parallel"),
        name="attn_sample",
    )(page_table, q, kn, vn, bias_s, lam_p, subln, *([cache_k] * n_pages), *([cache_v] * n_pages))


def _outproj_body(a_ref, x_ref, w_ref, g1_ref, o_ref):
    a = a_ref[...].reshape(x_ref.shape)
    m = jnp.dot(a.astype(BF16), w_ref[...], preferred_element_type=F32)
    o_ref[...] = x_ref[...] + _rms(m, g1_ref[...])


def _outproj(a, x, wo, mix_norm, layer, j, *, tm):
    n, d = x.shape
    row = pl.BlockSpec((tm, d), lambda i: (i, 0))
    return pl.pallas_call(
        _outproj_body,
        grid=(pl.cdiv(n, tm),),
        in_specs=[pl.BlockSpec((tm,) + a.shape[1:], lambda i: (i, 0, 0)), row,
                  _const_spec((None, d, d), (j, 0, 0)), _const_spec((None, 1, d), (layer * 2 + 1, 0, 0))],
        out_specs=row,
        out_shape=jax.ShapeDtypeStruct((n, d), F32),
        compiler_params=_params("parallel"),
        name="attn_outproj",
    )(a, x, wo, mix_norm)


def _pick_tile(n, cap, align=8):
    best = None
    for c in range(align, min(n, cap) + 1, align):
        if n % c == 0:
            best = c
    assert best is not None, (n, cap, align)
    return best


def kernel(x_prompt, x_sample, state_pool, state_conv, cache_k, cache_v, page_table, meta_tokens, rel_bias_table, ffn_norm, ffn_wg, ffn_wu, ffn_wd, mix_norm, pool_w, pool_scale, attn_wqkv, attn_wo, attn_lambda, attn_subln, conv_w1, conv_b1, conv_wdw, conv_bdw, conv_ln_g, conv_ln_b, conv_w2, conv_b2, final_norm):
    b, seq, d = x_prompt.shape
    db, n_s, _ = x_sample.shape
    depth = ffn_wg.shape[0]
    t = seq + N_META
    past = page_table.shape[1] * PAGE_SIZE

    wg, wu, wd = ffn_wg, ffn_wu, ffn_wd
    ffn_norm3 = ffn_norm.reshape(-1, 1, d)
    mix_norm3 = mix_norm.reshape(-1, 1, d)
    pool_w_b = pool_w.astype(BF16)
    pool_scale3 = pool_scale.reshape(-1, 1, d)
    wqkv_b, wo_b = attn_wqkv.astype(BF16), attn_wo.astype(BF16)
    subln3 = attn_subln.reshape(-1, 1, V_DIM)
    cw = (conv_w1.astype(BF16), conv_b1.reshape(-1, 1, 2 * d), conv_wdw, conv_bdw.reshape(-1, 1, d),
          conv_ln_g.reshape(-1, 1, d), conv_ln_b.reshape(-1, 1, d), conv_w2.astype(BF16), conv_b2.reshape(-1, 1, d))
    state_pool2 = state_pool.reshape(state_pool.shape[0], db, POOL_BUF * d)
    state_conv2 = state_conv.reshape(state_conv.shape[0], db, CONV_BUF * d)
    final_g = final_norm.reshape(1, d)

    meta = jnp.broadcast_to(meta_tokens[None].astype(x_prompt.dtype), (b, N_META, d))
    xp = jnp.concatenate([meta, x_prompt], axis=1).reshape(b * t, d)
    xs = x_sample.reshape(db * n_s, d)

    tm_p = _pick_tile(b * t, 768, 16)
    tm_s = min(512, db * n_s)
    ts_pool = _pick_tile(t, 1032)
    ts_conv = _pick_tile(t, 344)
    bt = _pick_tile(db, 32)

    bias_p = bias_s = None
    pool_p, pool_s, conv_p, conv_s = [], [], [], []
    k_p = v_p = k_s = v_s = None
    for i in range(depth):
        kind, j = i % N_MIXERS, i // N_MIXERS
        xp, xs = _ffn(xp, xs, ffn_norm3, final_g, wg, wu, wd, i, 0, tm=tm_p)
        if kind == 0:
            xp3, st = _pool_prompt(xp.reshape(b, t, d), mix_norm3, pool_w_b, pool_scale3, i, j, ts=ts_pool)
            xp = xp3.reshape(b * t, d)
            pool_p.append(st)
            xs2, st = _pool_sample(xs.reshape(db, n_s * d), state_pool2, mix_norm3, pool_w_b, pool_scale3, i, j,
                                   n_s=n_s, past=past, bt=bt)
            xs = xs2.reshape(db * n_s, d)
            pool_s.append(st.reshape(db, POOL_BUF, d))
        elif kind == 1:
            assert j == 0
            lam_init = _lambda_init(i)
            if bias_p is None:
                bias_p, bias_s = _bias_tiles(rel_bias_table, blk=ATTN_BLOCK, n_s=n_s)
            q, k_p, v_p, kb, vb = _qkv(xp, mix_norm3, wqkv_b, i, j, tm=tm_p, decode=False)
            xp = _attn_prompt(q.reshape(b, t, d), kb.reshape(b, t, d), vb.reshape(b, t, d), xp.reshape(b, t, d),
                              bias_p, attn_lambda, attn_subln.reshape(-1, V_DIM, 1), wo_b, mix_norm3, i, j,
                              blk=ATTN_BLOCK, lam_init=lam_init).reshape(b * t, d)
            q, k_s, v_s = _qkv(xs, mix_norm3, wqkv_b, i, j, tm=tm_s, decode=True)
            k_s = k_s.reshape(db, n_s, 1, N_HEADS, V_DIM)
            v_s = v_s.reshape(db, n_s, 1, N_HEADS, V_DIM)
            a = _attn_sample(q.reshape(db, n_s, N_HEADS, V_DIM), k_s, v_s, cache_k, cache_v,
                             page_table, bias_s, attn_lambda, subln3, j, n_pages=page_table.shape[1], lam_init=lam_init)
            xs = _outproj(a.reshape(db * n_s, N_HEADS, V_DIM), xs, wo_b, mix_norm3, i, j, tm=tm_s)
        else:
            xp3, st = _conv_prompt(xp.reshape(b, t, d), mix_norm3, cw, i, j, ts=ts_conv)
            xp = xp3.reshape(b * t, d)
            conv_p.append(st)
            xs2, st = _conv_sample(xs.reshape(db, n_s * d), state_conv2, mix_norm3, cw, i, j, n_s=n_s, bt=bt)
            xs = xs2.reshape(db * n_s, d)
            conv_s.append(st.reshape(db, CONV_BUF, d))
        xp, xs = _ffn(xp, xs, ffn_norm3, final_g, wg, wu, wd, i, 1, tm=tm_p,
                      final_seq=(t, N_META) if i == depth - 1 else None)

    kv_shape_p = (b, t, 1, N_HEADS, V_DIM)
    return (xp.reshape(b, seq, d), xs.reshape(db, n_s, d), jnp.stack(pool_p, axis=0), jnp.stack(pool_s, axis=0),
            jnp.stack(conv_p, axis=0), jnp.stack(conv_s, axis=0),
            k_p.reshape(kv_shape_p), v_p.reshape(kv_shape_p), k_s, v_s)
```

```python
import functools
import math

import jax
import jax.numpy as jnp
from jax import lax
from jax.experimental import pallas as pl
from jax.experimental.pallas import tpu as pltpu

F32 = jnp.float32
BF16 = jnp.bfloat16

N_MIXERS = 3
N_META = 16
N_HEADS = 8
HEAD_DIM = 64
V_DIM = 2 * HEAD_DIM
N_BUCKETS = 32
MAX_EXACT = N_BUCKETS // 2
MAX_DISTANCE = 128
POOL_WINDOWS = (2, 4, 8, 16)
POOL_BUF = max(POOL_WINDOWS) - 1
CONV_WIDTH = 31
CONV_BUF = CONV_WIDTH - 1
PAGE_SIZE = 128
RMS_EPS = 1e-6
LN_EPS = 1e-5
NEG_INF = -1e30
LOG2E = math.log2(math.e)
SUBLANES = 8
LANES = 128
BF16_ROWS = 16
MXU_TILE = 256
SUM_ROWS = 16
ATTN_BLOCK = 256
POOL_HALO = 16
CONV_HALO = 32
VMEM_LIMIT = 56 * 1024 * 1024
WEIGHT_CHUNK_BYTES = 3 * 256 * 1024
WEIGHT_SLOTS = 4


def _lambda_init(layer_idx):
    return 0.8 - 0.6 * math.exp(-0.3 * layer_idx)


def _first_far_distance():
    n = MAX_EXACT
    while MAX_EXACT + int(math.log(n / MAX_EXACT) / math.log(MAX_DISTANCE / MAX_EXACT) * (N_BUCKETS - MAX_EXACT)) < N_BUCKETS - 1:
        n += 1
    return n


def _params(*sem):
    return pltpu.CompilerParams(dimension_semantics=sem, vmem_limit_bytes=VMEM_LIMIT)


def _rms(x, g):
    return x * lax.rsqrt(jnp.mean(x * x, axis=-1, keepdims=True) + RMS_EPS) * g


def _silu(x):
    return x * jax.nn.sigmoid(x)


def _const_spec(shape, index, single=False):
    if single:
        return pl.BlockSpec(shape, lambda *_: index, pipeline_mode=pl.Buffered(1))
    return pl.BlockSpec(shape, lambda *_: index)


def _ffn_chunks(ff, n_chunks):
    tiles = -(-ff // MXU_TILE)
    edges = [min(ff, MXU_TILE * (-(-tiles * c // n_chunks))) for c in range(n_chunks + 1)]
    return [(lo, hi - lo) for lo, hi in zip(edges[:-1], edges[1:]) if hi > lo]


def _ffn_body(xp_ref, xs_ref, n0_ref, n1_ref, gf_ref, wg_hbm, wu_hbm, wd_hbm, op_ref, os_ref,
              wg_ref, wu_ref, wd_ref, stage_in, stage_out, sem, *, n_chunks, n_p, final, layer, f):
    i = pl.program_id(0)

    def fetch(w_hbm, stage, dst):
        slots, rows = stage.shape[0], stage.shape[1]
        n = dst.shape[0] // rows

        def copy(c):
            return pltpu.make_async_copy(w_hbm.at[layer, f, pl.ds(c * rows, rows), :], stage.at[c % slots],
                                         sem.at[c % slots])

        for c in range(min(slots - 1, n)):
            copy(c).start()
        for c in range(n):
            if c + slots - 1 < n:
                copy(c + slots - 1).start()
            copy(c).wait()
            dst[pl.ds(c * rows, rows), :] = stage[c % slots].astype(BF16)

    @pl.when(i == 0)
    def _():
        fetch(wg_hbm, stage_in, wg_ref)
        fetch(wu_hbm, stage_in, wu_ref)
        fetch(wd_hbm, stage_out, wd_ref)

    def apply(x_ref, o_ref):
        rows = x_ref.shape[0]
        first = -(-rows // (2 * BF16_ROWS)) * BF16_ROWS
        for lo_row, n_rows in ((0, first), (first, rows - first))[:2 if rows > first else 1]:
            rs = pl.ds(lo_row, n_rows)
            x = x_ref[rs, :]
            h = _rms(x, n0_ref[...]).astype(BF16)
            acc = jnp.zeros(x.shape, F32)
            for lo, width in _ffn_chunks(wg_ref.shape[1], n_chunks):
                sl = pl.ds(lo, width)
                g = jnp.dot(h, wg_ref[:, sl], preferred_element_type=F32)
                u = jnp.dot(h, wu_ref[:, sl], preferred_element_type=F32)
                a = (_silu(g) * u).astype(BF16)
                acc = acc + jnp.dot(a, wd_ref[sl, :], preferred_element_type=F32)
            y = x + 0.5 * _rms(acc, n1_ref[...])
            o_ref[rs, :] = _rms(y, gf_ref[...]) if final else y

    @pl.when(i == 0)
    def _():
        os_ref[...] = jnp.zeros(os_ref.shape, F32)

    @pl.when(i < n_p)
    def _():
        apply(xp_ref, op_ref)

    @pl.when(i >= n_p)
    def _():
        apply(xs_ref, os_ref)


def _ffn(xp, xs, norms, final_g, wg, wu, wd, layer, f, *, tm, n_chunks=2, final_seq=None):
    (n_rows_p, d), n_rows_s = xp.shape, xs.shape[0]
    ff = wg.shape[-1]
    nidx = (layer * 2 + f) * 2
    tm_s = _pick_tile(n_rows_s, tm, BF16_ROWS)
    if final_seq is None:
        n_p, out_rows_p = pl.cdiv(n_rows_p, tm), n_rows_p
        p_in = p_out = pl.BlockSpec((tm, d), lambda i: (jnp.minimum(i, n_p - 1), 0))
    else:
        t, skip = final_seq
        assert skip % BF16_ROWS == 0 and t % BF16_ROWS == 0
        tm = _pick_tile(t - skip, tm, BF16_ROWS)
        per_seq = (t - skip) // tm
        n_p, out_rows_p = (n_rows_p // t) * per_seq, (n_rows_p // t) * (t - skip)

        def in_rows(i):
            ip = jnp.minimum(i, n_p - 1)
            return pl.multiple_of((ip // per_seq) * t + skip + (ip % per_seq) * tm, BF16_ROWS), 0

        p_in = pl.BlockSpec((pl.Element(tm), pl.Element(d)), in_rows)
        p_out = pl.BlockSpec((tm, d), lambda i: (jnp.minimum(i, n_p - 1), 0))
    n_s = pl.cdiv(n_rows_s, tm_s)
    s_spec = pl.BlockSpec((tm_s, d), lambda i: (jnp.maximum(i - n_p, 0), 0))
    in_rows, out_rows = _pick_tile(d, WEIGHT_CHUNK_BYTES // (4 * ff)), _pick_tile(ff, WEIGHT_CHUNK_BYTES // (4 * d))
    hbm = pl.BlockSpec(memory_space=pl.ANY)
    return pl.pallas_call(
        functools.partial(_ffn_body, n_chunks=n_chunks, n_p=n_p, final=final_seq is not None, layer=layer, f=f),
        grid=(n_p + n_s,),
        in_specs=[
            p_in, s_spec,
            _const_spec((None, 1, d), (nidx, 0, 0)),
            _const_spec((None, 1, d), (nidx + 1, 0, 0)),
            _const_spec((1, d), (0, 0)),
            hbm, hbm, hbm,
        ],
        out_specs=[p_out, s_spec],
        out_shape=[jax.ShapeDtypeStruct((out_rows_p, d), F32), jax.ShapeDtypeStruct((n_rows_s, d), F32)],
        scratch_shapes=[
            pltpu.VMEM((d, ff), BF16), pltpu.VMEM((d, ff), BF16), pltpu.VMEM((ff, d), BF16),
            pltpu.VMEM((WEIGHT_SLOTS, in_rows, ff), F32), pltpu.VMEM((WEIGHT_SLOTS, out_rows, d), F32),
            pltpu.SemaphoreType.DMA((WEIGHT_SLOTS,)),
        ],
        compiler_params=_params("arbitrary"),
        name="ffn",
    )(xp, xs, norms, norms, final_g, wg, wu, wd)


def _pool_p_body(x_ref, g0_ref, g1_ref, w_ref, sc_ref, o_ref, st_ref, *lvl_refs, ts, n_t):
    t = pl.program_id(1)
    d = x_ref.shape[-1]
    n_g = len(POOL_WINDOWS)
    gd = d // n_g
    x = x_ref[0]
    h = _rms(x, g0_ref[...])
    cat_ref = lvl_refs[0]
    top = SUBLANES + POOL_HALO
    rows = POOL_HALO + ts

    @pl.when(t == 0)
    def _():
        cat_ref[0:top, :] = jnp.zeros((top, d), F32)
        for ref in lvl_refs[1:]:
            ref[0:SUBLANES, :] = jnp.zeros((SUBLANES, ref.shape[1]), F32)

    cat_ref[top:top + ts, :] = h
    for k in range(1, len(lvl_refs)):
        prev, cur = lvl_refs[k - 1], lvl_refs[k]
        lo = prev.shape[1] - cur.shape[1]
        cur[SUBLANES:SUBLANES + rows, :] = (prev[SUBLANES:SUBLANES + rows, lo:]
                                            + prev[SUBLANES - 2 ** (k - 1):SUBLANES - 2 ** (k - 1) + rows, lo:])
    pos = t * ts + lax.broadcasted_iota(jnp.int32, (ts, 1), 0)
    outs = []
    for g, w in enumerate(POOL_WINDOWS):
        c0 = g * gd
        k = w.bit_length() - 1
        if k < len(lvl_refs):
            ref = lvl_refs[k]
            l0 = c0 - (d - ref.shape[1])
            acc = ref[top:top + ts, l0:l0 + gd]
        else:
            ref = lvl_refs[k - 1]
            l0 = c0 - (d - ref.shape[1])
            acc = ref[top:top + ts, l0:l0 + gd] + ref[top - w // 2:top - w // 2 + ts, l0:l0 + gd]
        inv_cnt = 1.0 / jnp.minimum(w, pos + 1).astype(F32)
        pooled = (acc * inv_cnt - h[:, c0:c0 + gd]).astype(BF16)
        outs.append(jnp.dot(pooled, w_ref[g], preferred_element_type=F32))
    m = jnp.concatenate(outs, axis=-1) * sc_ref[...]
    o_ref[0] = x + _rms(m, g1_ref[...])

    @pl.when(t == n_t - 1)
    def _():
        st_ref[0] = cat_ref[top + ts - POOL_BUF:top + ts, :]

    cat_ref[SUBLANES:top, :] = cat_ref[SUBLANES + ts:top + ts, :]


def _pool_prompt(x, mix_norm, pool_w, pool_scale, layer, j, *, ts):
    b, t, d = x.shape
    n_t = t // ts
    assert n_t * ts == t and ts % 8 == 0 and ts >= POOL_HALO
    g = len(POOL_WINDOWS)
    assert all(w == 2 ** (i + 1) for i, w in enumerate(POOL_WINDOWS)) and POOL_WINDOWS[-1] // 2 == SUBLANES
    buf_rows = SUBLANES + POOL_HALO + ts
    level_lanes = [d] + [d - i * (d // g) for i in range(g - 1)]
    return pl.pallas_call(
        functools.partial(_pool_p_body, ts=ts, n_t=n_t),
        grid=(b, n_t),
        in_specs=[
            pl.BlockSpec((1, ts, d), lambda bi, ti: (bi, ti, 0)),
            _const_spec((None, 1, d), (layer * 2, 0, 0)),
            _const_spec((None, 1, d), (layer * 2 + 1, 0, 0)),
            _const_spec((None, g, d // g, d // g), (j, 0, 0, 0)),
            _const_spec((None, 1, d), (j, 0, 0)),
        ],
        out_specs=[
            pl.BlockSpec((1, ts, d), lambda bi, ti: (bi, ti, 0)),
            pl.BlockSpec((1, POOL_BUF, d), lambda bi, ti: (bi, 0, 0)),
        ],
        out_shape=[jax.ShapeDtypeStruct((b, t, d), F32), jax.ShapeDtypeStruct((b, POOL_BUF, d), F32)],
        scratch_shapes=[pltpu.VMEM((buf_rows, lanes), F32) for lanes in level_lanes],
        compiler_params=_params("parallel", "arbitrary"),
        name="pool_prompt",
    )(x, mix_norm, mix_norm, pool_w, pool_scale)


def _pool_s_body(x_ref, st_ref, g0_ref, g1_ref, w_ref, sc_ref, o_ref, so_ref, *, n_s, past):
    d = g0_ref.shape[-1]
    gd = d // len(POOL_WINDOWS)
    bt = x_ref.shape[0]
    xs = [x_ref[:, s * d:(s + 1) * d] for s in range(n_s)]
    hs = [_rms(x, g0_ref[...]) for x in xs]

    def cat(idx, c0, width):
        if idx < POOL_BUF:
            return st_ref[:, idx * d + c0:idx * d + c0 + width]
        return hs[idx - POOL_BUF][:, c0:c0 + width]

    outs = []
    for g, w in enumerate(POOL_WINDOWS):
        c0 = g * gd
        rows = []
        for s in range(n_s):
            acc = cat(POOL_BUF + s, c0, gd)
            for jj in range(1, w):
                acc = acc + cat(POOL_BUF + s - jj, c0, gd)
            rows.append(acc * (1.0 / min(w, past + s + 1)) - hs[s][:, c0:c0 + gd])
        pooled = jnp.concatenate(rows, axis=0).astype(BF16)
        outs.append(jnp.dot(pooled, w_ref[g], preferred_element_type=F32))
    for s in range(n_s):
        m = jnp.concatenate([o[s * bt:(s + 1) * bt] for o in outs], axis=-1) * sc_ref[...]
        o_ref[:, s * d:(s + 1) * d] = xs[s] + _rms(m, g1_ref[...])
    for jj in range(POOL_BUF):
        so_ref[:, jj * d:(jj + 1) * d] = cat(n_s + jj, 0, d)


def _pool_sample(x, state, mix_norm, pool_w, pool_scale, layer, j, *, n_s, past, bt):
    db = x.shape[0]
    d = mix_norm.shape[-1]
    g = len(POOL_WINDOWS)
    return pl.pallas_call(
        functools.partial(_pool_s_body, n_s=n_s, past=past),
        grid=(db // bt,),
        in_specs=[
            pl.BlockSpec((bt, n_s * d), lambda i: (i, 0)),
            pl.BlockSpec((None, bt, POOL_BUF * d), lambda i: (j, i, 0)),
            _const_spec((None, 1, d), (layer * 2, 0, 0)),
            _const_spec((None, 1, d), (layer * 2 + 1, 0, 0)),
            _const_spec((None, g, d // g, d // g), (j, 0, 0, 0)),
            _const_spec((None, 1, d), (j, 0, 0)),
        ],
        out_specs=[
            pl.BlockSpec((bt, n_s * d), lambda i: (i, 0)),
            pl.BlockSpec((bt, POOL_BUF * d), lambda i: (i, 0)),
        ],
        out_shape=[jax.ShapeDtypeStruct((db, n_s * d), F32), jax.ShapeDtypeStruct((db, POOL_BUF * d), F32)],
        compiler_params=_params("parallel"),
        name="pool_sample",
    )(x, state, mix_norm, mix_norm, pool_w, pool_scale)


def _layer_norm(y, g, b):
    mu = jnp.mean(y, axis=-1, keepdims=True)
    yc = y - mu
    var = jnp.mean(yc * yc, axis=-1, keepdims=True)
    return yc * lax.rsqrt(var + LN_EPS) * g + b


def _conv_p_body(x_ref, g0_ref, g1_ref, w1_ref, b1_ref, wdw_ref, bdw_ref, lng_ref, lnb_ref, w2_ref, b2_ref,
                 o_ref, st_ref, cat_ref, y_ref, *, ts, n_t):
    t = pl.program_id(1)
    d = x_ref.shape[-1]
    first = -(-ts // (2 * BF16_ROWS)) * BF16_ROWS
    parts = ((0, first), (first, ts - first))[:2 if ts > first else 1]

    @pl.when(t == 0)
    def _():
        cat_ref[0:CONV_HALO, :] = jnp.zeros((CONV_HALO, d), F32)

    for lo, n in parts:
        h = _rms(x_ref[0, lo:lo + n, :], g0_ref[...]).astype(BF16)
        glu = jnp.dot(h, w1_ref[...], preferred_element_type=F32) + b1_ref[...]
        cat_ref[CONV_HALO + lo:CONV_HALO + lo + n, :] = glu[:, :d] * jax.nn.sigmoid(glu[:, d:])

    off = CONV_HALO - CONV_BUF
    for lo, n in parts:
        for c0 in range(0, d, LANES):
            y = None
            for r in range(SUBLANES):
                zrows = n + (SUBLANES if r else 0)
                z = None
                for a in range((CONV_WIDTH + off) // SUBLANES + 1):
                    k = SUBLANES * a + r - off
                    if 0 <= k < CONV_WIDTH:
                        r0 = lo + SUBLANES * a
                        term = cat_ref[r0:r0 + zrows, c0:c0 + LANES] * wdw_ref[k:k + 1, c0:c0 + LANES]
                        z = term if z is None else z + term
                zs = z[r:r + n]
                y = zs if y is None else y + zs
            y_ref[lo:lo + n, c0:c0 + LANES] = y + bdw_ref[:, c0:c0 + LANES]
        a = _silu(_layer_norm(y_ref[lo:lo + n, :], lng_ref[...], lnb_ref[...])).astype(BF16)
        m = jnp.dot(a, w2_ref[...], preferred_element_type=F32) + b2_ref[...]
        o_ref[0, lo:lo + n, :] = x_ref[0, lo:lo + n, :] + _rms(m, g1_ref[...])

    @pl.when(t == n_t - 1)
    def _():
        st_ref[0] = cat_ref[CONV_HALO + ts - CONV_BUF:CONV_HALO + ts, :]

    cat_ref[0:CONV_HALO, :] = cat_ref[ts:ts + CONV_HALO, :]


def _conv_specs(layer, j, d):
    return [
        _const_spec((None, 1, d), (layer * 2, 0, 0)),
        _const_spec((None, 1, d), (layer * 2 + 1, 0, 0)),
        _const_spec((None, d, 2 * d), (j, 0, 0)),
        _const_spec((None, 1, 2 * d), (j, 0, 0)),
        _const_spec((None, CONV_WIDTH, d), (j, 0, 0)),
        _const_spec((None, 1, d), (j, 0, 0)),
        _const_spec((None, 1, d), (j, 0, 0)),
        _const_spec((None, 1, d), (j, 0, 0)),
        _const_spec((None, d, d), (j, 0, 0)),
        _const_spec((None, 1, d), (j, 0, 0)),
    ]


def _conv_prompt(x, mix_norm, cw, layer, j, *, ts):
    b, t, d = x.shape
    n_t = t // ts
    assert n_t * ts == t and ts % 8 == 0 and ts >= CONV_HALO
    return pl.pallas_call(
        functools.partial(_conv_p_body, ts=ts, n_t=n_t),
        grid=(b, n_t),
        in_specs=[pl.BlockSpec((1, ts, d), lambda bi, ti: (bi, ti, 0))] + _conv_specs(layer, j, d),
        out_specs=[
            pl.BlockSpec((1, ts, d), lambda bi, ti: (bi, ti, 0)),
            pl.BlockSpec((1, CONV_BUF, d), lambda bi, ti: (bi, 0, 0)),
        ],
        out_shape=[jax.ShapeDtypeStruct((b, t, d), F32), jax.ShapeDtypeStruct((b, CONV_BUF, d), F32)],
        scratch_shapes=[pltpu.VMEM((CONV_HALO + ts, d), F32), pltpu.VMEM((ts, d), F32)],
        compiler_params=_params("parallel", "arbitrary"),
        name="conv_prompt",
    )(x, mix_norm, mix_norm, *cw)


def _conv_s_body(x_ref, st_ref, g0_ref, g1_ref, w1_ref, b1_ref, wdw_ref, bdw_ref, lng_ref, lnb_ref, w2_ref, b2_ref,
                 o_ref, so_ref, *, n_s):
    d = g0_ref.shape[-1]
    bt = x_ref.shape[0]
    xs = [x_ref[:, s * d:(s + 1) * d] for s in range(n_s)]
    h = jnp.concatenate([_rms(x, g0_ref[...]) for x in xs], axis=0).astype(BF16)
    z = jnp.dot(h, w1_ref[...], preferred_element_type=F32) + b1_ref[...]
    u = z[:, :d] * jax.nn.sigmoid(z[:, d:])
    us = [u[s * bt:(s + 1) * bt] for s in range(n_s)]

    def cat(idx):
        if idx < CONV_BUF:
            return st_ref[:, idx * d:(idx + 1) * d]
        return us[idx - CONV_BUF]

    acts = []
    for s in range(n_s):
        y = cat(s) * wdw_ref[0:1, :] + bdw_ref[...]
        for k in range(1, CONV_WIDTH):
            y = y + cat(s + k) * wdw_ref[k:k + 1, :]
        acts.append(_silu(_layer_norm(y, lng_ref[...], lnb_ref[...])))
    a = jnp.concatenate(acts, axis=0).astype(BF16)
    m = jnp.dot(a, w2_ref[...], preferred_element_type=F32) + b2_ref[...]
    for s in range(n_s):
        o_ref[:, s * d:(s + 1) * d] = xs[s] + _rms(m[s * bt:(s + 1) * bt], g1_ref[...])
    for jj in range(CONV_BUF):
        so_ref[:, jj * d:(jj + 1) * d] = cat(n_s + jj)


def _conv_sample(x, state, mix_norm, cw, layer, j, *, n_s, bt):
    db = x.shape[0]
    d = mix_norm.shape[-1]
    return pl.pallas_call(
        functools.partial(_conv_s_body, n_s=n_s),
        grid=(db // bt,),
        in_specs=[
            pl.BlockSpec((bt, n_s * d), lambda i: (i, 0)),
            pl.BlockSpec((None, bt, CONV_BUF * d), lambda i: (j, i, 0)),
        ] + _conv_specs(layer, j, d),
        out_specs=[
            pl.BlockSpec((bt, n_s * d), lambda i: (i, 0)),
            pl.BlockSpec((bt, CONV_BUF * d), lambda i: (i, 0)),
        ],
        out_shape=[jax.ShapeDtypeStruct((db, n_s * d), F32), jax.ShapeDtypeStruct((db, CONV_BUF * d), F32)],
        compiler_params=_params("parallel"),
        name="conv_sample",
    )(x, state, mix_norm, mix_norm, *cw)


def _qkv_body(x_ref, g0_ref, w_ref, q_ref, k_ref, v_ref, *kv_bf16_refs, decode):
    rows, d = x_ref.shape
    h = _rms(x_ref[...], g0_ref[...]).astype(BF16)
    qkv = jnp.dot(h, w_ref[...], preferred_element_type=F32)
    q = qkv[:, :d] * (HEAD_DIM ** -0.5 * LOG2E)
    k = qkv[:, d:2 * d]
    v = qkv[:, 2 * d:]
    k_ref[...] = k.reshape(rows, N_HEADS, V_DIM)
    v_ref[...] = v.reshape(rows, N_HEADS, V_DIM)
    if decode:
        q_ref[...] = q.reshape(rows, N_HEADS, V_DIM)
    else:
        kb_ref, vb_ref = kv_bf16_refs
        q_ref[...] = q.astype(BF16)
        kb_ref[...] = k.astype(BF16)
        vb_ref[...] = v.astype(BF16)


def _qkv(x, mix_norm, wqkv, layer, j, *, tm, decode):
    n, d = x.shape
    row = pl.BlockSpec((tm, d), lambda i: (i, 0))
    per_head = pl.BlockSpec((tm, N_HEADS, V_DIM), lambda i: (i, 0, 0))
    head_shape = jax.ShapeDtypeStruct((n, N_HEADS, V_DIM), F32)
    flat_bf16 = jax.ShapeDtypeStruct((n, d), BF16)
    return pl.pallas_call(
        functools.partial(_qkv_body, decode=decode),
        grid=(pl.cdiv(n, tm),),
        in_specs=[row, _const_spec((None, 1, d), (layer * 2, 0, 0)), _const_spec((None, d, 3 * d), (j, 0, 0))],
        out_specs=[per_head] * 3 if decode else [row, per_head, per_head, row, row],
        out_shape=[head_shape] * 3 if decode else [flat_bf16, head_shape, head_shape, flat_bf16, flat_bf16],
        compiler_params=_params("parallel"),
        name="qkv_proj",
    )(x, mix_norm, wqkv)


def _bucket(n):
    nf = jnp.maximum(n, 1).astype(F32)
    large = MAX_EXACT + (jnp.log(nf / MAX_EXACT) / math.log(MAX_DISTANCE / MAX_EXACT)
                         * (N_BUCKETS - MAX_EXACT)).astype(jnp.int32)
    large = jnp.minimum(large, N_BUCKETS - 1)
    return jnp.where(n < MAX_EXACT, n, large)


def _lookup(bucket, entry):
    out = jnp.zeros(bucket.shape, F32)
    for b in range(N_BUCKETS):
        out = jnp.where(bucket == b, entry(b), out)
    return out


def _bias_p_body(table_ref, bp_ref, *, blk):
    head = pl.program_id(0)
    a = lax.broadcasted_iota(jnp.int32, (blk, blk), 0)
    b = lax.broadcasted_iota(jnp.int32, (blk, blk), 1)
    for sel in range(3):
        n = sel * blk + b - a
        vals = _lookup(_bucket(jnp.maximum(n, 0)), lambda bb: table_ref[bb * N_HEADS + head])
        bp_ref[0, sel] = jnp.where(n >= 0, vals * LOG2E, NEG_INF)


def _bias_s_body(tt_ref, bs_ref, *, n_s):
    rows = 2 * n_s * N_HEADS
    cols = PAGE_SIZE * N_HEADS
    tt = tt_ref[...]
    trow = jnp.broadcast_to(tt[None], (rows // N_HEADS, N_HEADS, N_BUCKETS)).reshape(rows, N_BUCKETS)

    def tile(width, dist):
        r = lax.broadcasted_iota(jnp.int32, (rows, width), 0)
        c = lax.broadcasted_iota(jnp.int32, (rows, width), 1)
        n = dist((r // N_HEADS) % n_s, c // N_HEADS)
        vals = _lookup(_bucket(jnp.maximum(n, 0)), lambda bb: trow[:, bb:bb + 1])
        return jnp.where((r % N_HEADS == c % N_HEADS) & (n >= 0) & (c < cols), vals * LOG2E, NEG_INF)

    bs_ref[:, 0:cols] = tile(cols, lambda qi, kk: 2 * PAGE_SIZE + qi - kk)
    bs_ref[:, cols:2 * cols] = tile(cols, lambda qi, kk: PAGE_SIZE + qi - kk)
    bs_ref[:, 2 * cols:2 * cols + PAGE_SIZE] = tile(
        PAGE_SIZE, lambda qi, kk: jnp.where(kk < n_s, qi - kk, -1))


def _bias_tiles(table, *, blk, n_s):
    far = _first_far_distance()
    assert blk + 1 >= far and PAGE_SIZE + 1 >= far and n_s * N_HEADS <= PAGE_SIZE
    bias_p = pl.pallas_call(
        functools.partial(_bias_p_body, blk=blk),
        grid=(N_HEADS,),
        in_specs=[pl.BlockSpec(memory_space=pltpu.SMEM)],
        out_specs=pl.BlockSpec((1, 3, blk, blk), lambda h: (h, 0, 0, 0)),
        out_shape=jax.ShapeDtypeStruct((N_HEADS, 3, blk, blk), F32),
        compiler_params=_params("parallel"),
        name="rel_bias_prompt",
    )(table.reshape(-1))
    width = 2 * PAGE_SIZE * N_HEADS + PAGE_SIZE
    bias_s = pl.pallas_call(
        functools.partial(_bias_s_body, n_s=n_s),
        out_shape=jax.ShapeDtypeStruct((2 * n_s * N_HEADS, width), F32),
        compiler_params=pltpu.CompilerParams(vmem_limit_bytes=VMEM_LIMIT),
        name="rel_bias_sample",
    )(table.T)
    return bias_p, bias_s


def _diff_lambda(lam_ref, lam_init):
    lp = lam_ref[...]
    s1 = jnp.sum(lp[0:1] * lp[1:2], axis=-1, keepdims=True)
    s2 = jnp.sum(lp[2:3] * lp[3:4], axis=-1, keepdims=True)
    return jnp.exp(s1) - jnp.exp(s2) + lam_init


_NT = (((1,), (1,)), ((), ()))


def _attn_p_body(q_ref, k_ref, v_ref, x_ref, bias_ref, lam_ref, sub_ref, wo_ref, g1_ref, o_ref,
                 qz_ref, vt_ref, m_ref, acc_ref, s_ref, a_ref, *, blk, n_full, tail, lam_init):
    i = pl.program_id(1)

    @pl.when(i == 0)
    def _():
        for hd in range(N_HEADS):
            c0 = hd * V_DIM

            ones_row = (lax.broadcasted_iota(jnp.int32, (SUM_ROWS, blk), 0) == 0).astype(BF16)

            def xpose(jb, carry):
                r0 = pl.multiple_of(jb * blk, blk)
                vt_ref[hd, jb, 0:V_DIM, :] = v_ref[0, pl.ds(r0, blk), c0:c0 + V_DIM].T
                vt_ref[hd, jb, V_DIM:V_DIM + SUM_ROWS, :] = ones_row
                return carry

            lax.fori_loop(0, n_full, xpose, 0)
            if tail:
                r0 = n_full * blk
                vt_ref[hd, n_full, 0:V_DIM, 0:tail] = (
                    v_ref[0, r0:r0 + tail, c0:c0 + V_DIM].astype(F32).T.astype(BF16))
                vt_ref[hd, n_full, V_DIM:V_DIM + SUM_ROWS, :] = ones_row

    def run(wq, n_loop, tail_keys):
        lane = lax.broadcasted_iota(jnp.int32, (wq, V_DIM), 1)
        for hd in range(N_HEADS):
            q = q_ref[0, 0:wq, hd * V_DIM:(hd + 1) * V_DIM]
            qz_ref[hd, 0:wq, :] = jnp.where(lane < HEAD_DIM, q, jnp.zeros_like(q))
            qz_ref[hd, wq:2 * wq, :] = jnp.where(lane >= HEAD_DIM, q, jnp.zeros_like(q))
        m_ref[:, :, 0:2 * wq] = jnp.full((N_HEADS, 1, 2 * wq), NEG_INF, F32)
        acc_ref[:, :, 0:2 * wq] = jnp.zeros((N_HEADS, V_DIM + SUM_ROWS, 2 * wq), F32)

        def all_heads(n, keys, vt, bias):
            def stage(hd):
                b2 = bias(hd)
                s_ref[hd % 2, 0:n, 0:2 * wq] = (
                    lax.dot_general(keys(hd), qz_ref[hd, 0:2 * wq, :], _NT, preferred_element_type=F32)
                    + jnp.concatenate([b2, b2], axis=1))

            stage(0)
            for hd in range(N_HEADS):
                if hd + 1 < N_HEADS:
                    stage(hd + 1)
                s = s_ref[hd % 2, 0:n, 0:2 * wq]
                m_old = m_ref[hd, :, 0:2 * wq]
                m_new = jnp.maximum(m_old, jnp.max(s, axis=0, keepdims=True))
                alpha = jnp.exp2(m_old - m_new)
                p = jnp.exp2(s - m_new)
                m_ref[hd, :, 0:2 * wq] = m_new
                acc_ref[hd, :, 0:2 * wq] = (alpha * acc_ref[hd, :, 0:2 * wq]
                                            + jnp.dot(vt(hd), p.astype(BF16), preferred_element_type=F32))

        def body(jb, carry):
            r0 = pl.multiple_of(jb * blk, blk)
            sel = jnp.minimum(i - jb, 2)
            all_heads(blk,
                      lambda hd: k_ref[0, pl.ds(r0, blk), hd * V_DIM:(hd + 1) * V_DIM],
                      lambda hd: vt_ref[hd, jb],
                      lambda hd: bias_ref[hd, sel, :, 0:wq])
            return carry

        lax.fori_loop(0, n_loop, body, 0)
        if tail_keys:
            r0 = n_full * blk
            all_heads(tail_keys,
                      lambda hd: k_ref[0, r0:r0 + tail_keys, hd * V_DIM:(hd + 1) * V_DIM],
                      lambda hd: vt_ref[hd, n_full, :, 0:tail_keys],
                      lambda hd: bias_ref[hd, 0, 0:tail_keys, 0:wq])

        lam = _diff_lambda(lam_ref, lam_init)
        for hd in range(N_HEADS):
            o_both = acc_ref[hd, 0:V_DIM, 0:2 * wq] * (1.0 / acc_ref[hd, V_DIM:V_DIM + 1, 0:2 * wq])
            o = o_both[:, 0:wq] - lam * o_both[:, wq:2 * wq]
            o = o * lax.rsqrt(jnp.mean(o * o, axis=0, keepdims=True) + RMS_EPS) * sub_ref[...] * (1.0 - lam_init)
            a_ref[0:wq, hd * V_DIM:(hd + 1) * V_DIM] = o.T.astype(BF16)
        mix = jnp.dot(a_ref[0:wq, :], wo_ref[...], preferred_element_type=F32)
        o_ref[0, 0:wq, :] = x_ref[0, 0:wq, :] + _rms(mix, g1_ref[...])

    @pl.when(i < n_full)
    def _():
        run(blk, i + 1, 0)

    if tail:
        @pl.when(i == n_full)
        def _():
            run(-(-tail // LANES) * LANES, n_full, tail)


def _attn_prompt(q, kb, vb, x, bias_p, lam_p, subln_col, wo, mix_norm, layer, j, *, blk, lam_init):
    b, t, d = q.shape
    n_full, tail = t // blk, t % blk
    assert tail % 16 == 0
    n_blocks = n_full + (1 if tail else 0)
    seq_blk = pl.BlockSpec((1, blk, d), lambda bi, qi: (bi, qi, 0))
    return pl.pallas_call(
        functools.partial(_attn_p_body, blk=blk, n_full=n_full, tail=tail, lam_init=lam_init),
        grid=(b, n_blocks),
        in_specs=[
            seq_blk,
            pl.BlockSpec((1, t, d), lambda bi, qi: (bi, 0, 0)),
            pl.BlockSpec((1, t, d), lambda bi, qi: (bi, 0, 0)),
            seq_blk,
            _const_spec((N_HEADS, 3, blk, blk), (0, 0, 0, 0)),
            _const_spec((None, 4, HEAD_DIM), (j, 0, 0)),
            _const_spec((None, V_DIM, 1), (j, 0, 0)),
            _const_spec((None, d, d), (j, 0, 0)),
            _const_spec((None, 1, d), (layer * 2 + 1, 0, 0)),
        ],
        out_specs=seq_blk,
        out_shape=jax.ShapeDtypeStruct((b, t, d), F32),
        scratch_shapes=[
            pltpu.VMEM((N_HEADS, 2 * blk, V_DIM), BF16),
            pltpu.VMEM((N_HEADS, n_blocks, V_DIM + SUM_ROWS, blk), BF16),
            pltpu.VMEM((N_HEADS, 1, 2 * blk), F32),
            pltpu.VMEM((N_HEADS, V_DIM + SUM_ROWS, 2 * blk), F32),
            pltpu.VMEM((2, blk, 2 * blk), F32),
            pltpu.VMEM((blk, d), BF16),
        ],
        compiler_params=_params("arbitrary", "arbitrary"),
        name="attn_prompt",
    )(q, kb, vb, x, bias_p, lam_p, subln_col, wo, mix_norm)


def _attn_s_body(pt_ref, q_ref, kn_ref, vn_ref, bias_ref, lam_ref, sub_ref, *rest, n_pages, n_s, lam_init):
    k_pages = rest[:n_pages]
    v_pages = rest[n_pages:2 * n_pages]
    o_ref, s_ref = rest[2 * n_pages:]
    half = n_s * N_HEADS
    cols = PAGE_SIZE * N_HEADS

    q = q_ref[...].reshape(half, V_DIM)
    lane = lax.broadcasted_iota(jnp.int32, (half, V_DIM), 1)
    q2 = jnp.concatenate([jnp.where(lane < HEAD_DIM, q, 0.0), jnp.where(lane >= HEAD_DIM, q, 0.0)], axis=0).astype(BF16)

    def scores(keys, bias):
        return lax.dot_general(q2, keys, _NT, preferred_element_type=F32) + bias

    pad = jnp.zeros((PAGE_SIZE - half, V_DIM), F32)
    kn = jnp.concatenate([kn_ref[...].reshape(half, V_DIM), pad], axis=0).astype(BF16)
    vn = jnp.concatenate([vn_ref[...].reshape(half, V_DIM), pad], axis=0).astype(BF16)
    s_new = scores(kn, bias_ref[:, 2 * cols:2 * cols + PAGE_SIZE])
    m = jnp.max(s_new, axis=-1, keepdims=True)
    for r_ in range(n_pages):
        near = r_ == n_pages - 1
        s = scores(k_pages[r_][...].reshape(cols, V_DIM).astype(BF16),
                   bias_ref[:, cols:2 * cols] if near else bias_ref[:, 0:cols])
        s_ref[r_] = s
        m = jnp.maximum(m, jnp.max(s, axis=-1, keepdims=True))

    p = jnp.exp2(s_new - m)
    l = jnp.sum(p, axis=-1, keepdims=True)
    acc = jnp.dot(p.astype(BF16), vn, preferred_element_type=F32)
    for r_ in range(n_pages):
        p = jnp.exp2(s_ref[r_] - m)
        l = l + jnp.sum(p, axis=-1, keepdims=True)
        acc = acc + jnp.dot(p.astype(BF16), v_pages[r_][...].reshape(cols, V_DIM).astype(BF16),
                            preferred_element_type=F32)

    lam = _diff_lambda(lam_ref, lam_init)
    o_all = acc * (1.0 / l)
    o = o_all[0:half] - lam * o_all[half:2 * half]
    o_ref[...] = (_rms(o, sub_ref[...]) * (1.0 - lam_init)).reshape(o_ref.shape)


def _attn_sample(q, kn, vn, cache_k, cache_v, page_table, bias_s, lam_p, subln, j, *, n_pages, lam_init):
    db, n_s = q.shape[:2]
    assert n_pages == page_table.shape[1]
    rows = 2 * n_s * N_HEADS
    new_kv = pl.BlockSpec((1, n_s, 1, N_HEADS, V_DIM), lambda b, pt: (b, 0, 0, 0, 0))
    q_spec = pl.BlockSpec((1, n_s, N_HEADS, V_DIM), lambda b, pt: (b, 0, 0, 0))

    def page_spec(r):
        return pl.BlockSpec((1, PAGE_SIZE, 1, N_HEADS, V_DIM), lambda b, pt: (pt[b, r], 0, j, 0, 0))

    grid_spec = pltpu.PrefetchScalarGridSpec(
        num_scalar_prefetch=1,
        grid=(db,),
        in_specs=[
            q_spec, new_kv, new_kv,
            pl.BlockSpec(bias_s.shape, lambda b, pt: (0, 0)),
            pl.BlockSpec((None, 4, HEAD_DIM), lambda b, pt: (j, 0, 0)),
            pl.BlockSpec((None, 1, V_DIM), lambda b, pt: (j, 0, 0)),
        ] + [page_spec(r) for r in range(n_pages)] * 2,
        out_specs=q_spec,
        scratch_shapes=[pltpu.VMEM((n_pages, rows, PAGE_SIZE * N_HEADS), F32)],
    )
    return pl.pallas_call(
        functools.partial(_attn_s_body, n_pages=n_pages, n_s=n_s, lam_init=lam_init),
        grid_spec=grid_spec,
        out_shape=jax.ShapeDtypeStruct((db, n_s, N_HEADS, V_DIM), F32),
        compiler_params=_params("parallel"),
        name="attn_sample",
    )(page_table, q, kn, vn, bias_s, lam_p, subln, *([cache_k] * n_pages), *([cache_v] * n_pages))


def _outproj_body(a_ref, x_ref, w_ref, g1_ref, o_ref):
    a = a_ref[...].reshape(x_ref.shape)
    m = jnp.dot(a.astype(BF16), w_ref[...], preferred_element_type=F32)
    o_ref[...] = x_ref[...] + _rms(m, g1_ref[...])


def _outproj(a, x, wo, mix_norm, layer, j, *, tm):
    n, d = x.shape
    row = pl.BlockSpec((tm, d), lambda i: (i, 0))
    return pl.pallas_call(
        _outproj_body,
        grid=(pl.cdiv(n, tm),),
        in_specs=[pl.BlockSpec((tm,) + a.shape[1:], lambda i: (i, 0, 0)), row,
                  _const_spec((None, d, d), (j, 0, 0)), _const_spec((None, 1, d), (layer * 2 + 1, 0, 0))],
        out_specs=row,
        out_shape=jax.ShapeDtypeStruct((n, d), F32),
        compiler_params=_params("parallel"),
        name="attn_outproj",
    )(a, x, wo, mix_norm)


def _pick_tile(n, cap, align=8):
    best = None
    for c in range(align, min(n, cap) + 1, align):
        if n % c == 0:
            best = c
    assert best is not None, (n, cap, align)
    return best


def kernel(x_prompt, x_sample, state_pool, state_conv, cache_k, cache_v, page_table, meta_tokens, rel_bias_table, ffn_norm, ffn_wg, ffn_wu, ffn_wd, mix_norm, pool_w, pool_scale, attn_wqkv, attn_wo, attn_lambda, attn_subln, conv_w1, conv_b1, conv_wdw, conv_bdw, conv_ln_g, conv_ln_b, conv_w2, conv_b2, final_norm):
    b, seq, d = x_prompt.shape
    db, n_s, _ = x_sample.shape
    depth = ffn_wg.shape[0]
    t = seq + N_META
    past = page_table.shape[1] * PAGE_SIZE

    wg, wu, wd = ffn_wg, ffn_wu, ffn_wd
    ffn_norm3 = ffn_norm.reshape(-1, 1, d)
    mix_norm3 = mix_norm.reshape(-1, 1, d)
    pool_w_b = pool_w.astype(BF16)
    pool_scale3 = pool_scale.reshape(-1, 1, d)
    wqkv_b, wo_b = attn_wqkv.astype(BF16), attn_wo.astype(BF16)
    subln3 = attn_subln.reshape(-1, 1, V_DIM)
    cw = (conv_w1.astype(BF16), conv_b1.reshape(-1, 1, 2 * d), conv_wdw, conv_bdw.reshape(-1, 1, d),
          conv_ln_g.reshape(-1, 1, d), conv_ln_b.reshape(-1, 1, d), conv_w2.astype(BF16), conv_b2.reshape(-1, 1, d))
    state_pool2 = state_pool.reshape(state_pool.shape[0], db, POOL_BUF * d)
    state_conv2 = state_conv.reshape(state_conv.shape[0], db, CONV_BUF * d)
    final_g = final_norm.reshape(1, d)

    meta = jnp.broadcast_to(meta_tokens[None].astype(x_prompt.dtype), (b, N_META, d))
    xp = jnp.concatenate([meta, x_prompt], axis=1).reshape(b * t, d)
    xs = x_sample.reshape(db * n_s, d)

    tm_p = _pick_tile(b * t, 768, 16)
    tm_s = min(512, db * n_s)
    ts_pool = _pick_tile(t, 1032)
    ts_conv = _pick_tile(t, 688, BF16_ROWS)
    bt = _pick_tile(db, 32)

    bias_p = bias_s = None
    pool_p, pool_s, conv_p, conv_s = [], [], [], []
    k_p = v_p = k_s = v_s = None
    for i in range(depth):
        kind, j = i % N_MIXERS, i // N_MIXERS
        xp, xs = _ffn(xp, xs, ffn_norm3, final_g, wg, wu, wd, i, 0, tm=tm_p)
        if kind == 0:
            xp3, st = _pool_prompt(xp.reshape(b, t, d), mix_norm3, pool_w_b, pool_scale3, i, j, ts=ts_pool)
            xp = xp3.reshape(b * t, d)
            pool_p.append(st)
            xs2, st = _pool_sample(xs.reshape(db, n_s * d), state_pool2, mix_norm3, pool_w_b, pool_scale3, i, j,
                                   n_s=n_s, past=past, bt=bt)
            xs = xs2.reshape(db * n_s, d)
            pool_s.append(st.reshape(db, POOL_BUF, d))
        elif kind == 1:
            assert j == 0
            lam_init = _lambda_init(i)
            if bias_p is None:
                bias_p, bias_s = _bias_tiles(rel_bias_table, blk=ATTN_BLOCK, n_s=n_s)
            q, k_p, v_p, kb, vb = _qkv(xp, mix_norm3, wqkv_b, i, j, tm=tm_p, decode=False)
            xp = _attn_prompt(q.reshape(b, t, d), kb.reshape(b, t, d), vb.reshape(b, t, d), xp.reshape(b, t, d),
                              bias_p, attn_lambda, attn_subln.reshape(-1, V_DIM, 1), wo_b, mix_norm3, i, j,
                              blk=ATTN_BLOCK, lam_init=lam_init).reshape(b * t, d)
            q, k_s, v_s = _qkv(xs, mix_norm3, wqkv_b, i, j, tm=tm_s, decode=True)
            k_s = k_s.reshape(db, n_s, 1, N_HEADS, V_DIM)
            v_s = v_s.reshape(db, n_s, 1, N_HEADS, V_DIM)
            a = _attn_sample(q.reshape(db, n_s, N_HEADS, V_DIM), k_s, v_s, cache_k, cache_v,
                             page_table, bias_s, attn_lambda, subln3, j, n_pages=page_table.shape[1], lam_init=lam_init)
            xs = _outproj(a.reshape(db * n_s, N_HEADS, V_DIM), xs, wo_b, mix_norm3, i, j, tm=tm_s)
        else:
            xp3, st = _conv_prompt(xp.reshape(b, t, d), mix_norm3, cw, i, j, ts=ts_conv)
            xp = xp3.reshape(b * t, d)
            conv_p.append(st)
            xs2, st = _conv_sample(xs.reshape(db, n_s * d), state_conv2, mix_norm3, cw, i, j, n_s=n_s, bt=bt)
            xs = xs2.reshape(db * n_s, d)
            conv_s.append(st.reshape(db, CONV_BUF, d))
        xp, xs = _ffn(xp, xs, ffn_norm3, final_g, wg, wu, wd, i, 1, tm=tm_p,
                      final_seq=(t, N_META) if i == depth - 1 else None)

    kv_shape_p = (b, t, 1, N_HEADS, V_DIM)
    return (xp.reshape(b, seq, d), xs.reshape(db, n_s, d), jnp.stack(pool_p, axis=0), jnp.stack(pool_s, axis=0),
            jnp.stack(conv_p, axis=0), jnp.stack(conv_s, axis=0),
            k_p.reshape(kv_shape_p), v_p.reshape(kv_shape_p), k_s, v_s)
```

```python
import functools
import math

import jax
import jax.numpy as jnp
from jax import lax
from jax.experimental import pallas as pl
from jax.experimental.pallas import tpu as pltpu

F32 = jnp.float32
BF16 = jnp.bfloat16

N_MIXERS = 3
N_META = 16
N_HEADS = 8
HEAD_DIM = 64
V_DIM = 2 * HEAD_DIM
N_BUCKETS = 32
MAX_EXACT = N_BUCKETS // 2
MAX_DISTANCE = 128
POOL_WINDOWS = (2, 4, 8, 16)
POOL_BUF = max(POOL_WINDOWS) - 1
CONV_WIDTH = 31
CONV_BUF = CONV_WIDTH - 1
PAGE_SIZE = 128
RMS_EPS = 1e-6
LN_EPS = 1e-5
NEG_INF = -1e30
LOG2E = math.log2(math.e)
SUBLANES = 8
LANES = 128
BF16_ROWS = 16
MXU_TILE = 256
SUM_ROWS = 16
ATTN_BLOCK = 256
POOL_HALO = 16
CONV_HALO = 32
VMEM_LIMIT = 56 * 1024 * 1024
WEIGHT_CHUNK_BYTES = 3 * 256 * 1024
WEIGHT_SLOTS = 4


def _lambda_init(layer_idx):
    return 0.8 - 0.6 * math.exp(-0.3 * layer_idx)


def _first_far_distance():
    n = MAX_EXACT
    while MAX_EXACT + int(math.log(n / MAX_EXACT) / math.log(MAX_DISTANCE / MAX_EXACT) * (N_BUCKETS - MAX_EXACT)) < N_BUCKETS - 1:
        n += 1
    return n


def _params(*sem):
    return pltpu.CompilerParams(dimension_semantics=sem, vmem_limit_bytes=VMEM_LIMIT)


def _rms(x, g):
    return x * lax.rsqrt(jnp.mean(x * x, axis=-1, keepdims=True) + RMS_EPS) * g


def _silu(x):
    return x * jax.nn.sigmoid(x)


def _const_spec(shape, index, single=False):
    if single:
        return pl.BlockSpec(shape, lambda *_: index, pipeline_mode=pl.Buffered(1))
    return pl.BlockSpec(shape, lambda *_: index)


def _ffn_chunks(ff, n_chunks):
    tiles = -(-ff // MXU_TILE)
    edges = [min(ff, MXU_TILE * (-(-tiles * c // n_chunks))) for c in range(n_chunks + 1)]
    return [(lo, hi - lo) for lo, hi in zip(edges[:-1], edges[1:]) if hi > lo]


def _ffn_body(xp_ref, xs_ref, n0_ref, n1_ref, gf_ref, wg_hbm, wu_hbm, wd_hbm, op_ref, os_ref,
              wg_ref, wu_ref, wd_ref, stage_in, stage_out, sem, *, n_chunks, n_p, final, layer, f):
    i = pl.program_id(0)

    def fetch(w_hbm, stage, dst):
        slots, rows = stage.shape[0], stage.shape[1]
        n = dst.shape[0] // rows

        def copy(c):
            return pltpu.make_async_copy(w_hbm.at[layer, f, pl.ds(c * rows, rows), :], stage.at[c % slots],
                                         sem.at[c % slots])

        for c in range(min(slots - 1, n)):
            copy(c).start()
        for c in range(n):
            if c + slots - 1 < n:
                copy(c + slots - 1).start()
            copy(c).wait()
            dst[pl.ds(c * rows, rows), :] = stage[c % slots].astype(BF16)

    @pl.when(i == 0)
    def _():
        fetch(wg_hbm, stage_in, wg_ref)
        fetch(wu_hbm, stage_in, wu_ref)
        fetch(wd_hbm, stage_out, wd_ref)

    def apply(x_ref, o_ref):
        rows = x_ref.shape[0]
        first = -(-rows // (2 * BF16_ROWS)) * BF16_ROWS
        for lo_row, n_rows in ((0, first), (first, rows - first))[:2 if rows > first else 1]:
            rs = pl.ds(lo_row, n_rows)
            x = x_ref[rs, :]
            h = _rms(x, n0_ref[...]).astype(BF16)
            acc = jnp.zeros(x.shape, F32)
            for lo, width in _ffn_chunks(wg_ref.shape[1], n_chunks):
                sl = pl.ds(lo, width)
                g = jnp.dot(h, wg_ref[:, sl], preferred_element_type=F32)
                u = jnp.dot(h, wu_ref[:, sl], preferred_element_type=F32)
                a = (_silu(g) * u).astype(BF16)
                acc = acc + jnp.dot(a, wd_ref[sl, :], preferred_element_type=F32)
            y = x + 0.5 * _rms(acc, n1_ref[...])
            o_ref[rs, :] = _rms(y, gf_ref[...]) if final else y

    @pl.when(i == 0)
    def _():
        os_ref[...] = jnp.zeros(os_ref.shape, F32)

    @pl.when(i < n_p)
    def _():
        apply(xp_ref, op_ref)

    @pl.when(i >= n_p)
    def _():
        apply(xs_ref, os_ref)


def _ffn(xp, xs, norms, final_g, wg, wu, wd, layer, f, *, tm, n_chunks=2, final_seq=None):
    (n_rows_p, d), n_rows_s = xp.shape, xs.shape[0]
    ff = wg.shape[-1]
    nidx = (layer * 2 + f) * 2
    tm_s = _pick_tile(n_rows_s, tm, BF16_ROWS)
    if final_seq is None:
        n_p, out_rows_p = pl.cdiv(n_rows_p, tm), n_rows_p
        p_in = p_out = pl.BlockSpec((tm, d), lambda i: (jnp.minimum(i, n_p - 1), 0))
    else:
        t, skip = final_seq
        assert skip % BF16_ROWS == 0 and t % BF16_ROWS == 0
        tm = _pick_tile(t - skip, tm, BF16_ROWS)
        per_seq = (t - skip) // tm
        n_p, out_rows_p = (n_rows_p // t) * per_seq, (n_rows_p // t) * (t - skip)

        def in_rows(i):
            ip = jnp.minimum(i, n_p - 1)
            return pl.multiple_of((ip // per_seq) * t + skip + (ip % per_seq) * tm, BF16_ROWS), 0

        p_in = pl.BlockSpec((pl.Element(tm), pl.Element(d)), in_rows)
        p_out = pl.BlockSpec((tm, d), lambda i: (jnp.minimum(i, n_p - 1), 0))
    n_s = pl.cdiv(n_rows_s, tm_s)
    s_spec = pl.BlockSpec((tm_s, d), lambda i: (jnp.maximum(i - n_p, 0), 0))
    in_rows, out_rows = _pick_tile(d, WEIGHT_CHUNK_BYTES // (4 * ff)), _pick_tile(ff, WEIGHT_CHUNK_BYTES // (4 * d))
    hbm = pl.BlockSpec(memory_space=pl.ANY)
    return pl.pallas_call(
        functools.partial(_ffn_body, n_chunks=n_chunks, n_p=n_p, final=final_seq is not None, layer=layer, f=f),
        grid=(n_p + n_s,),
        in_specs=[
            p_in, s_spec,
            _const_spec((None, 1, d), (nidx, 0, 0)),
            _const_spec((None, 1, d), (nidx + 1, 0, 0)),
            _const_spec((1, d), (0, 0)),
            hbm, hbm, hbm,
        ],
        out_specs=[p_out, s_spec],
        out_shape=[jax.ShapeDtypeStruct((out_rows_p, d), F32), jax.ShapeDtypeStruct((n_rows_s, d), F32)],
        scratch_shapes=[
            pltpu.VMEM((d, ff), BF16), pltpu.VMEM((d, ff), BF16), pltpu.VMEM((ff, d), BF16),
            pltpu.VMEM((WEIGHT_SLOTS, in_rows, ff), F32), pltpu.VMEM((WEIGHT_SLOTS, out_rows, d), F32),
            pltpu.SemaphoreType.DMA((WEIGHT_SLOTS,)),
        ],
        compiler_params=_params("arbitrary"),
        name="ffn",
    )(xp, xs, norms, norms, final_g, wg, wu, wd)


def _pool_p_body(x_ref, g0_ref, g1_ref, w_ref, sc_ref, o_ref, st_ref, *lvl_refs, ts, n_t):
    t = pl.program_id(1)
    d = x_ref.shape[-1]
    n_g = len(POOL_WINDOWS)
    gd = d // n_g
    x = x_ref[0]
    h = _rms(x, g0_ref[...])
    cat_ref = lvl_refs[0]
    top = SUBLANES + POOL_HALO
    rows = POOL_HALO + ts

    @pl.when(t == 0)
    def _():
        cat_ref[0:top, :] = jnp.zeros((top, d), F32)
        for ref in lvl_refs[1:]:
            ref[0:SUBLANES, :] = jnp.zeros((SUBLANES, ref.shape[1]), F32)

    cat_ref[top:top + ts, :] = h
    for k in range(1, len(lvl_refs)):
        prev, cur = lvl_refs[k - 1], lvl_refs[k]
        lo = prev.shape[1] - cur.shape[1]
        cur[SUBLANES:SUBLANES + rows, :] = (prev[SUBLANES:SUBLANES + rows, lo:]
                                            + prev[SUBLANES - 2 ** (k - 1):SUBLANES - 2 ** (k - 1) + rows, lo:])
    pos = t * ts + lax.broadcasted_iota(jnp.int32, (ts, 1), 0)
    outs = []
    for g, w in enumerate(POOL_WINDOWS):
        c0 = g * gd
        k = w.bit_length() - 1
        if k < len(lvl_refs):
            ref = lvl_refs[k]
            l0 = c0 - (d - ref.shape[1])
            acc = ref[top:top + ts, l0:l0 + gd]
        else:
            ref = lvl_refs[k - 1]
            l0 = c0 - (d - ref.shape[1])
            acc = ref[top:top + ts, l0:l0 + gd] + ref[top - w // 2:top - w // 2 + ts, l0:l0 + gd]
        inv_cnt = 1.0 / jnp.minimum(w, pos + 1).astype(F32)
        pooled = (acc * inv_cnt - h[:, c0:c0 + gd]).astype(BF16)
        outs.append(jnp.dot(pooled, w_ref[g], preferred_element_type=F32))
    m = jnp.concatenate(outs, axis=-1) * sc_ref[...]
    o_ref[0] = x + _rms(m, g1_ref[...])

    @pl.when(t == n_t - 1)
    def _():
        st_ref[0] = cat_ref[top + ts - POOL_BUF:top + ts, :]

    cat_ref[SUBLANES:top, :] = cat_ref[SUBLANES + ts:top + ts, :]


def _pool_prompt(x, mix_norm, pool_w, pool_scale, layer, j, *, ts):
    b, t, d = x.shape
    n_t = t // ts
    assert n_t * ts == t and ts % 8 == 0 and ts >= POOL_HALO
    g = len(POOL_WINDOWS)
    assert all(w == 2 ** (i + 1) for i, w in enumerate(POOL_WINDOWS)) and POOL_WINDOWS[-1] // 2 == SUBLANES
    buf_rows = SUBLANES + POOL_HALO + ts
    level_lanes = [d] + [d - i * (d // g) for i in range(g - 1)]
    return pl.pallas_call(
        functools.partial(_pool_p_body, ts=ts, n_t=n_t),
        grid=(b, n_t),
        in_specs=[
            pl.BlockSpec((1, ts, d), lambda bi, ti: (bi, ti, 0)),
            _const_spec((None, 1, d), (layer * 2, 0, 0)),
            _const_spec((None, 1, d), (layer * 2 + 1, 0, 0)),
            _const_spec((None, g, d // g, d // g), (j, 0, 0, 0)),
            _const_spec((None, 1, d), (j, 0, 0)),
        ],
        out_specs=[
            pl.BlockSpec((1, ts, d), lambda bi, ti: (bi, ti, 0)),
            pl.BlockSpec((1, POOL_BUF, d), lambda bi, ti: (bi, 0, 0)),
        ],
        out_shape=[jax.ShapeDtypeStruct((b, t, d), F32), jax.ShapeDtypeStruct((b, POOL_BUF, d), F32)],
        scratch_shapes=[pltpu.VMEM((buf_rows, lanes), F32) for lanes in level_lanes],
        compiler_params=_params("parallel", "arbitrary"),
        name="pool_prompt",
    )(x, mix_norm, mix_norm, pool_w, pool_scale)


def _pool_s_body(x_ref, st_ref, g0_ref, g1_ref, w_ref, sc_ref, o_ref, so_ref, *, n_s, past):
    d = g0_ref.shape[-1]
    gd = d // len(POOL_WINDOWS)
    bt = x_ref.shape[0]
    xs = [x_ref[:, s * d:(s + 1) * d] for s in range(n_s)]
    hs = [_rms(x, g0_ref[...]) for x in xs]

    def cat(idx, c0, width):
        if idx < POOL_BUF:
            return st_ref[:, idx * d + c0:idx * d + c0 + width]
        return hs[idx - POOL_BUF][:, c0:c0 + width]

    outs = []
    for g, w in enumerate(POOL_WINDOWS):
        c0 = g * gd
        rows = []
        for s in range(n_s):
            acc = cat(POOL_BUF + s, c0, gd)
            for jj in range(1, w):
                acc = acc + cat(POOL_BUF + s - jj, c0, gd)
            rows.append(acc * (1.0 / min(w, past + s + 1)) - hs[s][:, c0:c0 + gd])
        pooled = jnp.concatenate(rows, axis=0).astype(BF16)
        outs.append(jnp.dot(pooled, w_ref[g], preferred_element_type=F32))
    for s in range(n_s):
        m = jnp.concatenate([o[s * bt:(s + 1) * bt] for o in outs], axis=-1) * sc_ref[...]
        o_ref[:, s * d:(s + 1) * d] = xs[s] + _rms(m, g1_ref[...])
    for jj in range(POOL_BUF):
        so_ref[:, jj * d:(jj + 1) * d] = cat(n_s + jj, 0, d)


def _pool_sample(x, state, mix_norm, pool_w, pool_scale, layer, j, *, n_s, past, bt):
    db = x.shape[0]
    d = mix_norm.shape[-1]
    g = len(POOL_WINDOWS)
    return pl.pallas_call(
        functools.partial(_pool_s_body, n_s=n_s, past=past),
        grid=(db // bt,),
        in_specs=[
            pl.BlockSpec((bt, n_s * d), lambda i: (i, 0)),
            pl.BlockSpec((None, bt, POOL_BUF * d), lambda i: (j, i, 0)),
            _const_spec((None, 1, d), (layer * 2, 0, 0)),
            _const_spec((None, 1, d), (layer * 2 + 1, 0, 0)),
            _const_spec((None, g, d // g, d // g), (j, 0, 0, 0)),
            _const_spec((None, 1, d), (j, 0, 0)),
        ],
        out_specs=[
            pl.BlockSpec((bt, n_s * d), lambda i: (i, 0)),
            pl.BlockSpec((bt, POOL_BUF * d), lambda i: (i, 0)),
        ],
        out_shape=[jax.ShapeDtypeStruct((db, n_s * d), F32), jax.ShapeDtypeStruct((db, POOL_BUF * d), F32)],
        compiler_params=_params("parallel"),
        name="pool_sample",
    )(x, state, mix_norm, mix_norm, pool_w, pool_scale)


def _layer_norm(y, g, b):
    mu = jnp.mean(y, axis=-1, keepdims=True)
    yc = y - mu
    var = jnp.mean(yc * yc, axis=-1, keepdims=True)
    return yc * lax.rsqrt(var + LN_EPS) * g + b


def _conv_p_body(x_ref, g0_ref, g1_ref, w1_ref, b1_ref, wdw_ref, bdw_ref, lng_ref, lnb_ref, w2_ref, b2_ref,
                 o_ref, st_ref, cat_ref, y_ref, *, ts, n_t):
    t = pl.program_id(1)
    d = x_ref.shape[-1]
    first = -(-ts // (2 * BF16_ROWS)) * BF16_ROWS
    parts = ((0, first), (first, ts - first))[:2 if ts > first else 1]

    @pl.when(t == 0)
    def _():
        cat_ref[0:CONV_HALO, :] = jnp.zeros((CONV_HALO, d), F32)

    for lo, n in parts:
        h = _rms(x_ref[0, lo:lo + n, :], g0_ref[...]).astype(BF16)
        glu = jnp.dot(h, w1_ref[...], preferred_element_type=F32) + b1_ref[...]
        cat_ref[CONV_HALO + lo:CONV_HALO + lo + n, :] = glu[:, :d] * jax.nn.sigmoid(glu[:, d:])

    off = CONV_HALO - CONV_BUF
    for lo, n in parts:
        for c0 in range(0, d, LANES):
            y = None
            for r in range(SUBLANES):
                zrows = n + (SUBLANES if r else 0)
                z = None
                for a in range((CONV_WIDTH + off) // SUBLANES + 1):
                    k = SUBLANES * a + r - off
                    if 0 <= k < CONV_WIDTH:
                        r0 = lo + SUBLANES * a
                        term = cat_ref[r0:r0 + zrows, c0:c0 + LANES] * wdw_ref[k:k + 1, c0:c0 + LANES]
                        z = term if z is None else z + term
                zs = z[r:r + n]
                y = zs if y is None else y + zs
            y_ref[lo:lo + n, c0:c0 + LANES] = y + bdw_ref[:, c0:c0 + LANES]
        a = _silu(_layer_norm(y_ref[lo:lo + n, :], lng_ref[...], lnb_ref[...])).astype(BF16)
        m = jnp.dot(a, w2_ref[...], preferred_element_type=F32) + b2_ref[...]
        o_ref[0, lo:lo + n, :] = x_ref[0, lo:lo + n, :] + _rms(m, g1_ref[...])

    @pl.when(t == n_t - 1)
    def _():
        st_ref[0] = cat_ref[CONV_HALO + ts - CONV_BUF:CONV_HALO + ts, :]

    cat_ref[0:CONV_HALO, :] = cat_ref[ts:ts + CONV_HALO, :]


def _conv_specs(layer, j, d):
    return [
        _const_spec((None, 1, d), (layer * 2, 0, 0)),
        _const_spec((None, 1, d), (layer * 2 + 1, 0, 0)),
        _const_spec((None, d, 2 * d), (j, 0, 0)),
        _const_spec((None, 1, 2 * d), (j, 0, 0)),
        _const_spec((None, CONV_WIDTH, d), (j, 0, 0)),
        _const_spec((None, 1, d), (j, 0, 0)),
        _const_spec((None, 1, d), (j, 0, 0)),
        _const_spec((None, 1, d), (j, 0, 0)),
        _const_spec((None, d, d), (j, 0, 0)),
        _const_spec((None, 1, d), (j, 0, 0)),
    ]


def _conv_prompt(x, mix_norm, cw, layer, j, *, ts):
    b, t, d = x.shape
    n_t = t // ts
    assert n_t * ts == t and ts % 8 == 0 and ts >= CONV_HALO
    return pl.pallas_call(
        functools.partial(_conv_p_body, ts=ts, n_t=n_t),
        grid=(b, n_t),
        in_specs=[pl.BlockSpec((1, ts, d), lambda bi, ti: (bi, ti, 0))] + _conv_specs(layer, j, d),
        out_specs=[
            pl.BlockSpec((1, ts, d), lambda bi, ti: (bi, ti, 0)),
            pl.BlockSpec((1, CONV_BUF, d), lambda bi, ti: (bi, 0, 0)),
        ],
        out_shape=[jax.ShapeDtypeStruct((b, t, d), F32), jax.ShapeDtypeStruct((b, CONV_BUF, d), F32)],
        scratch_shapes=[pltpu.VMEM((CONV_HALO + ts, d), F32), pltpu.VMEM((ts, d), F32)],
        compiler_params=_params("parallel", "arbitrary"),
        name="conv_prompt",
    )(x, mix_norm, mix_norm, *cw)


def _conv_s_body(x_ref, st_ref, g0_ref, g1_ref, w1_ref, b1_ref, wdw_ref, bdw_ref, lng_ref, lnb_ref, w2_ref, b2_ref,
                 o_ref, so_ref, *, n_s):
    d = g0_ref.shape[-1]
    bt = x_ref.shape[0]
    xs = [x_ref[:, s * d:(s + 1) * d] for s in range(n_s)]
    h = jnp.concatenate([_rms(x, g0_ref[...]) for x in xs], axis=0).astype(BF16)
    z = jnp.dot(h, w1_ref[...], preferred_element_type=F32) + b1_ref[...]
    u = z[:, :d] * jax.nn.sigmoid(z[:, d:])
    us = [u[s * bt:(s + 1) * bt] for s in range(n_s)]

    def cat(idx):
        if idx < CONV_BUF:
            return st_ref[:, idx * d:(idx + 1) * d]
        return us[idx - CONV_BUF]

    acts = []
    for s in range(n_s):
        y = cat(s) * wdw_ref[0:1, :] + bdw_ref[...]
        for k in range(1, CONV_WIDTH):
            y = y + cat(s + k) * wdw_ref[k:k + 1, :]
        acts.append(_silu(_layer_norm(y, lng_ref[...], lnb_ref[...])))
    a = jnp.concatenate(acts, axis=0).astype(BF16)
    m = jnp.dot(a, w2_ref[...], preferred_element_type=F32) + b2_ref[...]
    for s in range(n_s):
        o_ref[:, s * d:(s + 1) * d] = xs[s] + _rms(m[s * bt:(s + 1) * bt], g1_ref[...])
    for jj in range(CONV_BUF):
        so_ref[:, jj * d:(jj + 1) * d] = cat(n_s + jj)


def _conv_sample(x, state, mix_norm, cw, layer, j, *, n_s, bt):
    db = x.shape[0]
    d = mix_norm.shape[-1]
    return pl.pallas_call(
        functools.partial(_conv_s_body, n_s=n_s),
        grid=(db // bt,),
        in_specs=[
            pl.BlockSpec((bt, n_s * d), lambda i: (i, 0)),
            pl.BlockSpec((None, bt, CONV_BUF * d), lambda i: (j, i, 0)),
        ] + _conv_specs(layer, j, d),
        out_specs=[
            pl.BlockSpec((bt, n_s * d), lambda i: (i, 0)),
            pl.BlockSpec((bt, CONV_BUF * d), lambda i: (i, 0)),
        ],
        out_shape=[jax.ShapeDtypeStruct((db, n_s * d), F32), jax.ShapeDtypeStruct((db, CONV_BUF * d), F32)],
        compiler_params=_params("parallel"),
        name="conv_sample",
    )(x, state, mix_norm, mix_norm, *cw)


def _qkv_body(x_ref, g0_ref, w_ref, q_ref, k_ref, v_ref, *kv_bf16_refs, decode):
    rows, d = x_ref.shape
    h = _rms(x_ref[...], g0_ref[...]).astype(BF16)
    qkv = jnp.dot(h, w_ref[...], preferred_element_type=F32)
    q = qkv[:, :d] * (HEAD_DIM ** -0.5 * LOG2E)
    k = qkv[:, d:2 * d]
    v = qkv[:, 2 * d:]
    k_ref[...] = k.reshape(rows, N_HEADS, V_DIM)
    v_ref[...] = v.reshape(rows, N_HEADS, V_DIM)
    if decode:
        q_ref[...] = q.reshape(rows, N_HEADS, V_DIM)
    else:
        kb_ref, vb_ref = kv_bf16_refs
        q_ref[...] = q.astype(BF16)
        kb_ref[...] = k.astype(BF16)
        vb_ref[...] = v.astype(BF16)


def _qkv(x, mix_norm, wqkv, layer, j, *, tm, decode):
    n, d = x.shape
    row = pl.BlockSpec((tm, d), lambda i: (i, 0))
    per_head = pl.BlockSpec((tm, N_HEADS, V_DIM), lambda i: (i, 0, 0))
    head_shape = jax.ShapeDtypeStruct((n, N_HEADS, V_DIM), F32)
    flat_bf16 = jax.ShapeDtypeStruct((n, d), BF16)
    return pl.pallas_call(
        functools.partial(_qkv_body, decode=decode),
        grid=(pl.cdiv(n, tm),),
        in_specs=[row, _const_spec((None, 1, d), (layer * 2, 0, 0)), _const_spec((None, d, 3 * d), (j, 0, 0))],
        out_specs=[per_head] * 3 if decode else [row, per_head, per_head, row, row],
        out_shape=[head_shape] * 3 if decode else [flat_bf16, head_shape, head_shape, flat_bf16, flat_bf16],
        compiler_params=_params("parallel"),
        name="qkv_proj",
    )(x, mix_norm, wqkv)


def _bucket(n):
    nf = jnp.maximum(n, 1).astype(F32)
    large = MAX_EXACT + (jnp.log(nf / MAX_EXACT) / math.log(MAX_DISTANCE / MAX_EXACT)
                         * (N_BUCKETS - MAX_EXACT)).astype(jnp.int32)
    large = jnp.minimum(large, N_BUCKETS - 1)
    return jnp.where(n < MAX_EXACT, n, large)


def _lookup(bucket, entry):
    out = jnp.zeros(bucket.shape, F32)
    for b in range(N_BUCKETS):
        out = jnp.where(bucket == b, entry(b), out)
    return out


def _bias_p_body(table_ref, bp_ref, *, blk):
    head = pl.program_id(0)
    a = lax.broadcasted_iota(jnp.int32, (blk, blk), 0)
    b = lax.broadcasted_iota(jnp.int32, (blk, blk), 1)
    for sel in range(3):
        n = sel * blk + b - a
        vals = _lookup(_bucket(jnp.maximum(n, 0)), lambda bb: table_ref[bb * N_HEADS + head])
        bp_ref[0, sel] = jnp.where(n >= 0, vals * LOG2E, NEG_INF)


def _bias_s_body(tt_ref, bs_ref, *, n_s):
    rows = 2 * n_s * N_HEADS
    cols = PAGE_SIZE * N_HEADS
    tt = tt_ref[...]
    trow = jnp.broadcast_to(tt[None], (rows // N_HEADS, N_HEADS, N_BUCKETS)).reshape(rows, N_BUCKETS)

    def tile(width, dist):
        r = lax.broadcasted_iota(jnp.int32, (rows, width), 0)
        c = lax.broadcasted_iota(jnp.int32, (rows, width), 1)
        n = dist((r // N_HEADS) % n_s, c // N_HEADS)
        vals = _lookup(_bucket(jnp.maximum(n, 0)), lambda bb: trow[:, bb:bb + 1])
        return jnp.where((r % N_HEADS == c % N_HEADS) & (n >= 0) & (c < cols), vals * LOG2E, NEG_INF)

    bs_ref[:, 0:cols] = tile(cols, lambda qi, kk: 2 * PAGE_SIZE + qi - kk)
    bs_ref[:, cols:2 * cols] = tile(cols, lambda qi, kk: PAGE_SIZE + qi - kk)
    bs_ref[:, 2 * cols:2 * cols + PAGE_SIZE] = tile(
        PAGE_SIZE, lambda qi, kk: jnp.where(kk < n_s, qi - kk, -1))


def _bias_tiles(table, *, blk, n_s):
    far = _first_far_distance()
    assert blk + 1 >= far and PAGE_SIZE + 1 >= far and n_s * N_HEADS <= PAGE_SIZE
    bias_p = pl.pallas_call(
        functools.partial(_bias_p_body, blk=blk),
        grid=(N_HEADS,),
        in_specs=[pl.BlockSpec(memory_space=pltpu.SMEM)],
        out_specs=pl.BlockSpec((1, 3, blk, blk), lambda h: (h, 0, 0, 0)),
        out_shape=jax.ShapeDtypeStruct((N_HEADS, 3, blk, blk), F32),
        compiler_params=_params("parallel"),
        name="rel_bias_prompt",
    )(table.reshape(-1))
    width = 2 * PAGE_SIZE * N_HEADS + PAGE_SIZE
    bias_s = pl.pallas_call(
        functools.partial(_bias_s_body, n_s=n_s),
        out_shape=jax.ShapeDtypeStruct((2 * n_s * N_HEADS, width), F32),
        compiler_params=pltpu.CompilerParams(vmem_limit_bytes=VMEM_LIMIT),
        name="rel_bias_sample",
    )(table.T)
    return bias_p, bias_s


def _diff_lambda(lam_ref, lam_init):
    lp = lam_ref[...]
    s1 = jnp.sum(lp[0:1] * lp[1:2], axis=-1, keepdims=True)
    s2 = jnp.sum(lp[2:3] * lp[3:4], axis=-1, keepdims=True)
    return jnp.exp(s1) - jnp.exp(s2) + lam_init


_NT = (((1,), (1,)), ((), ()))


def _attn_p_body(q_ref, k_ref, v_ref, x_ref, bias_ref, lam_ref, sub_ref, wo_ref, g1_ref, o_ref,
                 qz_ref, vt_ref, m_ref, acc_ref, s_ref, p_ref, al_ref, a_ref, *, blk, n_full, tail, lam_init):
    i = pl.program_id(1)

    @pl.when(i == 0)
    def _():
        for hd in range(N_HEADS):
            c0 = hd * V_DIM

            ones_row = (lax.broadcasted_iota(jnp.int32, (SUM_ROWS, blk), 0) == 0).astype(BF16)

            def xpose(jb, carry):
                r0 = pl.multiple_of(jb * blk, blk)
                vt_ref[hd, jb, 0:V_DIM, :] = v_ref[0, pl.ds(r0, blk), c0:c0 + V_DIM].T
                vt_ref[hd, jb, V_DIM:V_DIM + SUM_ROWS, :] = ones_row
                return carry

            lax.fori_loop(0, n_full, xpose, 0)
            if tail:
                r0 = n_full * blk
                vt_ref[hd, n_full, 0:V_DIM, 0:tail] = (
                    v_ref[0, r0:r0 + tail, c0:c0 + V_DIM].astype(F32).T.astype(BF16))
                vt_ref[hd, n_full, V_DIM:V_DIM + SUM_ROWS, :] = ones_row

    def run(wq, n_loop, tail_keys):
        lane = lax.broadcasted_iota(jnp.int32, (wq, V_DIM), 1)
        for hd in range(N_HEADS):
            q = q_ref[0, 0:wq, hd * V_DIM:(hd + 1) * V_DIM]
            qz_ref[hd, 0:wq, :] = jnp.where(lane < HEAD_DIM, q, jnp.zeros_like(q))
            qz_ref[hd, wq:2 * wq, :] = jnp.where(lane >= HEAD_DIM, q, jnp.zeros_like(q))
        m_ref[:, :, 0:2 * wq] = jnp.full((N_HEADS, 1, 2 * wq), NEG_INF, F32)
        acc_ref[:, :, 0:2 * wq] = jnp.zeros((N_HEADS, V_DIM + SUM_ROWS, 2 * wq), F32)

        def all_heads(n, keys, vt, bias):
            def stage(hd):
                b2 = bias(hd)
                s_ref[hd % 2, 0:n, 0:2 * wq] = (
                    lax.dot_general(keys(hd), qz_ref[hd, 0:2 * wq, :], _NT, preferred_element_type=F32)
                    + jnp.concatenate([b2, b2], axis=1))

            def softmax(hd):
                s = s_ref[hd % 2, 0:n, 0:2 * wq]
                m_old = m_ref[hd, :, 0:2 * wq]
                m_new = jnp.maximum(m_old, jnp.max(s, axis=0, keepdims=True))
                al_ref[hd % 2, :, 0:2 * wq] = jnp.exp2(m_old - m_new)
                p_ref[hd % 2, 0:n, 0:2 * wq] = jnp.exp2(s - m_new).astype(BF16)
                m_ref[hd, :, 0:2 * wq] = m_new

            def accumulate(hd):
                acc_ref[hd, :, 0:2 * wq] = (
                    al_ref[hd % 2, :, 0:2 * wq] * acc_ref[hd, :, 0:2 * wq]
                    + jnp.dot(vt(hd), p_ref[hd % 2, 0:n, 0:2 * wq], preferred_element_type=F32))

            stage(0)
            for hd in range(N_HEADS + 1):
                if hd + 1 < N_HEADS:
                    stage(hd + 1)
                if hd < N_HEADS:
                    softmax(hd)
                if hd >= 1:
                    accumulate(hd - 1)

        def body(jb, carry):
            r0 = pl.multiple_of(jb * blk, blk)
            sel = jnp.minimum(i - jb, 2)
            all_heads(blk,
                      lambda hd: k_ref[0, pl.ds(r0, blk), hd * V_DIM:(hd + 1) * V_DIM],
                      lambda hd: vt_ref[hd, jb],
                      lambda hd: bias_ref[hd, sel, :, 0:wq])
            return carry

        lax.fori_loop(0, n_loop, body, 0)
        if tail_keys:
            r0 = n_full * blk
            all_heads(tail_keys,
                      lambda hd: k_ref[0, r0:r0 + tail_keys, hd * V_DIM:(hd + 1) * V_DIM],
                      lambda hd: vt_ref[hd, n_full, :, 0:tail_keys],
                      lambda hd: bias_ref[hd, 0, 0:tail_keys, 0:wq])

        lam = _diff_lambda(lam_ref, lam_init)
        for hd in range(N_HEADS):
            o_both = acc_ref[hd, 0:V_DIM, 0:2 * wq] * (1.0 / acc_ref[hd, V_DIM:V_DIM + 1, 0:2 * wq])
            o = o_both[:, 0:wq] - lam * o_both[:, wq:2 * wq]
            o = o * lax.rsqrt(jnp.mean(o * o, axis=0, keepdims=True) + RMS_EPS) * sub_ref[...] * (1.0 - lam_init)
            a_ref[0:wq, hd * V_DIM:(hd + 1) * V_DIM] = o.T.astype(BF16)
        mix = jnp.dot(a_ref[0:wq, :], wo_ref[...], preferred_element_type=F32)
        o_ref[0, 0:wq, :] = x_ref[0, 0:wq, :] + _rms(mix, g1_ref[...])

    @pl.when(i < n_full)
    def _():
        run(blk, i + 1, 0)

    if tail:
        @pl.when(i == n_full)
        def _():
            run(-(-tail // LANES) * LANES, n_full, tail)


def _attn_prompt(q, kb, vb, x, bias_p, lam_p, subln_col, wo, mix_norm, layer, j, *, blk, lam_init):
    b, t, d = q.shape
    n_full, tail = t // blk, t % blk
    assert tail % 16 == 0
    n_blocks = n_full + (1 if tail else 0)
    seq_blk = pl.BlockSpec((1, blk, d), lambda bi, qi: (bi, qi, 0))
    return pl.pallas_call(
        functools.partial(_attn_p_body, blk=blk, n_full=n_full, tail=tail, lam_init=lam_init),
        grid=(b, n_blocks),
        in_specs=[
            seq_blk,
            pl.BlockSpec((1, t, d), lambda bi, qi: (bi, 0, 0)),
            pl.BlockSpec((1, t, d), lambda bi, qi: (bi, 0, 0)),
            seq_blk,
            _const_spec((N_HEADS, 3, blk, blk), (0, 0, 0, 0)),
            _const_spec((None, 4, HEAD_DIM), (j, 0, 0)),
            _const_spec((None, V_DIM, 1), (j, 0, 0)),
            _const_spec((None, d, d), (j, 0, 0)),
            _const_spec((None, 1, d), (layer * 2 + 1, 0, 0)),
        ],
        out_specs=seq_blk,
        out_shape=jax.ShapeDtypeStruct((b, t, d), F32),
        scratch_shapes=[
            pltpu.VMEM((N_HEADS, 2 * blk, V_DIM), BF16),
            pltpu.VMEM((N_HEADS, n_blocks, V_DIM + SUM_ROWS, blk), BF16),
            pltpu.VMEM((N_HEADS, 1, 2 * blk), F32),
            pltpu.VMEM((N_HEADS, V_DIM + SUM_ROWS, 2 * blk), F32),
            pltpu.VMEM((2, blk, 2 * blk), F32),
            pltpu.VMEM((2, blk, 2 * blk), BF16),
            pltpu.VMEM((2, 1, 2 * blk), F32),
            pltpu.VMEM((blk, d), BF16),
        ],
        compiler_params=_params("arbitrary", "arbitrary"),
        name="attn_prompt",
    )(q, kb, vb, x, bias_p, lam_p, subln_col, wo, mix_norm)


def _attn_s_body(pt_ref, q_ref, kn_ref, vn_ref, bias_ref, lam_ref, sub_ref, *rest, n_pages, n_s, lam_init):
    k_pages = rest[:n_pages]
    v_pages = rest[n_pages:2 * n_pages]
    o_ref, s_ref = rest[2 * n_pages:]
    half = n_s * N_HEADS
    cols = PAGE_SIZE * N_HEADS

    q = q_ref[...].reshape(half, V_DIM)
    lane = lax.broadcasted_iota(jnp.int32, (half, V_DIM), 1)
    q2 = jnp.concatenate([jnp.where(lane < HEAD_DIM, q, 0.0), jnp.where(lane >= HEAD_DIM, q, 0.0)], axis=0).astype(BF16)

    def scores(keys, bias):
        return lax.dot_general(q2, keys, _NT, preferred_element_type=F32) + bias

    pad = jnp.zeros((PAGE_SIZE - half, V_DIM), F32)
    kn = jnp.concatenate([kn_ref[...].reshape(half, V_DIM), pad], axis=0).astype(BF16)
    vn = jnp.concatenate([vn_ref[...].reshape(half, V_DIM), pad], axis=0).astype(BF16)
    s_new = scores(kn, bias_ref[:, 2 * cols:2 * cols + PAGE_SIZE])
    m = jnp.max(s_new, axis=-1, keepdims=True)
    for r_ in range(n_pages):
        near = r_ == n_pages - 1
        s = scores(k_pages[r_][...].reshape(cols, V_DIM).astype(BF16),
                   bias_ref[:, cols:2 * cols] if near else bias_ref[:, 0:cols])
        s_ref[r_] = s
        m = jnp.maximum(m, jnp.max(s, axis=-1, keepdims=True))

    p = jnp.exp2(s_new - m)
    l = jnp.sum(p, axis=-1, keepdims=True)
    acc = jnp.dot(p.astype(BF16), vn, preferred_element_type=F32)
    for r_ in range(n_pages):
        p = jnp.exp2(s_ref[r_] - m)
        l = l + jnp.sum(p, axis=-1, keepdims=True)
        acc = acc + jnp.dot(p.astype(BF16), v_pages[r_][...].reshape(cols, V_DIM).astype(BF16),
                            preferred_element_type=F32)

    lam = _diff_lambda(lam_ref, lam_init)
    o_all = acc * (1.0 / l)
    o = o_all[0:half] - lam * o_all[half:2 * half]
    o_ref[...] = (_rms(o, sub_ref[...]) * (1.0 - lam_init)).reshape(o_ref.shape)


def _attn_sample(q, kn, vn, cache_k, cache_v, page_table, bias_s, lam_p, subln, j, *, n_pages, lam_init):
    db, n_s = q.shape[:2]
    assert n_pages == page_table.shape[1]
    rows = 2 * n_s * N_HEADS
    new_kv = pl.BlockSpec((1, n_s, 1, N_HEADS, V_DIM), lambda b, pt: (b, 0, 0, 0, 0))
    q_spec = pl.BlockSpec((1, n_s, N_HEADS, V_DIM), lambda b, pt: (b, 0, 0, 0))

    def page_spec(r):
        return pl.BlockSpec((1, PAGE_SIZE, 1, N_HEADS, V_DIM), lambda b, pt: (pt[b, r], 0, j, 0, 0))

    grid_spec = pltpu.PrefetchScalarGridSpec(
        num_scalar_prefetch=1,
        grid=(db,),
        in_specs=[
            q_spec, new_kv, new_kv,
            pl.BlockSpec(bias_s.shape, lambda b, pt: (0, 0)),
            pl.BlockSpec((None, 4, HEAD_DIM), lambda b, pt: (j, 0, 0)),
            pl.BlockSpec((None, 1, V_DIM), lambda b, pt: (j, 0, 0)),
        ] + [page_spec(r) for r in range(n_pages)] * 2,
        out_specs=q_spec,
        scratch_shapes=[pltpu.VMEM((n_pages, rows, PAGE_SIZE * N_HEADS), F32)],
    )
    return pl.pallas_call(
        functools.partial(_attn_s_body, n_pages=n_pages, n_s=n_s, lam_init=lam_init),
        grid_spec=grid_spec,
        out_shape=jax.ShapeDtypeStruct((db, n_s, N_HEADS, V_DIM), F32),
        compiler_params=_params("parallel"),
        name="attn_sample",
    )(page_table, q, kn, vn, bias_s, lam_p, subln, *([cache_k] * n_pages), *([cache_v] * n_pages))


def _outproj_body(a_ref, x_ref, w_ref, g1_ref, o_ref):
    a = a_ref[...].reshape(x_ref.shape)
    m = jnp.dot(a.astype(BF16), w_ref[...], preferred_element_type=F32)
    o_ref[...] = x_ref[...] + _rms(m, g1_ref[...])


def _outproj(a, x, wo, mix_norm, layer, j, *, tm):
    n, d = x.shape
    row = pl.BlockSpec((tm, d), lambda i: (i, 0))
    return pl.pallas_call(
        _outproj_body,
        grid=(pl.cdiv(n, tm),),
        in_specs=[pl.BlockSpec((tm,) + a.shape[1:], lambda i: (i, 0, 0)), row,
                  _const_spec((None, d, d), (j, 0, 0)), _const_spec((None, 1, d), (layer * 2 + 1, 0, 0))],
        out_specs=row,
        out_shape=jax.ShapeDtypeStruct((n, d), F32),
        compiler_params=_params("parallel"),
        name="attn_outproj",
    )(a, x, wo, mix_norm)


def _pick_tile(n, cap, align=8):
    best = None
    for c in range(align, min(n, cap) + 1, align):
        if n % c == 0:
            best = c
    assert best is not None, (n, cap, align)
    return best


def kernel(x_prompt, x_sample, state_pool, state_conv, cache_k, cache_v, page_table, meta_tokens, rel_bias_table, ffn_norm, ffn_wg, ffn_wu, ffn_wd, mix_norm, pool_w, pool_scale, attn_wqkv, attn_wo, attn_lambda, attn_subln, conv_w1, conv_b1, conv_wdw, conv_bdw, conv_ln_g, conv_ln_b, conv_w2, conv_b2, final_norm):
    b, seq, d = x_prompt.shape
    db, n_s, _ = x_sample.shape
    depth = ffn_wg.shape[0]
    t = seq + N_META
    past = page_table.shape[1] * PAGE_SIZE

    wg, wu, wd = ffn_wg, ffn_wu, ffn_wd
    ffn_norm3 = ffn_norm.reshape(-1, 1, d)
    mix_norm3 = mix_norm.reshape(-1, 1, d)
    pool_w_b = pool_w.astype(BF16)
    pool_scale3 = pool_scale.reshape(-1, 1, d)
    wqkv_b, wo_b = attn_wqkv.astype(BF16), attn_wo.astype(BF16)
    subln3 = attn_subln.reshape(-1, 1, V_DIM)
    cw = (conv_w1.astype(BF16), conv_b1.reshape(-1, 1, 2 * d), conv_wdw, conv_bdw.reshape(-1, 1, d),
          conv_ln_g.reshape(-1, 1, d), conv_ln_b.reshape(-1, 1, d), conv_w2.astype(BF16), conv_b2.reshape(-1, 1, d))
    state_pool2 = state_pool.reshape(state_pool.shape[0], db, POOL_BUF * d)
    state_conv2 = state_conv.reshape(state_conv.shape[0], db, CONV_BUF * d)
    final_g = final_norm.reshape(1, d)

    meta = jnp.broadcast_to(meta_tokens[None].astype(x_prompt.dtype), (b, N_META, d))
    xp = jnp.concatenate([meta, x_prompt], axis=1).reshape(b * t, d)
    xs = x_sample.reshape(db * n_s, d)

    tm_p = _pick_tile(b * t, 768, 16)
    tm_s = min(512, db * n_s)
    ts_pool = _pick_tile(t, 1032)
    ts_conv = _pick_tile(t, 688, BF16_ROWS)
    bt = _pick_tile(db, 32)

    bias_p = bias_s = None
    pool_p, pool_s, conv_p, conv_s = [], [], [], []
    k_p = v_p = k_s = v_s = None
    for i in range(depth):
        kind, j = i % N_MIXERS, i // N_MIXERS
        xp, xs = _ffn(xp, xs, ffn_norm3, final_g, wg, wu, wd, i, 0, tm=tm_p)
        if kind == 0:
            xp3, st = _pool_prompt(xp.reshape(b, t, d), mix_norm3, pool_w_b, pool_scale3, i, j, ts=ts_pool)
            xp = xp3.reshape(b * t, d)
            pool_p.append(st)
            xs2, st = _pool_sample(xs.reshape(db, n_s * d), state_pool2, mix_norm3, pool_w_b, pool_scale3, i, j,
                                   n_s=n_s, past=past, bt=bt)
            xs = xs2.reshape(db * n_s, d)
            pool_s.append(st.reshape(db, POOL_BUF, d))
        elif kind == 1:
            assert j == 0
            lam_init = _lambda_init(i)
            if bias_p is None:
                bias_p, bias_s = _bias_tiles(rel_bias_table, blk=ATTN_BLOCK, n_s=n_s)
            q, k_p, v_p, kb, vb = _qkv(xp, mix_norm3, wqkv_b, i, j, tm=tm_p, decode=False)
            xp = _attn_prompt(q.reshape(b, t, d), kb.reshape(b, t, d), vb.reshape(b, t, d), xp.reshape(b, t, d),
                              bias_p, attn_lambda, attn_subln.reshape(-1, V_DIM, 1), wo_b, mix_norm3, i, j,
                              blk=ATTN_BLOCK, lam_init=lam_init).reshape(b * t, d)
            q, k_s, v_s = _qkv(xs, mix_norm3, wqkv_b, i, j, tm=tm_s, decode=True)
            k_s = k_s.reshape(db, n_s, 1, N_HEADS, V_DIM)
            v_s = v_s.reshape(db, n_s, 1, N_HEADS, V_DIM)
            a = _attn_sample(q.reshape(db, n_s, N_HEADS, V_DIM), k_s, v_s, cache_k, cache_v,
                             page_table, bias_s, attn_lambda, subln3, j, n_pages=page_table.shape[1], lam_init=lam_init)
            xs = _outproj(a.reshape(db * n_s, N_HEADS, V_DIM), xs, wo_b, mix_norm3, i, j, tm=tm_s)
        else:
            xp3, st = _conv_prompt(xp.reshape(b, t, d), mix_norm3, cw, i, j, ts=ts_conv)
            xp = xp3.reshape(b * t, d)
            conv_p.append(st)
            xs2, st = _conv_sample(xs.reshape(db, n_s * d), state_conv2, mix_norm3, cw, i, j, n_s=n_s, bt=bt)
            xs = xs2.reshape(db * n_s, d)
            conv_s.append(st.reshape(db, CONV_BUF, d))
        xp, xs = _ffn(xp, xs, ffn_norm3, final_g, wg, wu, wd, i, 1, tm=tm_p,
                      final_seq=(t, N_META) if i == depth - 1 else None)

    kv_shape_p = (b, t, 1, N_HEADS, V_DIM)
    return (xp.reshape(b, seq, d), xs.reshape(db, n_s, d), jnp.stack(pool_p, axis=0), jnp.stack(pool_s, axis=0),
            jnp.stack(conv_p, axis=0), jnp.stack(conv_s, axis=0),
            k_p.reshape(kv_shape_p), v_p.reshape(kv_shape_p), k_s, v_s)
```

```python
import functools
import math

import jax
import jax.numpy as jnp
from jax import lax
from jax.experimental import pallas as pl
from jax.experimental.pallas import tpu as pltpu

F32 = jnp.float32
BF16 = jnp.bfloat16

N_MIXERS = 3
N_META = 16
N_HEADS = 8
HEAD_DIM = 64
V_DIM = 2 * HEAD_DIM
N_BUCKETS = 32
MAX_EXACT = N_BUCKETS // 2
MAX_DISTANCE = 128
POOL_WINDOWS = (2, 4, 8, 16)
POOL_BUF = max(POOL_WINDOWS) - 1
CONV_WIDTH = 31
CONV_BUF = CONV_WIDTH - 1
PAGE_SIZE = 128
RMS_EPS = 1e-6
LN_EPS = 1e-5
NEG_INF = -1e30
LOG2E = math.log2(math.e)
SUBLANES = 8
LANES = 128
BF16_ROWS = 16
MXU_TILE = 256
SUM_ROWS = 16
ATTN_BLOCK = 256
POOL_HALO = 16
CONV_HALO = 32
VMEM_LIMIT = 56 * 1024 * 1024
WEIGHT_CHUNK_BYTES = 3 * 256 * 1024
WEIGHT_SLOTS = 4


def _lambda_init(layer_idx):
    return 0.8 - 0.6 * math.exp(-0.3 * layer_idx)


def _first_far_distance():
    n = MAX_EXACT
    while MAX_EXACT + int(math.log(n / MAX_EXACT) / math.log(MAX_DISTANCE / MAX_EXACT) * (N_BUCKETS - MAX_EXACT)) < N_BUCKETS - 1:
        n += 1
    return n


def _params(*sem):
    return pltpu.CompilerParams(dimension_semantics=sem, vmem_limit_bytes=VMEM_LIMIT)


def _rms(x, g):
    return x * lax.rsqrt(jnp.mean(x * x, axis=-1, keepdims=True) + RMS_EPS) * g


def _silu(x):
    return x * jax.nn.sigmoid(x)


def _const_spec(shape, index, single=False):
    if single:
        return pl.BlockSpec(shape, lambda *_: index, pipeline_mode=pl.Buffered(1))
    return pl.BlockSpec(shape, lambda *_: index)


def _ffn_chunks(ff, n_chunks):
    tiles = -(-ff // MXU_TILE)
    edges = [min(ff, MXU_TILE * (-(-tiles * c // n_chunks))) for c in range(n_chunks + 1)]
    return [(lo, hi - lo) for lo, hi in zip(edges[:-1], edges[1:]) if hi > lo]


def _ffn_body(xp_ref, xs_ref, n0_ref, n1_ref, gf_ref, wg_hbm, wu_hbm, wd_hbm, op_ref, os_ref,
              wg_ref, wu_ref, wd_ref, stage_in, stage_out, sem, *, n_chunks, n_p, final, layer, f):
    i = pl.program_id(0)

    def fetch(w_hbm, stage, dst):
        slots, rows = stage.shape[0], stage.shape[1]
        n = dst.shape[0] // rows

        def copy(c):
            return pltpu.make_async_copy(w_hbm.at[layer, f, pl.ds(c * rows, rows), :], stage.at[c % slots],
                                         sem.at[c % slots])

        for c in range(min(slots - 1, n)):
            copy(c).start()
        for c in range(n):
            if c + slots - 1 < n:
                copy(c + slots - 1).start()
            copy(c).wait()
            dst[pl.ds(c * rows, rows), :] = stage[c % slots].astype(BF16)

    @pl.when(i == 0)
    def _():
        fetch(wg_hbm, stage_in, wg_ref)
        fetch(wu_hbm, stage_in, wu_ref)
        fetch(wd_hbm, stage_out, wd_ref)

    def apply(x_ref, o_ref):
        rows = x_ref.shape[0]
        first = -(-rows // (2 * BF16_ROWS)) * BF16_ROWS
        for lo_row, n_rows in ((0, first), (first, rows - first))[:2 if rows > first else 1]:
            rs = pl.ds(lo_row, n_rows)
            x = x_ref[rs, :]
            h = _rms(x, n0_ref[...]).astype(BF16)
            acc = jnp.zeros(x.shape, F32)
            for lo, width in _ffn_chunks(wg_ref.shape[1], n_chunks):
                sl = pl.ds(lo, width)
                g = jnp.dot(h, wg_ref[:, sl], preferred_element_type=F32)
                u = jnp.dot(h, wu_ref[:, sl], preferred_element_type=F32)
                a = (_silu(g) * u).astype(BF16)
                acc = acc + jnp.dot(a, wd_ref[sl, :], preferred_element_type=F32)
            y = x + 0.5 * _rms(acc, n1_ref[...])
            o_ref[rs, :] = _rms(y, gf_ref[...]) if final else y

    @pl.when(i == 0)
    def _():
        os_ref[...] = jnp.zeros(os_ref.shape, F32)

    @pl.when(i < n_p)
    def _():
        apply(xp_ref, op_ref)

    @pl.when(i >= n_p)
    def _():
        apply(xs_ref, os_ref)


def _ffn(xp, xs, norms, final_g, wg, wu, wd, layer, f, *, tm, n_chunks=2, final_seq=None):
    (n_rows_p, d), n_rows_s = xp.shape, xs.shape[0]
    ff = wg.shape[-1]
    nidx = (layer * 2 + f) * 2
    tm_s = _pick_tile(n_rows_s, tm, BF16_ROWS)
    if final_seq is None:
        n_p, out_rows_p = pl.cdiv(n_rows_p, tm), n_rows_p
        p_in = p_out = pl.BlockSpec((tm, d), lambda i: (jnp.minimum(i, n_p - 1), 0))
    else:
        t, skip = final_seq
        assert skip % BF16_ROWS == 0 and t % BF16_ROWS == 0
        tm = _pick_tile(t - skip, tm, BF16_ROWS)
        per_seq = (t - skip) // tm
        n_p, out_rows_p = (n_rows_p // t) * per_seq, (n_rows_p // t) * (t - skip)

        def in_rows(i):
            ip = jnp.minimum(i, n_p - 1)
            return pl.multiple_of((ip // per_seq) * t + skip + (ip % per_seq) * tm, BF16_ROWS), 0

        p_in = pl.BlockSpec((pl.Element(tm), pl.Element(d)), in_rows)
        p_out = pl.BlockSpec((tm, d), lambda i: (jnp.minimum(i, n_p - 1), 0))
    n_s = pl.cdiv(n_rows_s, tm_s)
    s_spec = pl.BlockSpec((tm_s, d), lambda i: (jnp.maximum(i - n_p, 0), 0))
    in_rows, out_rows = _pick_tile(d, WEIGHT_CHUNK_BYTES // (4 * ff)), _pick_tile(ff, WEIGHT_CHUNK_BYTES // (4 * d))
    hbm = pl.BlockSpec(memory_space=pl.ANY)
    return pl.pallas_call(
        functools.partial(_ffn_body, n_chunks=n_chunks, n_p=n_p, final=final_seq is not None, layer=layer, f=f),
        grid=(n_p + n_s,),
        in_specs=[
            p_in, s_spec,
            _const_spec((None, 1, d), (nidx, 0, 0)),
            _const_spec((None, 1, d), (nidx + 1, 0, 0)),
            _const_spec((1, d), (0, 0)),
            hbm, hbm, hbm,
        ],
        out_specs=[p_out, s_spec],
        out_shape=[jax.ShapeDtypeStruct((out_rows_p, d), F32), jax.ShapeDtypeStruct((n_rows_s, d), F32)],
        scratch_shapes=[
            pltpu.VMEM((d, ff), BF16), pltpu.VMEM((d, ff), BF16), pltpu.VMEM((ff, d), BF16),
            pltpu.VMEM((WEIGHT_SLOTS, in_rows, ff), F32), pltpu.VMEM((WEIGHT_SLOTS, out_rows, d), F32),
            pltpu.SemaphoreType.DMA((WEIGHT_SLOTS,)),
        ],
        compiler_params=_params("arbitrary"),
        name="ffn",
    )(xp, xs, norms, norms, final_g, wg, wu, wd)


def _pool_p_body(x_ref, g0_ref, g1_ref, w_ref, sc_ref, o_ref, st_ref, *lvl_refs, ts, n_t):
    t = pl.program_id(1)
    d = x_ref.shape[-1]
    n_g = len(POOL_WINDOWS)
    gd = d // n_g
    x = x_ref[0]
    h = _rms(x, g0_ref[...])
    cat_ref = lvl_refs[0]
    top = SUBLANES + POOL_HALO
    rows = POOL_HALO + ts

    @pl.when(t == 0)
    def _():
        cat_ref[0:top, :] = jnp.zeros((top, d), F32)
        for ref in lvl_refs[1:]:
            ref[0:SUBLANES, :] = jnp.zeros((SUBLANES, ref.shape[1]), F32)

    cat_ref[top:top + ts, :] = h
    for k in range(1, len(lvl_refs)):
        prev, cur = lvl_refs[k - 1], lvl_refs[k]
        lo = prev.shape[1] - cur.shape[1]
        cur[SUBLANES:SUBLANES + rows, :] = (prev[SUBLANES:SUBLANES + rows, lo:]
                                            + prev[SUBLANES - 2 ** (k - 1):SUBLANES - 2 ** (k - 1) + rows, lo:])
    pos = t * ts + lax.broadcasted_iota(jnp.int32, (ts, 1), 0)
    outs = []
    for g, w in enumerate(POOL_WINDOWS):
        c0 = g * gd
        k = w.bit_length() - 1
        if k < len(lvl_refs):
            ref = lvl_refs[k]
            l0 = c0 - (d - ref.shape[1])
            acc = ref[top:top + ts, l0:l0 + gd]
        else:
            ref = lvl_refs[k - 1]
            l0 = c0 - (d - ref.shape[1])
            acc = ref[top:top + ts, l0:l0 + gd] + ref[top - w // 2:top - w // 2 + ts, l0:l0 + gd]
        inv_cnt = 1.0 / jnp.minimum(w, pos + 1).astype(F32)
        pooled = (acc * inv_cnt - h[:, c0:c0 + gd]).astype(BF16)
        outs.append(jnp.dot(pooled, w_ref[g], preferred_element_type=F32))
    m = jnp.concatenate(outs, axis=-1) * sc_ref[...]
    o_ref[0] = x + _rms(m, g1_ref[...])

    @pl.when(t == n_t - 1)
    def _():
        st_ref[0] = cat_ref[top + ts - POOL_BUF:top + ts, :]

    cat_ref[SUBLANES:top, :] = cat_ref[SUBLANES + ts:top + ts, :]


def _pool_prompt(x, mix_norm, pool_w, pool_scale, layer, j, *, ts):
    b, t, d = x.shape
    n_t = t // ts
    assert n_t * ts == t and ts % 8 == 0 and ts >= POOL_HALO
    g = len(POOL_WINDOWS)
    assert all(w == 2 ** (i + 1) for i, w in enumerate(POOL_WINDOWS)) and POOL_WINDOWS[-1] // 2 == SUBLANES
    buf_rows = SUBLANES + POOL_HALO + ts
    level_lanes = [d] + [d - i * (d // g) for i in range(g - 1)]
    return pl.pallas_call(
        functools.partial(_pool_p_body, ts=ts, n_t=n_t),
        grid=(b, n_t),
        in_specs=[
            pl.BlockSpec((1, ts, d), lambda bi, ti: (bi, ti, 0)),
            _const_spec((None, 1, d), (layer * 2, 0, 0)),
            _const_spec((None, 1, d), (layer * 2 + 1, 0, 0)),
            _const_spec((None, g, d // g, d // g), (j, 0, 0, 0)),
            _const_spec((None, 1, d), (j, 0, 0)),
        ],
        out_specs=[
            pl.BlockSpec((1, ts, d), lambda bi, ti: (bi, ti, 0)),
            pl.BlockSpec((1, POOL_BUF, d), lambda bi, ti: (bi, 0, 0)),
        ],
        out_shape=[jax.ShapeDtypeStruct((b, t, d), F32), jax.ShapeDtypeStruct((b, POOL_BUF, d), F32)],
        scratch_shapes=[pltpu.VMEM((buf_rows, lanes), F32) for lanes in level_lanes],
        compiler_params=_params("parallel", "arbitrary"),
        name="pool_prompt",
    )(x, mix_norm, mix_norm, pool_w, pool_scale)


def _pool_s_body(x_ref, st_ref, g0_ref, g1_ref, w_ref, sc_ref, o_ref, so_ref, cat_ref, *, n_s, past):
    d = g0_ref.shape[-1]
    gd = d // len(POOL_WINDOWS)
    bt = st_ref.shape[0]
    top = POOL_HALO
    x = x_ref[...]
    h = _rms(x, g0_ref[...])
    h3 = h.reshape(bt, n_s, d)
    cat_ref[:, 0:top - POOL_BUF, :] = jnp.zeros((bt, top - POOL_BUF, d), F32)
    cat_ref[:, top - POOL_BUF:top, :] = st_ref[...]
    cat_ref[:, top:top + n_s, :] = h3
    pos = past + lax.broadcasted_iota(jnp.int32, (1, n_s, 1), 1)
    outs = []
    for g, w in enumerate(POOL_WINDOWS):
        c0 = g * gd
        acc = h3[:, :, c0:c0 + gd]
        for jj in range(1, w):
            acc = acc + cat_ref[:, top - jj:top - jj + n_s, c0:c0 + gd]
        inv_cnt = 1.0 / jnp.minimum(w, pos + 1).astype(F32)
        pooled = (acc * inv_cnt - h3[:, :, c0:c0 + gd]).reshape(bt * n_s, gd).astype(BF16)
        outs.append(jnp.dot(pooled, w_ref[g], preferred_element_type=F32))
    m = jnp.concatenate(outs, axis=-1) * sc_ref[...]
    o_ref[...] = x + _rms(m, g1_ref[...])
    so_ref[...] = cat_ref[:, top + n_s - POOL_BUF:top + n_s, :]


def _pool_sample(x, state, mix_norm, pool_w, pool_scale, layer, j, *, n_s, past, bt):
    db = x.shape[0] // n_s
    d = mix_norm.shape[-1]
    g = len(POOL_WINDOWS)
    assert n_s % SUBLANES == 0
    return pl.pallas_call(
        functools.partial(_pool_s_body, n_s=n_s, past=past),
        grid=(db // bt,),
        in_specs=[
            pl.BlockSpec((bt * n_s, d), lambda i: (i, 0)),
            pl.BlockSpec((None, bt, POOL_BUF, d), lambda i: (j, i, 0, 0)),
            _const_spec((None, 1, d), (layer * 2, 0, 0)),
            _const_spec((None, 1, d), (layer * 2 + 1, 0, 0)),
            _const_spec((None, g, d // g, d // g), (j, 0, 0, 0)),
            _const_spec((None, 1, d), (j, 0, 0)),
        ],
        out_specs=[
            pl.BlockSpec((bt * n_s, d), lambda i: (i, 0)),
            pl.BlockSpec((bt, POOL_BUF, d), lambda i: (i, 0, 0)),
        ],
        out_shape=[jax.ShapeDtypeStruct((db * n_s, d), F32), jax.ShapeDtypeStruct((db, POOL_BUF, d), F32)],
        scratch_shapes=[pltpu.VMEM((bt, POOL_HALO + n_s, d), F32)],
        compiler_params=_params("parallel"),
        name="pool_sample",
    )(x, state, mix_norm, mix_norm, pool_w, pool_scale)


def _layer_norm(y, g, b):
    mu = jnp.mean(y, axis=-1, keepdims=True)
    yc = y - mu
    var = jnp.mean(yc * yc, axis=-1, keepdims=True)
    return yc * lax.rsqrt(var + LN_EPS) * g + b


def _conv_p_body(x_ref, g0_ref, g1_ref, w1_ref, b1_ref, wdw_ref, bdw_ref, lng_ref, lnb_ref, w2_ref, b2_ref,
                 o_ref, st_ref, cat_ref, y_ref, *, ts, n_t):
    t = pl.program_id(1)
    d = x_ref.shape[-1]
    first = -(-ts // (2 * BF16_ROWS)) * BF16_ROWS
    parts = ((0, first), (first, ts - first))[:2 if ts > first else 1]

    @pl.when(t == 0)
    def _():
        cat_ref[0:CONV_HALO, :] = jnp.zeros((CONV_HALO, d), F32)

    for lo, n in parts:
        h = _rms(x_ref[0, lo:lo + n, :], g0_ref[...]).astype(BF16)
        glu = jnp.dot(h, w1_ref[...], preferred_element_type=F32) + b1_ref[...]
        cat_ref[CONV_HALO + lo:CONV_HALO + lo + n, :] = glu[:, :d] * jax.nn.sigmoid(glu[:, d:])

    off = CONV_HALO - CONV_BUF
    for lo, n in parts:
        for c0 in range(0, d, LANES):
            y = None
            for r in range(SUBLANES):
                zrows = n + (SUBLANES if r else 0)
                z = None
                for a in range((CONV_WIDTH + off) // SUBLANES + 1):
                    k = SUBLANES * a + r - off
                    if 0 <= k < CONV_WIDTH:
                        r0 = lo + SUBLANES * a
                        term = cat_ref[r0:r0 + zrows, c0:c0 + LANES] * wdw_ref[k:k + 1, c0:c0 + LANES]
                        z = term if z is None else z + term
                zs = z[r:r + n]
                y = zs if y is None else y + zs
            y_ref[lo:lo + n, c0:c0 + LANES] = y + bdw_ref[:, c0:c0 + LANES]
        a = _silu(_layer_norm(y_ref[lo:lo + n, :], lng_ref[...], lnb_ref[...])).astype(BF16)
        m = jnp.dot(a, w2_ref[...], preferred_element_type=F32) + b2_ref[...]
        o_ref[0, lo:lo + n, :] = x_ref[0, lo:lo + n, :] + _rms(m, g1_ref[...])

    @pl.when(t == n_t - 1)
    def _():
        st_ref[0] = cat_ref[CONV_HALO + ts - CONV_BUF:CONV_HALO + ts, :]

    cat_ref[0:CONV_HALO, :] = cat_ref[ts:ts + CONV_HALO, :]


def _conv_specs(layer, j, d):
    return [
        _const_spec((None, 1, d), (layer * 2, 0, 0)),
        _const_spec((None, 1, d), (layer * 2 + 1, 0, 0)),
        _const_spec((None, d, 2 * d), (j, 0, 0)),
        _const_spec((None, 1, 2 * d), (j, 0, 0)),
        _const_spec((None, CONV_WIDTH, d), (j, 0, 0)),
        _const_spec((None, 1, d), (j, 0, 0)),
        _const_spec((None, 1, d), (j, 0, 0)),
        _const_spec((None, 1, d), (j, 0, 0)),
        _const_spec((None, d, d), (j, 0, 0)),
        _const_spec((None, 1, d), (j, 0, 0)),
    ]


def _conv_prompt(x, mix_norm, cw, layer, j, *, ts):
    b, t, d = x.shape
    n_t = t // ts
    assert n_t * ts == t and ts % 8 == 0 and ts >= CONV_HALO
    return pl.pallas_call(
        functools.partial(_conv_p_body, ts=ts, n_t=n_t),
        grid=(b, n_t),
        in_specs=[pl.BlockSpec((1, ts, d), lambda bi, ti: (bi, ti, 0))] + _conv_specs(layer, j, d),
        out_specs=[
            pl.BlockSpec((1, ts, d), lambda bi, ti: (bi, ti, 0)),
            pl.BlockSpec((1, CONV_BUF, d), lambda bi, ti: (bi, 0, 0)),
        ],
        out_shape=[jax.ShapeDtypeStruct((b, t, d), F32), jax.ShapeDtypeStruct((b, CONV_BUF, d), F32)],
        scratch_shapes=[pltpu.VMEM((CONV_HALO + ts, d), F32), pltpu.VMEM((ts, d), F32)],
        compiler_params=_params("parallel", "arbitrary"),
        name="conv_prompt",
    )(x, mix_norm, mix_norm, *cw)


def _conv_s_body(x_ref, st_ref, g0_ref, g1_ref, w1_ref, b1_ref, wdw_ref, bdw_ref, lng_ref, lnb_ref, w2_ref, b2_ref,
                 o_ref, so_ref, cat_ref, *, n_s):
    d = g0_ref.shape[-1]
    bt = st_ref.shape[0]
    top = CONV_HALO
    x = x_ref[...]
    h = _rms(x, g0_ref[...]).astype(BF16)
    z = jnp.dot(h, w1_ref[...], preferred_element_type=F32) + b1_ref[...]
    u = z[:, :d] * jax.nn.sigmoid(z[:, d:])
    cat_ref[:, 0:top - CONV_BUF, :] = jnp.zeros((bt, top - CONV_BUF, d), F32)
    cat_ref[:, top - CONV_BUF:top, :] = st_ref[...]
    cat_ref[:, top:top + n_s, :] = u.reshape(bt, n_s, d)
    off = top - CONV_BUF
    y = cat_ref[:, off:off + n_s, :] * wdw_ref[0:1, :] + bdw_ref[...]
    for k in range(1, CONV_WIDTH):
        y = y + cat_ref[:, off + k:off + k + n_s, :] * wdw_ref[k:k + 1, :]
    a = _silu(_layer_norm(y, lng_ref[...], lnb_ref[...])).reshape(bt * n_s, d).astype(BF16)
    m = jnp.dot(a, w2_ref[...], preferred_element_type=F32) + b2_ref[...]
    o_ref[...] = x + _rms(m, g1_ref[...])
    so_ref[...] = cat_ref[:, top + n_s - CONV_BUF:top + n_s, :]


def _conv_sample(x, state, mix_norm, cw, layer, j, *, n_s, bt):
    db = x.shape[0] // n_s
    d = mix_norm.shape[-1]
    assert n_s % SUBLANES == 0
    return pl.pallas_call(
        functools.partial(_conv_s_body, n_s=n_s),
        grid=(db // bt,),
        in_specs=[
            pl.BlockSpec((bt * n_s, d), lambda i: (i, 0)),
            pl.BlockSpec((None, bt, CONV_BUF, d), lambda i: (j, i, 0, 0)),
        ] + _conv_specs(layer, j, d),
        out_specs=[
            pl.BlockSpec((bt * n_s, d), lambda i: (i, 0)),
            pl.BlockSpec((bt, CONV_BUF, d), lambda i: (i, 0, 0)),
        ],
        out_shape=[jax.ShapeDtypeStruct((db * n_s, d), F32), jax.ShapeDtypeStruct((db, CONV_BUF, d), F32)],
        scratch_shapes=[pltpu.VMEM((bt, CONV_HALO + n_s, d), F32)],
        compiler_params=_params("parallel"),
        name="conv_sample",
    )(x, state, mix_norm, mix_norm, *cw)


def _qkv_body(x_ref, g0_ref, w_ref, q_ref, k_ref, v_ref, *kv_bf16_refs, decode):
    rows, d = x_ref.shape
    h = _rms(x_ref[...], g0_ref[...]).astype(BF16)
    qkv = jnp.dot(h, w_ref[...], preferred_element_type=F32)
    q = qkv[:, :d] * (HEAD_DIM ** -0.5 * LOG2E)
    k = qkv[:, d:2 * d]
    v = qkv[:, 2 * d:]
    k_ref[...] = k.reshape(rows, N_HEADS, V_DIM)
    v_ref[...] = v.reshape(rows, N_HEADS, V_DIM)
    if decode:
        q_ref[...] = q.reshape(rows, N_HEADS, V_DIM)
    else:
        kb_ref, vb_ref = kv_bf16_refs
        q_ref[...] = q.astype(BF16)
        kb_ref[...] = k.astype(BF16)
        vb_ref[...] = v.astype(BF16)


def _qkv(x, mix_norm, wqkv, layer, j, *, tm, decode):
    n, d = x.shape
    row = pl.BlockSpec((tm, d), lambda i: (i, 0))
    per_head = pl.BlockSpec((tm, N_HEADS, V_DIM), lambda i: (i, 0, 0))
    head_shape = jax.ShapeDtypeStruct((n, N_HEADS, V_DIM), F32)
    flat_bf16 = jax.ShapeDtypeStruct((n, d), BF16)
    return pl.pallas_call(
        functools.partial(_qkv_body, decode=decode),
        grid=(pl.cdiv(n, tm),),
        in_specs=[row, _const_spec((None, 1, d), (layer * 2, 0, 0)), _const_spec((None, d, 3 * d), (j, 0, 0))],
        out_specs=[per_head] * 3 if decode else [row, per_head, per_head, row, row],
        out_shape=[head_shape] * 3 if decode else [flat_bf16, head_shape, head_shape, flat_bf16, flat_bf16],
        compiler_params=_params("parallel"),
        name="qkv_proj",
    )(x, mix_norm, wqkv)


def _bucket(n):
    nf = jnp.maximum(n, 1).astype(F32)
    large = MAX_EXACT + (jnp.log(nf / MAX_EXACT) / math.log(MAX_DISTANCE / MAX_EXACT)
                         * (N_BUCKETS - MAX_EXACT)).astype(jnp.int32)
    large = jnp.minimum(large, N_BUCKETS - 1)
    return jnp.where(n < MAX_EXACT, n, large)


def _lookup(bucket, entry):
    out = jnp.zeros(bucket.shape, F32)
    for b in range(N_BUCKETS):
        out = jnp.where(bucket == b, entry(b), out)
    return out


def _bias_p_body(table_ref, bp_ref, *, blk):
    head = pl.program_id(0)
    a = lax.broadcasted_iota(jnp.int32, (blk, blk), 0)
    b = lax.broadcasted_iota(jnp.int32, (blk, blk), 1)
    for sel in range(3):
        n = sel * blk + b - a
        vals = _lookup(_bucket(jnp.maximum(n, 0)), lambda bb: table_ref[bb * N_HEADS + head])
        bp_ref[0, sel] = jnp.where(n >= 0, vals * LOG2E, NEG_INF)


def _bias_s_body(tt_ref, bs_ref, *, n_s):
    rows = 2 * n_s * N_HEADS
    cols = PAGE_SIZE * N_HEADS
    tt = tt_ref[...]
    trow = jnp.broadcast_to(tt[None], (rows // N_HEADS, N_HEADS, N_BUCKETS)).reshape(rows, N_BUCKETS)

    def tile(width, dist):
        r = lax.broadcasted_iota(jnp.int32, (rows, width), 0)
        c = lax.broadcasted_iota(jnp.int32, (rows, width), 1)
        n = dist((r // N_HEADS) % n_s, c // N_HEADS)
        vals = _lookup(_bucket(jnp.maximum(n, 0)), lambda bb: trow[:, bb:bb + 1])
        return jnp.where((r % N_HEADS == c % N_HEADS) & (n >= 0) & (c < cols), vals * LOG2E, NEG_INF)

    bs_ref[:, 0:cols] = tile(cols, lambda qi, kk: 2 * PAGE_SIZE + qi - kk)
    bs_ref[:, cols:2 * cols] = tile(cols, lambda qi, kk: PAGE_SIZE + qi - kk)
    bs_ref[:, 2 * cols:2 * cols + PAGE_SIZE] = tile(
        PAGE_SIZE, lambda qi, kk: jnp.where(kk < n_s, qi - kk, -1))


def _bias_tiles(table, *, blk, n_s):
    far = _first_far_distance()
    assert blk + 1 >= far and PAGE_SIZE + 1 >= far and n_s * N_HEADS <= PAGE_SIZE
    bias_p = pl.pallas_call(
        functools.partial(_bias_p_body, blk=blk),
        grid=(N_HEADS,),
        in_specs=[pl.BlockSpec(memory_space=pltpu.SMEM)],
        out_specs=pl.BlockSpec((1, 3, blk, blk), lambda h: (h, 0, 0, 0)),
        out_shape=jax.ShapeDtypeStruct((N_HEADS, 3, blk, blk), F32),
        compiler_params=_params("parallel"),
        name="rel_bias_prompt",
    )(table.reshape(-1))
    width = 2 * PAGE_SIZE * N_HEADS + PAGE_SIZE
    bias_s = pl.pallas_call(
        functools.partial(_bias_s_body, n_s=n_s),
        out_shape=jax.ShapeDtypeStruct((2 * n_s * N_HEADS, width), F32),
        compiler_params=pltpu.CompilerParams(vmem_limit_bytes=VMEM_LIMIT),
        name="rel_bias_sample",
    )(table.T)
    return bias_p, bias_s


def _diff_lambda(lam_ref, lam_init):
    lp = lam_ref[...]
    s1 = jnp.sum(lp[0:1] * lp[1:2], axis=-1, keepdims=True)
    s2 = jnp.sum(lp[2:3] * lp[3:4], axis=-1, keepdims=True)
    return jnp.exp(s1) - jnp.exp(s2) + lam_init


_NT = (((1,), (1,)), ((), ()))


def _attn_p_body(q_ref, k_ref, v_ref, x_ref, bias_ref, lam_ref, sub_ref, wo_ref, g1_ref, o_ref,
                 qz_ref, vt_ref, m_ref, acc_ref, s_ref, p_ref, al_ref, a_ref, *, blk, n_full, tail, lam_init):
    i = pl.program_id(1)

    @pl.when(i == 0)
    def _():
        for hd in range(N_HEADS):
            c0 = hd * V_DIM

            ones_row = (lax.broadcasted_iota(jnp.int32, (SUM_ROWS, blk), 0) == 0).astype(BF16)

            def xpose(jb, carry):
                r0 = pl.multiple_of(jb * blk, blk)
                vt_ref[hd, jb, 0:V_DIM, :] = v_ref[0, pl.ds(r0, blk), c0:c0 + V_DIM].T
                vt_ref[hd, jb, V_DIM:V_DIM + SUM_ROWS, :] = ones_row
                return carry

            lax.fori_loop(0, n_full, xpose, 0)
            if tail:
                r0 = n_full * blk
                vt_ref[hd, n_full, 0:V_DIM, 0:tail] = (
                    v_ref[0, r0:r0 + tail, c0:c0 + V_DIM].astype(F32).T.astype(BF16))
                vt_ref[hd, n_full, V_DIM:V_DIM + SUM_ROWS, :] = ones_row

    def run(wq, n_loop, tail_keys):
        lane = lax.broadcasted_iota(jnp.int32, (wq, V_DIM), 1)
        for hd in range(N_HEADS):
            q = q_ref[0, 0:wq, hd * V_DIM:(hd + 1) * V_DIM]
            qz_ref[hd, 0:wq, :] = jnp.where(lane < HEAD_DIM, q, jnp.zeros_like(q))
            qz_ref[hd, wq:2 * wq, :] = jnp.where(lane >= HEAD_DIM, q, jnp.zeros_like(q))
        m_ref[:, :, 0:2 * wq] = jnp.full((N_HEADS, 1, 2 * wq), NEG_INF, F32)
        acc_ref[:, :, 0:2 * wq] = jnp.zeros((N_HEADS, V_DIM + SUM_ROWS, 2 * wq), F32)

        def all_heads(n, keys, vt, bias):
            def stage(hd):
                b2 = bias(hd)
                s_ref[hd % 2, 0:n, 0:2 * wq] = (
                    lax.dot_general(keys(hd), qz_ref[hd, 0:2 * wq, :], _NT, preferred_element_type=F32)
                    + jnp.concatenate([b2, b2], axis=1))

            def softmax(hd):
                s = s_ref[hd % 2, 0:n, 0:2 * wq]
                m_old = m_ref[hd, :, 0:2 * wq]
                m_new = jnp.maximum(m_old, jnp.max(s, axis=0, keepdims=True))
                al_ref[hd % 2, :, 0:2 * wq] = jnp.exp2(m_old - m_new)
                p_ref[hd % 2, 0:n, 0:2 * wq] = jnp.exp2(s - m_new).astype(BF16)
                m_ref[hd, :, 0:2 * wq] = m_new

            def accumulate(hd):
                acc_ref[hd, :, 0:2 * wq] = (
                    al_ref[hd % 2, :, 0:2 * wq] * acc_ref[hd, :, 0:2 * wq]
                    + jnp.dot(vt(hd), p_ref[hd % 2, 0:n, 0:2 * wq], preferred_element_type=F32))

            stage(0)
            for hd in range(N_HEADS + 1):
                if hd + 1 < N_HEADS:
                    stage(hd + 1)
                if hd < N_HEADS:
                    softmax(hd)
                if hd >= 1:
                    accumulate(hd - 1)

        def body(jb, carry):
            r0 = pl.multiple_of(jb * blk, blk)
            sel = jnp.minimum(i - jb, 2)
            all_heads(blk,
                      lambda hd: k_ref[0, pl.ds(r0, blk), hd * V_DIM:(hd + 1) * V_DIM],
                      lambda hd: vt_ref[hd, jb],
                      lambda hd: bias_ref[hd, sel, :, 0:wq])
            return carry

        lax.fori_loop(0, n_loop, body, 0)
        if tail_keys:
            r0 = n_full * blk
            all_heads(tail_keys,
                      lambda hd: k_ref[0, r0:r0 + tail_keys, hd * V_DIM:(hd + 1) * V_DIM],
                      lambda hd: vt_ref[hd, n_full, :, 0:tail_keys],
                      lambda hd: bias_ref[hd, 0, 0:tail_keys, 0:wq])

        lam = _diff_lambda(lam_ref, lam_init)
        for hd in range(N_HEADS):
            o_both = acc_ref[hd, 0:V_DIM, 0:2 * wq] * (1.0 / acc_ref[hd, V_DIM:V_DIM + 1, 0:2 * wq])
            o = o_both[:, 0:wq] - lam * o_both[:, wq:2 * wq]
            o = o * lax.rsqrt(jnp.mean(o * o, axis=0, keepdims=True) + RMS_EPS) * sub_ref[...] * (1.0 - lam_init)
            a_ref[0:wq, hd * V_DIM:(hd + 1) * V_DIM] = o.T.astype(BF16)
        mix = jnp.dot(a_ref[0:wq, :], wo_ref[...], preferred_element_type=F32)
        o_ref[0, 0:wq, :] = x_ref[0, 0:wq, :] + _rms(mix, g1_ref[...])

    @pl.when(i < n_full)
    def _():
        run(blk, i + 1, 0)

    if tail:
        @pl.when(i == n_full)
        def _():
            run(-(-tail // LANES) * LANES, n_full, tail)


def _attn_prompt(q, kb, vb, x, bias_p, lam_p, subln_col, wo, mix_norm, layer, j, *, blk, lam_init):
    b, t, d = q.shape
    n_full, tail = t // blk, t % blk
    assert tail % 16 == 0
    n_blocks = n_full + (1 if tail else 0)
    seq_blk = pl.BlockSpec((1, blk, d), lambda bi, qi: (bi, qi, 0))
    return pl.pallas_call(
        functools.partial(_attn_p_body, blk=blk, n_full=n_full, tail=tail, lam_init=lam_init),
        grid=(b, n_blocks),
        in_specs=[
            seq_blk,
            pl.BlockSpec((1, t, d), lambda bi, qi: (bi, 0, 0)),
            pl.BlockSpec((1, t, d), lambda bi, qi: (bi, 0, 0)),
            seq_blk,
            _const_spec((N_HEADS, 3, blk, blk), (0, 0, 0, 0)),
            _const_spec((None, 4, HEAD_DIM), (j, 0, 0)),
            _const_spec((None, V_DIM, 1), (j, 0, 0)),
            _const_spec((None, d, d), (j, 0, 0)),
            _const_spec((None, 1, d), (layer * 2 + 1, 0, 0)),
        ],
        out_specs=seq_blk,
        out_shape=jax.ShapeDtypeStruct((b, t, d), F32),
        scratch_shapes=[
            pltpu.VMEM((N_HEADS, 2 * blk, V_DIM), BF16),
            pltpu.VMEM((N_HEADS, n_blocks, V_DIM + SUM_ROWS, blk), BF16),
            pltpu.VMEM((N_HEADS, 1, 2 * blk), F32),
            pltpu.VMEM((N_HEADS, V_DIM + SUM_ROWS, 2 * blk), F32),
            pltpu.VMEM((2, blk, 2 * blk), F32),
            pltpu.VMEM((2, blk, 2 * blk), BF16),
            pltpu.VMEM((2, 1, 2 * blk), F32),
            pltpu.VMEM((blk, d), BF16),
        ],
        compiler_params=_params("arbitrary", "arbitrary"),
        name="attn_prompt",
    )(q, kb, vb, x, bias_p, lam_p, subln_col, wo, mix_norm)


def _attn_s_body(pt_ref, q_ref, kn_ref, vn_ref, bias_ref, lam_ref, sub_ref, *rest, n_pages, n_s, lam_init):
    k_pages = rest[:n_pages]
    v_pages = rest[n_pages:2 * n_pages]
    o_ref, s_ref = rest[2 * n_pages:]
    half = n_s * N_HEADS
    cols = PAGE_SIZE * N_HEADS

    q = q_ref[...].reshape(half, V_DIM)
    lane = lax.broadcasted_iota(jnp.int32, (half, V_DIM), 1)
    q2 = jnp.concatenate([jnp.where(lane < HEAD_DIM, q, 0.0), jnp.where(lane >= HEAD_DIM, q, 0.0)], axis=0).astype(BF16)

    def scores(keys, bias):
        return lax.dot_general(q2, keys, _NT, preferred_element_type=F32) + bias

    pad = jnp.zeros((PAGE_SIZE - half, V_DIM), F32)
    kn = jnp.concatenate([kn_ref[...].reshape(half, V_DIM), pad], axis=0).astype(BF16)
    vn = jnp.concatenate([vn_ref[...].reshape(half, V_DIM), pad], axis=0).astype(BF16)
    s_new = scores(kn, bias_ref[:, 2 * cols:2 * cols + PAGE_SIZE])
    m = jnp.max(s_new, axis=-1, keepdims=True)
    for r_ in range(n_pages):
        near = r_ == n_pages - 1
        s = scores(k_pages[r_][...].reshape(cols, V_DIM).astype(BF16),
                   bias_ref[:, cols:2 * cols] if near else bias_ref[:, 0:cols])
        s_ref[r_] = s
        m = jnp.maximum(m, jnp.max(s, axis=-1, keepdims=True))

    p = jnp.exp2(s_new - m)
    l = jnp.sum(p, axis=-1, keepdims=True)
    acc = jnp.dot(p.astype(BF16), vn, preferred_element_type=F32)
    for r_ in range(n_pages):
        p = jnp.exp2(s_ref[r_] - m)
        l = l + jnp.sum(p, axis=-1, keepdims=True)
        acc = acc + jnp.dot(p.astype(BF16), v_pages[r_][...].reshape(cols, V_DIM).astype(BF16),
                            preferred_element_type=F32)

    lam = _diff_lambda(lam_ref, lam_init)
    o_all = acc * (1.0 / l)
    o = o_all[0:half] - lam * o_all[half:2 * half]
    o_ref[...] = (_rms(o, sub_ref[...]) * (1.0 - lam_init)).reshape(o_ref.shape)


def _attn_sample(q, kn, vn, cache_k, cache_v, page_table, bias_s, lam_p, subln, j, *, n_pages, lam_init):
    db, n_s = q.shape[:2]
    assert n_pages == page_table.shape[1]
    rows = 2 * n_s * N_HEADS
    new_kv = pl.BlockSpec((1, n_s, 1, N_HEADS, V_DIM), lambda b, pt: (b, 0, 0, 0, 0))
    q_spec = pl.BlockSpec((1, n_s, N_HEADS, V_DIM), lambda b, pt: (b, 0, 0, 0))

    def page_spec(r):
        return pl.BlockSpec((1, PAGE_SIZE, 1, N_HEADS, V_DIM), lambda b, pt: (pt[b, r], 0, j, 0, 0))

    grid_spec = pltpu.PrefetchScalarGridSpec(
        num_scalar_prefetch=1,
        grid=(db,),
        in_specs=[
            q_spec, new_kv, new_kv,
            pl.BlockSpec(bias_s.shape, lambda b, pt: (0, 0)),
            pl.BlockSpec((None, 4, HEAD_DIM), lambda b, pt: (j, 0, 0)),
            pl.BlockSpec((None, 1, V_DIM), lambda b, pt: (j, 0, 0)),
        ] + [page_spec(r) for r in range(n_pages)] * 2,
        out_specs=q_spec,
        scratch_shapes=[pltpu.VMEM((n_pages, rows, PAGE_SIZE * N_HEADS), F32)],
    )
    return pl.pallas_call(
        functools.partial(_attn_s_body, n_pages=n_pages, n_s=n_s, lam_init=lam_init),
        grid_spec=grid_spec,
        out_shape=jax.ShapeDtypeStruct((db, n_s, N_HEADS, V_DIM), F32),
        compiler_params=_params("parallel"),
        name="attn_sample",
    )(page_table, q, kn, vn, bias_s, lam_p, subln, *([cache_k] * n_pages), *([cache_v] * n_pages))


def _outproj_body(a_ref, x_ref, w_ref, g1_ref, o_ref):
    a = a_ref[...].reshape(x_ref.shape)
    m = jnp.dot(a.astype(BF16), w_ref[...], preferred_element_type=F32)
    o_ref[...] = x_ref[...] + _rms(m, g1_ref[...])


def _outproj(a, x, wo, mix_norm, layer, j, *, tm):
    n, d = x.shape
    row = pl.BlockSpec((tm, d), lambda i: (i, 0))
    return pl.pallas_call(
        _outproj_body,
        grid=(pl.cdiv(n, tm),),
        in_specs=[pl.BlockSpec((tm,) + a.shape[1:], lambda i: (i, 0, 0)), row,
                  _const_spec((None, d, d), (j, 0, 0)), _const_spec((None, 1, d), (layer * 2 + 1, 0, 0))],
        out_specs=row,
        out_shape=jax.ShapeDtypeStruct((n, d), F32),
        compiler_params=_params("parallel"),
        name="attn_outproj",
    )(a, x, wo, mix_norm)


def _pick_tile(n, cap, align=8):
    best = None
    for c in range(align, min(n, cap) + 1, align):
        if n % c == 0:
            best = c
    assert best is not None, (n, cap, align)
    return best


def kernel(x_prompt, x_sample, state_pool, state_conv, cache_k, cache_v, page_table, meta_tokens, rel_bias_table, ffn_norm, ffn_wg, ffn_wu, ffn_wd, mix_norm, pool_w, pool_scale, attn_wqkv, attn_wo, attn_lambda, attn_subln, conv_w1, conv_b1, conv_wdw, conv_bdw, conv_ln_g, conv_ln_b, conv_w2, conv_b2, final_norm):
    b, seq, d = x_prompt.shape
    db, n_s, _ = x_sample.shape
    depth = ffn_wg.shape[0]
    t = seq + N_META
    past = page_table.shape[1] * PAGE_SIZE

    wg, wu, wd = ffn_wg, ffn_wu, ffn_wd
    ffn_norm3 = ffn_norm.reshape(-1, 1, d)
    mix_norm3 = mix_norm.reshape(-1, 1, d)
    pool_w_b = pool_w.astype(BF16)
    pool_scale3 = pool_scale.reshape(-1, 1, d)
    wqkv_b, wo_b = attn_wqkv.astype(BF16), attn_wo.astype(BF16)
    subln3 = attn_subln.reshape(-1, 1, V_DIM)
    cw = (conv_w1.astype(BF16), conv_b1.reshape(-1, 1, 2 * d), conv_wdw, conv_bdw.reshape(-1, 1, d),
          conv_ln_g.reshape(-1, 1, d), conv_ln_b.reshape(-1, 1, d), conv_w2.astype(BF16), conv_b2.reshape(-1, 1, d))
    final_g = final_norm.reshape(1, d)

    meta = jnp.broadcast_to(meta_tokens[None].astype(x_prompt.dtype), (b, N_META, d))
    xp = jnp.concatenate([meta, x_prompt], axis=1).reshape(b * t, d)
    xs = x_sample.reshape(db * n_s, d)

    tm_p = _pick_tile(b * t, 768, 16)
    tm_s = min(512, db * n_s)
    ts_pool = _pick_tile(t, 1032)
    ts_conv = _pick_tile(t, 688, BF16_ROWS)
    bt = _pick_tile(db, 32)

    bias_p = bias_s = None
    pool_p, pool_s, conv_p, conv_s = [], [], [], []
    k_p = v_p = k_s = v_s = None
    for i in range(depth):
        kind, j = i % N_MIXERS, i // N_MIXERS
        xp, xs = _ffn(xp, xs, ffn_norm3, final_g, wg, wu, wd, i, 0, tm=tm_p)
        if kind == 0:
            xp3, st = _pool_prompt(xp.reshape(b, t, d), mix_norm3, pool_w_b, pool_scale3, i, j, ts=ts_pool)
            xp = xp3.reshape(b * t, d)
            pool_p.append(st)
            xs, st = _pool_sample(xs, state_pool, mix_norm3, pool_w_b, pool_scale3, i, j, n_s=n_s, past=past, bt=bt)
            pool_s.append(st)
        elif kind == 1:
            assert j == 0
            lam_init = _lambda_init(i)
            if bias_p is None:
                bias_p, bias_s = _bias_tiles(rel_bias_table, blk=ATTN_BLOCK, n_s=n_s)
            q, k_p, v_p, kb, vb = _qkv(xp, mix_norm3, wqkv_b, i, j, tm=tm_p, decode=False)
            xp = _attn_prompt(q.reshape(b, t, d), kb.reshape(b, t, d), vb.reshape(b, t, d), xp.reshape(b, t, d),
                              bias_p, attn_lambda, attn_subln.reshape(-1, V_DIM, 1), wo_b, mix_norm3, i, j,
                              blk=ATTN_BLOCK, lam_init=lam_init).reshape(b * t, d)
            q, k_s, v_s = _qkv(xs, mix_norm3, wqkv_b, i, j, tm=tm_s, decode=True)
            k_s = k_s.reshape(db, n_s, 1, N_HEADS, V_DIM)
            v_s = v_s.reshape(db, n_s, 1, N_HEADS, V_DIM)
            a = _attn_sample(q.reshape(db, n_s, N_HEADS, V_DIM), k_s, v_s, cache_k, cache_v,
                             page_table, bias_s, attn_lambda, subln3, j, n_pages=page_table.shape[1], lam_init=lam_init)
            xs = _outproj(a.reshape(db * n_s, N_HEADS, V_DIM), xs, wo_b, mix_norm3, i, j, tm=tm_s)
        else:
            xp3, st = _conv_prompt(xp.reshape(b, t, d), mix_norm3, cw, i, j, ts=ts_conv)
            xp = xp3.reshape(b * t, d)
            conv_p.append(st)
            xs, st = _conv_sample(xs, state_conv, mix_norm3, cw, i, j, n_s=n_s, bt=bt)
            conv_s.append(st)
        xp, xs = _ffn(xp, xs, ffn_norm3, final_g, wg, wu, wd, i, 1, tm=tm_p,
                      final_seq=(t, N_META) if i == depth - 1 else None)

    kv_shape_p = (b, t, 1, N_HEADS, V_DIM)
    return (xp.reshape(b, seq, d), xs.reshape(db, n_s, d), jnp.stack(pool_p, axis=0), jnp.stack(pool_s, axis=0),
            jnp.stack(conv_p, axis=0), jnp.stack(conv_s, axis=0),
            k_p.reshape(kv_shape_p), v_p.reshape(kv_shape_p), k_s, v_s)
```

```python
import functools
import math

import jax
import jax.numpy as jnp
from jax import lax
from jax.experimental import pallas as pl
from jax.experimental.pallas import tpu as pltpu

F32 = jnp.float32
BF16 = jnp.bfloat16

N_MIXERS = 3
N_META = 16
N_HEADS = 8
HEAD_DIM = 64
V_DIM = 2 * HEAD_DIM
N_BUCKETS = 32
MAX_EXACT = N_BUCKETS // 2
MAX_DISTANCE = 128
POOL_WINDOWS = (2, 4, 8, 16)
POOL_BUF = max(POOL_WINDOWS) - 1
CONV_WIDTH = 31
CONV_BUF = CONV_WIDTH - 1
PAGE_SIZE = 128
RMS_EPS = 1e-6
LN_EPS = 1e-5
NEG_INF = -1e30
LOG2E = math.log2(math.e)
SUBLANES = 8
LANES = 128
BF16_ROWS = 16
MXU_TILE = 256
SUM_ROWS = 16
ATTN_BLOCK = 256
POOL_HALO = 16
CONV_HALO = 32
VMEM_LIMIT = 56 * 1024 * 1024
WEIGHT_CHUNK_BYTES = 3 * 256 * 1024
WEIGHT_SLOTS = 4


def _lambda_init(layer_idx):
    return 0.8 - 0.6 * math.exp(-0.3 * layer_idx)


def _first_far_distance():
    n = MAX_EXACT
    while MAX_EXACT + int(math.log(n / MAX_EXACT) / math.log(MAX_DISTANCE / MAX_EXACT) * (N_BUCKETS - MAX_EXACT)) < N_BUCKETS - 1:
        n += 1
    return n


def _params(*sem):
    return pltpu.CompilerParams(dimension_semantics=sem, vmem_limit_bytes=VMEM_LIMIT)


def _rms(x, g):
    return x * lax.rsqrt(jnp.mean(x * x, axis=-1, keepdims=True) + RMS_EPS) * g


def _silu(x):
    return x * jax.nn.sigmoid(x)


def _const_spec(shape, index, single=False):
    if single:
        return pl.BlockSpec(shape, lambda *_: index, pipeline_mode=pl.Buffered(1))
    return pl.BlockSpec(shape, lambda *_: index)


def _ffn_chunks(ff, n_chunks):
    tiles = -(-ff // MXU_TILE)
    edges = [min(ff, MXU_TILE * (-(-tiles * c // n_chunks))) for c in range(n_chunks + 1)]
    return [(lo, hi - lo) for lo, hi in zip(edges[:-1], edges[1:]) if hi > lo]


def _ffn_body(xp_ref, xs_ref, n0_ref, n1_ref, gf_ref, wg_hbm, wu_hbm, wd_hbm, op_ref, os_ref,
              wg_ref, wu_ref, wd_ref, stage_in, stage_out, sem, *, n_chunks, n_p, final, layer, f):
    i = pl.program_id(0)

    def fetch(w_hbm, stage, dst):
        slots, rows = stage.shape[0], stage.shape[1]
        n = dst.shape[0] // rows

        def copy(c):
            return pltpu.make_async_copy(w_hbm.at[layer, f, pl.ds(c * rows, rows), :], stage.at[c % slots],
                                         sem.at[c % slots])

        for c in range(min(slots - 1, n)):
            copy(c).start()
        for c in range(n):
            if c + slots - 1 < n:
                copy(c + slots - 1).start()
            copy(c).wait()
            dst[pl.ds(c * rows, rows), :] = stage[c % slots].astype(BF16)

    @pl.when(i == 0)
    def _():
        fetch(wg_hbm, stage_in, wg_ref)
        fetch(wu_hbm, stage_in, wu_ref)
        fetch(wd_hbm, stage_out, wd_ref)

    def apply(x_ref, o_ref):
        rows = x_ref.shape[0]
        first = -(-rows // (2 * BF16_ROWS)) * BF16_ROWS
        for lo_row, n_rows in ((0, first), (first, rows - first))[:2 if rows > first else 1]:
            rs = pl.ds(lo_row, n_rows)
            x = x_ref[rs, :]
            h = _rms(x, n0_ref[...]).astype(BF16)
            acc = jnp.zeros(x.shape, F32)
            for lo, width in _ffn_chunks(wg_ref.shape[1], n_chunks):
                sl = pl.ds(lo, width)
                g = jnp.dot(h, wg_ref[:, sl], preferred_element_type=F32)
                u = jnp.dot(h, wu_ref[:, sl], preferred_element_type=F32)
                a = (_silu(g) * u).astype(BF16)
                acc = acc + jnp.dot(a, wd_ref[sl, :], preferred_element_type=F32)
            y = x + 0.5 * _rms(acc, n1_ref[...])
            o_ref[rs, :] = _rms(y, gf_ref[...]) if final else y

    @pl.when(i == 0)
    def _():
        os_ref[...] = jnp.zeros(os_ref.shape, F32)

    @pl.when(i < n_p)
    def _():
        apply(xp_ref, op_ref)

    @pl.when(i >= n_p)
    def _():
        apply(xs_ref, os_ref)


def _ffn(xp, xs, norms, final_g, wg, wu, wd, layer, f, *, tm, n_chunks=2, final_seq=None):
    (n_rows_p, d), n_rows_s = xp.shape, xs.shape[0]
    ff = wg.shape[-1]
    nidx = (layer * 2 + f) * 2
    tm_s = _pick_tile(n_rows_s, tm, BF16_ROWS)
    if final_seq is None:
        n_p, out_rows_p = pl.cdiv(n_rows_p, tm), n_rows_p
        p_in = p_out = pl.BlockSpec((tm, d), lambda i: (jnp.minimum(i, n_p - 1), 0))
    else:
        t, skip = final_seq
        assert skip % BF16_ROWS == 0 and t % BF16_ROWS == 0
        tm = _pick_tile(t - skip, tm, BF16_ROWS)
        per_seq = (t - skip) // tm
        n_p, out_rows_p = (n_rows_p // t) * per_seq, (n_rows_p // t) * (t - skip)

        def in_rows(i):
            ip = jnp.minimum(i, n_p - 1)
            return pl.multiple_of((ip // per_seq) * t + skip + (ip % per_seq) * tm, BF16_ROWS), 0

        p_in = pl.BlockSpec((pl.Element(tm), pl.Element(d)), in_rows)
        p_out = pl.BlockSpec((tm, d), lambda i: (jnp.minimum(i, n_p - 1), 0))
    n_s = pl.cdiv(n_rows_s, tm_s)
    s_spec = pl.BlockSpec((tm_s, d), lambda i: (jnp.maximum(i - n_p, 0), 0))
    in_rows, out_rows = _pick_tile(d, WEIGHT_CHUNK_BYTES // (4 * ff)), _pick_tile(ff, WEIGHT_CHUNK_BYTES // (4 * d))
    hbm = pl.BlockSpec(memory_space=pl.ANY)
    return pl.pallas_call(
        functools.partial(_ffn_body, n_chunks=n_chunks, n_p=n_p, final=final_seq is not None, layer=layer, f=f),
        grid=(n_p + n_s,),
        in_specs=[
            p_in, s_spec,
            _const_spec((None, 1, d), (nidx, 0, 0)),
            _const_spec((None, 1, d), (nidx + 1, 0, 0)),
            _const_spec((1, d), (0, 0)),
            hbm, hbm, hbm,
        ],
        out_specs=[p_out, s_spec],
        out_shape=[jax.ShapeDtypeStruct((out_rows_p, d), F32), jax.ShapeDtypeStruct((n_rows_s, d), F32)],
        scratch_shapes=[
            pltpu.VMEM((d, ff), BF16), pltpu.VMEM((d, ff), BF16), pltpu.VMEM((ff, d), BF16),
            pltpu.VMEM((WEIGHT_SLOTS, in_rows, ff), F32), pltpu.VMEM((WEIGHT_SLOTS, out_rows, d), F32),
            pltpu.SemaphoreType.DMA((WEIGHT_SLOTS,)),
        ],
        compiler_params=_params("arbitrary"),
        name="ffn",
    )(xp, xs, norms, norms, final_g, wg, wu, wd)


def _pool_p_body(x_ref, g0_ref, g1_ref, w_ref, sc_ref, o_ref, st_ref, *lvl_refs, ts, n_t):
    t = pl.program_id(1)
    d = x_ref.shape[-1]
    n_g = len(POOL_WINDOWS)
    gd = d // n_g
    x = x_ref[0]
    h = _rms(x, g0_ref[...])
    cat_ref = lvl_refs[0]
    top = SUBLANES + POOL_HALO
    rows = POOL_HALO + ts

    @pl.when(t == 0)
    def _():
        cat_ref[0:top, :] = jnp.zeros((top, d), F32)
        for ref in lvl_refs[1:]:
            ref[0:SUBLANES, :] = jnp.zeros((SUBLANES, ref.shape[1]), F32)

    cat_ref[top:top + ts, :] = h
    for k in range(1, len(lvl_refs)):
        prev, cur = lvl_refs[k - 1], lvl_refs[k]
        lo = prev.shape[1] - cur.shape[1]
        cur[SUBLANES:SUBLANES + rows, :] = (prev[SUBLANES:SUBLANES + rows, lo:]
                                            + prev[SUBLANES - 2 ** (k - 1):SUBLANES - 2 ** (k - 1) + rows, lo:])
    pos = t * ts + lax.broadcasted_iota(jnp.int32, (ts, 1), 0)
    outs = []
    for g, w in enumerate(POOL_WINDOWS):
        c0 = g * gd
        k = w.bit_length() - 1
        if k < len(lvl_refs):
            ref = lvl_refs[k]
            l0 = c0 - (d - ref.shape[1])
            acc = ref[top:top + ts, l0:l0 + gd]
        else:
            ref = lvl_refs[k - 1]
            l0 = c0 - (d - ref.shape[1])
            acc = ref[top:top + ts, l0:l0 + gd] + ref[top - w // 2:top - w // 2 + ts, l0:l0 + gd]
        inv_cnt = 1.0 / jnp.minimum(w, pos + 1).astype(F32)
        pooled = (acc * inv_cnt - h[:, c0:c0 + gd]).astype(BF16)
        outs.append(jnp.dot(pooled, w_ref[g], preferred_element_type=F32))
    m = jnp.concatenate(outs, axis=-1) * sc_ref[...]
    o_ref[0] = x + _rms(m, g1_ref[...])

    @pl.when(t == n_t - 1)
    def _():
        st_ref[0] = cat_ref[top + ts - POOL_BUF:top + ts, :]

    cat_ref[SUBLANES:top, :] = cat_ref[SUBLANES + ts:top + ts, :]


def _pool_prompt(x, mix_norm, pool_w, pool_scale, layer, j, *, ts):
    b, t, d = x.shape
    n_t = t // ts
    assert n_t * ts == t and ts % 8 == 0 and ts >= POOL_HALO
    g = len(POOL_WINDOWS)
    assert all(w == 2 ** (i + 1) for i, w in enumerate(POOL_WINDOWS)) and POOL_WINDOWS[-1] // 2 == SUBLANES
    buf_rows = SUBLANES + POOL_HALO + ts
    level_lanes = [d] + [d - i * (d // g) for i in range(g - 1)]
    return pl.pallas_call(
        functools.partial(_pool_p_body, ts=ts, n_t=n_t),
        grid=(b, n_t),
        in_specs=[
            pl.BlockSpec((1, ts, d), lambda bi, ti: (bi, ti, 0)),
            _const_spec((None, 1, d), (layer * 2, 0, 0)),
            _const_spec((None, 1, d), (layer * 2 + 1, 0, 0)),
            _const_spec((None, g, d // g, d // g), (j, 0, 0, 0)),
            _const_spec((None, 1, d), (j, 0, 0)),
        ],
        out_specs=[
            pl.BlockSpec((1, ts, d), lambda bi, ti: (bi, ti, 0)),
            pl.BlockSpec((1, POOL_BUF, d), lambda bi, ti: (bi, 0, 0)),
        ],
        out_shape=[jax.ShapeDtypeStruct((b, t, d), F32), jax.ShapeDtypeStruct((b, POOL_BUF, d), F32)],
        scratch_shapes=[pltpu.VMEM((buf_rows, lanes), F32) for lanes in level_lanes],
        compiler_params=_params("parallel", "arbitrary"),
        name="pool_prompt",
    )(x, mix_norm, mix_norm, pool_w, pool_scale)


def _pool_s_body(x_ref, st_ref, g0_ref, g1_ref, w_ref, sc_ref, o_ref, so_ref, cat_ref, *, n_s, past):
    d = g0_ref.shape[-1]
    gd = d // len(POOL_WINDOWS)
    bt = st_ref.shape[0]
    top = POOL_HALO
    x = x_ref[...]
    h = _rms(x, g0_ref[...])
    h3 = h.reshape(bt, n_s, d)
    cat_ref[:, 0:top - POOL_BUF, :] = jnp.zeros((bt, top - POOL_BUF, d), F32)
    cat_ref[:, top - POOL_BUF:top, :] = st_ref[...]
    cat_ref[:, top:top + n_s, :] = h3
    pos = past + lax.broadcasted_iota(jnp.int32, (1, n_s, 1), 1)
    outs = []
    for g, w in enumerate(POOL_WINDOWS):
        c0 = g * gd
        acc = h3[:, :, c0:c0 + gd]
        for jj in range(1, w):
            acc = acc + cat_ref[:, top - jj:top - jj + n_s, c0:c0 + gd]
        inv_cnt = 1.0 / jnp.minimum(w, pos + 1).astype(F32)
        pooled = (acc * inv_cnt - h3[:, :, c0:c0 + gd]).reshape(bt * n_s, gd).astype(BF16)
        outs.append(jnp.dot(pooled, w_ref[g], preferred_element_type=F32))
    m = jnp.concatenate(outs, axis=-1) * sc_ref[...]
    o_ref[...] = x + _rms(m, g1_ref[...])
    so_ref[...] = cat_ref[:, top + n_s - POOL_BUF:top + n_s, :]


def _pool_sample(x, state, mix_norm, pool_w, pool_scale, layer, j, *, n_s, past, bt):
    db = x.shape[0] // n_s
    d = mix_norm.shape[-1]
    g = len(POOL_WINDOWS)
    assert n_s % SUBLANES == 0
    return pl.pallas_call(
        functools.partial(_pool_s_body, n_s=n_s, past=past),
        grid=(db // bt,),
        in_specs=[
            pl.BlockSpec((bt * n_s, d), lambda i: (i, 0)),
            pl.BlockSpec((None, bt, POOL_BUF, d), lambda i: (j, i, 0, 0)),
            _const_spec((None, 1, d), (layer * 2, 0, 0)),
            _const_spec((None, 1, d), (layer * 2 + 1, 0, 0)),
            _const_spec((None, g, d // g, d // g), (j, 0, 0, 0)),
            _const_spec((None, 1, d), (j, 0, 0)),
        ],
        out_specs=[
            pl.BlockSpec((bt * n_s, d), lambda i: (i, 0)),
            pl.BlockSpec((bt, POOL_BUF, d), lambda i: (i, 0, 0)),
        ],
        out_shape=[jax.ShapeDtypeStruct((db * n_s, d), F32), jax.ShapeDtypeStruct((db, POOL_BUF, d), F32)],
        scratch_shapes=[pltpu.VMEM((bt, POOL_HALO + n_s, d), F32)],
        compiler_params=_params("parallel"),
        name="pool_sample",
    )(x, state, mix_norm, mix_norm, pool_w, pool_scale)


def _layer_norm(y, g, b):
    mu = jnp.mean(y, axis=-1, keepdims=True)
    yc = y - mu
    var = jnp.mean(yc * yc, axis=-1, keepdims=True)
    return yc * lax.rsqrt(var + LN_EPS) * g + b


def _conv_p_body(x_ref, g0_ref, g1_ref, w1_ref, b1_ref, wdw_ref, bdw_ref, lng_ref, lnb_ref, w2_ref, b2_ref,
                 o_ref, st_ref, cat_ref, y_ref, *, ts, n_t):
    t = pl.program_id(1)
    d = x_ref.shape[-1]
    first = -(-ts // (2 * BF16_ROWS)) * BF16_ROWS
    parts = ((0, first), (first, ts - first))[:2 if ts > first else 1]

    @pl.when(t == 0)
    def _():
        cat_ref[0:CONV_HALO, :] = jnp.zeros((CONV_HALO, d), F32)

    for lo, n in parts:
        h = _rms(x_ref[0, lo:lo + n, :], g0_ref[...]).astype(BF16)
        glu = jnp.dot(h, w1_ref[...], preferred_element_type=F32) + b1_ref[...]
        cat_ref[CONV_HALO + lo:CONV_HALO + lo + n, :] = glu[:, :d] * jax.nn.sigmoid(glu[:, d:])

    off = CONV_HALO - CONV_BUF
    for lo, n in parts:
        for c0 in range(0, d, LANES):
            y = None
            for r in range(SUBLANES):
                zrows = n + (SUBLANES if r else 0)
                z = None
                for a in range((CONV_WIDTH + off) // SUBLANES + 1):
                    k = SUBLANES * a + r - off
                    if 0 <= k < CONV_WIDTH:
                        r0 = lo + SUBLANES * a
                        term = cat_ref[r0:r0 + zrows, c0:c0 + LANES] * wdw_ref[k:k + 1, c0:c0 + LANES]
                        z = term if z is None else z + term
                zs = z[r:r + n]
                y = zs if y is None else y + zs
            y_ref[lo:lo + n, c0:c0 + LANES] = y + bdw_ref[:, c0:c0 + LANES]
        a = _silu(_layer_norm(y_ref[lo:lo + n, :], lng_ref[...], lnb_ref[...])).astype(BF16)
        m = jnp.dot(a, w2_ref[...], preferred_element_type=F32) + b2_ref[...]
        o_ref[0, lo:lo + n, :] = x_ref[0, lo:lo + n, :] + _rms(m, g1_ref[...])

    @pl.when(t == n_t - 1)
    def _():
        st_ref[0] = cat_ref[CONV_HALO + ts - CONV_BUF:CONV_HALO + ts, :]

    cat_ref[0:CONV_HALO, :] = cat_ref[ts:ts + CONV_HALO, :]


def _conv_specs(layer, j, d):
    return [
        _const_spec((None, 1, d), (layer * 2, 0, 0)),
        _const_spec((None, 1, d), (layer * 2 + 1, 0, 0)),
        _const_spec((None, d, 2 * d), (j, 0, 0)),
        _const_spec((None, 1, 2 * d), (j, 0, 0)),
        _const_spec((None, CONV_WIDTH, d), (j, 0, 0)),
        _const_spec((None, 1, d), (j, 0, 0)),
        _const_spec((None, 1, d), (j, 0, 0)),
        _const_spec((None, 1, d), (j, 0, 0)),
        _const_spec((None, d, d), (j, 0, 0)),
        _const_spec((None, 1, d), (j, 0, 0)),
    ]


def _conv_prompt(x, mix_norm, cw, layer, j, *, ts):
    b, t, d = x.shape
    n_t = t // ts
    assert n_t * ts == t and ts % 8 == 0 and ts >= CONV_HALO
    return pl.pallas_call(
        functools.partial(_conv_p_body, ts=ts, n_t=n_t),
        grid=(b, n_t),
        in_specs=[pl.BlockSpec((1, ts, d), lambda bi, ti: (bi, ti, 0))] + _conv_specs(layer, j, d),
        out_specs=[
            pl.BlockSpec((1, ts, d), lambda bi, ti: (bi, ti, 0)),
            pl.BlockSpec((1, CONV_BUF, d), lambda bi, ti: (bi, 0, 0)),
        ],
        out_shape=[jax.ShapeDtypeStruct((b, t, d), F32), jax.ShapeDtypeStruct((b, CONV_BUF, d), F32)],
        scratch_shapes=[pltpu.VMEM((CONV_HALO + ts, d), F32), pltpu.VMEM((ts, d), F32)],
        compiler_params=_params("parallel", "arbitrary"),
        name="conv_prompt",
    )(x, mix_norm, mix_norm, *cw)


def _conv_s_body(x_ref, st_ref, g0_ref, g1_ref, w1_ref, b1_ref, wdw_ref, bdw_ref, lng_ref, lnb_ref, w2_ref, b2_ref,
                 o_ref, so_ref, cat_ref, *, n_s):
    d = g0_ref.shape[-1]
    bt = st_ref.shape[0]
    top = CONV_HALO
    x = x_ref[...]
    h = _rms(x, g0_ref[...]).astype(BF16)
    z = jnp.dot(h, w1_ref[...], preferred_element_type=F32) + b1_ref[...]
    u = z[:, :d] * jax.nn.sigmoid(z[:, d:])
    cat_ref[:, 0:top - CONV_BUF, :] = jnp.zeros((bt, top - CONV_BUF, d), F32)
    cat_ref[:, top - CONV_BUF:top, :] = st_ref[...]
    cat_ref[:, top:top + n_s, :] = u.reshape(bt, n_s, d)
    off = top - CONV_BUF
    y = cat_ref[:, off:off + n_s, :] * wdw_ref[0:1, :] + bdw_ref[...]
    for k in range(1, CONV_WIDTH):
        y = y + cat_ref[:, off + k:off + k + n_s, :] * wdw_ref[k:k + 1, :]
    a = _silu(_layer_norm(y, lng_ref[...], lnb_ref[...])).reshape(bt * n_s, d).astype(BF16)
    m = jnp.dot(a, w2_ref[...], preferred_element_type=F32) + b2_ref[...]
    o_ref[...] = x + _rms(m, g1_ref[...])
    so_ref[...] = cat_ref[:, top + n_s - CONV_BUF:top + n_s, :]


def _conv_sample(x, state, mix_norm, cw, layer, j, *, n_s, bt):
    db = x.shape[0] // n_s
    d = mix_norm.shape[-1]
    assert n_s % SUBLANES == 0
    return pl.pallas_call(
        functools.partial(_conv_s_body, n_s=n_s),
        grid=(db // bt,),
        in_specs=[
            pl.BlockSpec((bt * n_s, d), lambda i: (i, 0)),
            pl.BlockSpec((None, bt, CONV_BUF, d), lambda i: (j, i, 0, 0)),
        ] + _conv_specs(layer, j, d),
        out_specs=[
            pl.BlockSpec((bt * n_s, d), lambda i: (i, 0)),
            pl.BlockSpec((bt, CONV_BUF, d), lambda i: (i, 0, 0)),
        ],
        out_shape=[jax.ShapeDtypeStruct((db * n_s, d), F32), jax.ShapeDtypeStruct((db, CONV_BUF, d), F32)],
        scratch_shapes=[pltpu.VMEM((bt, CONV_HALO + n_s, d), F32)],
        compiler_params=_params("parallel"),
        name="conv_sample",
    )(x, state, mix_norm, mix_norm, *cw)


def _qkv_body(x_ref, g0_ref, w_ref, q_ref, k_ref, v_ref, *kv_bf16_refs, decode):
    rows, d = x_ref.shape
    h = _rms(x_ref[...], g0_ref[...]).astype(BF16)
    qkv = jnp.dot(h, w_ref[...], preferred_element_type=F32)
    q = qkv[:, :d] * (HEAD_DIM ** -0.5 * LOG2E)
    k = qkv[:, d:2 * d]
    v = qkv[:, 2 * d:]
    k_ref[...] = k.reshape(rows, N_HEADS, V_DIM)
    v_ref[...] = v.reshape(rows, N_HEADS, V_DIM)
    if decode:
        q_ref[...] = q.reshape(rows, N_HEADS, V_DIM)
    else:
        kb_ref, vb_ref = kv_bf16_refs
        q_ref[...] = q.astype(BF16)
        kb_ref[...] = k.astype(BF16)
        vb_ref[...] = v.astype(BF16)


def _qkv(x, mix_norm, wqkv, layer, j, *, tm, decode):
    n, d = x.shape
    row = pl.BlockSpec((tm, d), lambda i: (i, 0))
    per_head = pl.BlockSpec((tm, N_HEADS, V_DIM), lambda i: (i, 0, 0))
    head_shape = jax.ShapeDtypeStruct((n, N_HEADS, V_DIM), F32)
    flat_bf16 = jax.ShapeDtypeStruct((n, d), BF16)
    return pl.pallas_call(
        functools.partial(_qkv_body, decode=decode),
        grid=(pl.cdiv(n, tm),),
        in_specs=[row, _const_spec((None, 1, d), (layer * 2, 0, 0)), _const_spec((None, d, 3 * d), (j, 0, 0))],
        out_specs=[per_head] * 3 if decode else [row, per_head, per_head, row, row],
        out_shape=[head_shape] * 3 if decode else [flat_bf16, head_shape, head_shape, flat_bf16, flat_bf16],
        compiler_params=_params("parallel"),
        name="qkv_proj",
    )(x, mix_norm, wqkv)


def _bucket(n):
    nf = jnp.maximum(n, 1).astype(F32)
    large = MAX_EXACT + (jnp.log(nf / MAX_EXACT) / math.log(MAX_DISTANCE / MAX_EXACT)
                         * (N_BUCKETS - MAX_EXACT)).astype(jnp.int32)
    large = jnp.minimum(large, N_BUCKETS - 1)
    return jnp.where(n < MAX_EXACT, n, large)


def _lookup(bucket, entry):
    out = jnp.zeros(bucket.shape, F32)
    for b in range(N_BUCKETS):
        out = jnp.where(bucket == b, entry(b), out)
    return out


def _bias_p_body(table_ref, bp_ref, *, blk):
    head = pl.program_id(0)
    a = lax.broadcasted_iota(jnp.int32, (blk, blk), 0)
    b = lax.broadcasted_iota(jnp.int32, (blk, blk), 1)
    for sel in range(3):
        n = sel * blk + b - a
        vals = _lookup(_bucket(jnp.maximum(n, 0)), lambda bb: table_ref[bb * N_HEADS + head])
        bp_ref[0, sel] = jnp.where(n >= 0, vals * LOG2E, NEG_INF)


def _bias_s_body(tt_ref, bs_ref, *, n_s):
    rows = 2 * n_s * N_HEADS
    cols = PAGE_SIZE * N_HEADS
    tt = tt_ref[...]
    trow = jnp.broadcast_to(tt[None], (rows // N_HEADS, N_HEADS, N_BUCKETS)).reshape(rows, N_BUCKETS)

    def tile(width, dist):
        r = lax.broadcasted_iota(jnp.int32, (rows, width), 0)
        c = lax.broadcasted_iota(jnp.int32, (rows, width), 1)
        n = dist((r // N_HEADS) % n_s, c // N_HEADS)
        vals = _lookup(_bucket(jnp.maximum(n, 0)), lambda bb: trow[:, bb:bb + 1])
        return jnp.where((r % N_HEADS == c % N_HEADS) & (n >= 0) & (c < cols), vals * LOG2E, NEG_INF)

    bs_ref[:, 0:cols] = tile(cols, lambda qi, kk: 2 * PAGE_SIZE + qi - kk)
    bs_ref[:, cols:2 * cols] = tile(cols, lambda qi, kk: PAGE_SIZE + qi - kk)
    bs_ref[:, 2 * cols:2 * cols + PAGE_SIZE] = tile(
        PAGE_SIZE, lambda qi, kk: jnp.where(kk < n_s, qi - kk, -1))


def _bias_tiles(table, *, blk, n_s):
    far = _first_far_distance()
    assert blk + 1 >= far and PAGE_SIZE + 1 >= far and n_s * N_HEADS <= PAGE_SIZE
    bias_p = pl.pallas_call(
        functools.partial(_bias_p_body, blk=blk),
        grid=(N_HEADS,),
        in_specs=[pl.BlockSpec(memory_space=pltpu.SMEM)],
        out_specs=pl.BlockSpec((1, 3, blk, blk), lambda h: (h, 0, 0, 0)),
        out_shape=jax.ShapeDtypeStruct((N_HEADS, 3, blk, blk), F32),
        compiler_params=_params("parallel"),
        name="rel_bias_prompt",
    )(table.reshape(-1))
    width = 2 * PAGE_SIZE * N_HEADS + PAGE_SIZE
    bias_s = pl.pallas_call(
        functools.partial(_bias_s_body, n_s=n_s),
        out_shape=jax.ShapeDtypeStruct((2 * n_s * N_HEADS, width), F32),
        compiler_params=pltpu.CompilerParams(vmem_limit_bytes=VMEM_LIMIT),
        name="rel_bias_sample",
    )(table.T)
    return bias_p, bias_s


def _diff_lambda(lam_ref, lam_init):
    lp = lam_ref[...]
    s1 = jnp.sum(lp[0:1] * lp[1:2], axis=-1, keepdims=True)
    s2 = jnp.sum(lp[2:3] * lp[3:4], axis=-1, keepdims=True)
    return jnp.exp(s1) - jnp.exp(s2) + lam_init


_NT = (((1,), (1,)), ((), ()))


def _attn_p_body(q_ref, k_ref, v_ref, x_ref, bias_ref, lam_ref, sub_ref, wo_ref, g1_ref, o_ref,
                 qz_ref, vt_ref, m_ref, acc_ref, s_ref, p_ref, al_ref, a_ref, *, blk, n_full, tail, lam_init):
    i = pl.program_id(1)

    @pl.when(i == 0)
    def _():
        for hd in range(N_HEADS):
            c0 = hd * V_DIM

            ones_row = (lax.broadcasted_iota(jnp.int32, (SUM_ROWS, blk), 0) == 0).astype(BF16)

            def xpose(jb, carry):
                r0 = pl.multiple_of(jb * blk, blk)
                vt_ref[hd, jb, 0:V_DIM, :] = v_ref[0, pl.ds(r0, blk), c0:c0 + V_DIM].T
                vt_ref[hd, jb, V_DIM:V_DIM + SUM_ROWS, :] = ones_row
                return carry

            lax.fori_loop(0, n_full, xpose, 0)
            if tail:
                r0 = n_full * blk
                vt_ref[hd, n_full, 0:V_DIM, 0:tail] = (
                    v_ref[0, r0:r0 + tail, c0:c0 + V_DIM].astype(F32).T.astype(BF16))
                vt_ref[hd, n_full, V_DIM:V_DIM + SUM_ROWS, :] = ones_row

    def run(wq, n_far, near, tail_keys):
        lane = lax.broadcasted_iota(jnp.int32, (wq, V_DIM), 1)
        for hd in range(N_HEADS):
            q = q_ref[0, 0:wq, hd * V_DIM:(hd + 1) * V_DIM]
            qz_ref[hd, 0:wq, :] = jnp.where(lane < HEAD_DIM, q, jnp.zeros_like(q))
            qz_ref[hd, wq:2 * wq, :] = jnp.where(lane >= HEAD_DIM, q, jnp.zeros_like(q))
        m_ref[:, :, 0:2 * wq] = jnp.full((N_HEADS, 1, 2 * wq), NEG_INF, F32)
        acc_ref[:, :, 0:2 * wq] = jnp.zeros((N_HEADS, V_DIM + SUM_ROWS, 2 * wq), F32)

        def all_heads(n, keys, vt, bias, const_bias=False):
            def stage(hd):
                s = lax.dot_general(keys(hd), qz_ref[hd, 0:2 * wq, :], _NT, preferred_element_type=F32)
                if not const_bias:
                    b2 = bias(hd)
                    s = s + jnp.concatenate([b2, b2], axis=1)
                s_ref[hd % 2, 0:n, 0:2 * wq] = s

            def softmax(hd):
                s = s_ref[hd % 2, 0:n, 0:2 * wq]
                m_old = m_ref[hd, :, 0:2 * wq]
                col_max = jnp.max(s, axis=0, keepdims=True)
                if const_bias:
                    c = bias(hd)
                    m_new = jnp.maximum(m_old, col_max + c)
                    shift = m_new - c
                else:
                    shift = m_new = jnp.maximum(m_old, col_max)
                al_ref[hd % 2, :, 0:2 * wq] = jnp.exp2(m_old - m_new)
                p_ref[hd % 2, 0:n, 0:2 * wq] = jnp.exp2(s - shift).astype(BF16)
                m_ref[hd, :, 0:2 * wq] = m_new

            def accumulate(hd):
                acc_ref[hd, :, 0:2 * wq] = (
                    al_ref[hd % 2, :, 0:2 * wq] * acc_ref[hd, :, 0:2 * wq]
                    + jnp.dot(vt(hd), p_ref[hd % 2, 0:n, 0:2 * wq], preferred_element_type=F32))

            stage(0)
            for hd in range(N_HEADS + 1):
                if hd + 1 < N_HEADS:
                    stage(hd + 1)
                if hd < N_HEADS:
                    softmax(hd)
                if hd >= 1:
                    accumulate(hd - 1)

        def full_block(jb, sel):
            r0 = pl.multiple_of(jb * blk, blk)
            far = sel == 2
            all_heads(blk,
                      lambda hd: k_ref[0, pl.ds(r0, blk), hd * V_DIM:(hd + 1) * V_DIM],
                      lambda hd: vt_ref[hd, jb],
                      (lambda hd: bias_ref[hd, 2, 0:1, 0:1]) if far else (lambda hd: bias_ref[hd, sel, :, 0:wq]),
                      const_bias=far)

        def far_body(jb, carry):
            full_block(jb, 2)
            return carry

        lax.fori_loop(0, n_far, far_body, 0)
        for jb, sel, cond in near:
            if cond is None:
                full_block(jb, sel)
            else:
                pl.when(cond)(functools.partial(full_block, jb, sel))
        if tail_keys:
            r0 = n_full * blk
            all_heads(tail_keys,
                      lambda hd: k_ref[0, r0:r0 + tail_keys, hd * V_DIM:(hd + 1) * V_DIM],
                      lambda hd: vt_ref[hd, n_full, :, 0:tail_keys],
                      lambda hd: bias_ref[hd, 0, 0:tail_keys, 0:wq])

        lam = _diff_lambda(lam_ref, lam_init)
        for hd in range(N_HEADS):
            o_both = acc_ref[hd, 0:V_DIM, 0:2 * wq] * (1.0 / acc_ref[hd, V_DIM:V_DIM + 1, 0:2 * wq])
            o = o_both[:, 0:wq] - lam * o_both[:, wq:2 * wq]
            o = o * lax.rsqrt(jnp.mean(o * o, axis=0, keepdims=True) + RMS_EPS) * sub_ref[...] * (1.0 - lam_init)
            a_ref[0:wq, hd * V_DIM:(hd + 1) * V_DIM] = o.T.astype(BF16)
        mix = jnp.dot(a_ref[0:wq, :], wo_ref[...], preferred_element_type=F32)
        o_ref[0, 0:wq, :] = x_ref[0, 0:wq, :] + _rms(mix, g1_ref[...])

    @pl.when(i < n_full)
    def _():
        run(blk, jnp.maximum(i - 1, 0), [(jnp.maximum(i - 1, 0), 1, i >= 1), (i, 0, None)], 0)

    if tail:
        @pl.when(i == n_full)
        def _():
            run(-(-tail // LANES) * LANES, n_full - 1, [(n_full - 1, 1, None)], tail)


def _attn_prompt(q, kb, vb, x, bias_p, lam_p, subln_col, wo, mix_norm, layer, j, *, blk, lam_init):
    b, t, d = q.shape
    n_full, tail = t // blk, t % blk
    assert tail % 16 == 0
    n_blocks = n_full + (1 if tail else 0)
    seq_blk = pl.BlockSpec((1, blk, d), lambda bi, qi: (bi, qi, 0))
    return pl.pallas_call(
        functools.partial(_attn_p_body, blk=blk, n_full=n_full, tail=tail, lam_init=lam_init),
        grid=(b, n_blocks),
        in_specs=[
            seq_blk,
            pl.BlockSpec((1, t, d), lambda bi, qi: (bi, 0, 0)),
            pl.BlockSpec((1, t, d), lambda bi, qi: (bi, 0, 0)),
            seq_blk,
            _const_spec((N_HEADS, 3, blk, blk), (0, 0, 0, 0)),
            _const_spec((None, 4, HEAD_DIM), (j, 0, 0)),
            _const_spec((None, V_DIM, 1), (j, 0, 0)),
            _const_spec((None, d, d), (j, 0, 0)),
            _const_spec((None, 1, d), (layer * 2 + 1, 0, 0)),
        ],
        out_specs=seq_blk,
        out_shape=jax.ShapeDtypeStruct((b, t, d), F32),
        scratch_shapes=[
            pltpu.VMEM((N_HEADS, 2 * blk, V_DIM), BF16),
            pltpu.VMEM((N_HEADS, n_blocks, V_DIM + SUM_ROWS, blk), BF16),
            pltpu.VMEM((N_HEADS, 1, 2 * blk), F32),
            pltpu.VMEM((N_HEADS, V_DIM + SUM_ROWS, 2 * blk), F32),
            pltpu.VMEM((2, blk, 2 * blk), F32),
            pltpu.VMEM((2, blk, 2 * blk), BF16),
            pltpu.VMEM((2, 1, 2 * blk), F32),
            pltpu.VMEM((blk, d), BF16),
        ],
        compiler_params=_params("arbitrary", "arbitrary"),
        name="attn_prompt",
    )(q, kb, vb, x, bias_p, lam_p, subln_col, wo, mix_norm)


def _attn_s_body(pt_ref, q_ref, kn_ref, vn_ref, bias_ref, lam_ref, sub_ref, *rest, n_pages, n_s, lam_init):
    k_pages = rest[:n_pages]
    v_pages = rest[n_pages:2 * n_pages]
    o_ref, s_ref = rest[2 * n_pages:]
    half = n_s * N_HEADS
    cols = PAGE_SIZE * N_HEADS

    q = q_ref[...].reshape(half, V_DIM)
    lane = lax.broadcasted_iota(jnp.int32, (half, V_DIM), 1)
    q2 = jnp.concatenate([jnp.where(lane < HEAD_DIM, q, 0.0), jnp.where(lane >= HEAD_DIM, q, 0.0)], axis=0).astype(BF16)

    def scores(keys, bias):
        return lax.dot_general(q2, keys, _NT, preferred_element_type=F32) + bias

    pad = jnp.zeros((PAGE_SIZE - half, V_DIM), F32)
    kn = jnp.concatenate([kn_ref[...].reshape(half, V_DIM), pad], axis=0).astype(BF16)
    vn = jnp.concatenate([vn_ref[...].reshape(half, V_DIM), pad], axis=0).astype(BF16)
    s_new = scores(kn, bias_ref[:, 2 * cols:2 * cols + PAGE_SIZE])
    m = jnp.max(s_new, axis=-1, keepdims=True)
    for r_ in range(n_pages):
        near = r_ == n_pages - 1
        s = scores(k_pages[r_][...].reshape(cols, V_DIM).astype(BF16),
                   bias_ref[:, cols:2 * cols] if near else bias_ref[:, 0:cols])
        s_ref[r_] = s
        m = jnp.maximum(m, jnp.max(s, axis=-1, keepdims=True))

    p = jnp.exp2(s_new - m)
    l = jnp.sum(p, axis=-1, keepdims=True)
    acc = jnp.dot(p.astype(BF16), vn, preferred_element_type=F32)
    for r_ in range(n_pages):
        p = jnp.exp2(s_ref[r_] - m)
        l = l + jnp.sum(p, axis=-1, keepdims=True)
        acc = acc + jnp.dot(p.astype(BF16), v_pages[r_][...].reshape(cols, V_DIM).astype(BF16),
                            preferred_element_type=F32)

    lam = _diff_lambda(lam_ref, lam_init)
    o_all = acc * (1.0 / l)
    o = o_all[0:half] - lam * o_all[half:2 * half]
    o_ref[...] = (_rms(o, sub_ref[...]) * (1.0 - lam_init)).reshape(o_ref.shape)


def _attn_sample(q, kn, vn, cache_k, cache_v, page_table, bias_s, lam_p, subln, j, *, n_pages, lam_init):
    db, n_s = q.shape[:2]
    assert n_pages == page_table.shape[1]
    rows = 2 * n_s * N_HEADS
    new_kv = pl.BlockSpec((1, n_s, 1, N_HEADS, V_DIM), lambda b, pt: (b, 0, 0, 0, 0))
    q_spec = pl.BlockSpec((1, n_s, N_HEADS, V_DIM), lambda b, pt: (b, 0, 0, 0))

    def page_spec(r):
        return pl.BlockSpec((1, PAGE_SIZE, 1, N_HEADS, V_DIM), lambda b, pt: (pt[b, r], 0, j, 0, 0))

    grid_spec = pltpu.PrefetchScalarGridSpec(
        num_scalar_prefetch=1,
        grid=(db,),
        in_specs=[
            q_spec, new_kv, new_kv,
            pl.BlockSpec(bias_s.shape, lambda b, pt: (0, 0)),
            pl.BlockSpec((None, 4, HEAD_DIM), lambda b, pt: (j, 0, 0)),
            pl.BlockSpec((None, 1, V_DIM), lambda b, pt: (j, 0, 0)),
        ] + [page_spec(r) for r in range(n_pages)] * 2,
        out_specs=q_spec,
        scratch_shapes=[pltpu.VMEM((n_pages, rows, PAGE_SIZE * N_HEADS), F32)],
    )
    return pl.pallas_call(
        functools.partial(_attn_s_body, n_pages=n_pages, n_s=n_s, lam_init=lam_init),
        grid_spec=grid_spec,
        out_shape=jax.ShapeDtypeStruct((db, n_s, N_HEADS, V_DIM), F32),
        compiler_params=_params("parallel"),
        name="attn_sample",
    )(page_table, q, kn, vn, bias_s, lam_p, subln, *([cache_k] * n_pages), *([cache_v] * n_pages))


def _outproj_body(a_ref, x_ref, w_ref, g1_ref, o_ref):
    a = a_ref[...].reshape(x_ref.shape)
    m = jnp.dot(a.astype(BF16), w_ref[...], preferred_element_type=F32)
    o_ref[...] = x_ref[...] + _rms(m, g1_ref[...])


def _outproj(a, x, wo, mix_norm, layer, j, *, tm):
    n, d = x.shape
    row = pl.BlockSpec((tm, d), lambda i: (i, 0))
    return pl.pallas_call(
        _outproj_body,
        grid=(pl.cdiv(n, tm),),
        in_specs=[pl.BlockSpec((tm,) + a.shape[1:], lambda i: (i, 0, 0)), row,
                  _const_spec((None, d, d), (j, 0, 0)), _const_spec((None, 1, d), (layer * 2 + 1, 0, 0))],
        out_specs=row,
        out_shape=jax.ShapeDtypeStruct((n, d), F32),
        compiler_params=_params("parallel"),
        name="attn_outproj",
    )(a, x, wo, mix_norm)


def _pick_tile(n, cap, align=8):
    best = None
    for c in range(align, min(n, cap) + 1, align):
        if n % c == 0:
            best = c
    assert best is not None, (n, cap, align)
    return best


def kernel(x_prompt, x_sample, state_pool, state_conv, cache_k, cache_v, page_table, meta_tokens, rel_bias_table, ffn_norm, ffn_wg, ffn_wu, ffn_wd, mix_norm, pool_w, pool_scale, attn_wqkv, attn_wo, attn_lambda, attn_subln, conv_w1, conv_b1, conv_wdw, conv_bdw, conv_ln_g, conv_ln_b, conv_w2, conv_b2, final_norm):
    b, seq, d = x_prompt.shape
    db, n_s, _ = x_sample.shape
    depth = ffn_wg.shape[0]
    t = seq + N_META
    past = page_table.shape[1] * PAGE_SIZE

    wg, wu, wd = ffn_wg, ffn_wu, ffn_wd
    ffn_norm3 = ffn_norm.reshape(-1, 1, d)
    mix_norm3 = mix_norm.reshape(-1, 1, d)
    pool_w_b = pool_w.astype(BF16)
    pool_scale3 = pool_scale.reshape(-1, 1, d)
    wqkv_b, wo_b = attn_wqkv.astype(BF16), attn_wo.astype(BF16)
    subln3 = attn_subln.reshape(-1, 1, V_DIM)
    cw = (conv_w1.astype(BF16), conv_b1.reshape(-1, 1, 2 * d), conv_wdw, conv_bdw.reshape(-1, 1, d),
          conv_ln_g.reshape(-1, 1, d), conv_ln_b.reshape(-1, 1, d), conv_w2.astype(BF16), conv_b2.reshape(-1, 1, d))
    final_g = final_norm.reshape(1, d)

    meta = jnp.broadcast_to(meta_tokens[None].astype(x_prompt.dtype), (b, N_META, d))
    xp = jnp.concatenate([meta, x_prompt], axis=1).reshape(b * t, d)
    xs = x_sample.reshape(db * n_s, d)

    tm_p = _pick_tile(b * t, 768, 16)
    tm_s = min(512, db * n_s)
    ts_pool = _pick_tile(t, 1032)
    ts_conv = _pick_tile(t, 688, BF16_ROWS)
    bt = _pick_tile(db, 32)

    bias_p = bias_s = None
    pool_p, pool_s, conv_p, conv_s = [], [], [], []
    k_p = v_p = k_s = v_s = None
    for i in range(depth):
        kind, j = i % N_MIXERS, i // N_MIXERS
        xp, xs = _ffn(xp, xs, ffn_norm3, final_g, wg, wu, wd, i, 0, tm=tm_p)
        if kind == 0:
            xp3, st = _pool_prompt(xp.reshape(b, t, d), mix_norm3, pool_w_b, pool_scale3, i, j, ts=ts_pool)
            xp = xp3.reshape(b * t, d)
            pool_p.append(st)
            xs, st = _pool_sample(xs, state_pool, mix_norm3, pool_w_b, pool_scale3, i, j, n_s=n_s, past=past, bt=bt)
            pool_s.append(st)
        elif kind == 1:
            assert j == 0
            lam_init = _lambda_init(i)
            if bias_p is None:
                bias_p, bias_s = _bias_tiles(rel_bias_table, blk=ATTN_BLOCK, n_s=n_s)
            q, k_p, v_p, kb, vb = _qkv(xp, mix_norm3, wqkv_b, i, j, tm=tm_p, decode=False)
            xp = _attn_prompt(q.reshape(b, t, d), kb.reshape(b, t, d), vb.reshape(b, t, d), xp.reshape(b, t, d),
                              bias_p, attn_lambda, attn_subln.reshape(-1, V_DIM, 1), wo_b, mix_norm3, i, j,
                              blk=ATTN_BLOCK, lam_init=lam_init).reshape(b * t, d)
            q, k_s, v_s = _qkv(xs, mix_norm3, wqkv_b, i, j, tm=tm_s, decode=True)
            k_s = k_s.reshape(db, n_s, 1, N_HEADS, V_DIM)
            v_s = v_s.reshape(db, n_s, 1, N_HEADS, V_DIM)
            a = _attn_sample(q.reshape(db, n_s, N_HEADS, V_DIM), k_s, v_s, cache_k, cache_v,
                             page_table, bias_s, attn_lambda, subln3, j, n_pages=page_table.shape[1], lam_init=lam_init)
            xs = _outproj(a.reshape(db * n_s, N_HEADS, V_DIM), xs, wo_b, mix_norm3, i, j, tm=tm_s)
        else:
            xp3, st = _conv_prompt(xp.reshape(b, t, d), mix_norm3, cw, i, j, ts=ts_conv)
            xp = xp3.reshape(b * t, d)
            conv_p.append(st)
            xs, st = _conv_sample(xs, state_conv, mix_norm3, cw, i, j, n_s=n_s, bt=bt)
            conv_s.append(st)
        xp, xs = _ffn(xp, xs, ffn_norm3, final_g, wg, wu, wd, i, 1, tm=tm_p,
                      final_seq=(t, N_META) if i == depth - 1 else None)

    kv_shape_p = (b, t, 1, N_HEADS, V_DIM)
    return (xp.reshape(b, seq, d), xs.reshape(db, n_s, d), jnp.stack(pool_p, axis=0), jnp.stack(pool_s, axis=0),
            jnp.stack(conv_p, axis=0), jnp.stack(conv_s, axis=0),
            k_p.reshape(kv_shape_p), v_p.reshape(kv_shape_p), k_s, v_s)
```

```python
import functools
import math

import jax
import jax.numpy as jnp
from jax import lax
from jax.experimental import pallas as pl
from jax.experimental.pallas import tpu as pltpu

F32 = jnp.float32
BF16 = jnp.bfloat16

N_MIXERS = 3
N_META = 16
N_HEADS = 8
HEAD_DIM = 64
V_DIM = 2 * HEAD_DIM
N_BUCKETS = 32
MAX_EXACT = N_BUCKETS // 2
MAX_DISTANCE = 128
POOL_WINDOWS = (2, 4, 8, 16)
POOL_BUF = max(POOL_WINDOWS) - 1
CONV_WIDTH = 31
CONV_BUF = CONV_WIDTH - 1
PAGE_SIZE = 128
RMS_EPS = 1e-6
LN_EPS = 1e-5
NEG_INF = -1e30
LOG2E = math.log2(math.e)
SUBLANES = 8
LANES = 128
BF16_ROWS = 16
MXU_TILE = 256
SUM_ROWS = 16
ATTN_BLOCK = 256
POOL_HALO = 16
CONV_HALO = 32
VMEM_LIMIT = 56 * 1024 * 1024
WEIGHT_CHUNK_BYTES = 3 * 256 * 1024
WEIGHT_SLOTS = 4


def _lambda_init(layer_idx):
    return 0.8 - 0.6 * math.exp(-0.3 * layer_idx)


def _first_far_distance():
    n = MAX_EXACT
    while MAX_EXACT + int(math.log(n / MAX_EXACT) / math.log(MAX_DISTANCE / MAX_EXACT) * (N_BUCKETS - MAX_EXACT)) < N_BUCKETS - 1:
        n += 1
    return n


def _params(*sem):
    return pltpu.CompilerParams(dimension_semantics=sem, vmem_limit_bytes=VMEM_LIMIT)


def _rms(x, g):
    return x * lax.rsqrt(jnp.mean(x * x, axis=-1, keepdims=True) + RMS_EPS) * g


def _silu(x):
    return x * jax.nn.sigmoid(x)


def _const_spec(shape, index, single=False):
    if single:
        return pl.BlockSpec(shape, lambda *_: index, pipeline_mode=pl.Buffered(1))
    return pl.BlockSpec(shape, lambda *_: index)


def _ffn_chunks(ff, n_chunks):
    tiles = -(-ff // MXU_TILE)
    edges = [min(ff, MXU_TILE * (-(-tiles * c // n_chunks))) for c in range(n_chunks + 1)]
    return [(lo, hi - lo) for lo, hi in zip(edges[:-1], edges[1:]) if hi > lo]


def _ffn_body(xp_ref, xs_ref, n0_ref, n1_ref, gf_ref, wg_hbm, wu_hbm, wd_hbm, op_ref, os_ref,
              wg_ref, wu_ref, wd_ref, stage_in, stage_out, sem, *, n_chunks, n_p, final, layer, f):
    i = pl.program_id(0)

    def fetch(w_hbm, stage, dst):
        slots, rows = stage.shape[0], stage.shape[1]
        n = dst.shape[0] // rows

        def copy(c):
            return pltpu.make_async_copy(w_hbm.at[layer, f, pl.ds(c * rows, rows), :], stage.at[c % slots],
                                         sem.at[c % slots])

        for c in range(min(slots - 1, n)):
            copy(c).start()
        for c in range(n):
            if c + slots - 1 < n:
                copy(c + slots - 1).start()
            copy(c).wait()
            dst[pl.ds(c * rows, rows), :] = stage[c % slots].astype(BF16)

    @pl.when(i == 0)
    def _():
        fetch(wg_hbm, stage_in, wg_ref)
        fetch(wu_hbm, stage_in, wu_ref)
        fetch(wd_hbm, stage_out, wd_ref)

    def apply(x_ref, o_ref):
        rows = x_ref.shape[0]
        first = -(-rows // (2 * BF16_ROWS)) * BF16_ROWS
        for lo_row, n_rows in ((0, first), (first, rows - first))[:2 if rows > first else 1]:
            rs = pl.ds(lo_row, n_rows)
            x = x_ref[rs, :]
            h = _rms(x, n0_ref[...]).astype(BF16)
            acc = jnp.zeros(x.shape, F32)
            for lo, width in _ffn_chunks(wg_ref.shape[1], n_chunks):
                sl = pl.ds(lo, width)
                g = jnp.dot(h, wg_ref[:, sl], preferred_element_type=F32)
                u = jnp.dot(h, wu_ref[:, sl], preferred_element_type=F32)
                a = (_silu(g) * u).astype(BF16)
                acc = acc + jnp.dot(a, wd_ref[sl, :], preferred_element_type=F32)
            y = x + 0.5 * _rms(acc, n1_ref[...])
            o_ref[rs, :] = _rms(y, gf_ref[...]) if final else y

    @pl.when(i == 0)
    def _():
        os_ref[...] = jnp.zeros(os_ref.shape, F32)

    @pl.when(i < n_p)
    def _():
        apply(xp_ref, op_ref)

    @pl.when(i >= n_p)
    def _():
        apply(xs_ref, os_ref)


def _ffn(xp, xs, norms, final_g, wg, wu, wd, layer, f, *, tm, n_chunks=2, final_seq=None):
    (n_rows_p, d), n_rows_s = xp.shape, xs.shape[0]
    ff = wg.shape[-1]
    nidx = (layer * 2 + f) * 2
    tm_s = _pick_tile(n_rows_s, tm, BF16_ROWS)
    if final_seq is None:
        n_p, out_rows_p = pl.cdiv(n_rows_p, tm), n_rows_p
        p_in = p_out = pl.BlockSpec((tm, d), lambda i: (jnp.minimum(i, n_p - 1), 0))
    else:
        t, skip = final_seq
        assert skip % BF16_ROWS == 0 and t % BF16_ROWS == 0
        tm = _pick_tile(t - skip, tm, BF16_ROWS)
        per_seq = (t - skip) // tm
        n_p, out_rows_p = (n_rows_p // t) * per_seq, (n_rows_p // t) * (t - skip)

        def in_rows(i):
            ip = jnp.minimum(i, n_p - 1)
            return pl.multiple_of((ip // per_seq) * t + skip + (ip % per_seq) * tm, BF16_ROWS), 0

        p_in = pl.BlockSpec((pl.Element(tm), pl.Element(d)), in_rows)
        p_out = pl.BlockSpec((tm, d), lambda i: (jnp.minimum(i, n_p - 1), 0))
    n_s = pl.cdiv(n_rows_s, tm_s)
    s_spec = pl.BlockSpec((tm_s, d), lambda i: (jnp.maximum(i - n_p, 0), 0))
    in_rows, out_rows = _pick_tile(d, WEIGHT_CHUNK_BYTES // (4 * ff)), _pick_tile(ff, WEIGHT_CHUNK_BYTES // (4 * d))
    hbm = pl.BlockSpec(memory_space=pl.ANY)
    return pl.pallas_call(
        functools.partial(_ffn_body, n_chunks=n_chunks, n_p=n_p, final=final_seq is not None, layer=layer, f=f),
        grid=(n_p + n_s,),
        in_specs=[
            p_in, s_spec,
            _const_spec((None, 1, d), (nidx, 0, 0)),
            _const_spec((None, 1, d), (nidx + 1, 0, 0)),
            _const_spec((1, d), (0, 0)),
            hbm, hbm, hbm,
        ],
        out_specs=[p_out, s_spec],
        out_shape=[jax.ShapeDtypeStruct((out_rows_p, d), F32), jax.ShapeDtypeStruct((n_rows_s, d), F32)],
        scratch_shapes=[
            pltpu.VMEM((d, ff), BF16), pltpu.VMEM((d, ff), BF16), pltpu.VMEM((ff, d), BF16),
            pltpu.VMEM((WEIGHT_SLOTS, in_rows, ff), F32), pltpu.VMEM((WEIGHT_SLOTS, out_rows, d), F32),
            pltpu.SemaphoreType.DMA((WEIGHT_SLOTS,)),
        ],
        compiler_params=_params("arbitrary"),
        name="ffn",
    )(xp, xs, norms, norms, final_g, wg, wu, wd)


def _pool_p_body(x_ref, g0_ref, g1_ref, w_ref, sc_ref, o_ref, st_ref, *lvl_refs, ts, n_t):
    t = pl.program_id(1)
    d = x_ref.shape[-1]
    n_g = len(POOL_WINDOWS)
    gd = d // n_g
    x = x_ref[0]
    h = _rms(x, g0_ref[...])
    cat_ref = lvl_refs[0]
    top = SUBLANES + POOL_HALO
    rows = POOL_HALO + ts

    @pl.when(t == 0)
    def _():
        cat_ref[0:top, :] = jnp.zeros((top, d), F32)
        for ref in lvl_refs[1:]:
            ref[0:SUBLANES, :] = jnp.zeros((SUBLANES, ref.shape[1]), F32)

    cat_ref[top:top + ts, :] = h
    for k in range(1, len(lvl_refs)):
        prev, cur = lvl_refs[k - 1], lvl_refs[k]
        lo = prev.shape[1] - cur.shape[1]
        cur[SUBLANES:SUBLANES + rows, :] = (prev[SUBLANES:SUBLANES + rows, lo:]
                                            + prev[SUBLANES - 2 ** (k - 1):SUBLANES - 2 ** (k - 1) + rows, lo:])
    pos = t * ts + lax.broadcasted_iota(jnp.int32, (ts, 1), 0)
    outs = []
    for g, w in enumerate(POOL_WINDOWS):
        c0 = g * gd
        k = w.bit_length() - 1
        if k < len(lvl_refs):
            ref = lvl_refs[k]
            l0 = c0 - (d - ref.shape[1])
            acc = ref[top:top + ts, l0:l0 + gd]
        else:
            ref = lvl_refs[k - 1]
            l0 = c0 - (d - ref.shape[1])
            acc = ref[top:top + ts, l0:l0 + gd] + ref[top - w // 2:top - w // 2 + ts, l0:l0 + gd]
        inv_cnt = 1.0 / jnp.minimum(w, pos + 1).astype(F32)
        pooled = (acc * inv_cnt - h[:, c0:c0 + gd]).astype(BF16)
        outs.append(jnp.dot(pooled, w_ref[g], preferred_element_type=F32))
    m = jnp.concatenate(outs, axis=-1) * sc_ref[...]
    o_ref[0] = x + _rms(m, g1_ref[...])

    @pl.when(t == n_t - 1)
    def _():
        st_ref[0] = cat_ref[top + ts - POOL_BUF:top + ts, :]

    cat_ref[SUBLANES:top, :] = cat_ref[SUBLANES + ts:top + ts, :]


def _pool_prompt(x, mix_norm, pool_w, pool_scale, layer, j, *, ts):
    b, t, d = x.shape
    n_t = t // ts
    assert n_t * ts == t and ts % 8 == 0 and ts >= POOL_HALO
    g = len(POOL_WINDOWS)
    assert all(w == 2 ** (i + 1) for i, w in enumerate(POOL_WINDOWS)) and POOL_WINDOWS[-1] // 2 == SUBLANES
    buf_rows = SUBLANES + POOL_HALO + ts
    level_lanes = [d] + [d - i * (d // g) for i in range(g - 1)]
    return pl.pallas_call(
        functools.partial(_pool_p_body, ts=ts, n_t=n_t),
        grid=(b, n_t),
        in_specs=[
            pl.BlockSpec((1, ts, d), lambda bi, ti: (bi, ti, 0)),
            _const_spec((None, 1, d), (layer * 2, 0, 0)),
            _const_spec((None, 1, d), (layer * 2 + 1, 0, 0)),
            _const_spec((None, g, d // g, d // g), (j, 0, 0, 0)),
            _const_spec((None, 1, d), (j, 0, 0)),
        ],
        out_specs=[
            pl.BlockSpec((1, ts, d), lambda bi, ti: (bi, ti, 0)),
            pl.BlockSpec((1, POOL_BUF, d), lambda bi, ti: (bi, 0, 0)),
        ],
        out_shape=[jax.ShapeDtypeStruct((b, t, d), F32), jax.ShapeDtypeStruct((b, POOL_BUF, d), F32)],
        scratch_shapes=[pltpu.VMEM((buf_rows, lanes), F32) for lanes in level_lanes],
        compiler_params=_params("parallel", "arbitrary"),
        name="pool_prompt",
    )(x, mix_norm, mix_norm, pool_w, pool_scale)


def _pool_s_body(x_ref, st_ref, g0_ref, g1_ref, w_ref, sc_ref, o_ref, so_ref, cat_ref, *, n_s, past):
    d = g0_ref.shape[-1]
    gd = d // len(POOL_WINDOWS)
    bt = st_ref.shape[0]
    top = POOL_HALO
    x = x_ref[...]
    h = _rms(x, g0_ref[...])
    h3 = h.reshape(bt, n_s, d)
    cat_ref[:, 0:top - POOL_BUF, :] = jnp.zeros((bt, top - POOL_BUF, d), F32)
    cat_ref[:, top - POOL_BUF:top, :] = st_ref[...]
    cat_ref[:, top:top + n_s, :] = h3
    pos = past + lax.broadcasted_iota(jnp.int32, (1, n_s, 1), 1)
    outs = []
    for g, w in enumerate(POOL_WINDOWS):
        c0 = g * gd
        acc = h3[:, :, c0:c0 + gd]
        for jj in range(1, w):
            acc = acc + cat_ref[:, top - jj:top - jj + n_s, c0:c0 + gd]
        inv_cnt = 1.0 / jnp.minimum(w, pos + 1).astype(F32)
        pooled = (acc * inv_cnt - h3[:, :, c0:c0 + gd]).reshape(bt * n_s, gd).astype(BF16)
        outs.append(jnp.dot(pooled, w_ref[g], preferred_element_type=F32))
    m = jnp.concatenate(outs, axis=-1) * sc_ref[...]
    o_ref[...] = x + _rms(m, g1_ref[...])
    so_ref[...] = cat_ref[:, top + n_s - POOL_BUF:top + n_s, :]


def _pool_sample(x, state, mix_norm, pool_w, pool_scale, layer, j, *, n_s, past, bt):
    db = x.shape[0] // n_s
    d = mix_norm.shape[-1]
    g = len(POOL_WINDOWS)
    assert n_s % SUBLANES == 0
    return pl.pallas_call(
        functools.partial(_pool_s_body, n_s=n_s, past=past),
        grid=(db // bt,),
        in_specs=[
            pl.BlockSpec((bt * n_s, d), lambda i: (i, 0)),
            pl.BlockSpec((None, bt, POOL_BUF, d), lambda i: (j, i, 0, 0)),
            _const_spec((None, 1, d), (layer * 2, 0, 0)),
            _const_spec((None, 1, d), (layer * 2 + 1, 0, 0)),
            _const_spec((None, g, d // g, d // g), (j, 0, 0, 0)),
            _const_spec((None, 1, d), (j, 0, 0)),
        ],
        out_specs=[
            pl.BlockSpec((bt * n_s, d), lambda i: (i, 0)),
            pl.BlockSpec((bt, POOL_BUF, d), lambda i: (i, 0, 0)),
        ],
        out_shape=[jax.ShapeDtypeStruct((db * n_s, d), F32), jax.ShapeDtypeStruct((db, POOL_BUF, d), F32)],
        scratch_shapes=[pltpu.VMEM((bt, POOL_HALO + n_s, d), F32)],
        compiler_params=_params("parallel"),
        name="pool_sample",
    )(x, state, mix_norm, mix_norm, pool_w, pool_scale)


def _layer_norm(y, g, b):
    mu = jnp.mean(y, axis=-1, keepdims=True)
    yc = y - mu
    var = jnp.mean(yc * yc, axis=-1, keepdims=True)
    return yc * lax.rsqrt(var + LN_EPS) * g + b


def _conv_p_body(x_ref, g0_ref, g1_ref, w1_ref, b1_ref, wdw_ref, bdw_ref, lng_ref, lnb_ref, w2_ref, b2_ref,
                 o_ref, st_ref, cat_ref, y_ref, *, ts, n_t):
    t = pl.program_id(1)
    d = x_ref.shape[-1]
    first = -(-ts // (2 * BF16_ROWS)) * BF16_ROWS
    parts = ((0, first), (first, ts - first))[:2 if ts > first else 1]

    @pl.when(t == 0)
    def _():
        cat_ref[0:CONV_HALO, :] = jnp.zeros((CONV_HALO, d), F32)

    for lo, n in parts:
        h = _rms(x_ref[0, lo:lo + n, :], g0_ref[...]).astype(BF16)
        glu = jnp.dot(h, w1_ref[...], preferred_element_type=F32) + b1_ref[...]
        cat_ref[CONV_HALO + lo:CONV_HALO + lo + n, :] = glu[:, :d] * jax.nn.sigmoid(glu[:, d:])

    off = CONV_HALO - CONV_BUF
    for lo, n in parts:
        for c0 in range(0, d, LANES):
            y = None
            for r in range(SUBLANES):
                zrows = n + (SUBLANES if r else 0)
                z = None
                for a in range((CONV_WIDTH + off) // SUBLANES + 1):
                    k = SUBLANES * a + r - off
                    if 0 <= k < CONV_WIDTH:
                        r0 = lo + SUBLANES * a
                        term = cat_ref[r0:r0 + zrows, c0:c0 + LANES] * wdw_ref[k:k + 1, c0:c0 + LANES]
                        z = term if z is None else z + term
                zs = z[r:r + n]
                y = zs if y is None else y + zs
            y_ref[lo:lo + n, c0:c0 + LANES] = y + bdw_ref[:, c0:c0 + LANES]
        a = _silu(_layer_norm(y_ref[lo:lo + n, :], lng_ref[...], lnb_ref[...])).astype(BF16)
        m = jnp.dot(a, w2_ref[...], preferred_element_type=F32) + b2_ref[...]
        o_ref[0, lo:lo + n, :] = x_ref[0, lo:lo + n, :] + _rms(m, g1_ref[...])

    @pl.when(t == n_t - 1)
    def _():
        st_ref[0] = cat_ref[CONV_HALO + ts - CONV_BUF:CONV_HALO + ts, :]

    cat_ref[0:CONV_HALO, :] = cat_ref[ts:ts + CONV_HALO, :]


def _conv_specs(layer, j, d):
    return [
        _const_spec((None, 1, d), (layer * 2, 0, 0)),
        _const_spec((None, 1, d), (layer * 2 + 1, 0, 0)),
        _const_spec((None, d, 2 * d), (j, 0, 0)),
        _const_spec((None, 1, 2 * d), (j, 0, 0)),
        _const_spec((None, CONV_WIDTH, d), (j, 0, 0)),
        _const_spec((None, 1, d), (j, 0, 0)),
        _const_spec((None, 1, d), (j, 0, 0)),
        _const_spec((None, 1, d), (j, 0, 0)),
        _const_spec((None, d, d), (j, 0, 0)),
        _const_spec((None, 1, d), (j, 0, 0)),
    ]


def _conv_prompt(x, mix_norm, cw, layer, j, *, ts):
    b, t, d = x.shape
    n_t = t // ts
    assert n_t * ts == t and ts % 8 == 0 and ts >= CONV_HALO
    return pl.pallas_call(
        functools.partial(_conv_p_body, ts=ts, n_t=n_t),
        grid=(b, n_t),
        in_specs=[pl.BlockSpec((1, ts, d), lambda bi, ti: (bi, ti, 0))] + _conv_specs(layer, j, d),
        out_specs=[
            pl.BlockSpec((1, ts, d), lambda bi, ti: (bi, ti, 0)),
            pl.BlockSpec((1, CONV_BUF, d), lambda bi, ti: (bi, 0, 0)),
        ],
        out_shape=[jax.ShapeDtypeStruct((b, t, d), F32), jax.ShapeDtypeStruct((b, CONV_BUF, d), F32)],
        scratch_shapes=[pltpu.VMEM((CONV_HALO + ts, d), F32), pltpu.VMEM((ts, d), F32)],
        compiler_params=_params("parallel", "arbitrary"),
        name="conv_prompt",
    )(x, mix_norm, mix_norm, *cw)


def _conv_s_body(x_ref, st_ref, g0_ref, g1_ref, w1_ref, b1_ref, wdw_ref, bdw_ref, lng_ref, lnb_ref, w2_ref, b2_ref,
                 o_ref, so_ref, cat_ref, *, n_s):
    d = g0_ref.shape[-1]
    bt = st_ref.shape[0]
    top = CONV_HALO
    x = x_ref[...]
    h = _rms(x, g0_ref[...]).astype(BF16)
    z = jnp.dot(h, w1_ref[...], preferred_element_type=F32) + b1_ref[...]
    u = z[:, :d] * jax.nn.sigmoid(z[:, d:])
    cat_ref[:, 0:top - CONV_BUF, :] = jnp.zeros((bt, top - CONV_BUF, d), F32)
    cat_ref[:, top - CONV_BUF:top, :] = st_ref[...]
    cat_ref[:, top:top + n_s, :] = u.reshape(bt, n_s, d)
    off = top - CONV_BUF
    y = cat_ref[:, off:off + n_s, :] * wdw_ref[0:1, :] + bdw_ref[...]
    for k in range(1, CONV_WIDTH):
        y = y + cat_ref[:, off + k:off + k + n_s, :] * wdw_ref[k:k + 1, :]
    a = _silu(_layer_norm(y, lng_ref[...], lnb_ref[...])).reshape(bt * n_s, d).astype(BF16)
    m = jnp.dot(a, w2_ref[...], preferred_element_type=F32) + b2_ref[...]
    o_ref[...] = x + _rms(m, g1_ref[...])
    so_ref[...] = cat_ref[:, top + n_s - CONV_BUF:top + n_s, :]


def _conv_sample(x, state, mix_norm, cw, layer, j, *, n_s, bt):
    db = x.shape[0] // n_s
    d = mix_norm.shape[-1]
    assert n_s % SUBLANES == 0
    return pl.pallas_call(
        functools.partial(_conv_s_body, n_s=n_s),
        grid=(db // bt,),
        in_specs=[
            pl.BlockSpec((bt * n_s, d), lambda i: (i, 0)),
            pl.BlockSpec((None, bt, CONV_BUF, d), lambda i: (j, i, 0, 0)),
        ] + _conv_specs(layer, j, d),
        out_specs=[
            pl.BlockSpec((bt * n_s, d), lambda i: (i, 0)),
            pl.BlockSpec((bt, CONV_BUF, d), lambda i: (i, 0, 0)),
        ],
        out_shape=[jax.ShapeDtypeStruct((db * n_s, d), F32), jax.ShapeDtypeStruct((db, CONV_BUF, d), F32)],
        scratch_shapes=[pltpu.VMEM((bt, CONV_HALO + n_s, d), F32)],
        compiler_params=_params("parallel"),
        name="conv_sample",
    )(x, state, mix_norm, mix_norm, *cw)


def _qkv_body(x_ref, g0_ref, w_ref, q_ref, k_ref, v_ref, *kv_bf16_refs, decode):
    rows, d = x_ref.shape
    h = _rms(x_ref[...], g0_ref[...]).astype(BF16)
    qkv = jnp.dot(h, w_ref[...], preferred_element_type=F32)
    q = qkv[:, :d] * (HEAD_DIM ** -0.5 * LOG2E)
    k = qkv[:, d:2 * d]
    v = qkv[:, 2 * d:]
    k_ref[...] = k.reshape(rows, N_HEADS, V_DIM)
    v_ref[...] = v.reshape(rows, N_HEADS, V_DIM)
    if decode:
        q_ref[...] = q.reshape(rows, N_HEADS, V_DIM)
    else:
        kb_ref, vb_ref = kv_bf16_refs
        q_ref[...] = q.astype(BF16)
        kb_ref[...] = k.astype(BF16)
        vb_ref[...] = v.astype(BF16)


def _qkv(x, mix_norm, wqkv, layer, j, *, tm, decode):
    n, d = x.shape
    row = pl.BlockSpec((tm, d), lambda i: (i, 0))
    per_head = pl.BlockSpec((tm, N_HEADS, V_DIM), lambda i: (i, 0, 0))
    head_shape = jax.ShapeDtypeStruct((n, N_HEADS, V_DIM), F32)
    flat_bf16 = jax.ShapeDtypeStruct((n, d), BF16)
    return pl.pallas_call(
        functools.partial(_qkv_body, decode=decode),
        grid=(pl.cdiv(n, tm),),
        in_specs=[row, _const_spec((None, 1, d), (layer * 2, 0, 0)), _const_spec((None, d, 3 * d), (j, 0, 0))],
        out_specs=[per_head] * 3 if decode else [row, per_head, per_head, row, row],
        out_shape=[head_shape] * 3 if decode else [flat_bf16, head_shape, head_shape, flat_bf16, flat_bf16],
        compiler_params=_params("parallel"),
        name="qkv_proj",
    )(x, mix_norm, wqkv)


def _bucket(n):
    nf = jnp.maximum(n, 1).astype(F32)
    large = MAX_EXACT + (jnp.log(nf / MAX_EXACT) / math.log(MAX_DISTANCE / MAX_EXACT)
                         * (N_BUCKETS - MAX_EXACT)).astype(jnp.int32)
    large = jnp.minimum(large, N_BUCKETS - 1)
    return jnp.where(n < MAX_EXACT, n, large)


def _lookup(bucket, entry):
    out = jnp.zeros(bucket.shape, F32)
    for b in range(N_BUCKETS):
        out = jnp.where(bucket == b, entry(b), out)
    return out


def _bias_p_body(table_ref, bp_ref, *, blk):
    head = pl.program_id(0)
    a = lax.broadcasted_iota(jnp.int32, (blk, blk), 0)
    b = lax.broadcasted_iota(jnp.int32, (blk, blk), 1)
    for sel in range(3):
        n = sel * blk + b - a
        vals = _lookup(_bucket(jnp.maximum(n, 0)), lambda bb: table_ref[bb * N_HEADS + head])
        bp_ref[0, sel] = jnp.where(n >= 0, vals * LOG2E, NEG_INF)


def _bias_s_body(tt_ref, bs_ref, *, n_s):
    rows = 2 * n_s * N_HEADS
    cols = PAGE_SIZE * N_HEADS
    tt = tt_ref[...]
    trow = jnp.broadcast_to(tt[None], (rows // N_HEADS, N_HEADS, N_BUCKETS)).reshape(rows, N_BUCKETS)

    def tile(width, dist):
        r = lax.broadcasted_iota(jnp.int32, (rows, width), 0)
        c = lax.broadcasted_iota(jnp.int32, (rows, width), 1)
        n = dist((r // N_HEADS) % n_s, c // N_HEADS)
        vals = _lookup(_bucket(jnp.maximum(n, 0)), lambda bb: trow[:, bb:bb + 1])
        return jnp.where((r % N_HEADS == c % N_HEADS) & (n >= 0) & (c < cols), vals * LOG2E, NEG_INF)

    bs_ref[:, 0:cols] = tile(cols, lambda qi, kk: 2 * PAGE_SIZE + qi - kk)
    bs_ref[:, cols:2 * cols] = tile(cols, lambda qi, kk: PAGE_SIZE + qi - kk)
    bs_ref[:, 2 * cols:2 * cols + PAGE_SIZE] = tile(
        PAGE_SIZE, lambda qi, kk: jnp.where(kk < n_s, qi - kk, -1))


def _bias_tiles(table, *, blk, n_s):
    far = _first_far_distance()
    assert blk + 1 >= far and PAGE_SIZE + 1 >= far and n_s * N_HEADS <= PAGE_SIZE
    bias_p = pl.pallas_call(
        functools.partial(_bias_p_body, blk=blk),
        grid=(N_HEADS,),
        in_specs=[pl.BlockSpec(memory_space=pltpu.SMEM)],
        out_specs=pl.BlockSpec((1, 3, blk, blk), lambda h: (h, 0, 0, 0)),
        out_shape=jax.ShapeDtypeStruct((N_HEADS, 3, blk, blk), F32),
        compiler_params=_params("parallel"),
        name="rel_bias_prompt",
    )(table.reshape(-1))
    width = 2 * PAGE_SIZE * N_HEADS + PAGE_SIZE
    bias_s = pl.pallas_call(
        functools.partial(_bias_s_body, n_s=n_s),
        out_shape=jax.ShapeDtypeStruct((2 * n_s * N_HEADS, width), F32),
        compiler_params=pltpu.CompilerParams(vmem_limit_bytes=VMEM_LIMIT),
        name="rel_bias_sample",
    )(table.T)
    return bias_p, bias_s


def _diff_lambda(lam_ref, lam_init):
    lp = lam_ref[...]
    s1 = jnp.sum(lp[0:1] * lp[1:2], axis=-1, keepdims=True)
    s2 = jnp.sum(lp[2:3] * lp[3:4], axis=-1, keepdims=True)
    return jnp.exp(s1) - jnp.exp(s2) + lam_init


_NT = (((1,), (1,)), ((), ()))


def _attn_p_body(q_ref, k_ref, v_ref, x_ref, bias_ref, lam_ref, sub_ref, wo_ref, g1_ref, o_ref,
                 qz_ref, vt_ref, m_ref, acc_ref, s_ref, a_ref, *, blk, n_full, tail, lam_init):
    i = pl.program_id(1)

    @pl.when(i == 0)
    def _():
        for hd in range(N_HEADS):
            c0 = hd * V_DIM

            ones_row = (lax.broadcasted_iota(jnp.int32, (SUM_ROWS, blk), 0) == 0).astype(BF16)

            def xpose(jb, carry):
                r0 = pl.multiple_of(jb * blk, blk)
                vt_ref[hd, jb, 0:V_DIM, :] = v_ref[0, pl.ds(r0, blk), c0:c0 + V_DIM].T
                vt_ref[hd, jb, V_DIM:V_DIM + SUM_ROWS, :] = ones_row
                return carry

            lax.fori_loop(0, n_full, xpose, 0)
            if tail:
                r0 = n_full * blk
                vt_ref[hd, n_full, 0:V_DIM, 0:tail] = (
                    v_ref[0, r0:r0 + tail, c0:c0 + V_DIM].astype(F32).T.astype(BF16))
                vt_ref[hd, n_full, V_DIM:V_DIM + SUM_ROWS, :] = ones_row

    def run(wq, n_far, near, tail_keys):
        lane = lax.broadcasted_iota(jnp.int32, (wq, V_DIM), 1)
        for hd in range(N_HEADS):
            q = q_ref[0, 0:wq, hd * V_DIM:(hd + 1) * V_DIM]
            qz_ref[hd, 0:wq, :] = jnp.where(lane < HEAD_DIM, q, jnp.zeros_like(q))
            qz_ref[hd, wq:2 * wq, :] = jnp.where(lane >= HEAD_DIM, q, jnp.zeros_like(q))
        m_ref[:, :, 0:2 * wq] = jnp.full((N_HEADS, 1, 2 * wq), NEG_INF, F32)
        acc_ref[:, :, 0:2 * wq] = jnp.zeros((N_HEADS, V_DIM + SUM_ROWS, 2 * wq), F32)

        def all_heads(n, keys, vt, bias, const_bias=False):
            def stage(hd):
                s = lax.dot_general(keys(hd), qz_ref[hd, 0:2 * wq, :], _NT, preferred_element_type=F32)
                if not const_bias:
                    b2 = bias(hd)
                    s = s + jnp.concatenate([b2, b2], axis=1)
                s_ref[hd % 2, 0:n, 0:2 * wq] = s

            def softmax(hd):
                s = s_ref[hd % 2, 0:n, 0:2 * wq]
                m_old = m_ref[hd, :, 0:2 * wq]
                col_max = jnp.max(s, axis=0, keepdims=True)
                if const_bias:
                    c = bias(hd)
                    m_new = jnp.maximum(m_old, col_max + c)
                    shift = m_new - c
                else:
                    shift = m_new = jnp.maximum(m_old, col_max)
                alpha = jnp.exp2(m_old - m_new)
                p = jnp.exp2(s - shift).astype(BF16)
                m_ref[hd, :, 0:2 * wq] = m_new
                acc_ref[hd, :, 0:2 * wq] = (alpha * acc_ref[hd, :, 0:2 * wq]
                                            + jnp.dot(vt(hd), p, preferred_element_type=F32))

            stage(0)
            for hd in range(N_HEADS):
                if hd + 1 < N_HEADS:
                    stage(hd + 1)
                softmax(hd)

        def full_block(jb, sel):
            r0 = pl.multiple_of(jb * blk, blk)
            far = sel == 2
            all_heads(blk,
                      lambda hd: k_ref[0, pl.ds(r0, blk), hd * V_DIM:(hd + 1) * V_DIM],
                      lambda hd: vt_ref[hd, jb],
                      (lambda hd: bias_ref[hd, 2, 0:1, 0:1]) if far else (lambda hd: bias_ref[hd, sel, :, 0:wq]),
                      const_bias=far)

        def far_body(jb, carry):
            full_block(jb, 2)
            return carry

        lax.fori_loop(0, n_far, far_body, 0)
        for jb, sel, cond in near:
            if cond is None:
                full_block(jb, sel)
            else:
                pl.when(cond)(functools.partial(full_block, jb, sel))
        if tail_keys:
            r0 = n_full * blk
            all_heads(tail_keys,
                      lambda hd: k_ref[0, r0:r0 + tail_keys, hd * V_DIM:(hd + 1) * V_DIM],
                      lambda hd: vt_ref[hd, n_full, :, 0:tail_keys],
                      lambda hd: bias_ref[hd, 0, 0:tail_keys, 0:wq])

        lam = _diff_lambda(lam_ref, lam_init)
        for hd in range(N_HEADS):
            o_both = acc_ref[hd, 0:V_DIM, 0:2 * wq] * (1.0 / acc_ref[hd, V_DIM:V_DIM + 1, 0:2 * wq])
            o = o_both[:, 0:wq] - lam * o_both[:, wq:2 * wq]
            o = o * lax.rsqrt(jnp.mean(o * o, axis=0, keepdims=True) + RMS_EPS) * sub_ref[...] * (1.0 - lam_init)
            a_ref[0:wq, hd * V_DIM:(hd + 1) * V_DIM] = o.T.astype(BF16)
        mix = jnp.dot(a_ref[0:wq, :], wo_ref[...], preferred_element_type=F32)
        o_ref[0, 0:wq, :] = x_ref[0, 0:wq, :] + _rms(mix, g1_ref[...])

    @pl.when(i < n_full)
    def _():
        run(blk, jnp.maximum(i - 1, 0), [(jnp.maximum(i - 1, 0), 1, i >= 1), (i, 0, None)], 0)

    if tail:
        @pl.when(i == n_full)
        def _():
            run(-(-tail // LANES) * LANES, n_full - 1, [(n_full - 1, 1, None)], tail)


def _attn_prompt(q, kb, vb, x, bias_p, lam_p, subln_col, wo, mix_norm, layer, j, *, blk, lam_init):
    b, t, d = q.shape
    n_full, tail = t // blk, t % blk
    assert tail % 16 == 0
    n_blocks = n_full + (1 if tail else 0)
    seq_blk = pl.BlockSpec((1, blk, d), lambda bi, qi: (bi, qi, 0))
    return pl.pallas_call(
        functools.partial(_attn_p_body, blk=blk, n_full=n_full, tail=tail, lam_init=lam_init),
        grid=(b, n_blocks),
        in_specs=[
            seq_blk,
            pl.BlockSpec((1, t, d), lambda bi, qi: (bi, 0, 0)),
            pl.BlockSpec((1, t, d), lambda bi, qi: (bi, 0, 0)),
            seq_blk,
            _const_spec((N_HEADS, 3, blk, blk), (0, 0, 0, 0)),
            _const_spec((None, 4, HEAD_DIM), (j, 0, 0)),
            _const_spec((None, V_DIM, 1), (j, 0, 0)),
            _const_spec((None, d, d), (j, 0, 0)),
            _const_spec((None, 1, d), (layer * 2 + 1, 0, 0)),
        ],
        out_specs=seq_blk,
        out_shape=jax.ShapeDtypeStruct((b, t, d), F32),
        scratch_shapes=[
            pltpu.VMEM((N_HEADS, 2 * blk, V_DIM), BF16),
            pltpu.VMEM((N_HEADS, n_blocks, V_DIM + SUM_ROWS, blk), BF16),
            pltpu.VMEM((N_HEADS, 1, 2 * blk), F32),
            pltpu.VMEM((N_HEADS, V_DIM + SUM_ROWS, 2 * blk), F32),
            pltpu.VMEM((2, blk, 2 * blk), F32),
            pltpu.VMEM((blk, d), BF16),
        ],
        compiler_params=_params("arbitrary", "arbitrary"),
        name="attn_prompt",
    )(q, kb, vb, x, bias_p, lam_p, subln_col, wo, mix_norm)


def _attn_s_body(pt_ref, q_ref, kn_ref, vn_ref, bias_ref, lam_ref, sub_ref, *rest, n_pages, n_s, lam_init):
    k_pages = rest[:n_pages]
    v_pages = rest[n_pages:2 * n_pages]
    o_ref, s_ref = rest[2 * n_pages:]
    half = n_s * N_HEADS
    cols = PAGE_SIZE * N_HEADS

    q = q_ref[...].reshape(half, V_DIM)
    lane = lax.broadcasted_iota(jnp.int32, (half, V_DIM), 1)
    q2 = jnp.concatenate([jnp.where(lane < HEAD_DIM, q, 0.0), jnp.where(lane >= HEAD_DIM, q, 0.0)], axis=0).astype(BF16)

    def scores(keys, bias):
        return lax.dot_general(q2, keys, _NT, preferred_element_type=F32) + bias

    pad = jnp.zeros((PAGE_SIZE - half, V_DIM), F32)
    kn = jnp.concatenate([kn_ref[...].reshape(half, V_DIM), pad], axis=0).astype(BF16)
    vn = jnp.concatenate([vn_ref[...].reshape(half, V_DIM), pad], axis=0).astype(BF16)
    s_new = scores(kn, bias_ref[:, 2 * cols:2 * cols + PAGE_SIZE])
    m = jnp.max(s_new, axis=-1, keepdims=True)
    for r_ in range(n_pages):
        near = r_ == n_pages - 1
        s = scores(k_pages[r_][...].reshape(cols, V_DIM).astype(BF16),
                   bias_ref[:, cols:2 * cols] if near else bias_ref[:, 0:cols])
        s_ref[r_] = s
        m = jnp.maximum(m, jnp.max(s, axis=-1, keepdims=True))

    p = jnp.exp2(s_new - m)
    l = jnp.sum(p, axis=-1, keepdims=True)
    acc = jnp.dot(p.astype(BF16), vn, preferred_element_type=F32)
    for r_ in range(n_pages):
        p = jnp.exp2(s_ref[r_] - m)
        l = l + jnp.sum(p, axis=-1, keepdims=True)
        acc = acc + jnp.dot(p.astype(BF16), v_pages[r_][...].reshape(cols, V_DIM).astype(BF16),
                            preferred_element_type=F32)

    lam = _diff_lambda(lam_ref, lam_init)
    o_all = acc * (1.0 / l)
    o = o_all[0:half] - lam * o_all[half:2 * half]
    o_ref[...] = (_rms(o, sub_ref[...]) * (1.0 - lam_init)).reshape(o_ref.shape)


def _attn_sample(q, kn, vn, cache_k, cache_v, page_table, bias_s, lam_p, subln, j, *, n_pages, lam_init):
    db, n_s = q.shape[:2]
    assert n_pages == page_table.shape[1]
    rows = 2 * n_s * N_HEADS
    new_kv = pl.BlockSpec((1, n_s, 1, N_HEADS, V_DIM), lambda b, pt: (b, 0, 0, 0, 0))
    q_spec = pl.BlockSpec((1, n_s, N_HEADS, V_DIM), lambda b, pt: (b, 0, 0, 0))

    def page_spec(r):
        return pl.BlockSpec((1, PAGE_SIZE, 1, N_HEADS, V_DIM), lambda b, pt: (pt[b, r], 0, j, 0, 0))

    grid_spec = pltpu.PrefetchScalarGridSpec(
        num_scalar_prefetch=1,
        grid=(db,),
        in_specs=[
            q_spec, new_kv, new_kv,
            pl.BlockSpec(bias_s.shape, lambda b, pt: (0, 0)),
            pl.BlockSpec((None, 4, HEAD_DIM), lambda b, pt: (j, 0, 0)),
            pl.BlockSpec((None, 1, V_DIM), lambda b, pt: (j, 0, 0)),
        ] + [page_spec(r) for r in range(n_pages)] * 2,
        out_specs=q_spec,
        scratch_shapes=[pltpu.VMEM((n_pages, rows, PAGE_SIZE * N_HEADS), F32)],
    )
    return pl.pallas_call(
        functools.partial(_attn_s_body, n_pages=n_pages, n_s=n_s, lam_init=lam_init),
        grid_spec=grid_spec,
        out_shape=jax.ShapeDtypeStruct((db, n_s, N_HEADS, V_DIM), F32),
        compiler_params=_params("parallel"),
        name="attn_sample",
    )(page_table, q, kn, vn, bias_s, lam_p, subln, *([cache_k] * n_pages), *([cache_v] * n_pages))


def _outproj_body(a_ref, x_ref, w_ref, g1_ref, o_ref):
    a = a_ref[...].reshape(x_ref.shape)
    m = jnp.dot(a.astype(BF16), w_ref[...], preferred_element_type=F32)
    o_ref[...] = x_ref[...] + _rms(m, g1_ref[...])


def _outproj(a, x, wo, mix_norm, layer, j, *, tm):
    n, d = x.shape
    row = pl.BlockSpec((tm, d), lambda i: (i, 0))
    return pl.pallas_call(
        _outproj_body,
        grid=(pl.cdiv(n, tm),),
        in_specs=[pl.BlockSpec((tm,) + a.shape[1:], lambda i: (i, 0, 0)), row,
                  _const_spec((None, d, d), (j, 0, 0)), _const_spec((None, 1, d), (layer * 2 + 1, 0, 0))],
        out_specs=row,
        out_shape=jax.ShapeDtypeStruct((n, d), F32),
        compiler_params=_params("parallel"),
        name="attn_outproj",
    )(a, x, wo, mix_norm)


def _pick_tile(n, cap, align=8):
    best = None
    for c in range(align, min(n, cap) + 1, align):
        if n % c == 0:
            best = c
    assert best is not None, (n, cap, align)
    return best


def kernel(x_prompt, x_sample, state_pool, state_conv, cache_k, cache_v, page_table, meta_tokens, rel_bias_table, ffn_norm, ffn_wg, ffn_wu, ffn_wd, mix_norm, pool_w, pool_scale, attn_wqkv, attn_wo, attn_lambda, attn_subln, conv_w1, conv_b1, conv_wdw, conv_bdw, conv_ln_g, conv_ln_b, conv_w2, conv_b2, final_norm):
    b, seq, d = x_prompt.shape
    db, n_s, _ = x_sample.shape
    depth = ffn_wg.shape[0]
    t = seq + N_META
    past = page_table.shape[1] * PAGE_SIZE

    wg, wu, wd = ffn_wg, ffn_wu, ffn_wd
    ffn_norm3 = ffn_norm.reshape(-1, 1, d)
    mix_norm3 = mix_norm.reshape(-1, 1, d)
    pool_w_b = pool_w.astype(BF16)
    pool_scale3 = pool_scale.reshape(-1, 1, d)
    wqkv_b, wo_b = attn_wqkv.astype(BF16), attn_wo.astype(BF16)
    subln3 = attn_subln.reshape(-1, 1, V_DIM)
    cw = (conv_w1.astype(BF16), conv_b1.reshape(-1, 1, 2 * d), conv_wdw, conv_bdw.reshape(-1, 1, d),
          conv_ln_g.reshape(-1, 1, d), conv_ln_b.reshape(-1, 1, d), conv_w2.astype(BF16), conv_b2.reshape(-1, 1, d))
    final_g = final_norm.reshape(1, d)

    meta = jnp.broadcast_to(meta_tokens[None].astype(x_prompt.dtype), (b, N_META, d))
    xp = jnp.concatenate([meta, x_prompt], axis=1).reshape(b * t, d)
    xs = x_sample.reshape(db * n_s, d)

    tm_p = _pick_tile(b * t, 768, 16)
    tm_s = min(512, db * n_s)
    ts_pool = _pick_tile(t, 1032)
    ts_conv = _pick_tile(t, 688, BF16_ROWS)
    bt = _pick_tile(db, 32)

    bias_p = bias_s = None
    pool_p, pool_s, conv_p, conv_s = [], [], [], []
    k_p = v_p = k_s = v_s = None
    for i in range(depth):
        kind, j = i % N_MIXERS, i // N_MIXERS
        xp, xs = _ffn(xp, xs, ffn_norm3, final_g, wg, wu, wd, i, 0, tm=tm_p)
        if kind == 0:
            xp3, st = _pool_prompt(xp.reshape(b, t, d), mix_norm3, pool_w_b, pool_scale3, i, j, ts=ts_pool)
            xp = xp3.reshape(b * t, d)
            pool_p.append(st)
            xs, st = _pool_sample(xs, state_pool, mix_norm3, pool_w_b, pool_scale3, i, j, n_s=n_s, past=past, bt=bt)
            pool_s.append(st)
        elif kind == 1:
            assert j == 0
            lam_init = _lambda_init(i)
            if bias_p is None:
                bias_p, bias_s = _bias_tiles(rel_bias_table, blk=ATTN_BLOCK, n_s=n_s)
            q, k_p, v_p, kb, vb = _qkv(xp, mix_norm3, wqkv_b, i, j, tm=tm_p, decode=False)
            xp = _attn_prompt(q.reshape(b, t, d), kb.reshape(b, t, d), vb.reshape(b, t, d), xp.reshape(b, t, d),
                              bias_p, attn_lambda, attn_subln.reshape(-1, V_DIM, 1), wo_b, mix_norm3, i, j,
                              blk=ATTN_BLOCK, lam_init=lam_init).reshape(b * t, d)
            q, k_s, v_s = _qkv(xs, mix_norm3, wqkv_b, i, j, tm=tm_s, decode=True)
            k_s = k_s.reshape(db, n_s, 1, N_HEADS, V_DIM)
            v_s = v_s.reshape(db, n_s, 1, N_HEADS, V_DIM)
            a = _attn_sample(q.reshape(db, n_s, N_HEADS, V_DIM), k_s, v_s, cache_k, cache_v,
                             page_table, bias_s, attn_lambda, subln3, j, n_pages=page_table.shape[1], lam_init=lam_init)
            xs = _outproj(a.reshape(db * n_s, N_HEADS, V_DIM), xs, wo_b, mix_norm3, i, j, tm=tm_s)
        else:
            xp3, st = _conv_prompt(xp.reshape(b, t, d), mix_norm3, cw, i, j, ts=ts_conv)
            xp = xp3.reshape(b * t, d)
            conv_p.append(st)
            xs, st = _conv_sample(xs, state_conv, mix_norm3, cw, i, j, n_s=n_s, bt=bt)
            conv_s.append(st)
        xp, xs = _ffn(xp, xs, ffn_norm3, final_g, wg, wu, wd, i, 1, tm=tm_p,
                      final_seq=(t, N_META) if i == depth - 1 else None)

    kv_shape_p = (b, t, 1, N_HEADS, V_DIM)
    return (xp.reshape(b, seq, d), xs.reshape(db, n_s, d), jnp.stack(pool_p, axis=0), jnp.stack(pool_s, axis=0),
            jnp.stack(conv_p, axis=0), jnp.stack(conv_s, axis=0),
            k_p.reshape(kv_shape_p), v_p.reshape(kv_shape_p), k_s, v_s)
```
